```python
import jax, jax.numpy as jnp
from jax import lax
import numpy as np

D_MODEL = 2048
BATCH = 8
SEQ = 2048
DEPTH = 1

GRID_W = 64
CTX_LEN = 256
NORM_EPS = 1e-6

ML_HEADS = 4
ML_DV = D_MODEL // 2 // ML_HEADS
ML_DK = ML_DV // 2
ML_QK = ML_HEADS * ML_DK
ML_V = ML_HEADS * ML_DV
ML_GATES = 2 * 2 * ML_HEADS
ML_CHUNK = 64
CONV_W = 5

AT_HEADS = 8
AT_KV_HEADS = 2
AT_DH = D_MODEL // 2 // AT_HEADS
WINDOW = 128
BLOCK_Q = 128
ROPE_THETA = 10000.0

MIX_WIDTH = ML_V + AT_HEADS * AT_DH
IN_SPLITS = (2 * ML_QK, ML_V, ML_V, ML_GATES, AT_HEADS * AT_DH, AT_KV_HEADS * AT_DH, AT_KV_HEADS * AT_DH)
IN_WIDTH = 2 * ML_QK + 2 * ML_V + ML_GATES + (AT_HEADS + 2 * AT_KV_HEADS) * AT_DH

N_EXPERTS = 256
TOP_K = 8
N_GROUPS = 8
TOPK_GROUPS = 4
D_EXPERT = 512
D_SHARED = 512
ROUTE_SCALE = 2.5
MOE_BLOCK = 128

kernel_name = "hybrid_mlstm_swa_moe_dit_layer"


def rms_norm(x, w):
    xf = x.astype(jnp.float32)
    y = xf * lax.rsqrt(jnp.mean(xf * xf, axis=-1, keepdims=True) + NORM_EPS)
    return (y * w.astype(jnp.float32)).astype(x.dtype)


def modulation(cond, w_ada, b_ada):
    m = jax.nn.silu(cond) @ w_ada + b_ada
    return jnp.split(m[..., None, :], 6, axis=-1)


def modulate(h, shift, scale):
    return h * (1 + scale) + shift


def split_columns(p):
    idx = np.cumsum(IN_SPLITS)[:-1].tolist()
    return jnp.split(p, idx, axis=-1)


def to_heads(t, n_heads):
    b, s, _ = t.shape
    return t.reshape(b, s, n_heads, -1)


def axial_rope_tables(n_tok, head_dim):
    rows = n_tok // GRID_W
    row = jnp.broadcast_to(jnp.arange(rows)[:, None], (rows, GRID_W)).reshape(-1)
    col = jnp.broadcast_to(jnp.arange(GRID_W)[None, :], (rows, GRID_W)).reshape(-1)
    n_freq = head_dim // 4
    freqs = ROPE_THETA ** (-jnp.arange(n_freq, dtype=jnp.float32) / n_freq)
    pos = jnp.stack([row, col], axis=-1).astype(jnp.float32)
    ang = pos[:, :, None] * freqs
    return jnp.cos(ang), jnp.sin(ang)


def rope_2d(x, cos, sin):
    b, s, h, dh = x.shape
    xr = x.astype(jnp.float32).reshape(b, s, h, 2, 2, dh // 4)
    x1, x2 = xr[..., 0, :], xr[..., 1, :]
    c = cos[None, :, None]
    sn = sin[None, :, None]
    out = jnp.stack([x1 * c - x2 * sn, x2 * c + x1 * sn], axis=-2)
    return out.reshape(b, s, h, dh).astype(x.dtype)


def centred_depthwise_conv(x, w):
    ch = x.shape[-1]
    return lax.conv_general_dilated(x, w[:, None, :], window_strides=(1,),
                                    padding=[(CONV_W // 2, CONV_W // 2)],
                                    dimension_numbers=('NWC', 'WIO', 'NWC'),
                                    feature_group_count=ch)


def mlstm_streams(qk_raw, v_raw, g_raw, conv_w, gate_b):
    b, s, _ = qk_raw.shape
    qk = jax.nn.silu(centred_depthwise_conv(qk_raw, conv_w))
    q, k = jnp.split(qk, 2, axis=-1)
    heads = lambda t: jnp.transpose(t.reshape(b, s, ML_HEADS, -1), (0, 2, 1, 3))
    q = heads(q) * (ML_DK ** -0.5)
    k = heads(k)
    v = heads(v_raw)
    g = jnp.transpose((g_raw + gate_b).astype(jnp.float32).reshape(b, s, 2, 2, ML_HEADS), (2, 3, 0, 4, 1))
    return q, k, v, g[:, 0], jax.nn.log_sigmoid(g[:, 1])


def mlstm_chunkwise(q, k, v, log_i, log_f, state, with_outputs):
    b, nh, s, dk = q.shape
    dv = v.shape[-1]
    nc = s // ML_CHUNK

    def to_chunks(t):
        t = t.astype(jnp.float32).reshape(b, nh, nc, ML_CHUNK, *t.shape[3:])
        return jnp.moveaxis(t, 2, 0)

    xs = tuple(to_chunks(t) for t in (q, k, v, log_i, log_f))
    causal = jnp.tril(jnp.ones((ML_CHUNK, ML_CHUNK), dtype=bool))

    def step(carry, chunk):
        C, n, m = carry
        qc, kc, vc, li, lf = chunk
        bcum = jnp.cumsum(lf, axis=-1)
        btot = bcum[..., -1]
        log_w_end = btot[..., None] - bcum + li
        m_new = jnp.maximum(btot + m, jnp.max(log_w_end, axis=-1))
        w_end = jnp.exp(log_w_end - m_new[..., None])
        decay = jnp.exp(btot + m - m_new)
        C_new = decay[..., None, None] * C + jnp.einsum('bhs,bhsv,bhsd->bhvd', w_end, vc, kc)
        n_new = decay[..., None] * n + jnp.einsum('bhs,bhsd->bhd', w_end, kc)
        if not with_outputs:
            return (C_new, n_new, m_new), None
        log_w = jnp.where(causal, bcum[..., :, None] - bcum[..., None, :] + li[..., None, :], -jnp.inf)
        log_inter = bcum + m[..., None]
        m_q = jnp.maximum(log_inter, jnp.max(log_w, axis=-1))
        s_qk = jnp.einsum('bhtd,bhsd->bhts', qc, kc) * jnp.exp(log_w - m_q[..., None])
        w_inter = jnp.exp(log_inter - m_q)
        num = jnp.einsum('bhts,bhsv->bhtv', s_qk, vc) + w_inter[..., None] * jnp.einsum('bhvd,bhtd->bhtv', C, qc)
        den = jnp.sum(s_qk, axis=-1) + w_inter * jnp.einsum('bhd,bhtd->bht', n, qc)
        hc = num / jnp.maximum(jnp.abs(den), jnp.exp(-m_q))[..., None]
        return (C_new, n_new, m_new), hc

    state, h = lax.scan(step, state, xs)
    if not with_outputs:
        return state, None
    return state, jnp.moveaxis(h, 0, 2).reshape(b, nh, s, dv)


def bidirectional_mlstm(lat, ctx, with_ctx_out):
    q, k, v, li, lf = lat
    qc, kc, vc, lic, lfc = ctx
    b = q.shape[0]
    zero = (jnp.zeros((b, ML_HEADS, ML_DV, ML_DK), jnp.float32),
            jnp.zeros((b, ML_HEADS, ML_DK), jnp.float32),
            jnp.zeros((b, ML_HEADS), jnp.float32))
    rev = lambda t: jnp.flip(t, axis=2)
    st_f, hc_f = mlstm_chunkwise(qc, kc, vc, lic[0], lfc[0], zero, with_ctx_out)
    _, h_f = mlstm_chunkwise(q, k, v, li[0], lf[0], st_f, True)
    st_b, hc_b = mlstm_chunkwise(rev(qc), rev(kc), rev(vc), rev(lic[1]), rev(lfc[1]), zero, with_ctx_out)
    _, h_b = mlstm_chunkwise(rev(q), rev(k), rev(v), rev(li[1]), rev(lf[1]), st_b, True)
    h_lat = h_f + rev(h_b)
    h_ctx = hc_f + rev(hc_b) if with_ctx_out else None
    return h_lat, h_ctx


def mlstm_output(h, o_raw, norm_w):
    b, nh, s, dv = h.shape
    hf = jnp.transpose(h, (0, 2, 1, 3)).astype(jnp.float32)
    hf = hf * lax.rsqrt(jnp.mean(hf * hf, axis=-1, keepdims=True) + NORM_EPS)
    hf = hf.reshape(b, s, nh * dv) * norm_w.astype(jnp.float32)
    return (hf * jax.nn.sigmoid(o_raw.astype(jnp.float32))).astype(o_raw.dtype)


def windowed_gqa_with_context(q, k, v, k_ctx, v_ctx, sink):
    b, s, hq, dh = q.shape
    hkv = k.shape[2]
    g = hq // hkv
    nb = s // BLOCK_Q
    bq = BLOCK_Q
    nk = 3 * bq
    lc = k_ctx.shape[1]
    scale = dh ** -0.5
    qb = q.reshape(b, nb, bq, hkv, g, dh)
    pad = ((0, 0), (bq, bq), (0, 0), (0, 0))
    kp = jnp.pad(k, pad).reshape(b, nb + 2, bq, hkv, dh)
    vp = jnp.pad(v, pad).reshape(b, nb + 2, bq, hkv, dh)
    kb = jnp.concatenate([kp[:, :-2], kp[:, 1:-1], kp[:, 2:]], axis=2)
    vb = jnp.concatenate([vp[:, :-2], vp[:, 1:-1], vp[:, 2:]], axis=2)
    s_loc = jnp.einsum('bnqhgd,bnkhd->bnhgqk', qb, kb, preferred_element_type=jnp.float32) * scale
    s_ctx = jnp.einsum('bnqhgd,bchd->bnhgqc', qb, k_ctx, preferred_element_type=jnp.float32) * scale
    q_off = jnp.arange(bq)[:, None]
    k_off = jnp.arange(nk)[None, :] - bq
    k_abs = jnp.arange(nb)[:, None, None] * bq + k_off[None]
    valid = (jnp.abs(k_off - q_off) <= WINDOW)[None] & (k_abs >= 0) & (k_abs < s)
    s_loc = jnp.where(valid[None, :, None, None], s_loc, -jnp.inf)
    sink_b = jnp.broadcast_to(sink.astype(jnp.float32).reshape(hkv, g)[None, None, :, :, None, None],
                              s_loc.shape[:-1] + (1,))
    p = jax.nn.softmax(jnp.concatenate([s_loc, s_ctx, sink_b], axis=-1), axis=-1).astype(v.dtype)
    out = (jnp.einsum('bnhgqk,bnkhd->bnqhgd', p[..., :nk], vb)
           + jnp.einsum('bnhgqc,bchd->bnqhgd', p[..., nk:nk + lc], v_ctx))
    return out.reshape(b, s, hq * dh)


def context_attention(q, k, v, sink):
    b, lc, hq, dh = q.shape
    hkv = k.shape[2]
    g = hq // hkv
    qg = q.reshape(b, lc, hkv, g, dh)
    sc = jnp.einsum('bqhgd,bkhd->bhgqk', qg, k, preferred_element_type=jnp.float32) * (dh ** -0.5)
    sink_b = jnp.broadcast_to(sink.astype(jnp.float32).reshape(hkv, g)[None, :, :, None, None],
                              sc.shape[:-1] + (1,))
    p = jax.nn.softmax(jnp.concatenate([sc, sink_b], axis=-1), axis=-1)[..., :lc].astype(v.dtype)
    return jnp.einsum('bhgqk,bkhd->bqhgd', p, v).reshape(b, lc, hq * dh)


def moe_ffn(h, l, w_router, b_router, w_exp_gate, w_exp_up, w_exp_down, w_sh_gate, w_sh_up, w_sh_down):
    b, s, d = h.shape
    n_tok = b * s
    xt = h.reshape(n_tok, d)
    scores = jax.nn.sigmoid((xt @ w_router[l]).astype(jnp.float32))
    biased = scores + b_router[l].astype(jnp.float32)
    grp = biased.reshape(n_tok, N_GROUPS, N_EXPERTS // N_GROUPS)
    grp_score = jnp.sum(lax.top_k(grp, 2)[0], axis=-1)
    _, top_grp = lax.top_k(grp_score, TOPK_GROUPS)
    grp_keep = jnp.any(top_grp[:, :, None] == jnp.arange(N_GROUPS)[None, None, :], axis=1)
    expert_keep = jnp.repeat(grp_keep, N_EXPERTS // N_GROUPS, axis=-1)
    _, e_idx = lax.top_k(jnp.where(expert_keep, biased, -jnp.inf), TOP_K)
    e_w = jnp.take_along_axis(scores, e_idx, axis=-1)
    e_w = e_w / jnp.sum(e_w, axis=-1, keepdims=True) * ROUTE_SCALE

    n_assign = n_tok * TOP_K
    flat_e = e_idx.reshape(-1)
    order = jnp.argsort(flat_e)
    sorted_e = flat_e[order]
    sorted_tok = (order // TOP_K).astype(jnp.int32)
    sorted_w = e_w.reshape(-1)[order].astype(h.dtype)
    counts = jnp.bincount(flat_e, length=N_EXPERTS)
    padded = (counts + MOE_BLOCK - 1) // MOE_BLOCK * MOE_BLOCK
    pad_end = jnp.cumsum(padded)
    pad_start = pad_end - padded
    start = jnp.cumsum(counts) - counts
    dest = pad_start[sorted_e] + jnp.arange(n_assign) - start[sorted_e]
    n_blocks = -(-(n_assign + N_EXPERTS * (MOE_BLOCK - 1)) // MOE_BLOCK)
    n_rows = n_blocks * MOE_BLOCK
    row_tok = jnp.zeros((n_rows,), jnp.int32).at[dest].set(sorted_tok)
    row_w = jnp.zeros((n_rows,), h.dtype).at[dest].set(sorted_w)
    block_e = jnp.minimum(jnp.searchsorted(pad_end, jnp.arange(n_blocks) * MOE_BLOCK, side='right'),
                          N_EXPERTS - 1)

    def expert_block(acc, blk):
        e, tok, w_row = blk
        xb = xt[tok]
        a = jax.nn.silu(xb @ w_exp_gate[l, e]) * (xb @ w_exp_up[l, e])
        return acc.at[tok].add((a @ w_exp_down[l, e]) * w_row[:, None]), None

    routed, _ = lax.scan(expert_block, jnp.zeros_like(xt),
                         (block_e, row_tok.reshape(n_blocks, MOE_BLOCK), row_w.reshape(n_blocks, MOE_BLOCK)))
    shared = (jax.nn.silu(xt @ w_sh_gate[l]) * (xt @ w_sh_up[l])) @ w_sh_down[l]
    return (routed + shared).reshape(b, s, d)


def hybrid_layer(l, x, ctx, c, c_ctx, w_ada, b_ada, norms, w_in, ml_conv, ml_gate_b, ml_norm_w,
                 attn_sink, w_out, w_router, b_router, w_exp_gate, w_exp_up, w_exp_down,
                 w_sh_gate, w_sh_up, w_sh_down, update_context):
    nw = norms[l]
    sh1, sc1, g1, sh2, sc2, g2 = modulation(c, w_ada[l], b_ada[l])
    csh1, csc1, cg1, csh2, csc2, cg2 = modulation(c_ctx, w_ada[l], b_ada[l])

    def moe(hh):
        return moe_ffn(hh, l, w_router, b_router, w_exp_gate, w_exp_up, w_exp_down, w_sh_gate, w_sh_up, w_sh_down)

    p_lat = split_columns(modulate(rms_norm(x, nw[0]), sh1, sc1) @ w_in[l])
    p_ctx = split_columns(modulate(rms_norm(ctx, nw[0]), csh1, csc1) @ w_in[l])

    lat_st = mlstm_streams(p_lat[0], p_lat[1], p_lat[3], ml_conv[l], ml_gate_b[l])
    ctx_st = mlstm_streams(p_ctx[0], p_ctx[1], p_ctx[3], ml_conv[l], ml_gate_b[l])
    h_lat, h_ctx = bidirectional_mlstm(lat_st, ctx_st, update_context)
    y_ml = mlstm_output(h_lat, p_lat[2], ml_norm_w[l])

    cos, sin = axial_rope_tables(x.shape[1], AT_DH)
    q = rope_2d(to_heads(p_lat[4], AT_HEADS), cos, sin)
    k = rope_2d(to_heads(p_lat[5], AT_KV_HEADS), cos, sin)
    v = to_heads(p_lat[6], AT_KV_HEADS)
    k_ctx = to_heads(p_ctx[5], AT_KV_HEADS)
    v_ctx = to_heads(p_ctx[6], AT_KV_HEADS)
    y_at = windowed_gqa_with_context(q, k, v, k_ctx, v_ctx, attn_sink[l])

    y = jnp.concatenate([y_ml, y_at], axis=-1) @ w_out[l]
    x = x + g1 * rms_norm(y, nw[1])
    x = x + g2 * rms_norm(moe(modulate(rms_norm(x, nw[2]), sh2, sc2)), nw[3])

    if update_context:
        y_ml_c = mlstm_output(h_ctx, p_ctx[2], ml_norm_w[l])
        y_at_c = context_attention(to_heads(p_ctx[4], AT_HEADS), k_ctx, v_ctx, attn_sink[l])
        y_c = jnp.concatenate([y_ml_c, y_at_c], axis=-1) @ w_out[l]
        ctx = ctx + cg1 * rms_norm(y_c, nw[1])
        ctx = ctx + cg2 * rms_norm(moe(modulate(rms_norm(ctx, nw[2]), csh2, csc2)), nw[3])
    return x, ctx


def setup_inputs(seed: int = 0) -> dict:
    key = jax.random.key(seed)
    ks = jax.random.split(key, 24)
    nrm = lambda k, shape, sc: jax.random.normal(k, shape, jnp.float32) * sc
    D = D_MODEL
    i_b = nrm(ks[9], (DEPTH, 2, 1, ML_HEADS), 0.1)
    f_b = jnp.linspace(3.0, 6.0, ML_HEADS)[None, None, None, :] + nrm(ks[10], (DEPTH, 2, 1, ML_HEADS), 0.1)
    return {
        'x': nrm(ks[0], (BATCH, SEQ, D), 1.0),
        'c': nrm(ks[1], (BATCH, D), 1.0),
        'ctx': nrm(ks[2], (BATCH, CTX_LEN, D), 1.0),
        'c_ctx': nrm(ks[3], (D,), 1.0),
        'w_ada': nrm(ks[4], (DEPTH, D, 6 * D), 0.5 * D ** -0.5),
        'b_ada': nrm(ks[5], (DEPTH, 6 * D), 0.02),
        'norms': 1.0 + nrm(ks[6], (DEPTH, 4, D), 0.05),
        'w_in': nrm(ks[7], (DEPTH, D, IN_WIDTH), D ** -0.5),
        'ml_conv': nrm(ks[8], (DEPTH, CONV_W, 2 * ML_QK), CONV_W ** -0.5),
        'ml_gate_b': jnp.concatenate([i_b, f_b], axis=2).reshape(DEPTH, ML_GATES),
        'ml_norm_w': 1.0 + nrm(ks[11], (DEPTH, ML_V), 0.05),
        'attn_sink': nrm(ks[12], (DEPTH, AT_HEADS), 0.5),
        'w_out': nrm(ks[13], (DEPTH, MIX_WIDTH, D), MIX_WIDTH ** -0.5),
        'w_router': nrm(ks[14], (DEPTH, D, N_EXPERTS), D ** -0.5),
        'b_router': nrm(ks[15], (DEPTH, N_EXPERTS), 0.01),
        'w_exp_gate': nrm(ks[16], (DEPTH, N_EXPERTS, D, D_EXPERT), D ** -0.5),
        'w_exp_up': nrm(ks[17], (DEPTH, N_EXPERTS, D, D_EXPERT), D ** -0.5),
        'w_exp_down': nrm(ks[18], (DEPTH, N_EXPERTS, D_EXPERT, D), D_EXPERT ** -0.5),
        'w_sh_gate': nrm(ks[19], (DEPTH, D, D_SHARED), D ** -0.5),
        'w_sh_up': nrm(ks[20], (DEPTH, D, D_SHARED), D ** -0.5),
        'w_sh_down': nrm(ks[21], (DEPTH, D_SHARED, D), D_SHARED ** -0.5),
    }


def reference(x, c, ctx, c_ctx, w_ada, b_ada, norms, w_in, ml_conv, ml_gate_b, ml_norm_w, attn_sink,
              w_out, w_router, b_router, w_exp_gate, w_exp_up, w_exp_down, w_sh_gate, w_sh_up, w_sh_down):
    for l in range(DEPTH):
        x, ctx = hybrid_layer(l, x, ctx, c, c_ctx, w_ada, b_ada, norms, w_in, ml_conv, ml_gate_b, ml_norm_w,
                              attn_sink, w_out, w_router, b_router, w_exp_gate, w_exp_up, w_exp_down,
                              w_sh_gate, w_sh_up, w_sh_down, update_context=l < DEPTH - 1)
    return x
```

```python
import functools

import jax
import jax.numpy as jnp
from jax import lax
from jax.experimental import pallas as pl
from jax.experimental.pallas import tpu as pltpu

F32 = jnp.float32
BF16 = jnp.bfloat16
I32 = jnp.int32

LANES = 128
SUBLANES = 8
VMEM_LIMIT_BYTES = 56 * 1024 * 1024

NORM_EPS = 1e-6
ML_HEADS = 4
CONV_W = 5
AT_HEADS = 8
AT_KV_HEADS = 2
GRID_W = 64
WINDOW = 128
BLOCK_Q = 128
ROPE_THETA = 10000.0
N_GROUPS = 8
TOPK_GROUPS = 4
TOP_K = 8
ROUTE_SCALE = 2.5
MOE_BLOCK = 128
NEG_BIG = -1e30


def _cparams(sem):
    return pltpu.CompilerParams(dimension_semantics=sem, vmem_limit_bytes=VMEM_LIMIT_BYTES)


def _rms(xf, w):
    return xf * lax.rsqrt(jnp.mean(xf * xf, axis=-1, keepdims=True) + NORM_EPS) * w


def _silu(x):
    return x * jax.nn.sigmoid(x)


def _mm(a, b):
    return jnp.dot(a, b, preferred_element_type=F32)


def _ada_kernel(c_ref, w_ref, b_ref, o_ref):
    s = _silu(c_ref[...]).astype(BF16)
    o_ref[...] = _mm(s, w_ref[...].astype(BF16)) + b_ref[...]


def _ada(cond, w, b):
    r, d = cond.shape
    n = w.shape[1]
    tn = min(n, 1024)
    return pl.pallas_call(
        _ada_kernel,
        grid=(n // tn,),
        in_specs=[pl.BlockSpec((r, d), lambda j: (0, 0)),
                  pl.BlockSpec((d, tn), lambda j: (0, j)),
                  pl.BlockSpec((1, tn), lambda j: (0, j))],
        out_specs=pl.BlockSpec((r, tn), lambda j: (0, j)),
        out_shape=jax.ShapeDtypeStruct((r, n), F32),
        compiler_params=_cparams(("arbitrary",)),
        name="ada",
    )(cond, w, b)


def _inproj_kernel(x_ref, sh_ref, sc_ref, nw_ref, w_ref, rc_ref, ra_ref, rb_ref,
                   ml_ref, aq_ref, ak_ref, av_ref, g_ref, hb_ref, *, layout, dh, rope, qscale, cw):
    h = _rms(x_ref[0], nw_ref[...]) * (1.0 + sc_ref[0]) + sh_ref[0]
    hb_ref[...] = h.astype(BF16)
    outs = dict(ml=ml_ref, aq=aq_ref, ak=ak_ref, av=av_ref, g=g_ref)
    for name, col0, width in layout:
        o_ref = outs[name]
        for c in range(0, width, cw):
            step = min(cw, width - c)
            acc = _mm(hb_ref[...], w_ref[:, col0 + c:col0 + c + step])
            if rope and name in ("aq", "ak"):
                for hh in range(step // dh):
                    a = acc[:, hh * dh:(hh + 1) * dh]
                    r = (a * rc_ref[...] + pltpu.roll(a, dh - dh // 4, 1) * ra_ref[...]
                         + pltpu.roll(a, dh // 4, 1) * rb_ref[...])
                    if name == "aq":
                        r = r * qscale
                    o_ref[0, :, c + hh * dh:c + (hh + 1) * dh] = r.astype(o_ref.dtype)
            else:
                if name == "aq":
                    acc = acc * qscale
                o_ref[0, :, c:c + step] = acc.astype(o_ref.dtype)


def _inproj(x, sh, sc, nw, wp, tabs, *, layout, widths, dh, rope, tm):
    b, s, d = x.shape
    npad = wp.shape[1]
    rc, ra, rb = tabs
    kern = functools.partial(_inproj_kernel, layout=layout, dh=dh, rope=rope, qscale=float(dh) ** -0.5, cw=512)
    vec = pl.BlockSpec((1, 1, d), lambda bi, i: (bi, 0, 0))
    tab = pl.BlockSpec((tm, dh), lambda bi, i: (i, 0))
    out_dt = dict(ml=BF16, aq=BF16, ak=BF16, av=BF16, g=F32)
    names = ("ml", "aq", "ak", "av", "g")
    return pl.pallas_call(
        kern,
        grid=(b, s // tm),
        in_specs=[pl.BlockSpec((1, tm, d), lambda bi, i: (bi, i, 0)), vec, vec,
                  pl.BlockSpec((1, d), lambda bi, i: (0, 0)),
                  pl.BlockSpec((d, npad), lambda bi, i: (0, 0), pipeline_mode=pl.Buffered(1)),
                  tab, tab, tab],
        out_specs=[pl.BlockSpec((1, tm, widths[n]), lambda bi, i: (bi, i, 0)) for n in names],
        out_shape=[jax.ShapeDtypeStruct((b, s, widths[n]), out_dt[n]) for n in names],
        scratch_shapes=[pltpu.VMEM((tm, d), BF16)],
        compiler_params=_cparams(("parallel", "arbitrary")),
        name="inproj_rope" if rope else "inproj_ctx",
    )(x, sh, sc, nw, wp, rc, ra, rb)


def _rope_tables(s, dh):
    rows = s // GRID_W
    row = jnp.repeat(jnp.arange(rows), GRID_W)
    col = jnp.tile(jnp.arange(GRID_W), rows)
    nf = dh // 4
    freqs = ROPE_THETA ** (-jnp.arange(nf, dtype=F32) / nf)
    pos = jnp.stack([row, col], axis=-1).astype(F32)
    ang = pos[:, :, None] * freqs
    cos, sin = jnp.cos(ang), jnp.sin(ang)
    z = jnp.zeros_like(sin[:, 0])
    rc = jnp.concatenate([cos[:, 0], cos[:, 0], cos[:, 1], cos[:, 1]], axis=-1)
    ra = jnp.concatenate([-sin[:, 0], z, -sin[:, 1], z], axis=-1)
    rb = jnp.concatenate([z, sin[:, 0], z, sin[:, 1]], axis=-1)
    return rc, ra, rb


def _gate_prep_kernel(g_ref, b_ref, gl_ref, pp_ref, glt_ref, ppt_ref, tot_ref, *, L, H):
    z = g_ref[0] + b_ref[...]
    lane = lax.broadcasted_iota(I32, z.shape, 1)
    row = lax.broadcasted_iota(I32, z.shape, 0)
    is_f = ((lane // H) % 2 == 1) & (lane < 4 * H)
    log_sig = jnp.minimum(z, 0.0) - jnp.log1p(jnp.exp(-jnp.abs(z)))
    gl = jnp.where(lane < 4 * H, jnp.where(is_f, log_sig, z), 0.0)
    cs = gl
    k = 1
    while k < L:
        cs = cs + jnp.where(row >= k, pltpu.roll(cs, k, 0), 0.0)
        k *= 2
    tot = cs[L - 1:L, :]
    suf = tot - cs + gl
    pp = jnp.where(lane >= 2 * H, suf, cs)
    gl_ref[0, 0] = gl
    pp_ref[0, 0] = pp
    glt_ref[0, 0] = gl.T
    ppt_ref[0, 0] = pp.T
    tot_ref[0, 0] = jnp.broadcast_to(tot, (SUBLANES, LANES))


def _gate_prep(g, gate_b, L):
    b, s, _ = g.shape
    nc = s // L
    kern = functools.partial(_gate_prep_kernel, L=L, H=ML_HEADS)
    col = pl.BlockSpec((1, 1, L, LANES), lambda bi, c: (bi, c, 0, 0))
    rowb = pl.BlockSpec((1, 1, LANES, L), lambda bi, c: (bi, c, 0, 0))
    return pl.pallas_call(
        kern,
        grid=(b, nc),
        in_specs=[pl.BlockSpec((1, L, LANES), lambda bi, c: (bi, c, 0)),
                  pl.BlockSpec((1, LANES), lambda bi, c: (0, 0))],
        out_specs=[col, col, rowb, rowb, pl.BlockSpec((1, 1, SUBLANES, LANES), lambda bi, c: (bi, c, 0, 0))],
        out_shape=[jax.ShapeDtypeStruct((b, nc, L, LANES), F32), jax.ShapeDtypeStruct((b, nc, L, LANES), F32),
                   jax.ShapeDtypeStruct((b, nc, LANES, L), F32), jax.ShapeDtypeStruct((b, nc, LANES, L), F32),
                   jax.ShapeDtypeStruct((b, nc, SUBLANES, LANES), F32)],
        compiler_params=_cparams(("parallel", "arbitrary")),
        name="gate_prep",
    )(g, gate_b)


CONV_PAD = 8


def _lane_pick(tile, ch):
    lane = lax.broadcasted_iota(I32, tile.shape, 1)
    return jnp.sum(jnp.where(lane == ch, tile, 0.0), axis=1, keepdims=True)


def _sublane_pick(tile, ch):
    sub = lax.broadcasted_iota(I32, tile.shape, 0)
    return jnp.sum(jnp.where(sub == ch, tile, 0.0), axis=0, keepdims=True)


def _mlstm_chunk(d, ch_i, ch_f, q, k, v, gl, pp, glt, ppt, tot8, ct_ref, n_ref, m_ref):
    L = k.shape[0]
    p_col = _lane_pick(pp, ch_f)
    li_col = _lane_pick(gl, ch_i)
    btot = _lane_pick(tot8[0:1], ch_f)
    m_old = m_ref[d][0:1, 0:1]
    ct = ct_ref[d]
    n_row = n_ref[d]
    h = None
    if q is not None:
        p_row = _sublane_pick(ppt, ch_f)
        li_row = _sublane_pick(glt, ch_i)
        ti = lax.broadcasted_iota(I32, (L, L), 0)
        si = lax.broadcasted_iota(I32, (L, L), 1)
        valid = (si <= ti) if d == 0 else (si >= ti)
        logw = jnp.where(valid, p_col - p_row + li_row, NEG_BIG)
        log_inter = p_col + m_old
        m_q = jnp.maximum(log_inter, jnp.max(logw, axis=1, keepdims=True))
        sqk = lax.dot_general(q, k, (((1,), (1,)), ((), ())), preferred_element_type=F32) * jnp.exp(logw - m_q)
        w_inter = jnp.exp(log_inter - m_q)
        num = _mm(sqk.astype(BF16), v) + w_inter * _mm(q, ct.astype(BF16))
        den = (jnp.sum(sqk, axis=1, keepdims=True)
               + w_inter * jnp.sum(q.astype(F32) * n_row, axis=1, keepdims=True))
        h = num / jnp.maximum(jnp.abs(den), jnp.exp(-m_q))
    log_w_end = btot - p_col + li_col
    m_new = jnp.maximum(btot + m_old, jnp.max(log_w_end, axis=0, keepdims=True))
    w_end = jnp.exp(log_w_end - m_new)
    decay = jnp.exp(btot + m_old - m_new)
    kw = k.astype(F32) * w_end
    ct_ref[d] = decay * ct + lax.dot_general(kw.astype(BF16), v, (((0,), (0,)), ((), ())),
                                             preferred_element_type=F32)
    n_ref[d] = decay * n_row + jnp.sum(kw, axis=0, keepdims=True)
    m_ref[d] = jnp.broadcast_to(m_new, (SUBLANES, LANES))
    return h


def _mlstm_kernel(q_ref, k_ref, v_ref, o_ref, kc_ref, vc_ref,
                  gl_ref, pp_ref, glt_ref, ppt_ref, tot_ref, glc_ref, ppc_ref, totc_ref,
                  cwq_ref, cwk_ref, nw_ref, y_ref,
                  xq_s, xk_s, xkc_s, qs_s, ks_s, kcs_s, hf_s, hb_s, ct_s, n_s, m_s,
                  *, L, H, S, Lc, dk, dv):
    hh = pl.program_id(1)
    nc, ncc = S // L, Lc // L
    zpad = jnp.zeros((CONV_PAD, dk), F32)

    def conv_silu(x_s, cw_ref, n_rows, out_s, scale):
        for c in range(n_rows // L):
            acc = jnp.zeros((L, dk), F32)
            for j in range(CONV_W):
                acc = acc + cw_ref[j:j + 1, :] * x_s[pl.ds(CONV_PAD + c * L + j - CONV_W // 2, L), :]
            out_s[c] = (_silu(acc) * scale).astype(BF16)

    def stage(x_s, src, n_rows):
        x_s[0:CONV_PAD, :] = zpad
        x_s[CONV_PAD + n_rows:2 * CONV_PAD + n_rows, :] = zpad
        x_s[CONV_PAD:CONV_PAD + n_rows, :] = src.astype(F32)

    stage(xq_s, q_ref[0], S)
    stage(xk_s, k_ref[0], S)
    stage(xkc_s, kc_ref[0], Lc)
    conv_silu(xq_s, cwq_ref, S, qs_s, float(dk) ** -0.5)
    conv_silu(xk_s, cwk_ref, S, ks_s, 1.0)
    conv_silu(xkc_s, cwk_ref, Lc, kcs_s, 1.0)

    ct_s[...] = jnp.zeros(ct_s.shape, F32)
    n_s[...] = jnp.zeros(n_s.shape, F32)
    m_s[...] = jnp.zeros(m_s.shape, F32)

    def chans(d):
        return d * 2 * H + hh, d * 2 * H + H + hh

    for d in (0, 1):
        ch_i, ch_f = chans(d)
        for c in (range(ncc) if d == 0 else range(ncc - 1, -1, -1)):
            _mlstm_chunk(d, ch_i, ch_f, None, kcs_s[c], vc_ref[0, c * L:(c + 1) * L, :],
                         glc_ref[0, c], ppc_ref[0, c], None, None, totc_ref[0, c], ct_s, n_s, m_s)

    def body(i, carry):
        for d in (0, 1):
            ch_i, ch_f = chans(d)
            c = i if d == 0 else nc - 1 - i
            r0 = pl.multiple_of(c * L, L)
            h = _mlstm_chunk(d, ch_i, ch_f, qs_s[c], ks_s[c], v_ref[0, pl.ds(r0, L), :],
                             gl_ref[0, c], pp_ref[0, c], glt_ref[0, c], ppt_ref[0, c], tot_ref[0, c],
                             ct_s, n_s, m_s)
            if d == 0:
                hf_s[c] = h
            else:
                hb_s[c] = h
        return carry

    lax.fori_loop(0, nc, body, 0)

    for c in range(nc):
        hs = hf_s[c] + hb_s[c]
        hn = _rms(hs, nw_ref[...])
        y_ref[0, c * L:(c + 1) * L, :] = (hn * jax.nn.sigmoid(o_ref[0, c * L:(c + 1) * L, :].astype(F32))).astype(BF16)


def _mlstm(ml, mlc, gp, gpc, conv_w, norm_w, L):
    b, s, _ = ml.shape
    lc = mlc.shape[1]
    H = ML_HEADS
    dv = norm_w.shape[1] // H
    dk = dv // 2
    nc, ncc = s // L, lc // L
    gl, pp, glt, ppt, tot = gp
    glc, ppc, _, _, totc = gpc
    kern = functools.partial(_mlstm_kernel, L=L, H=H, S=s, Lc=lc, dk=dk, dv=dv)

    def colspec(rows, w, off):
        return pl.BlockSpec((1, rows, w), lambda bi, h: (bi, 0, off + h))

    def gspec(n, r, c):
        return pl.BlockSpec((1, n, r, c), lambda bi, h: (bi, 0, 0, 0))

    return pl.pallas_call(
        kern,
        grid=(b, H),
        in_specs=[colspec(s, dk, 0), colspec(s, dk, H), colspec(s, dv, H), colspec(s, dv, 2 * H),
                  colspec(lc, dk, H), colspec(lc, dv, H),
                  gspec(nc, L, LANES), gspec(nc, L, LANES), gspec(nc, LANES, L), gspec(nc, LANES, L),
                  gspec(nc, SUBLANES, LANES),
                  gspec(ncc, L, LANES), gspec(ncc, L, LANES), gspec(ncc, SUBLANES, LANES),
                  pl.BlockSpec((CONV_W, dk), lambda bi, h: (0, h)),
                  pl.BlockSpec((CONV_W, dk), lambda bi, h: (0, H + h)),
                  pl.BlockSpec((1, dv), lambda bi, h: (0, h))],
        out_specs=pl.BlockSpec((1, s, dv), lambda bi, h: (bi, 0, h)),
        out_shape=jax.ShapeDtypeStruct((b, s, H * dv), BF16),
        scratch_shapes=[pltpu.VMEM((s + 2 * CONV_PAD, dk), F32), pltpu.VMEM((s + 2 * CONV_PAD, dk), F32),
                        pltpu.VMEM((lc + 2 * CONV_PAD, dk), F32),
                        pltpu.VMEM((nc, L, dk), BF16), pltpu.VMEM((nc, L, dk), BF16), pltpu.VMEM((ncc, L, dk), BF16),
                        pltpu.VMEM((nc, L, dv), F32), pltpu.VMEM((nc, L, dv), F32),
                        pltpu.VMEM((2, dk, dv), F32), pltpu.VMEM((2, 1, dk), F32),
                        pltpu.VMEM((2, SUBLANES, LANES), F32)],
        compiler_params=_cparams(("parallel", "arbitrary")),
        name="mlstm",
    )(ml, ml, ml, ml, mlc, mlc, gl, pp, glt, ppt, tot, glc, ppc, totc, conv_w, conv_w, norm_w)


def _attn_kernel(sink_ref, q_ref, kp_ref, kc_ref, kn_ref, vp_ref, vc_ref, vn_ref, kx_ref, vx_ref, bias_ref,
                 o_ref, *, bq, dh, G):
    q = q_ref[0]
    for g in range(AT_KV_HEADS):
        sl = slice(g * dh, (g + 1) * dh)
        qg = jnp.concatenate([q[:, (g * G + j) * dh:(g * G + j + 1) * dh] for j in range(G)], axis=0)
        kcat = jnp.concatenate([kp_ref[0][:, sl], kc_ref[0][:, sl], kn_ref[0][:, sl], kx_ref[0][:, sl]], axis=0)
        vcat = jnp.concatenate([vp_ref[0][:, sl], vc_ref[0][:, sl], vn_ref[0][:, sl], vx_ref[0][:, sl]], axis=0)
        s = lax.dot_general(qg, kcat, (((1,), (1,)), ((), ())), preferred_element_type=F32) + bias_ref[0]
        ri = lax.broadcasted_iota(I32, (G * bq, 1), 0)
        sk = jnp.zeros((G * bq, 1), F32)
        for j in range(G):
            sk = jnp.where((ri >= j * bq) & (ri < (j + 1) * bq), sink_ref[g * G + j], sk)
        m = jnp.maximum(jnp.max(s, axis=1, keepdims=True), sk)
        p = jnp.exp(s - m)
        den = jnp.sum(p, axis=1, keepdims=True) + jnp.exp(sk - m)
        o = _mm(p.astype(BF16), vcat) / den
        for j in range(G):
            o_ref[0, :, (g * G + j) * dh:(g * G + j + 1) * dh] = o[j * bq:(j + 1) * bq, :].astype(BF16)


def _attn_bias(bq, lc, G):
    qoff = (jnp.arange(G * bq) % bq)[:, None]
    ci = jnp.arange(3 * bq + lc)[None, :]
    prev = (ci < bq) & (qoff <= ci)
    cur = (ci >= bq) & (ci < 2 * bq)
    nxt = (ci >= 2 * bq) & (ci < 3 * bq) & (ci - 2 * bq <= qoff)
    ctx = ci >= 3 * bq
    inner = prev | cur | nxt | ctx
    first = cur | nxt | ctx
    last = prev | cur | ctx
    if WINDOW != bq:
        raise NotImplementedError("window must equal the query block")
    return jnp.where(jnp.stack([inner, first, last]), 0.0, NEG_BIG).astype(F32)


def _attn(aq, ak, av, akc, avc, sink):
    b, s, hd = aq.shape
    dh = hd // AT_HEADS
    G = AT_HEADS // AT_KV_HEADS
    bq = BLOCK_Q
    nb = s // bq
    lc = akc.shape[1]
    kvw = AT_KV_HEADS * dh
    bias = _attn_bias(bq, lc, G)
    if nb < 2:
        raise NotImplementedError("needs at least two query blocks")
    kern = functools.partial(_attn_kernel, bq=bq, dh=dh, G=G)
    prev = pl.BlockSpec((1, bq, kvw), lambda bi, n: (bi, jnp.maximum(n - 1, 0), 0))
    cur = pl.BlockSpec((1, bq, kvw), lambda bi, n: (bi, n, 0))
    nxt = pl.BlockSpec((1, bq, kvw), lambda bi, n: (bi, jnp.minimum(n + 1, nb - 1), 0))
    cx = pl.BlockSpec((1, lc, kvw), lambda bi, n: (bi, 0, 0))
    return pl.pallas_call(
        kern,
        grid=(b, nb),
        in_specs=[pl.BlockSpec(memory_space=pltpu.SMEM),
                  pl.BlockSpec((1, bq, hd), lambda bi, n: (bi, n, 0)),
                  prev, cur, nxt, prev, cur, nxt, cx, cx,
                  pl.BlockSpec((1, G * bq, 3 * bq + lc),
                               lambda bi, n: (jnp.where(n == 0, 1, jnp.where(n == nb - 1, 2, 0)), 0, 0))],
        out_specs=pl.BlockSpec((1, bq, hd), lambda bi, n: (bi, n, 0)),
        out_shape=jax.ShapeDtypeStruct((b, s, hd), BF16),
        compiler_params=_cparams(("parallel", "arbitrary")),
        name="attn",
    )(sink, aq, ak, ak, ak, av, av, av, akc, avc, bias)


def _outproj_kernel(ym_ref, ya_ref, x_ref, g1_ref, sh2_ref, sc2_ref, nw1_ref, nw2_ref, wo_ref, wr_ref,
                    x1_ref, h2_ref, lg_ref, y_s, *, dm, cw):
    d = y_s.shape[1]
    for c in range(0, d, cw):
        y_s[:, c:c + cw] = _mm(ym_ref[0], wo_ref[0:dm, c:c + cw]) + _mm(ya_ref[0], wo_ref[dm:, c:c + cw])
    x1 = x_ref[0] + g1_ref[0] * _rms(y_s[...], nw1_ref[...])
    x1_ref[0] = x1
    h2 = _rms(x1, nw2_ref[...]) * (1.0 + sc2_ref[0]) + sh2_ref[0]
    h2_ref[0] = h2.astype(BF16)
    lg_ref[0] = jnp.dot(h2, wr_ref[...], preferred_element_type=F32, precision=lax.Precision.HIGHEST)


def _outproj(y_ml, y_at, x, g1, sh2, sc2, nw1, nw2, wo, wr, tm):
    b, s, d = x.shape
    dm = y_ml.shape[2]
    e = wr.shape[1]
    kern = functools.partial(_outproj_kernel, dm=dm, cw=min(d, 512))
    vec = pl.BlockSpec((1, 1, d), lambda bi, i: (bi, 0, 0))
    nspec = pl.BlockSpec((1, d), lambda bi, i: (0, 0))
    return pl.pallas_call(
        kern,
        grid=(b, s // tm),
        in_specs=[pl.BlockSpec((1, tm, dm), lambda bi, i: (bi, i, 0)),
                  pl.BlockSpec((1, tm, d - dm), lambda bi, i: (bi, i, 0)),
                  pl.BlockSpec((1, tm, d), lambda bi, i: (bi, i, 0)),
                  vec, vec, vec, nspec, nspec,
                  pl.BlockSpec((d, d), lambda bi, i: (0, 0), pipeline_mode=pl.Buffered(1)),
                  pl.BlockSpec((d, e), lambda bi, i: (0, 0), pipeline_mode=pl.Buffered(1))],
        out_specs=[pl.BlockSpec((1, tm, d), lambda bi, i: (bi, i, 0)),
                   pl.BlockSpec((1, tm, d), lambda bi, i: (bi, i, 0)),
                   pl.BlockSpec((1, tm, e), lambda bi, i: (bi, i, 0))],
        out_shape=[jax.ShapeDtypeStruct((b, s, d), F32), jax.ShapeDtypeStruct((b, s, d), BF16),
                   jax.ShapeDtypeStruct((b, s, e), F32)],
        scratch_shapes=[pltpu.VMEM((tm, d), F32)],
        compiler_params=_cparams(("parallel", "arbitrary")),
        name="outproj",
    )(y_ml, y_at, x, g1, sh2, sc2, nw1, nw2, wo, wr)


def _route_kernel(lg_ref, br_ref, eidx_ref, ew_ref, rank_ref, cnt_ref, carry_s, *, E, tr):
    i = pl.program_id(0)

    @pl.when(i == 0)
    def _():
        carry_s[...] = jnp.zeros(carry_s.shape, F32)

    scores = jax.nn.sigmoid(lg_ref[...].T)
    biased = scores + br_ref[:, 0:1]
    row = lax.broadcasted_iota(I32, (E, tr), 0).astype(F32)
    gs = E // N_GROUPS
    ninf = -jnp.inf
    grp = []
    for g in range(N_GROUPS):
        xg = biased[g * gs:(g + 1) * gs]
        rg = (lax.broadcasted_iota(I32, (gs, tr), 0) + g * gs).astype(F32)
        m1 = jnp.max(xg, axis=0, keepdims=True)
        i1 = jnp.min(jnp.where(xg == m1, rg, float(E)), axis=0, keepdims=True)
        m2 = jnp.max(jnp.where(rg == i1, ninf, xg), axis=0, keepdims=True)
        grp.append(m1 + m2)
    gsc = jnp.concatenate(grp, axis=0)
    gi = lax.broadcasted_iota(I32, (N_GROUPS, tr), 0)
    beaten = jnp.zeros((N_GROUPS, tr), I32)
    for g2 in range(N_GROUPS):
        sg = gsc[g2:g2 + 1]
        beaten = beaten + jnp.where((sg > gsc) | ((sg == gsc) & (gi > g2)), 1, 0)
    keep_g = jnp.where(beaten < TOPK_GROUPS, 1.0, 0.0)
    keep = jnp.concatenate([jnp.broadcast_to(keep_g[g:g + 1], (gs, tr)) for g in range(N_GROUPS)], axis=0)
    masked = jnp.where(keep > 0.5, biased, ninf)
    idxs, ws = [], []
    for _ in range(TOP_K):
        m = jnp.max(masked, axis=0, keepdims=True)
        ik = jnp.min(jnp.where(masked == m, row, float(E)), axis=0, keepdims=True)
        sel = row == ik
        ws.append(jnp.sum(jnp.where(sel, scores, 0.0), axis=0, keepdims=True))
        idxs.append(ik)
        masked = jnp.where(sel, ninf, masked)
    w = jnp.concatenate(ws, axis=0)
    ew_ref[...] = w / jnp.sum(w, axis=0, keepdims=True) * ROUTE_SCALE
    eidx_ref[...] = jnp.concatenate(idxs, axis=0).astype(I32)

    assign = jnp.zeros((E, tr), F32)
    for ik in idxs:
        assign = assign + jnp.where(row == ik, 1.0, 0.0)
    upper = jnp.where(lax.broadcasted_iota(I32, (tr, tr), 0) < lax.broadcasted_iota(I32, (tr, tr), 1), 1.0, 0.0)
    base = _mm(assign.astype(BF16), upper.astype(BF16)) + carry_s[:, 0:1]
    ranks = [jnp.sum(jnp.where(row == ik, base, 0.0), axis=0, keepdims=True) for ik in idxs]
    rank_ref[...] = jnp.concatenate(ranks, axis=0).astype(I32)
    carry_s[...] = carry_s[...] + jnp.sum(assign, axis=1, keepdims=True)
    cnt_ref[...] = carry_s[...]


def _route(logits, b_router, tr):
    t, e = logits.shape
    kern = functools.partial(_route_kernel, E=e, tr=tr)
    kt = pl.BlockSpec((TOP_K, tr), lambda i: (0, i))
    return pl.pallas_call(
        kern,
        grid=(t // tr,),
        in_specs=[pl.BlockSpec((tr, e), lambda i: (i, 0)), pl.BlockSpec((e, LANES), lambda i: (0, 0))],
        out_specs=[kt, kt, kt, pl.BlockSpec((e, LANES), lambda i: (0, 0))],
        out_shape=[jax.ShapeDtypeStruct((TOP_K, t), I32), jax.ShapeDtypeStruct((TOP_K, t), F32),
                   jax.ShapeDtypeStruct((TOP_K, t), I32), jax.ShapeDtypeStruct((e, LANES), F32)],
        scratch_shapes=[pltpu.VMEM((e, LANES), F32)],
        compiler_params=_cparams(("arbitrary",)),
        name="route",
    )(logits, b_router)


def _dispatch_kernel(zstart_ref, zvalid_ref, pos_hbm, h2_ref, xs_hbm, rows_s, zero_s, pos_s, sem_p, sem_z, sem_r,
                     *, E, tt):
    i = pl.program_id(0)

    def zero_copy(e):
        start = pl.multiple_of(zstart_ref[e], MOE_BLOCK)
        return pltpu.make_async_copy(zero_s, xs_hbm.at[pl.ds(start, MOE_BLOCK)], sem_z)

    @pl.when(i == 0)
    def _():
        zero_s[...] = jnp.zeros(zero_s.shape, F32)

        def start(e, c):
            @pl.when(zvalid_ref[e] > 0)
            def _():
                zero_copy(e).start()
            return c

        def wait(e, c):
            @pl.when(zvalid_ref[e] > 0)
            def _():
                zero_copy(e).wait()
            return c

        lax.fori_loop(0, E, start, 0)
        lax.fori_loop(0, E, wait, 0)

    pos_copy = pltpu.make_async_copy(pos_hbm.at[i], pos_s, sem_p)
    pos_copy.start()
    rows_s[...] = h2_ref[...].astype(F32)
    pos_copy.wait()

    def row_copy(t, k):
        return pltpu.make_async_copy(rows_s.at[pl.ds(t, 1)], xs_hbm.at[pl.ds(pos_s[k, t], 1)], sem_r)

    def start_rows(t, c):
        for k in range(TOP_K):
            row_copy(t, k).start()
        return c

    def wait_rows(t, c):
        for k in range(TOP_K):
            row_copy(t, k).wait()
        return c

    lax.fori_loop(0, tt, start_rows, 0)
    lax.fori_loop(0, tt, wait_rows, 0)


def _dispatch(h2, pos_tiles, zstart, zvalid, n_rows, tt):
    t, d = h2.shape
    e = zstart.shape[0]
    kern = functools.partial(_dispatch_kernel, E=e, tt=tt)
    return pl.pallas_call(
        kern,
        grid_spec=pltpu.PrefetchScalarGridSpec(
            num_scalar_prefetch=2,
            grid=(t // tt,),
            in_specs=[pl.BlockSpec(memory_space=pl.ANY), pl.BlockSpec((tt, d), lambda i, zs, zv: (i, 0))],
            out_specs=pl.BlockSpec(memory_space=pl.ANY),
            scratch_shapes=[pltpu.VMEM((tt, d), F32), pltpu.VMEM((MOE_BLOCK, d), F32),
                            pltpu.SMEM((TOP_K, tt), I32),
                            pltpu.SemaphoreType.DMA, pltpu.SemaphoreType.DMA, pltpu.SemaphoreType.DMA]),
        out_shape=jax.ShapeDtypeStruct((n_rows, d), F32),
        compiler_params=pltpu.CompilerParams(dimension_semantics=("arbitrary",), vmem_limit_bytes=VMEM_LIMIT_BYTES,
                                             has_side_effects=True),
        name="dispatch",
    )(zstart, zvalid, pos_tiles, h2)


def _experts_kernel(be_ref, nu_ref, xs_ref, wg_ref, wu_ref, wd_ref, ys_ref, wgb_s, wub_s, wdb_s):
    b = pl.program_id(0)
    n_used = nu_ref[0]

    @pl.when(b < n_used)
    def _():
        prev = be_ref[jnp.maximum(b - 1, 0)]

        @pl.when((b == 0) | (be_ref[b] != prev))
        def _():
            wgb_s[...] = wg_ref[0].astype(BF16)
            wub_s[...] = wu_ref[0].astype(BF16)
            wdb_s[...] = wd_ref[0].astype(BF16)

        xb = xs_ref[...].astype(BF16)
        a = _silu(_mm(xb, wgb_s[...])) * _mm(xb, wub_s[...])
        ys_ref[...] = _mm(a.astype(BF16), wdb_s[...])

    @pl.when(b >= n_used)
    def _():
        ys_ref[...] = jnp.zeros(ys_ref.shape, F32)


def _experts(xs, block_e, n_used, wg, wu, wd):
    n_rows, d = xs.shape
    nblk = n_rows // MOE_BLOCK
    de = wg.shape[2]

    def live(b, nu):
        return jnp.minimum(b, jnp.maximum(nu[0] - 1, 0))

    return pl.pallas_call(
        _experts_kernel,
        grid_spec=pltpu.PrefetchScalarGridSpec(
            num_scalar_prefetch=2,
            grid=(nblk,),
            in_specs=[pl.BlockSpec((MOE_BLOCK, d), lambda b, be, nu: (live(b, nu), 0)),
                      pl.BlockSpec((1, d, de), lambda b, be, nu: (be[live(b, nu)], 0, 0)),
                      pl.BlockSpec((1, d, de), lambda b, be, nu: (be[live(b, nu)], 0, 0)),
                      pl.BlockSpec((1, de, d), lambda b, be, nu: (be[live(b, nu)], 0, 0))],
            out_specs=pl.BlockSpec((MOE_BLOCK, d), lambda b, be, nu: (b, 0)),
            scratch_shapes=[pltpu.VMEM((d, de), BF16), pltpu.VMEM((d, de), BF16), pltpu.VMEM((de, d), BF16)]),
        out_shape=jax.ShapeDtypeStruct((n_rows, d), F32),
        compiler_params=_cparams(("arbitrary",)),
        name="experts",
    )(block_e, n_used, xs, wg, wu, wd)


def _combine_kernel(pos_hbm, ys_hbm, ew_ref, x1_ref, h2_ref, g2_ref, nw3_ref, wsg_ref, wsu_ref, wsd_ref,
                    out_ref, buf_s, pos_s, sem_p, sem_r, *, tc, nt):
    tile = pl.program_id(0) * nt + pl.program_id(1)
    pos_copy = pltpu.make_async_copy(pos_hbm.at[tile], pos_s, sem_p)
    pos_copy.start()
    pos_copy.wait()

    def row_copy(t, k):
        return pltpu.make_async_copy(ys_hbm.at[pl.ds(pos_s[k, t], 1)], buf_s.at[k, pl.ds(t, 1)], sem_r)

    def start_rows(t, c):
        for k in range(TOP_K):
            row_copy(t, k).start()
        return c

    def wait_rows(t, c):
        for k in range(TOP_K):
            row_copy(t, k).wait()
        return c

    lax.fori_loop(0, tc, start_rows, 0)
    hb = h2_ref[0]
    a = _silu(_mm(hb, wsg_ref[...])) * _mm(hb, wsu_ref[...])
    acc = _mm(a.astype(BF16), wsd_ref[...])
    lax.fori_loop(0, tc, wait_rows, 0)
    ew = ew_ref[0]
    for k in range(TOP_K):
        acc = acc + buf_s[k] * ew[:, k:k + 1]
    out_ref[0] = x1_ref[0] + g2_ref[0] * _rms(acc, nw3_ref[...])


def _combine(pos_tiles, ys, ew, x1, h2, g2, nw3, wsg, wsu, wsd, tc):
    b, s, d = x1.shape
    ds_ = wsg.shape[1]
    nt = s // tc
    kern = functools.partial(_combine_kernel, tc=tc, nt=nt)
    tok = lambda w: pl.BlockSpec((1, tc, w), lambda bi, i: (bi, i, 0))
    return pl.pallas_call(
        kern,
        grid=(b, nt),
        in_specs=[pl.BlockSpec(memory_space=pl.ANY), pl.BlockSpec(memory_space=pl.ANY),
                  tok(TOP_K), tok(d), tok(d),
                  pl.BlockSpec((1, 1, d), lambda bi, i: (bi, 0, 0)),
                  pl.BlockSpec((1, d), lambda bi, i: (0, 0)),
                  pl.BlockSpec((d, ds_), lambda bi, i: (0, 0)),
                  pl.BlockSpec((d, ds_), lambda bi, i: (0, 0)),
                  pl.BlockSpec((ds_, d), lambda bi, i: (0, 0))],
        out_specs=tok(d),
        out_shape=jax.ShapeDtypeStruct((b, s, d), F32),
        scratch_shapes=[pltpu.VMEM((TOP_K, tc, d), F32), pltpu.SMEM((TOP_K, tc), I32),
                        pltpu.SemaphoreType.DMA, pltpu.SemaphoreType.DMA],
        compiler_params=_cparams(("arbitrary", "arbitrary")),
        name="combine",
    )(pos_tiles, ys, ew, x1, h2, g2, nw3, wsg, wsu, wsd)


def _tile(n, pref):
    t = min(n, pref)
    if n % t:
        raise NotImplementedError(f"size {n} is not a multiple of tile {t}")
    return t


def kernel(x, c, ctx, c_ctx, w_ada, b_ada, norms, w_in, ml_conv, ml_gate_b, ml_norm_w, attn_sink, w_out,
           w_router, b_router, w_exp_gate, w_exp_up, w_exp_down, w_sh_gate, w_sh_up, w_sh_down):
    if w_ada.shape[0] != 1:
        raise NotImplementedError("single-layer configuration only")
    b, s, d = x.shape
    lc = ctx.shape[1]
    t = b * s
    H = ML_HEADS
    dv = d // 2 // H
    dk = dv // 2
    dh = d // 2 // AT_HEADS
    qk_w, v_w = 2 * H * dk, H * dv
    ng = 4 * H
    aq_w, akv_w = AT_HEADS * dh, AT_KV_HEADS * dh
    e = w_router.shape[-1]
    nw = norms[0]

    rows = -(-(b + 1) // SUBLANES) * SUBLANES
    cond = jnp.zeros((rows, d), F32).at[:b].set(c).at[b].set(c_ctx)
    mods = _ada(cond, w_ada[0], b_ada[0][None, :])
    sh1, sc1, g1, sh2, sc2, g2 = [m[:b, None, :] for m in jnp.split(mods, 6, axis=-1)]
    csh1, csc1 = [jnp.broadcast_to(m[b:b + 1, None, :], (b, 1, d)) for m in jnp.split(mods, 6, axis=-1)[:2]]

    w0 = w_in[0]
    o_ml, o_g, o_q = 0, qk_w + 2 * v_w, qk_w + 2 * v_w + ng
    ml_w = qk_w + 2 * v_w
    wp = jnp.concatenate([w0[:, o_ml:o_g], w0[:, o_q:], w0[:, o_g:o_q], jnp.zeros((d, LANES - ng), F32)],
                         axis=1).astype(BF16)
    layout = (("ml", 0, ml_w), ("aq", ml_w, aq_w), ("ak", ml_w + aq_w, akv_w),
              ("av", ml_w + aq_w + akv_w, akv_w), ("g", ml_w + aq_w + 2 * akv_w, LANES))
    widths = dict(ml=ml_w, aq=aq_w, ak=akv_w, av=akv_w, g=LANES)
    tabs = _rope_tables(s, dh)
    proj = functools.partial(_inproj, layout=layout, widths=widths, dh=dh)
    ml, aq, ak, av, gates = proj(x, sh1, sc1, nw[0:1], wp, tabs, rope=True, tm=_tile(s, 512))
    tm_c = _tile(lc, 512)
    mlc, _, akc, avc, gates_c = proj(ctx, csh1, csc1, nw[0:1], wp, tuple(tb[:tm_c] for tb in tabs),
                                     rope=False, tm=tm_c)

    L = _tile(lc, 256)
    if s % L:
        raise NotImplementedError("sequence must be a multiple of the mLSTM chunk")
    gate_b = jnp.zeros((1, LANES), F32).at[0, :ng].set(ml_gate_b[0])
    gp = _gate_prep(gates, gate_b, L)
    gpc = _gate_prep(gates_c, gate_b, L)
    y_ml = _mlstm(ml, mlc, gp, gpc, ml_conv[0], ml_norm_w[0][None, :], L)

    y_at = _attn(aq, ak, av, akc, avc, attn_sink[0])

    x1, h2, logits = _outproj(y_ml, y_at, x, g1, sh2, sc2, nw[1:2], nw[2:3], w_out[0].astype(BF16),
                              w_router[0], _tile(s, 256))

    eidx, ew, rank, cnt = _route(logits.reshape(t, e), jnp.broadcast_to(b_router[0][:, None], (e, LANES)),
                                 _tile(t, 512))
    counts = cnt[:, 0].astype(I32)
    padded = (counts + MOE_BLOCK - 1) // MOE_BLOCK * MOE_BLOCK
    pad_end = jnp.cumsum(padded)
    pad_start = pad_end - padded
    n_blocks = -(-(t * TOP_K + e * (MOE_BLOCK - 1)) // MOE_BLOCK)
    n_rows = n_blocks * MOE_BLOCK
    block_e = jnp.minimum(jnp.searchsorted(pad_end, jnp.arange(n_blocks) * MOE_BLOCK, side="right"),
                          e - 1).astype(I32)
    n_used = (pad_end[-1] // MOE_BLOCK).astype(I32)[None]
    pos = pad_start[eidx] + rank

    tt = _tile(t, 256)
    pos_d = pos.reshape(TOP_K, t // tt, tt).transpose(1, 0, 2)
    xs = _dispatch(h2.reshape(t, d), pos_d, (pad_end - MOE_BLOCK).astype(I32), (padded > 0).astype(I32),
                   n_rows, tt)
    ys = _experts(xs, block_e, n_used, w_exp_gate[0], w_exp_up[0], w_exp_down[0])
    tc = _tile(s, 128)
    pos_c = pos.reshape(TOP_K, t // tc, tc).transpose(1, 0, 2)
    ew_t = ew.T.reshape(b, s, TOP_K)
    return _combine(pos_c, ys, ew_t, x1, h2, g2, nw[3:4], w_sh_gate[0].astype(BF16), w_sh_up[0].astype(BF16),
                    w_sh_down[0].astype(BF16), tc)
```

```python
import functools

import jax
import jax.numpy as jnp
from jax import lax
from jax.experimental import pallas as pl
from jax.experimental.pallas import tpu as pltpu

F32 = jnp.float32
BF16 = jnp.bfloat16
I32 = jnp.int32

LANES = 128
SUBLANES = 8
VMEM_LIMIT_BYTES = 56 * 1024 * 1024

NORM_EPS = 1e-6
ML_HEADS = 4
CONV_W = 5
AT_HEADS = 8
AT_KV_HEADS = 2
GRID_W = 64
WINDOW = 128
BLOCK_Q = 128
ROPE_THETA = 10000.0
N_GROUPS = 8
TOPK_GROUPS = 4
TOP_K = 8
ROUTE_SCALE = 2.5
MOE_BLOCK = 128
NEG_BIG = -1e30


def _cparams(sem):
    return pltpu.CompilerParams(dimension_semantics=sem, vmem_limit_bytes=VMEM_LIMIT_BYTES)


def _rms(xf, w):
    return xf * lax.rsqrt(jnp.mean(xf * xf, axis=-1, keepdims=True) + NORM_EPS) * w


def _silu(x):
    return x * jax.nn.sigmoid(x)


def _mm(a, b):
    return jnp.dot(a, b, preferred_element_type=F32)


def _ada_kernel(c_ref, w_ref, b_ref, o_ref):
    s = _silu(c_ref[...]).astype(BF16)
    o_ref[...] = _mm(s, w_ref[...].astype(BF16)) + b_ref[...]


def _ada(cond, w, b):
    r, d = cond.shape
    n = w.shape[1]
    tn = min(n, 1024)
    return pl.pallas_call(
        _ada_kernel,
        grid=(n // tn,),
        in_specs=[pl.BlockSpec((r, d), lambda j: (0, 0)),
                  pl.BlockSpec((d, tn), lambda j: (0, j)),
                  pl.BlockSpec((1, tn), lambda j: (0, j))],
        out_specs=pl.BlockSpec((r, tn), lambda j: (0, j)),
        out_shape=jax.ShapeDtypeStruct((r, n), F32),
        compiler_params=_cparams(("arbitrary",)),
        name="ada",
    )(cond, w, b)


def _inproj_kernel(x_ref, sh_ref, sc_ref, nw_ref, w_ref, rc_ref, ra_ref, rb_ref,
                   ml_ref, aq_ref, ak_ref, av_ref, g_ref, hb_ref, *, layout, dh, rope, qscale, cw):
    h = _rms(x_ref[0], nw_ref[...]) * (1.0 + sc_ref[0]) + sh_ref[0]
    hb_ref[...] = h.astype(BF16)
    outs = dict(ml=ml_ref, aq=aq_ref, ak=ak_ref, av=av_ref, g=g_ref)
    for name, col0, width in layout:
        o_ref = outs[name]
        for c in range(0, width, cw):
            step = min(cw, width - c)
            acc = _mm(hb_ref[...], w_ref[:, col0 + c:col0 + c + step])
            if rope and name in ("aq", "ak"):
                for hh in range(step // dh):
                    a = acc[:, hh * dh:(hh + 1) * dh]
                    r = (a * rc_ref[...] + pltpu.roll(a, dh - dh // 4, 1) * ra_ref[...]
                         + pltpu.roll(a, dh // 4, 1) * rb_ref[...])
                    if name == "aq":
                        r = r * qscale
                    o_ref[0, :, c + hh * dh:c + (hh + 1) * dh] = r.astype(o_ref.dtype)
            else:
                if name == "aq":
                    acc = acc * qscale
                o_ref[0, :, c:c + step] = acc.astype(o_ref.dtype)


def _inproj(x, sh, sc, nw, wp, tabs, *, layout, widths, dh, rope, tm):
    b, s, d = x.shape
    npad = wp.shape[1]
    rc, ra, rb = tabs
    kern = functools.partial(_inproj_kernel, layout=layout, dh=dh, rope=rope, qscale=float(dh) ** -0.5, cw=512)
    vec = pl.BlockSpec((1, 1, d), lambda bi, i: (bi, 0, 0))
    tab = pl.BlockSpec((tm, dh), lambda bi, i: (i, 0))
    out_dt = dict(ml=BF16, aq=BF16, ak=BF16, av=BF16, g=F32)
    names = ("ml", "aq", "ak", "av", "g")
    return pl.pallas_call(
        kern,
        grid=(b, s // tm),
        in_specs=[pl.BlockSpec((1, tm, d), lambda bi, i: (bi, i, 0)), vec, vec,
                  pl.BlockSpec((1, d), lambda bi, i: (0, 0)),
                  pl.BlockSpec((d, npad), lambda bi, i: (0, 0), pipeline_mode=pl.Buffered(1)),
                  tab, tab, tab],
        out_specs=[pl.BlockSpec((1, tm, widths[n]), lambda bi, i: (bi, i, 0)) for n in names],
        out_shape=[jax.ShapeDtypeStruct((b, s, widths[n]), out_dt[n]) for n in names],
        scratch_shapes=[pltpu.VMEM((tm, d), BF16)],
        compiler_params=_cparams(("arbitrary", "arbitrary")),
        name="inproj_rope" if rope else "inproj_ctx",
    )(x, sh, sc, nw, wp, rc, ra, rb)


def _rope_tables(s, dh):
    rows = s // GRID_W
    row = jnp.repeat(jnp.arange(rows), GRID_W)
    col = jnp.tile(jnp.arange(GRID_W), rows)
    nf = dh // 4
    freqs = ROPE_THETA ** (-jnp.arange(nf, dtype=F32) / nf)
    pos = jnp.stack([row, col], axis=-1).astype(F32)
    ang = pos[:, :, None] * freqs
    cos, sin = jnp.cos(ang), jnp.sin(ang)
    z = jnp.zeros_like(sin[:, 0])
    rc = jnp.concatenate([cos[:, 0], cos[:, 0], cos[:, 1], cos[:, 1]], axis=-1)
    ra = jnp.concatenate([-sin[:, 0], z, -sin[:, 1], z], axis=-1)
    rb = jnp.concatenate([z, sin[:, 0], z, sin[:, 1]], axis=-1)
    return rc, ra, rb


def _gate_prep_kernel(g_ref, b_ref, gl_ref, pp_ref, glt_ref, ppt_ref, tot_ref, *, L, H):
    z = g_ref[0] + b_ref[...]
    lane = lax.broadcasted_iota(I32, z.shape, 1)
    row = lax.broadcasted_iota(I32, z.shape, 0)
    is_f = ((lane // H) % 2 == 1) & (lane < 4 * H)
    log_sig = jnp.minimum(z, 0.0) - jnp.log1p(jnp.exp(-jnp.abs(z)))
    gl = jnp.where(lane < 4 * H, jnp.where(is_f, log_sig, z), 0.0)
    cs = gl
    k = 1
    while k < L:
        cs = cs + jnp.where(row >= k, pltpu.roll(cs, k, 0), 0.0)
        k *= 2
    tot = cs[L - 1:L, :]
    suf = tot - cs + gl
    pp = jnp.where(lane >= 2 * H, suf, cs)
    gl_ref[0, 0] = gl
    pp_ref[0, 0] = pp
    glt_ref[0, 0] = gl.T
    ppt_ref[0, 0] = pp.T
    tot_ref[0, 0] = jnp.broadcast_to(tot, (SUBLANES, LANES))


def _gate_prep(g, gate_b, L):
    b, s, _ = g.shape
    nc = s // L
    kern = functools.partial(_gate_prep_kernel, L=L, H=ML_HEADS)
    col = pl.BlockSpec((1, 1, L, LANES), lambda bi, c: (bi, c, 0, 0))
    rowb = pl.BlockSpec((1, 1, LANES, L), lambda bi, c: (bi, c, 0, 0))
    return pl.pallas_call(
        kern,
        grid=(b, nc),
        in_specs=[pl.BlockSpec((1, L, LANES), lambda bi, c: (bi, c, 0)),
                  pl.BlockSpec((1, LANES), lambda bi, c: (0, 0))],
        out_specs=[col, col, rowb, rowb, pl.BlockSpec((1, 1, SUBLANES, LANES), lambda bi, c: (bi, c, 0, 0))],
        out_shape=[jax.ShapeDtypeStruct((b, nc, L, LANES), F32), jax.ShapeDtypeStruct((b, nc, L, LANES), F32),
                   jax.ShapeDtypeStruct((b, nc, LANES, L), F32), jax.ShapeDtypeStruct((b, nc, LANES, L), F32),
                   jax.ShapeDtypeStruct((b, nc, SUBLANES, LANES), F32)],
        compiler_params=_cparams(("arbitrary", "arbitrary")),
        name="gate_prep",
    )(g, gate_b)


CONV_PAD = 8


def _lane_pick(tile, ch):
    lane = lax.broadcasted_iota(I32, tile.shape, 1)
    return jnp.sum(jnp.where(lane == ch, tile, 0.0), axis=1, keepdims=True)


def _sublane_pick(tile, ch):
    sub = lax.broadcasted_iota(I32, tile.shape, 0)
    return jnp.sum(jnp.where(sub == ch, tile, 0.0), axis=0, keepdims=True)


def _mlstm_chunk(d, ch_i, ch_f, q, k, v, gl, pp, glt, ppt, tot8, ct_ref, n_ref, m_ref):
    L = k.shape[0]
    p_col = _lane_pick(pp, ch_f)
    li_col = _lane_pick(gl, ch_i)
    btot = _lane_pick(tot8[0:1], ch_f)
    m_old = m_ref[d][0:1, 0:1]
    ct = ct_ref[d]
    n_row = n_ref[d]
    h = None
    if q is not None:
        p_row = _sublane_pick(ppt, ch_f)
        li_row = _sublane_pick(glt, ch_i)
        ti = lax.broadcasted_iota(I32, (L, L), 0)
        si = lax.broadcasted_iota(I32, (L, L), 1)
        valid = (si <= ti) if d == 0 else (si >= ti)
        logw = jnp.where(valid, p_col - p_row + li_row, NEG_BIG)
        log_inter = p_col + m_old
        m_q = jnp.maximum(log_inter, jnp.max(logw, axis=1, keepdims=True))
        sqk = lax.dot_general(q, k, (((1,), (1,)), ((), ())), preferred_element_type=F32) * jnp.exp(logw - m_q)
        w_inter = jnp.exp(log_inter - m_q)
        num = _mm(sqk.astype(BF16), v) + w_inter * _mm(q, ct.astype(BF16))
        den = (jnp.sum(sqk, axis=1, keepdims=True)
               + w_inter * jnp.sum(q.astype(F32) * n_row, axis=1, keepdims=True))
        h = num / jnp.maximum(jnp.abs(den), jnp.exp(-m_q))
    log_w_end = btot - p_col + li_col
    m_new = jnp.maximum(btot + m_old, jnp.max(log_w_end, axis=0, keepdims=True))
    w_end = jnp.exp(log_w_end - m_new)
    decay = jnp.exp(btot + m_old - m_new)
    kw = k.astype(F32) * w_end
    ct_ref[d] = decay * ct + lax.dot_general(kw.astype(BF16), v, (((0,), (0,)), ((), ())),
                                             preferred_element_type=F32)
    n_ref[d] = decay * n_row + jnp.sum(kw, axis=0, keepdims=True)
    m_ref[d] = jnp.broadcast_to(m_new, (SUBLANES, LANES))
    return h


def _mlstm_kernel(q_ref, k_ref, v_ref, o_ref, kc_ref, vc_ref,
                  gl_ref, pp_ref, glt_ref, ppt_ref, tot_ref, glc_ref, ppc_ref, totc_ref,
                  cwq_ref, cwk_ref, nw_ref, y_ref,
                  xq_s, xk_s, xkc_s, qs_s, ks_s, kcs_s, hf_s, hb_s, ct_s, n_s, m_s,
                  *, L, H, S, Lc, dk, dv):
    hh = pl.program_id(1)
    nc, ncc = S // L, Lc // L
    zpad = jnp.zeros((CONV_PAD, dk), F32)

    def conv_silu(x_s, cw_ref, n_rows, out_s, scale):
        for c in range(n_rows // L):
            acc = jnp.zeros((L, dk), F32)
            for j in range(CONV_W):
                acc = acc + cw_ref[j:j + 1, :] * x_s[pl.ds(CONV_PAD + c * L + j - CONV_W // 2, L), :]
            out_s[c] = (_silu(acc) * scale).astype(BF16)

    def stage(x_s, src, n_rows):
        x_s[0:CONV_PAD, :] = zpad
        x_s[CONV_PAD + n_rows:2 * CONV_PAD + n_rows, :] = zpad
        x_s[CONV_PAD:CONV_PAD + n_rows, :] = src.astype(F32)

    stage(xq_s, q_ref[0], S)
    stage(xk_s, k_ref[0], S)
    stage(xkc_s, kc_ref[0], Lc)
    conv_silu(xq_s, cwq_ref, S, qs_s, float(dk) ** -0.5)
    conv_silu(xk_s, cwk_ref, S, ks_s, 1.0)
    conv_silu(xkc_s, cwk_ref, Lc, kcs_s, 1.0)

    ct_s[...] = jnp.zeros(ct_s.shape, F32)
    n_s[...] = jnp.zeros(n_s.shape, F32)
    m_s[...] = jnp.zeros(m_s.shape, F32)

    def chans(d):
        return d * 2 * H + hh, d * 2 * H + H + hh

    for d in (0, 1):
        ch_i, ch_f = chans(d)
        for c in (range(ncc) if d == 0 else range(ncc - 1, -1, -1)):
            _mlstm_chunk(d, ch_i, ch_f, None, kcs_s[c], vc_ref[0, c * L:(c + 1) * L, :],
                         glc_ref[0, c], ppc_ref[0, c], None, None, totc_ref[0, c], ct_s, n_s, m_s)

    def body(i, carry):
        for d in (0, 1):
            ch_i, ch_f = chans(d)
            c = i if d == 0 else nc - 1 - i
            r0 = pl.multiple_of(c * L, L)
            h = _mlstm_chunk(d, ch_i, ch_f, qs_s[c], ks_s[c], v_ref[0, pl.ds(r0, L), :],
                             gl_ref[0, c], pp_ref[0, c], glt_ref[0, c], ppt_ref[0, c], tot_ref[0, c],
                             ct_s, n_s, m_s)
            if d == 0:
                hf_s[c] = h
            else:
                hb_s[c] = h
        return carry

    lax.fori_loop(0, nc, body, 0)

    for c in range(nc):
        hs = hf_s[c] + hb_s[c]
        hn = _rms(hs, nw_ref[...])
        y_ref[0, c * L:(c + 1) * L, :] = (hn * jax.nn.sigmoid(o_ref[0, c * L:(c + 1) * L, :].astype(F32))).astype(BF16)


def _mlstm(ml, mlc, gp, gpc, conv_w, norm_w, L):
    b, s, _ = ml.shape
    lc = mlc.shape[1]
    H = ML_HEADS
    dv = norm_w.shape[1] // H
    dk = dv // 2
    nc, ncc = s // L, lc // L
    gl, pp, glt, ppt, tot = gp
    glc, ppc, _, _, totc = gpc
    kern = functools.partial(_mlstm_kernel, L=L, H=H, S=s, Lc=lc, dk=dk, dv=dv)

    def colspec(rows, w, off):
        return pl.BlockSpec((1, rows, w), lambda bi, h: (bi, 0, off + h))

    def gspec(n, r, c):
        return pl.BlockSpec((1, n, r, c), lambda bi, h: (bi, 0, 0, 0))

    return pl.pallas_call(
        kern,
        grid=(b, H),
        in_specs=[colspec(s, dk, 0), colspec(s, dk, H), colspec(s, dv, H), colspec(s, dv, 2 * H),
                  colspec(lc, dk, H), colspec(lc, dv, H),
                  gspec(nc, L, LANES), gspec(nc, L, LANES), gspec(nc, LANES, L), gspec(nc, LANES, L),
                  gspec(nc, SUBLANES, LANES),
                  gspec(ncc, L, LANES), gspec(ncc, L, LANES), gspec(ncc, SUBLANES, LANES),
                  pl.BlockSpec((CONV_W, dk), lambda bi, h: (0, h)),
                  pl.BlockSpec((CONV_W, dk), lambda bi, h: (0, H + h)),
                  pl.BlockSpec((1, dv), lambda bi, h: (0, h))],
        out_specs=pl.BlockSpec((1, s, dv), lambda bi, h: (bi, 0, h)),
        out_shape=jax.ShapeDtypeStruct((b, s, H * dv), BF16),
        scratch_shapes=[pltpu.VMEM((s + 2 * CONV_PAD, dk), F32), pltpu.VMEM((s + 2 * CONV_PAD, dk), F32),
                        pltpu.VMEM((lc + 2 * CONV_PAD, dk), F32),
                        pltpu.VMEM((nc, L, dk), BF16), pltpu.VMEM((nc, L, dk), BF16), pltpu.VMEM((ncc, L, dk), BF16),
                        pltpu.VMEM((nc, L, dv), F32), pltpu.VMEM((nc, L, dv), F32),
                        pltpu.VMEM((2, dk, dv), F32), pltpu.VMEM((2, 1, dk), F32),
                        pltpu.VMEM((2, SUBLANES, LANES), F32)],
        compiler_params=_cparams(("arbitrary", "arbitrary")),
        name="mlstm",
    )(ml, ml, ml, ml, mlc, mlc, gl, pp, glt, ppt, tot, glc, ppc, totc, conv_w, conv_w, norm_w)


def _attn_kernel(sink_ref, q_ref, kp_ref, kc_ref, kn_ref, vp_ref, vc_ref, vn_ref, kx_ref, vx_ref, bias_ref,
                 o_ref, *, bq, dh, G):
    q = q_ref[0]
    for g in range(AT_KV_HEADS):
        sl = slice(g * dh, (g + 1) * dh)
        qg = jnp.concatenate([q[:, (g * G + j) * dh:(g * G + j + 1) * dh] for j in range(G)], axis=0)
        kcat = jnp.concatenate([kp_ref[0][:, sl], kc_ref[0][:, sl], kn_ref[0][:, sl], kx_ref[0][:, sl]], axis=0)
        vcat = jnp.concatenate([vp_ref[0][:, sl], vc_ref[0][:, sl], vn_ref[0][:, sl], vx_ref[0][:, sl]], axis=0)
        s = lax.dot_general(qg, kcat, (((1,), (1,)), ((), ())), preferred_element_type=F32) + bias_ref[0]
        ri = lax.broadcasted_iota(I32, (G * bq, 1), 0)
        sk = jnp.zeros((G * bq, 1), F32)
        for j in range(G):
            sk = jnp.where((ri >= j * bq) & (ri < (j + 1) * bq), sink_ref[g * G + j], sk)
        m = jnp.maximum(jnp.max(s, axis=1, keepdims=True), sk)
        p = jnp.exp(s - m)
        den = jnp.sum(p, axis=1, keepdims=True) + jnp.exp(sk - m)
        o = _mm(p.astype(BF16), vcat) / den
        for j in range(G):
            o_ref[0, :, (g * G + j) * dh:(g * G + j + 1) * dh] = o[j * bq:(j + 1) * bq, :].astype(BF16)


def _attn_bias(bq, lc, G):
    qoff = (jnp.arange(G * bq) % bq)[:, None]
    ci = jnp.arange(3 * bq + lc)[None, :]
    prev = (ci < bq) & (qoff <= ci)
    cur = (ci >= bq) & (ci < 2 * bq)
    nxt = (ci >= 2 * bq) & (ci < 3 * bq) & (ci - 2 * bq <= qoff)
    ctx = ci >= 3 * bq
    inner = prev | cur | nxt | ctx
    first = cur | nxt | ctx
    last = prev | cur | ctx
    if WINDOW != bq:
        raise NotImplementedError("window must equal the query block")
    return jnp.where(jnp.stack([inner, first, last]), 0.0, NEG_BIG).astype(F32)


def _attn(aq, ak, av, akc, avc, sink):
    b, s, hd = aq.shape
    dh = hd // AT_HEADS
    G = AT_HEADS // AT_KV_HEADS
    bq = BLOCK_Q
    nb = s // bq
    lc = akc.shape[1]
    kvw = AT_KV_HEADS * dh
    bias = _attn_bias(bq, lc, G)
    if nb < 2:
        raise NotImplementedError("needs at least two query blocks")
    kern = functools.partial(_attn_kernel, bq=bq, dh=dh, G=G)
    prev = pl.BlockSpec((1, bq, kvw), lambda bi, n: (bi, jnp.maximum(n - 1, 0), 0))
    cur = pl.BlockSpec((1, bq, kvw), lambda bi, n: (bi, n, 0))
    nxt = pl.BlockSpec((1, bq, kvw), lambda bi, n: (bi, jnp.minimum(n + 1, nb - 1), 0))
    cx = pl.BlockSpec((1, lc, kvw), lambda bi, n: (bi, 0, 0))
    return pl.pallas_call(
        kern,
        grid=(b, nb),
        in_specs=[pl.BlockSpec(memory_space=pltpu.SMEM),
                  pl.BlockSpec((1, bq, hd), lambda bi, n: (bi, n, 0)),
                  prev, cur, nxt, prev, cur, nxt, cx, cx,
                  pl.BlockSpec((1, G * bq, 3 * bq + lc),
                               lambda bi, n: (jnp.where(n == 0, 1, jnp.where(n == nb - 1, 2, 0)), 0, 0))],
        out_specs=pl.BlockSpec((1, bq, hd), lambda bi, n: (bi, n, 0)),
        out_shape=jax.ShapeDtypeStruct((b, s, hd), BF16),
        compiler_params=_cparams(("arbitrary", "arbitrary")),
        name="attn",
    )(sink, aq, ak, ak, ak, av, av, av, akc, avc, bias)


def _outproj_kernel(ym_ref, ya_ref, x_ref, g1_ref, sh2_ref, sc2_ref, nw1_ref, nw2_ref, wo_ref, wrh_ref, wrl_ref,
                    x1_ref, h2_ref, lg_ref, y_s, *, dm, cw):
    d = y_s.shape[1]
    for c in range(0, d, cw):
        y_s[:, c:c + cw] = _mm(ym_ref[0], wo_ref[0:dm, c:c + cw]) + _mm(ya_ref[0], wo_ref[dm:, c:c + cw])
    x1 = x_ref[0] + g1_ref[0] * _rms(y_s[...], nw1_ref[...])
    x1_ref[0] = x1
    h2 = _rms(x1, nw2_ref[...]) * (1.0 + sc2_ref[0]) + sh2_ref[0]
    hi = h2.astype(BF16)
    h2_ref[0] = hi
    lo = (h2 - hi.astype(F32)).astype(BF16)
    lg_ref[0] = _mm(hi, wrh_ref[...]) + (_mm(hi, wrl_ref[...]) + _mm(lo, wrh_ref[...]))


def _outproj(y_ml, y_at, x, g1, sh2, sc2, nw1, nw2, wo, wr, tm):
    b, s, d = x.shape
    dm = y_ml.shape[2]
    e = wr.shape[1]
    wr_hi = wr.astype(BF16)
    wr_lo = (wr - wr_hi.astype(F32)).astype(BF16)
    kern = functools.partial(_outproj_kernel, dm=dm, cw=min(d, 512))
    vec = pl.BlockSpec((1, 1, d), lambda bi, i: (bi, 0, 0))
    nspec = pl.BlockSpec((1, d), lambda bi, i: (0, 0))
    return pl.pallas_call(
        kern,
        grid=(b, s // tm),
        in_specs=[pl.BlockSpec((1, tm, dm), lambda bi, i: (bi, i, 0)),
                  pl.BlockSpec((1, tm, d - dm), lambda bi, i: (bi, i, 0)),
                  pl.BlockSpec((1, tm, d), lambda bi, i: (bi, i, 0)),
                  vec, vec, vec, nspec, nspec,
                  pl.BlockSpec((d, d), lambda bi, i: (0, 0), pipeline_mode=pl.Buffered(1)),
                  pl.BlockSpec((d, e), lambda bi, i: (0, 0), pipeline_mode=pl.Buffered(1)),
                  pl.BlockSpec((d, e), lambda bi, i: (0, 0), pipeline_mode=pl.Buffered(1))],
        out_specs=[pl.BlockSpec((1, tm, d), lambda bi, i: (bi, i, 0)),
                   pl.BlockSpec((1, tm, d), lambda bi, i: (bi, i, 0)),
                   pl.BlockSpec((1, tm, e), lambda bi, i: (bi, i, 0))],
        out_shape=[jax.ShapeDtypeStruct((b, s, d), F32), jax.ShapeDtypeStruct((b, s, d), BF16),
                   jax.ShapeDtypeStruct((b, s, e), F32)],
        scratch_shapes=[pltpu.VMEM((tm, d), F32)],
        compiler_params=_cparams(("arbitrary", "arbitrary")),
        name="outproj",
    )(y_ml, y_at, x, g1, sh2, sc2, nw1, nw2, wo, wr_hi, wr_lo)


def _route_kernel(lg_ref, br_ref, eidx_ref, ew_ref, rank_ref, cnt_ref, carry_s, *, E, tr):
    i = pl.program_id(0)

    @pl.when(i == 0)
    def _():
        carry_s[...] = jnp.zeros(carry_s.shape, F32)

    scores = jax.nn.sigmoid(lg_ref[...].T)
    biased = scores + br_ref[:, 0:1]
    row = lax.broadcasted_iota(I32, (E, tr), 0).astype(F32)
    gs = E // N_GROUPS
    ninf = -jnp.inf
    grp = []
    for g in range(N_GROUPS):
        xg = biased[g * gs:(g + 1) * gs]
        rg = (lax.broadcasted_iota(I32, (gs, tr), 0) + g * gs).astype(F32)
        m1 = jnp.max(xg, axis=0, keepdims=True)
        i1 = jnp.min(jnp.where(xg == m1, rg, float(E)), axis=0, keepdims=True)
        m2 = jnp.max(jnp.where(rg == i1, ninf, xg), axis=0, keepdims=True)
        grp.append(m1 + m2)
    gsc = jnp.concatenate(grp, axis=0)
    gi = lax.broadcasted_iota(I32, (N_GROUPS, tr), 0)
    beaten = jnp.zeros((N_GROUPS, tr), I32)
    for g2 in range(N_GROUPS):
        sg = gsc[g2:g2 + 1]
        beaten = beaten + jnp.where((sg > gsc) | ((sg == gsc) & (gi > g2)), 1, 0)
    keep_g = jnp.where(beaten < TOPK_GROUPS, 1.0, 0.0)
    keep = jnp.concatenate([jnp.broadcast_to(keep_g[g:g + 1], (gs, tr)) for g in range(N_GROUPS)], axis=0)
    masked = jnp.where(keep > 0.5, biased, ninf)
    idxs, ws = [], []
    for _ in range(TOP_K):
        m = jnp.max(masked, axis=0, keepdims=True)
        ik = jnp.min(jnp.where(masked == m, row, float(E)), axis=0, keepdims=True)
        sel = row == ik
        ws.append(jnp.sum(jnp.where(sel, scores, 0.0), axis=0, keepdims=True))
        idxs.append(ik)
        masked = jnp.where(sel, ninf, masked)
    w = jnp.concatenate(ws, axis=0)
    ew_ref[...] = w / jnp.sum(w, axis=0, keepdims=True) * ROUTE_SCALE
    eidx_ref[...] = jnp.concatenate(idxs, axis=0).astype(I32)

    assign = jnp.zeros((E, tr), F32)
    for ik in idxs:
        assign = assign + jnp.where(row == ik, 1.0, 0.0)
    upper = jnp.where(lax.broadcasted_iota(I32, (tr, tr), 0) < lax.broadcasted_iota(I32, (tr, tr), 1), 1.0, 0.0)
    base = _mm(assign.astype(BF16), upper.astype(BF16)) + carry_s[:, 0:1]
    ranks = [jnp.sum(jnp.where(row == ik, base, 0.0), axis=0, keepdims=True) for ik in idxs]
    rank_ref[...] = jnp.concatenate(ranks, axis=0).astype(I32)
    carry_s[...] = carry_s[...] + jnp.sum(assign, axis=1, keepdims=True)
    cnt_ref[...] = carry_s[...]


def _route(logits, b_router, tr):
    t, e = logits.shape
    kern = functools.partial(_route_kernel, E=e, tr=tr)
    kt = pl.BlockSpec((TOP_K, tr), lambda i: (0, i))
    return pl.pallas_call(
        kern,
        grid=(t // tr,),
        in_specs=[pl.BlockSpec((tr, e), lambda i: (i, 0)), pl.BlockSpec((e, LANES), lambda i: (0, 0))],
        out_specs=[kt, kt, kt, pl.BlockSpec((e, LANES), lambda i: (0, 0))],
        out_shape=[jax.ShapeDtypeStruct((TOP_K, t), I32), jax.ShapeDtypeStruct((TOP_K, t), F32),
                   jax.ShapeDtypeStruct((TOP_K, t), I32), jax.ShapeDtypeStruct((e, LANES), F32)],
        scratch_shapes=[pltpu.VMEM((e, LANES), F32)],
        compiler_params=_cparams(("arbitrary",)),
        name="route",
    )(logits, b_router)


def _positions_kernel(eidx_ref, rank_ref, ps_ref, pos_ref, *, E, tr):
    row = lax.broadcasted_iota(I32, (E, tr), 0)
    start = ps_ref[:, 0:1]
    rows = [jnp.sum(jnp.where(row == eidx_ref[k:k + 1, :], start, 0.0), axis=0, keepdims=True)
            for k in range(TOP_K)]
    pos_ref[...] = jnp.concatenate(rows, axis=0).astype(I32) + rank_ref[...]


def _positions(eidx, rank, pad_start, tr):
    t = eidx.shape[1]
    e = pad_start.shape[0]
    kern = functools.partial(_positions_kernel, E=e, tr=tr)
    kt = pl.BlockSpec((TOP_K, tr), lambda i: (0, i))
    return pl.pallas_call(
        kern,
        grid=(t // tr,),
        in_specs=[kt, kt, pl.BlockSpec((e, LANES), lambda i: (0, 0))],
        out_specs=kt,
        out_shape=jax.ShapeDtypeStruct((TOP_K, t), I32),
        compiler_params=_cparams(("arbitrary",)),
        name="positions",
    )(eidx, rank, jnp.broadcast_to(pad_start.astype(F32)[:, None], (e, LANES)))


def _dispatch_kernel(zstart_ref, zvalid_ref, pos_hbm, h2_ref, xs_hbm, rows_s, zero_s, pos_s, sem_p, sem_z, sem_r,
                     *, E, tt):
    i = pl.program_id(0)

    def zero_copy(e):
        start = pl.multiple_of(zstart_ref[e], MOE_BLOCK)
        return pltpu.make_async_copy(zero_s, xs_hbm.at[pl.ds(start, MOE_BLOCK)], sem_z)

    @pl.when(i == 0)
    def _():
        zero_s[...] = jnp.zeros(zero_s.shape, F32)

        def start(e, c):
            @pl.when(zvalid_ref[e] > 0)
            def _():
                zero_copy(e).start()
            return c

        def wait(e, c):
            @pl.when(zvalid_ref[e] > 0)
            def _():
                zero_copy(e).wait()
            return c

        lax.fori_loop(0, E, start, 0)
        lax.fori_loop(0, E, wait, 0)

    pos_copy = pltpu.make_async_copy(pos_hbm.at[i], pos_s, sem_p)
    pos_copy.start()
    rows_s[...] = h2_ref[...].astype(F32)
    pos_copy.wait()

    def row_copy(t, k):
        return pltpu.make_async_copy(rows_s.at[pl.ds(t, 1)], xs_hbm.at[pl.ds(pos_s[k, t], 1)], sem_r)

    def start_rows(t, c):
        for k in range(TOP_K):
            row_copy(t, k).start(priority=k % 2)
        return c

    def wait_rows(t, c):
        for k in range(TOP_K):
            row_copy(t, k).wait()
        return c

    lax.fori_loop(0, tt, start_rows, 0)
    lax.fori_loop(0, tt, wait_rows, 0)


def _dispatch(h2, pos_tiles, zstart, zvalid, n_rows, tt):
    t, d = h2.shape
    e = zstart.shape[0]
    kern = functools.partial(_dispatch_kernel, E=e, tt=tt)
    return pl.pallas_call(
        kern,
        grid_spec=pltpu.PrefetchScalarGridSpec(
            num_scalar_prefetch=2,
            grid=(t // tt,),
            in_specs=[pl.BlockSpec(memory_space=pl.ANY), pl.BlockSpec((tt, d), lambda i, zs, zv: (i, 0))],
            out_specs=pl.BlockSpec(memory_space=pl.ANY),
            scratch_shapes=[pltpu.VMEM((tt, d), F32), pltpu.VMEM((MOE_BLOCK, d), F32),
                            pltpu.SMEM((TOP_K, tt), I32),
                            pltpu.SemaphoreType.DMA, pltpu.SemaphoreType.DMA, pltpu.SemaphoreType.DMA]),
        out_shape=jax.ShapeDtypeStruct((n_rows, d), F32),
        compiler_params=pltpu.CompilerParams(dimension_semantics=("arbitrary",), vmem_limit_bytes=VMEM_LIMIT_BYTES,
                                             has_side_effects=True),
        name="dispatch",
    )(zstart, zvalid, pos_tiles, h2)


def _experts_kernel(be_ref, nu_ref, nxt_ref, ord_ref, xs_ref, wg_hbm, wu_hbm, wd_hbm, ys_ref,
                    sg_s, su_s, sd_s, wgb_s, wub_s, wdb_s, sem, *, E):
    b = pl.program_id(0)
    n_used = nu_ref[0]

    def copies(e, slot):
        return (pltpu.make_async_copy(wg_hbm.at[e], sg_s.at[slot], sem.at[slot, 0]),
                pltpu.make_async_copy(wu_hbm.at[e], su_s.at[slot], sem.at[slot, 1]),
                pltpu.make_async_copy(wd_hbm.at[e], sd_s.at[slot], sem.at[slot, 2]))

    @pl.when(b < n_used)
    def _():
        e = be_ref[b]
        prev = be_ref[jnp.maximum(b - 1, 0)]
        slot = ord_ref[e] & 1

        @pl.when(b == 0)
        def _():
            for cp in copies(e, slot):
                cp.start()

        @pl.when((b == 0) | (e != prev))
        def _():
            for cp in copies(e, slot):
                cp.wait()
            ne = nxt_ref[e]

            @pl.when(ne < E)
            def _():
                for cp in copies(ne, 1 - slot):
                    cp.start()

            wgb_s[...] = sg_s[slot].astype(BF16)
            wub_s[...] = su_s[slot].astype(BF16)
            wdb_s[...] = sd_s[slot].astype(BF16)

        xb = xs_ref[...].astype(BF16)
        a = _silu(_mm(xb, wgb_s[...])) * _mm(xb, wub_s[...])
        ys_ref[...] = _mm(a.astype(BF16), wdb_s[...])

    @pl.when(b >= n_used)
    def _():
        ys_ref[...] = jnp.zeros(ys_ref.shape, F32)


def _experts(xs, block_e, n_used, next_used, used_ord, wg, wu, wd):
    n_rows, d = xs.shape
    nblk = n_rows // MOE_BLOCK
    e, _, de = wg.shape
    kern = functools.partial(_experts_kernel, E=e)

    def live(b, be, nu, nx, od):
        return (jnp.minimum(b, jnp.maximum(nu[0] - 1, 0)), 0)

    hbm = pl.BlockSpec(memory_space=pl.ANY)
    return pl.pallas_call(
        kern,
        grid_spec=pltpu.PrefetchScalarGridSpec(
            num_scalar_prefetch=4,
            grid=(nblk,),
            in_specs=[pl.BlockSpec((MOE_BLOCK, d), live), hbm, hbm, hbm],
            out_specs=pl.BlockSpec((MOE_BLOCK, d), lambda b, be, nu, nx, od: (b, 0)),
            scratch_shapes=[pltpu.VMEM((2, d, de), F32), pltpu.VMEM((2, d, de), F32), pltpu.VMEM((2, de, d), F32),
                            pltpu.VMEM((d, de), BF16), pltpu.VMEM((d, de), BF16), pltpu.VMEM((de, d), BF16),
                            pltpu.SemaphoreType.DMA((2, 3))]),
        out_shape=jax.ShapeDtypeStruct((n_rows, d), F32),
        compiler_params=_cparams(("arbitrary",)),
        name="experts",
    )(block_e, n_used, next_used, used_ord, xs, wg, wu, wd)


def _combine_kernel(pos_hbm, ys_hbm, ew_ref, x1_ref, h2_ref, g2_ref, nw3_ref, wsg_ref, wsu_ref, wsd_ref,
                    out_ref, buf_s, pos_s, sem_p, sem_r, *, tc, nt):
    tile = pl.program_id(0) * nt + pl.program_id(1)
    pos_copy = pltpu.make_async_copy(pos_hbm.at[tile], pos_s, sem_p)
    pos_copy.start()
    pos_copy.wait()

    def row_copy(t, k):
        return pltpu.make_async_copy(ys_hbm.at[pl.ds(pos_s[k, t], 1)], buf_s.at[k, pl.ds(t, 1)], sem_r)

    def start_rows(t, c):
        for k in range(TOP_K):
            row_copy(t, k).start(priority=k % 2)
        return c

    def wait_rows(t, c):
        for k in range(TOP_K):
            row_copy(t, k).wait()
        return c

    lax.fori_loop(0, tc, start_rows, 0)
    hb = h2_ref[0]
    a = _silu(_mm(hb, wsg_ref[...])) * _mm(hb, wsu_ref[...])
    acc = _mm(a.astype(BF16), wsd_ref[...])
    lax.fori_loop(0, tc, wait_rows, 0)
    ew = ew_ref[0]
    for k in range(TOP_K):
        acc = acc + buf_s[k] * ew[:, k:k + 1]
    out_ref[0] = x1_ref[0] + g2_ref[0] * _rms(acc, nw3_ref[...])


def _combine(pos_tiles, ys, ew, x1, h2, g2, nw3, wsg, wsu, wsd, tc):
    b, s, d = x1.shape
    ds_ = wsg.shape[1]
    nt = s // tc
    kern = functools.partial(_combine_kernel, tc=tc, nt=nt)
    tok = lambda w: pl.BlockSpec((1, tc, w), lambda bi, i: (bi, i, 0))
    return pl.pallas_call(
        kern,
        grid=(b, nt),
        in_specs=[pl.BlockSpec(memory_space=pl.ANY), pl.BlockSpec(memory_space=pl.ANY),
                  tok(TOP_K), tok(d), tok(d),
                  pl.BlockSpec((1, 1, d), lambda bi, i: (bi, 0, 0)),
                  pl.BlockSpec((1, d), lambda bi, i: (0, 0)),
                  pl.BlockSpec((d, ds_), lambda bi, i: (0, 0)),
                  pl.BlockSpec((d, ds_), lambda bi, i: (0, 0)),
                  pl.BlockSpec((ds_, d), lambda bi, i: (0, 0))],
        out_specs=tok(d),
        out_shape=jax.ShapeDtypeStruct((b, s, d), F32),
        scratch_shapes=[pltpu.VMEM((TOP_K, tc, d), F32), pltpu.SMEM((TOP_K, tc), I32),
                        pltpu.SemaphoreType.DMA, pltpu.SemaphoreType.DMA],
        compiler_params=_cparams(("arbitrary", "arbitrary")),
        name="combine",
    )(pos_tiles, ys, ew, x1, h2, g2, nw3, wsg, wsu, wsd)


def _tile(n, pref):
    t = min(n, pref)
    if n % t:
        raise NotImplementedError(f"size {n} is not a multiple of tile {t}")
    return t


def kernel(x, c, ctx, c_ctx, w_ada, b_ada, norms, w_in, ml_conv, ml_gate_b, ml_norm_w, attn_sink, w_out,
           w_router, b_router, w_exp_gate, w_exp_up, w_exp_down, w_sh_gate, w_sh_up, w_sh_down):
    if w_ada.shape[0] != 1:
        raise NotImplementedError("single-layer configuration only")
    b, s, d = x.shape
    lc = ctx.shape[1]
    t = b * s
    H = ML_HEADS
    dv = d // 2 // H
    dk = dv // 2
    dh = d // 2 // AT_HEADS
    qk_w, v_w = 2 * H * dk, H * dv
    ng = 4 * H
    aq_w, akv_w = AT_HEADS * dh, AT_KV_HEADS * dh
    e = w_router.shape[-1]
    nw = norms[0]

    rows = -(-(b + 1) // SUBLANES) * SUBLANES
    cond = jnp.zeros((rows, d), F32).at[:b].set(c).at[b].set(c_ctx)
    mods = _ada(cond, w_ada[0], b_ada[0][None, :])
    sh1, sc1, g1, sh2, sc2, g2 = [m[:b, None, :] for m in jnp.split(mods, 6, axis=-1)]
    csh1, csc1 = [jnp.broadcast_to(m[b:b + 1, None, :], (b, 1, d)) for m in jnp.split(mods, 6, axis=-1)[:2]]

    w0 = w_in[0]
    o_ml, o_g, o_q = 0, qk_w + 2 * v_w, qk_w + 2 * v_w + ng
    ml_w = qk_w + 2 * v_w
    wp = jnp.concatenate([w0[:, o_ml:o_g], w0[:, o_q:], w0[:, o_g:o_q], jnp.zeros((d, LANES - ng), F32)],
                         axis=1).astype(BF16)
    layout = (("ml", 0, ml_w), ("aq", ml_w, aq_w), ("ak", ml_w + aq_w, akv_w),
              ("av", ml_w + aq_w + akv_w, akv_w), ("g", ml_w + aq_w + 2 * akv_w, LANES))
    widths = dict(ml=ml_w, aq=aq_w, ak=akv_w, av=akv_w, g=LANES)
    tabs = _rope_tables(s, dh)
    proj = functools.partial(_inproj, layout=layout, widths=widths, dh=dh)
    ml, aq, ak, av, gates = proj(x, sh1, sc1, nw[0:1], wp, tabs, rope=True, tm=_tile(s, 512))
    tm_c = _tile(lc, 512)
    mlc, _, akc, avc, gates_c = proj(ctx, csh1, csc1, nw[0:1], wp, tuple(tb[:tm_c] for tb in tabs),
                                     rope=False, tm=tm_c)

    L = _tile(lc, 256)
    if s % L:
        raise NotImplementedError("sequence must be a multiple of the mLSTM chunk")
    gate_b = jnp.zeros((1, LANES), F32).at[0, :ng].set(ml_gate_b[0])
    gp = _gate_prep(gates, gate_b, L)
    gpc = _gate_prep(gates_c, gate_b, L)
    y_ml = _mlstm(ml, mlc, gp, gpc, ml_conv[0], ml_norm_w[0][None, :], L)

    y_at = _attn(aq, ak, av, akc, avc, attn_sink[0])

    x1, h2, logits = _outproj(y_ml, y_at, x, g1, sh2, sc2, nw[1:2], nw[2:3], w_out[0].astype(BF16),
                              w_router[0], _tile(s, 256))

    eidx, ew, rank, cnt = _route(logits.reshape(t, e), jnp.broadcast_to(b_router[0][:, None], (e, LANES)),
                                 _tile(t, 512))
    counts = cnt[:, 0].astype(I32)
    padded = (counts + MOE_BLOCK - 1) // MOE_BLOCK * MOE_BLOCK
    pad_end = jnp.cumsum(padded)
    pad_start = pad_end - padded
    n_blocks = -(-(t * TOP_K + e * (MOE_BLOCK - 1)) // MOE_BLOCK)
    n_rows = n_blocks * MOE_BLOCK
    blk_row = jnp.arange(n_blocks, dtype=I32) * MOE_BLOCK
    block_e = jnp.minimum(jnp.sum((pad_end[None, :] <= blk_row[:, None]).astype(I32), axis=1), e - 1)
    n_used = (pad_end[-1] // MOE_BLOCK).astype(I32)[None]
    used = counts > 0
    eid = jnp.arange(e, dtype=I32)
    next_used = jnp.concatenate([lax.cummin(jnp.where(used, eid, e), reverse=True)[1:], jnp.full((1,), e, I32)])
    used_ord = jnp.cumsum(used.astype(I32)) - 1
    pos = _positions(eidx, rank, pad_start, _tile(t, 512))

    tt = _tile(t, 256)
    pos_d = pos.reshape(TOP_K, t // tt, tt).transpose(1, 0, 2)
    xs = _dispatch(h2.reshape(t, d), pos_d, (pad_end - MOE_BLOCK).astype(I32), (padded > 0).astype(I32),
                   n_rows, tt)
    ys = _experts(xs, block_e, n_used, next_used, used_ord, w_exp_gate[0], w_exp_up[0], w_exp_down[0])
    tc = _tile(s, 128)
    pos_c = pos.reshape(TOP_K, t // tc, tc).transpose(1, 0, 2)
    ew_t = ew.T.reshape(b, s, TOP_K)
    return _combine(pos_c, ys, ew_t, x1, h2, g2, nw[3:4], w_sh_gate[0].astype(BF16), w_sh_up[0].astype(BF16),
                    w_sh_down[0].astype(BF16), tc)
```

```python
import functools

import jax
import jax.numpy as jnp
from jax import lax
from jax.experimental import pallas as pl
from jax.experimental.pallas import tpu as pltpu

F32 = jnp.float32
BF16 = jnp.bfloat16
I32 = jnp.int32

LANES = 128
SUBLANES = 8
VMEM_LIMIT_BYTES = 56 * 1024 * 1024

NORM_EPS = 1e-6
ML_HEADS = 4
CONV_W = 5
AT_HEADS = 8
AT_KV_HEADS = 2
GRID_W = 64
WINDOW = 128
BLOCK_Q = 128
ROPE_THETA = 10000.0
N_GROUPS = 8
TOPK_GROUPS = 4
TOP_K = 8
ROUTE_SCALE = 2.5
MOE_BLOCK = 128
NEG_BIG = -1e30


def _cparams(sem):
    return pltpu.CompilerParams(dimension_semantics=sem, vmem_limit_bytes=VMEM_LIMIT_BYTES)


def _rms(xf, w):
    return xf * lax.rsqrt(jnp.mean(xf * xf, axis=-1, keepdims=True) + NORM_EPS) * w


def _silu(x):
    return x * jax.nn.sigmoid(x)


def _mm(a, b):
    return jnp.dot(a, b, preferred_element_type=F32)


U32 = jnp.uint32


def _pack_halves(x):
    n = x.shape[1] // 2
    u = lax.bitcast_convert_type(x.astype(BF16).astype(F32), U32)
    return (u[:, :n] >> 16) | (u[:, n:] & jnp.uint32(0xFFFF0000))


def _unpack_halves(u):
    return (lax.bitcast_convert_type(u << 16, F32),
            lax.bitcast_convert_type(u & jnp.uint32(0xFFFF0000), F32))


def _ada_kernel(c_ref, w_ref, b_ref, o_ref):
    s = _silu(c_ref[...]).astype(BF16)
    o_ref[...] = _mm(s, w_ref[...].astype(BF16)) + b_ref[...]


def _ada(cond, w, b):
    r, d = cond.shape
    n = w.shape[1]
    tn = min(n, 1024)
    return pl.pallas_call(
        _ada_kernel,
        grid=(n // tn,),
        in_specs=[pl.BlockSpec((r, d), lambda j: (0, 0)),
                  pl.BlockSpec((d, tn), lambda j: (0, j)),
                  pl.BlockSpec((1, tn), lambda j: (0, j))],
        out_specs=pl.BlockSpec((r, tn), lambda j: (0, j)),
        out_shape=jax.ShapeDtypeStruct((r, n), F32),
        compiler_params=_cparams(("arbitrary",)),
        name="ada",
    )(cond, w, b)


def _inproj_kernel(x_ref, sh_ref, sc_ref, nw_ref, w_ref, rc_ref, ra_ref, rb_ref,
                   ml_ref, aq_ref, ak_ref, av_ref, g_ref, hb_ref, *, layout, dh, rope, qscale, cw):
    h = _rms(x_ref[0], nw_ref[...]) * (1.0 + sc_ref[0]) + sh_ref[0]
    hb_ref[...] = h.astype(BF16)
    outs = dict(ml=ml_ref, aq=aq_ref, ak=ak_ref, av=av_ref, g=g_ref)
    for name, col0, width in layout:
        o_ref = outs[name]
        for c in range(0, width, cw):
            step = min(cw, width - c)
            acc = _mm(hb_ref[...], w_ref[:, col0 + c:col0 + c + step])
            if rope and name in ("aq", "ak"):
                for hh in range(step // dh):
                    a = acc[:, hh * dh:(hh + 1) * dh]
                    r = (a * rc_ref[...] + pltpu.roll(a, dh - dh // 4, 1) * ra_ref[...]
                         + pltpu.roll(a, dh // 4, 1) * rb_ref[...])
                    if name == "aq":
                        r = r * qscale
                    o_ref[0, :, c + hh * dh:c + (hh + 1) * dh] = r.astype(o_ref.dtype)
            else:
                if name == "aq":
                    acc = acc * qscale
                o_ref[0, :, c:c + step] = acc.astype(o_ref.dtype)


def _inproj(x, sh, sc, nw, wp, tabs, *, layout, widths, dh, rope, tm):
    b, s, d = x.shape
    npad = wp.shape[1]
    rc, ra, rb = tabs
    kern = functools.partial(_inproj_kernel, layout=layout, dh=dh, rope=rope, qscale=float(dh) ** -0.5, cw=512)
    vec = pl.BlockSpec((1, 1, d), lambda bi, i: (bi, 0, 0))
    tab = pl.BlockSpec((tm, dh), lambda bi, i: (i, 0))
    out_dt = dict(ml=BF16, aq=BF16, ak=BF16, av=BF16, g=F32)
    names = ("ml", "aq", "ak", "av", "g")
    return pl.pallas_call(
        kern,
        grid=(b, s // tm),
        in_specs=[pl.BlockSpec((1, tm, d), lambda bi, i: (bi, i, 0)), vec, vec,
                  pl.BlockSpec((1, d), lambda bi, i: (0, 0)),
                  pl.BlockSpec((d, npad), lambda bi, i: (0, 0), pipeline_mode=pl.Buffered(1)),
                  tab, tab, tab],
        out_specs=[pl.BlockSpec((1, tm, widths[n]), lambda bi, i: (bi, i, 0)) for n in names],
        out_shape=[jax.ShapeDtypeStruct((b, s, widths[n]), out_dt[n]) for n in names],
        scratch_shapes=[pltpu.VMEM((tm, d), BF16)],
        compiler_params=_cparams(("arbitrary", "arbitrary")),
        name="inproj_rope" if rope else "inproj_ctx",
    )(x, sh, sc, nw, wp, rc, ra, rb)


def _rope_tables(s, dh):
    rows = s // GRID_W
    row = jnp.repeat(jnp.arange(rows), GRID_W)
    col = jnp.tile(jnp.arange(GRID_W), rows)
    nf = dh // 4
    freqs = ROPE_THETA ** (-jnp.arange(nf, dtype=F32) / nf)
    pos = jnp.stack([row, col], axis=-1).astype(F32)
    ang = pos[:, :, None] * freqs
    cos, sin = jnp.cos(ang), jnp.sin(ang)
    z = jnp.zeros_like(sin[:, 0])
    rc = jnp.concatenate([cos[:, 0], cos[:, 0], cos[:, 1], cos[:, 1]], axis=-1)
    ra = jnp.concatenate([-sin[:, 0], z, -sin[:, 1], z], axis=-1)
    rb = jnp.concatenate([z, sin[:, 0], z, sin[:, 1]], axis=-1)
    return rc, ra, rb


def _gate_prep_kernel(g_ref, b_ref, gl_ref, pp_ref, glt_ref, ppt_ref, tot_ref, *, L, H):
    z = g_ref[0] + b_ref[...]
    lane = lax.broadcasted_iota(I32, z.shape, 1)
    row = lax.broadcasted_iota(I32, z.shape, 0)
    is_f = ((lane // H) % 2 == 1) & (lane < 4 * H)
    log_sig = jnp.minimum(z, 0.0) - jnp.log1p(jnp.exp(-jnp.abs(z)))
    gl = jnp.where(lane < 4 * H, jnp.where(is_f, log_sig, z), 0.0)
    cs = gl
    k = 1
    while k < L:
        cs = cs + jnp.where(row >= k, pltpu.roll(cs, k, 0), 0.0)
        k *= 2
    tot = cs[L - 1:L, :]
    suf = tot - cs + gl
    pp = jnp.where(lane >= 2 * H, suf, cs)
    gl_ref[0, 0] = gl
    pp_ref[0, 0] = pp
    glt_ref[0, 0] = gl.T
    ppt_ref[0, 0] = pp.T
    tot_ref[0, 0] = jnp.broadcast_to(tot, (SUBLANES, LANES))


def _gate_prep(g, gate_b, L):
    b, s, _ = g.shape
    nc = s // L
    kern = functools.partial(_gate_prep_kernel, L=L, H=ML_HEADS)
    col = pl.BlockSpec((1, 1, L, LANES), lambda bi, c: (bi, c, 0, 0))
    rowb = pl.BlockSpec((1, 1, LANES, L), lambda bi, c: (bi, c, 0, 0))
    return pl.pallas_call(
        kern,
        grid=(b, nc),
        in_specs=[pl.BlockSpec((1, L, LANES), lambda bi, c: (bi, c, 0)),
                  pl.BlockSpec((1, LANES), lambda bi, c: (0, 0))],
        out_specs=[col, col, rowb, rowb, pl.BlockSpec((1, 1, SUBLANES, LANES), lambda bi, c: (bi, c, 0, 0))],
        out_shape=[jax.ShapeDtypeStruct((b, nc, L, LANES), F32), jax.ShapeDtypeStruct((b, nc, L, LANES), F32),
                   jax.ShapeDtypeStruct((b, nc, LANES, L), F32), jax.ShapeDtypeStruct((b, nc, LANES, L), F32),
                   jax.ShapeDtypeStruct((b, nc, SUBLANES, LANES), F32)],
        compiler_params=_cparams(("arbitrary", "arbitrary")),
        name="gate_prep",
    )(g, gate_b)


CONV_PAD = 8


def _lane_pick(tile, ch):
    lane = lax.broadcasted_iota(I32, tile.shape, 1)
    return jnp.sum(jnp.where(lane == ch, tile, 0.0), axis=1, keepdims=True)


def _sublane_pick(tile, ch):
    sub = lax.broadcasted_iota(I32, tile.shape, 0)
    return jnp.sum(jnp.where(sub == ch, tile, 0.0), axis=0, keepdims=True)


def _mlstm_chunk(d, ch_i, ch_f, q, k, v, gl, pp, glt, ppt, tot8, ct_ref, n_ref, m_ref):
    L = k.shape[0]
    p_col = _lane_pick(pp, ch_f)
    li_col = _lane_pick(gl, ch_i)
    btot = _lane_pick(tot8[0:1], ch_f)
    m_old = m_ref[d][0:1, 0:1]
    ct = ct_ref[d]
    n_row = n_ref[d]
    h = None
    if q is not None:
        p_row = _sublane_pick(ppt, ch_f)
        li_row = _sublane_pick(glt, ch_i)
        ti = lax.broadcasted_iota(I32, (L, L), 0)
        si = lax.broadcasted_iota(I32, (L, L), 1)
        valid = (si <= ti) if d == 0 else (si >= ti)
        logw = jnp.where(valid, p_col - p_row + li_row, NEG_BIG)
        log_inter = p_col + m_old
        m_q = jnp.maximum(log_inter, jnp.max(logw, axis=1, keepdims=True))
        sqk = lax.dot_general(q, k, (((1,), (1,)), ((), ())), preferred_element_type=F32) * jnp.exp(logw - m_q)
        w_inter = jnp.exp(log_inter - m_q)
        num = _mm(sqk.astype(BF16), v) + w_inter * _mm(q, ct.astype(BF16))
        den = (jnp.sum(sqk, axis=1, keepdims=True)
               + w_inter * jnp.sum(q.astype(F32) * n_row, axis=1, keepdims=True))
        h = num / jnp.maximum(jnp.abs(den), jnp.exp(-m_q))
    log_w_end = btot - p_col + li_col
    m_new = jnp.maximum(btot + m_old, jnp.max(log_w_end, axis=0, keepdims=True))
    w_end = jnp.exp(log_w_end - m_new)
    decay = jnp.exp(btot + m_old - m_new)
    kw = k.astype(F32) * w_end
    ct_ref[d] = decay * ct + lax.dot_general(kw.astype(BF16), v, (((0,), (0,)), ((), ())),
                                             preferred_element_type=F32)
    n_ref[d] = decay * n_row + jnp.sum(kw, axis=0, keepdims=True)
    m_ref[d] = jnp.broadcast_to(m_new, (SUBLANES, LANES))
    return h


def _mlstm_kernel(q_ref, k_ref, v_ref, o_ref, kc_ref, vc_ref,
                  gl_ref, pp_ref, glt_ref, ppt_ref, tot_ref, glc_ref, ppc_ref, totc_ref,
                  cwq_ref, cwk_ref, nw_ref, y_ref,
                  xq_s, xk_s, xkc_s, qs_s, ks_s, kcs_s, hf_s, hb_s, ct_s, n_s, m_s,
                  *, L, H, S, Lc, dk, dv):
    hh = pl.program_id(1)
    nc, ncc = S // L, Lc // L
    zpad = jnp.zeros((CONV_PAD, dk), F32)

    def conv_silu(x_s, cw_ref, n_rows, out_s, scale):
        for c in range(n_rows // L):
            acc = jnp.zeros((L, dk), F32)
            for j in range(CONV_W):
                acc = acc + cw_ref[j:j + 1, :] * x_s[pl.ds(CONV_PAD + c * L + j - CONV_W // 2, L), :]
            out_s[c] = (_silu(acc) * scale).astype(BF16)

    def stage(x_s, src, n_rows):
        x_s[0:CONV_PAD, :] = zpad
        x_s[CONV_PAD + n_rows:2 * CONV_PAD + n_rows, :] = zpad
        x_s[CONV_PAD:CONV_PAD + n_rows, :] = src.astype(F32)

    stage(xq_s, q_ref[0], S)
    stage(xk_s, k_ref[0], S)
    stage(xkc_s, kc_ref[0], Lc)
    conv_silu(xq_s, cwq_ref, S, qs_s, float(dk) ** -0.5)
    conv_silu(xk_s, cwk_ref, S, ks_s, 1.0)
    conv_silu(xkc_s, cwk_ref, Lc, kcs_s, 1.0)

    ct_s[...] = jnp.zeros(ct_s.shape, F32)
    n_s[...] = jnp.zeros(n_s.shape, F32)
    m_s[...] = jnp.zeros(m_s.shape, F32)

    def chans(d):
        return d * 2 * H + hh, d * 2 * H + H + hh

    for d in (0, 1):
        ch_i, ch_f = chans(d)
        for c in (range(ncc) if d == 0 else range(ncc - 1, -1, -1)):
            _mlstm_chunk(d, ch_i, ch_f, None, kcs_s[c], vc_ref[0, c * L:(c + 1) * L, :],
                         glc_ref[0, c], ppc_ref[0, c], None, None, totc_ref[0, c], ct_s, n_s, m_s)

    def body(i, carry):
        for d in (0, 1):
            ch_i, ch_f = chans(d)
            c = i if d == 0 else nc - 1 - i
            r0 = pl.multiple_of(c * L, L)
            h = _mlstm_chunk(d, ch_i, ch_f, qs_s[c], ks_s[c], v_ref[0, pl.ds(r0, L), :],
                             gl_ref[0, c], pp_ref[0, c], glt_ref[0, c], ppt_ref[0, c], tot_ref[0, c],
                             ct_s, n_s, m_s)
            if d == 0:
                hf_s[c] = h
            else:
                hb_s[c] = h
        return carry

    lax.fori_loop(0, nc, body, 0)

    for c in range(nc):
        hs = hf_s[c] + hb_s[c]
        hn = _rms(hs, nw_ref[...])
        y_ref[0, c * L:(c + 1) * L, :] = (hn * jax.nn.sigmoid(o_ref[0, c * L:(c + 1) * L, :].astype(F32))).astype(BF16)


def _mlstm(ml, mlc, gp, gpc, conv_w, norm_w, L):
    b, s, _ = ml.shape
    lc = mlc.shape[1]
    H = ML_HEADS
    dv = norm_w.shape[1] // H
    dk = dv // 2
    nc, ncc = s // L, lc // L
    gl, pp, glt, ppt, tot = gp
    glc, ppc, _, _, totc = gpc
    kern = functools.partial(_mlstm_kernel, L=L, H=H, S=s, Lc=lc, dk=dk, dv=dv)

    def colspec(rows, w, off):
        return pl.BlockSpec((1, rows, w), lambda bi, h: (bi, 0, off + h))

    def gspec(n, r, c):
        return pl.BlockSpec((1, n, r, c), lambda bi, h: (bi, 0, 0, 0))

    return pl.pallas_call(
        kern,
        grid=(b, H),
        in_specs=[colspec(s, dk, 0), colspec(s, dk, H), colspec(s, dv, H), colspec(s, dv, 2 * H),
                  colspec(lc, dk, H), colspec(lc, dv, H),
                  gspec(nc, L, LANES), gspec(nc, L, LANES), gspec(nc, LANES, L), gspec(nc, LANES, L),
                  gspec(nc, SUBLANES, LANES),
                  gspec(ncc, L, LANES), gspec(ncc, L, LANES), gspec(ncc, SUBLANES, LANES),
                  pl.BlockSpec((CONV_W, dk), lambda bi, h: (0, h)),
                  pl.BlockSpec((CONV_W, dk), lambda bi, h: (0, H + h)),
                  pl.BlockSpec((1, dv), lambda bi, h: (0, h))],
        out_specs=pl.BlockSpec((1, s, dv), lambda bi, h: (bi, 0, h)),
        out_shape=jax.ShapeDtypeStruct((b, s, H * dv), BF16),
        scratch_shapes=[pltpu.VMEM((s + 2 * CONV_PAD, dk), F32), pltpu.VMEM((s + 2 * CONV_PAD, dk), F32),
                        pltpu.VMEM((lc + 2 * CONV_PAD, dk), F32),
                        pltpu.VMEM((nc, L, dk), BF16), pltpu.VMEM((nc, L, dk), BF16), pltpu.VMEM((ncc, L, dk), BF16),
                        pltpu.VMEM((nc, L, dv), F32), pltpu.VMEM((nc, L, dv), F32),
                        pltpu.VMEM((2, dk, dv), F32), pltpu.VMEM((2, 1, dk), F32),
                        pltpu.VMEM((2, SUBLANES, LANES), F32)],
        compiler_params=_cparams(("arbitrary", "arbitrary")),
        name="mlstm",
    )(ml, ml, ml, ml, mlc, mlc, gl, pp, glt, ppt, tot, glc, ppc, totc, conv_w, conv_w, norm_w)


def _attn_kernel(sink_ref, q_ref, kp_ref, kc_ref, kn_ref, vp_ref, vc_ref, vn_ref, kx_ref, vx_ref, bias_ref,
                 o_ref, *, bq, dh, G):
    q = q_ref[0]
    for g in range(AT_KV_HEADS):
        sl = slice(g * dh, (g + 1) * dh)
        qg = jnp.concatenate([q[:, (g * G + j) * dh:(g * G + j + 1) * dh] for j in range(G)], axis=0)
        kcat = jnp.concatenate([kp_ref[0][:, sl], kc_ref[0][:, sl], kn_ref[0][:, sl], kx_ref[0][:, sl]], axis=0)
        vcat = jnp.concatenate([vp_ref[0][:, sl], vc_ref[0][:, sl], vn_ref[0][:, sl], vx_ref[0][:, sl]], axis=0)
        s = lax.dot_general(qg, kcat, (((1,), (1,)), ((), ())), preferred_element_type=F32) + bias_ref[0]
        ri = lax.broadcasted_iota(I32, (G * bq, 1), 0)
        sk = jnp.zeros((G * bq, 1), F32)
        for j in range(G):
            sk = jnp.where((ri >= j * bq) & (ri < (j + 1) * bq), sink_ref[g * G + j], sk)
        m = jnp.maximum(jnp.max(s, axis=1, keepdims=True), sk)
        p = jnp.exp(s - m)
        den = jnp.sum(p, axis=1, keepdims=True) + jnp.exp(sk - m)
        o = _mm(p.astype(BF16), vcat) / den
        for j in range(G):
            o_ref[0, :, (g * G + j) * dh:(g * G + j + 1) * dh] = o[j * bq:(j + 1) * bq, :].astype(BF16)


def _attn_bias(bq, lc, G):
    qoff = (jnp.arange(G * bq) % bq)[:, None]
    ci = jnp.arange(3 * bq + lc)[None, :]
    prev = (ci < bq) & (qoff <= ci)
    cur = (ci >= bq) & (ci < 2 * bq)
    nxt = (ci >= 2 * bq) & (ci < 3 * bq) & (ci - 2 * bq <= qoff)
    ctx = ci >= 3 * bq
    inner = prev | cur | nxt | ctx
    first = cur | nxt | ctx
    last = prev | cur | ctx
    if WINDOW != bq:
        raise NotImplementedError("window must equal the query block")
    return jnp.where(jnp.stack([inner, first, last]), 0.0, NEG_BIG).astype(F32)


def _attn(aq, ak, av, akc, avc, sink):
    b, s, hd = aq.shape
    dh = hd // AT_HEADS
    G = AT_HEADS // AT_KV_HEADS
    bq = BLOCK_Q
    nb = s // bq
    lc = akc.shape[1]
    kvw = AT_KV_HEADS * dh
    bias = _attn_bias(bq, lc, G)
    if nb < 2:
        raise NotImplementedError("needs at least two query blocks")
    kern = functools.partial(_attn_kernel, bq=bq, dh=dh, G=G)
    prev = pl.BlockSpec((1, bq, kvw), lambda bi, n: (bi, jnp.maximum(n - 1, 0), 0))
    cur = pl.BlockSpec((1, bq, kvw), lambda bi, n: (bi, n, 0))
    nxt = pl.BlockSpec((1, bq, kvw), lambda bi, n: (bi, jnp.minimum(n + 1, nb - 1), 0))
    cx = pl.BlockSpec((1, lc, kvw), lambda bi, n: (bi, 0, 0))
    return pl.pallas_call(
        kern,
        grid=(b, nb),
        in_specs=[pl.BlockSpec(memory_space=pltpu.SMEM),
                  pl.BlockSpec((1, bq, hd), lambda bi, n: (bi, n, 0)),
                  prev, cur, nxt, prev, cur, nxt, cx, cx,
                  pl.BlockSpec((1, G * bq, 3 * bq + lc),
                               lambda bi, n: (jnp.where(n == 0, 1, jnp.where(n == nb - 1, 2, 0)), 0, 0))],
        out_specs=pl.BlockSpec((1, bq, hd), lambda bi, n: (bi, n, 0)),
        out_shape=jax.ShapeDtypeStruct((b, s, hd), BF16),
        compiler_params=_cparams(("arbitrary", "arbitrary")),
        name="attn",
    )(sink, aq, ak, ak, ak, av, av, av, akc, avc, bias)


def _outproj_kernel(ym_ref, ya_ref, x_ref, g1_ref, sh2_ref, sc2_ref, nw1_ref, nw2_ref, wo_ref, wrh_ref, wrl_ref,
                    x1_ref, h2_ref, lg_ref, y_s, *, dm, cw):
    d = y_s.shape[1]
    for c in range(0, d, cw):
        y_s[:, c:c + cw] = _mm(ym_ref[0], wo_ref[0:dm, c:c + cw]) + _mm(ya_ref[0], wo_ref[dm:, c:c + cw])
    x1 = x_ref[0] + g1_ref[0] * _rms(y_s[...], nw1_ref[...])
    x1_ref[0] = x1
    h2 = _rms(x1, nw2_ref[...]) * (1.0 + sc2_ref[0]) + sh2_ref[0]
    hi = h2.astype(BF16)
    h2_ref[0] = _pack_halves(h2)
    lo = (h2 - hi.astype(F32)).astype(BF16)
    lg_ref[0] = _mm(hi, wrh_ref[...]) + (_mm(hi, wrl_ref[...]) + _mm(lo, wrh_ref[...]))


def _outproj(y_ml, y_at, x, g1, sh2, sc2, nw1, nw2, wo, wr, tm):
    b, s, d = x.shape
    dm = y_ml.shape[2]
    e = wr.shape[1]
    wr_hi = wr.astype(BF16)
    wr_lo = (wr - wr_hi.astype(F32)).astype(BF16)
    kern = functools.partial(_outproj_kernel, dm=dm, cw=min(d, 512))
    vec = pl.BlockSpec((1, 1, d), lambda bi, i: (bi, 0, 0))
    nspec = pl.BlockSpec((1, d), lambda bi, i: (0, 0))
    return pl.pallas_call(
        kern,
        grid=(b, s // tm),
        in_specs=[pl.BlockSpec((1, tm, dm), lambda bi, i: (bi, i, 0)),
                  pl.BlockSpec((1, tm, d - dm), lambda bi, i: (bi, i, 0)),
                  pl.BlockSpec((1, tm, d), lambda bi, i: (bi, i, 0)),
                  vec, vec, vec, nspec, nspec,
                  pl.BlockSpec((d, d), lambda bi, i: (0, 0), pipeline_mode=pl.Buffered(1)),
                  pl.BlockSpec((d, e), lambda bi, i: (0, 0), pipeline_mode=pl.Buffered(1)),
                  pl.BlockSpec((d, e), lambda bi, i: (0, 0), pipeline_mode=pl.Buffered(1))],
        out_specs=[pl.BlockSpec((1, tm, d), lambda bi, i: (bi, i, 0)),
                   pl.BlockSpec((1, tm, d // 2), lambda bi, i: (bi, i, 0)),
                   pl.BlockSpec((1, tm, e), lambda bi, i: (bi, i, 0))],
        out_shape=[jax.ShapeDtypeStruct((b, s, d), F32), jax.ShapeDtypeStruct((b, s, d // 2), U32),
                   jax.ShapeDtypeStruct((b, s, e), F32)],
        scratch_shapes=[pltpu.VMEM((tm, d), F32)],
        compiler_params=_cparams(("arbitrary", "arbitrary")),
        name="outproj",
    )(y_ml, y_at, x, g1, sh2, sc2, nw1, nw2, wo, wr_hi, wr_lo)


def _route_kernel(lg_ref, br_ref, eidx_ref, ew_ref, rank_ref, cnt_ref, carry_s, *, E, tr):
    i = pl.program_id(0)

    @pl.when(i == 0)
    def _():
        carry_s[...] = jnp.zeros(carry_s.shape, F32)

    scores = jax.nn.sigmoid(lg_ref[...].T)
    biased = scores + br_ref[:, 0:1]
    row = lax.broadcasted_iota(I32, (E, tr), 0).astype(F32)
    gs = E // N_GROUPS
    ninf = -jnp.inf
    grp = []
    for g in range(N_GROUPS):
        xg = biased[g * gs:(g + 1) * gs]
        rg = (lax.broadcasted_iota(I32, (gs, tr), 0) + g * gs).astype(F32)
        m1 = jnp.max(xg, axis=0, keepdims=True)
        i1 = jnp.min(jnp.where(xg == m1, rg, float(E)), axis=0, keepdims=True)
        m2 = jnp.max(jnp.where(rg == i1, ninf, xg), axis=0, keepdims=True)
        grp.append(m1 + m2)
    gsc = jnp.concatenate(grp, axis=0)
    gi = lax.broadcasted_iota(I32, (N_GROUPS, tr), 0)
    beaten = jnp.zeros((N_GROUPS, tr), I32)
    for g2 in range(N_GROUPS):
        sg = gsc[g2:g2 + 1]
        beaten = beaten + jnp.where((sg > gsc) | ((sg == gsc) & (gi > g2)), 1, 0)
    keep_g = jnp.where(beaten < TOPK_GROUPS, 1.0, 0.0)
    keep = jnp.concatenate([jnp.broadcast_to(keep_g[g:g + 1], (gs, tr)) for g in range(N_GROUPS)], axis=0)
    masked = jnp.where(keep > 0.5, biased, ninf)
    idxs, ws = [], []
    for _ in range(TOP_K):
        m = jnp.max(masked, axis=0, keepdims=True)
        ik = jnp.min(jnp.where(masked == m, row, float(E)), axis=0, keepdims=True)
        sel = row == ik
        ws.append(jnp.sum(jnp.where(sel, scores, 0.0), axis=0, keepdims=True))
        idxs.append(ik)
        masked = jnp.where(sel, ninf, masked)
    w = jnp.concatenate(ws, axis=0)
    ew_ref[...] = w / jnp.sum(w, axis=0, keepdims=True) * ROUTE_SCALE
    eidx_ref[...] = jnp.concatenate(idxs, axis=0).astype(I32)

    assign = jnp.zeros((E, tr), F32)
    for ik in idxs:
        assign = assign + jnp.where(row == ik, 1.0, 0.0)
    upper = jnp.where(lax.broadcasted_iota(I32, (tr, tr), 0) < lax.broadcasted_iota(I32, (tr, tr), 1), 1.0, 0.0)
    base = _mm(assign.astype(BF16), upper.astype(BF16)) + carry_s[:, 0:1]
    ranks = [jnp.sum(jnp.where(row == ik, base, 0.0), axis=0, keepdims=True) for ik in idxs]
    rank_ref[...] = jnp.concatenate(ranks, axis=0).astype(I32)
    carry_s[...] = carry_s[...] + jnp.sum(assign, axis=1, keepdims=True)
    cnt_ref[...] = carry_s[...]


def _route(logits, b_router, tr):
    t, e = logits.shape
    kern = functools.partial(_route_kernel, E=e, tr=tr)
    kt = pl.BlockSpec((TOP_K, tr), lambda i: (0, i))
    return pl.pallas_call(
        kern,
        grid=(t // tr,),
        in_specs=[pl.BlockSpec((tr, e), lambda i: (i, 0)), pl.BlockSpec((e, LANES), lambda i: (0, 0))],
        out_specs=[kt, kt, kt, pl.BlockSpec((e, LANES), lambda i: (0, 0))],
        out_shape=[jax.ShapeDtypeStruct((TOP_K, t), I32), jax.ShapeDtypeStruct((TOP_K, t), F32),
                   jax.ShapeDtypeStruct((TOP_K, t), I32), jax.ShapeDtypeStruct((e, LANES), F32)],
        scratch_shapes=[pltpu.VMEM((e, LANES), F32)],
        compiler_params=_cparams(("arbitrary",)),
        name="route",
    )(logits, b_router)


def _positions_kernel(eidx_ref, rank_ref, ps_ref, pos_ref, *, E, tr):
    row = lax.broadcasted_iota(I32, (E, tr), 0)
    start = ps_ref[:, 0:1]
    rows = [jnp.sum(jnp.where(row == eidx_ref[k:k + 1, :], start, 0.0), axis=0, keepdims=True)
            for k in range(TOP_K)]
    pos_ref[...] = jnp.concatenate(rows, axis=0).astype(I32) + rank_ref[...]


def _positions(eidx, rank, pad_start, tr):
    t = eidx.shape[1]
    e = pad_start.shape[0]
    kern = functools.partial(_positions_kernel, E=e, tr=tr)
    kt = pl.BlockSpec((TOP_K, tr), lambda i: (0, i))
    return pl.pallas_call(
        kern,
        grid=(t // tr,),
        in_specs=[kt, kt, pl.BlockSpec((e, LANES), lambda i: (0, 0))],
        out_specs=kt,
        out_shape=jax.ShapeDtypeStruct((TOP_K, t), I32),
        compiler_params=_cparams(("arbitrary",)),
        name="positions",
    )(eidx, rank, jnp.broadcast_to(pad_start.astype(F32)[:, None], (e, LANES)))


def _dispatch_kernel(zstart_ref, zvalid_ref, pos_hbm, h2_ref, xs_hbm, zero_s, pos_s, sem_p, sem_z, sem_r,
                     *, E, tt):
    i = pl.program_id(0)

    def zero_copy(e):
        start = pl.multiple_of(zstart_ref[e], MOE_BLOCK)
        return pltpu.make_async_copy(zero_s, xs_hbm.at[pl.ds(start, MOE_BLOCK)], sem_z)

    @pl.when(i == 0)
    def _():
        zero_s[...] = jnp.zeros(zero_s.shape, U32)

        def start(e, c):
            @pl.when(zvalid_ref[e] > 0)
            def _():
                zero_copy(e).start()
            return c

        def wait(e, c):
            @pl.when(zvalid_ref[e] > 0)
            def _():
                zero_copy(e).wait()
            return c

        lax.fori_loop(0, E, start, 0)
        lax.fori_loop(0, E, wait, 0)

    pos_copy = pltpu.make_async_copy(pos_hbm.at[i], pos_s, sem_p)
    pos_copy.start()
    pos_copy.wait()

    def row_copy(t, k):
        return pltpu.make_async_copy(h2_ref.at[pl.ds(t, 1)], xs_hbm.at[pl.ds(pos_s[k, t], 1)], sem_r)

    def start_rows(t, c):
        for k in range(TOP_K):
            row_copy(t, k).start(priority=k % 2)
        return c

    def wait_rows(t, c):
        for k in range(TOP_K):
            row_copy(t, k).wait()
        return c

    lax.fori_loop(0, tt, start_rows, 0)
    lax.fori_loop(0, tt, wait_rows, 0)


def _dispatch(h2, pos_tiles, zstart, zvalid, n_rows, tt):
    t, d = h2.shape
    e = zstart.shape[0]
    kern = functools.partial(_dispatch_kernel, E=e, tt=tt)
    return pl.pallas_call(
        kern,
        grid_spec=pltpu.PrefetchScalarGridSpec(
            num_scalar_prefetch=2,
            grid=(t // tt,),
            in_specs=[pl.BlockSpec(memory_space=pl.ANY), pl.BlockSpec((tt, d), lambda i, zs, zv: (i, 0))],
            out_specs=pl.BlockSpec(memory_space=pl.ANY),
            scratch_shapes=[pltpu.VMEM((MOE_BLOCK, d), U32), pltpu.SMEM((TOP_K, tt), I32),
                            pltpu.SemaphoreType.DMA, pltpu.SemaphoreType.DMA, pltpu.SemaphoreType.DMA]),
        out_shape=jax.ShapeDtypeStruct((n_rows, d), U32),
        compiler_params=_cparams(("arbitrary",)),
        name="dispatch",
    )(zstart, zvalid, pos_tiles, h2)


def _experts_kernel(be_ref, nu_ref, nxt_ref, ord_ref, xs_ref, wg_hbm, wu_hbm, wd_hbm, ys_ref,
                    sg_s, su_s, sd_s, wgb_s, wub_s, wdb_s, sem, *, E):
    b = pl.program_id(0)
    n_used = nu_ref[0]

    def copies(e, slot):
        return (pltpu.make_async_copy(wg_hbm.at[e], sg_s.at[slot], sem.at[slot, 0]),
                pltpu.make_async_copy(wu_hbm.at[e], su_s.at[slot], sem.at[slot, 1]),
                pltpu.make_async_copy(wd_hbm.at[e], sd_s.at[slot], sem.at[slot, 2]))

    @pl.when(b < n_used)
    def _():
        e = be_ref[b]
        prev = be_ref[jnp.maximum(b - 1, 0)]
        slot = ord_ref[e] & 1

        @pl.when(b == 0)
        def _():
            for cp in copies(e, slot):
                cp.start()

        @pl.when((b == 0) | (e != prev))
        def _():
            for cp in copies(e, slot):
                cp.wait()
            ne = nxt_ref[e]

            @pl.when(ne < E)
            def _():
                for cp in copies(ne, 1 - slot):
                    cp.start()

            wgb_s[...] = sg_s[slot].astype(BF16)
            wub_s[...] = su_s[slot].astype(BF16)
            wdb_s[...] = sd_s[slot].astype(BF16)

        x_lo, x_hi = (v.astype(BF16) for v in _unpack_halves(xs_ref[...]))
        half = x_lo.shape[1]
        gate = _mm(x_lo, wgb_s[0:half, :]) + _mm(x_hi, wgb_s[half:, :])
        up = _mm(x_lo, wub_s[0:half, :]) + _mm(x_hi, wub_s[half:, :])
        a = _silu(gate) * up
        ys_ref[...] = _pack_halves(_mm(a.astype(BF16), wdb_s[...]))

    @pl.when(b >= n_used)
    def _():
        ys_ref[...] = jnp.zeros(ys_ref.shape, U32)


def _experts(xs, block_e, n_used, next_used, used_ord, wg, wu, wd):
    n_rows, dp = xs.shape
    nblk = n_rows // MOE_BLOCK
    e, d, de = wg.shape
    kern = functools.partial(_experts_kernel, E=e)

    def live(b, be, nu, nx, od):
        return (jnp.minimum(b, jnp.maximum(nu[0] - 1, 0)), 0)

    hbm = pl.BlockSpec(memory_space=pl.ANY)
    return pl.pallas_call(
        kern,
        grid_spec=pltpu.PrefetchScalarGridSpec(
            num_scalar_prefetch=4,
            grid=(nblk,),
            in_specs=[pl.BlockSpec((MOE_BLOCK, dp), live), hbm, hbm, hbm],
            out_specs=pl.BlockSpec((MOE_BLOCK, dp), lambda b, be, nu, nx, od: (b, 0)),
            scratch_shapes=[pltpu.VMEM((2, d, de), F32), pltpu.VMEM((2, d, de), F32), pltpu.VMEM((2, de, d), F32),
                            pltpu.VMEM((d, de), BF16), pltpu.VMEM((d, de), BF16), pltpu.VMEM((de, d), BF16),
                            pltpu.SemaphoreType.DMA((2, 3))]),
        out_shape=jax.ShapeDtypeStruct((n_rows, dp), U32),
        compiler_params=_cparams(("arbitrary",)),
        name="experts",
    )(block_e, n_used, next_used, used_ord, xs, wg, wu, wd)


def _combine_kernel(pos_hbm, ys_hbm, ew_ref, x1_ref, h2_ref, g2_ref, nw3_ref, wsg_ref, wsu_ref, wsd_ref,
                    out_ref, buf_s, pos_s, sem_p, sem_r, *, tc, nt):
    tile = pl.program_id(0) * nt + pl.program_id(1)
    pos_copy = pltpu.make_async_copy(pos_hbm.at[tile], pos_s, sem_p)
    pos_copy.start()
    pos_copy.wait()

    def row_copy(t, k):
        return pltpu.make_async_copy(ys_hbm.at[pl.ds(pos_s[k, t], 1)], buf_s.at[k, pl.ds(t, 1)], sem_r)

    def start_rows(t, c):
        for k in range(TOP_K):
            row_copy(t, k).start(priority=k % 2)
        return c

    def wait_rows(t, c):
        for k in range(TOP_K):
            row_copy(t, k).wait()
        return c

    lax.fori_loop(0, tc, start_rows, 0)
    h_lo, h_hi = (v.astype(BF16) for v in _unpack_halves(h2_ref[0]))
    half = h_lo.shape[1]
    gate = _mm(h_lo, wsg_ref[0:half, :]) + _mm(h_hi, wsg_ref[half:, :])
    up = _mm(h_lo, wsu_ref[0:half, :]) + _mm(h_hi, wsu_ref[half:, :])
    shared = _mm((_silu(gate) * up).astype(BF16), wsd_ref[...])
    lax.fori_loop(0, tc, wait_rows, 0)
    ew = ew_ref[0]
    acc_lo, acc_hi = shared[:, 0:half], shared[:, half:]
    for k in range(TOP_K):
        y_lo, y_hi = _unpack_halves(buf_s[k])
        acc_lo = acc_lo + y_lo * ew[:, k:k + 1]
        acc_hi = acc_hi + y_hi * ew[:, k:k + 1]
    acc = jnp.concatenate([acc_lo, acc_hi], axis=1)
    out_ref[0] = x1_ref[0] + g2_ref[0] * _rms(acc, nw3_ref[...])


def _combine(pos_tiles, ys, ew, x1, h2, g2, nw3, wsg, wsu, wsd, tc):
    b, s, d = x1.shape
    ds_ = wsg.shape[1]
    nt = s // tc
    kern = functools.partial(_combine_kernel, tc=tc, nt=nt)
    tok = lambda w: pl.BlockSpec((1, tc, w), lambda bi, i: (bi, i, 0))
    return pl.pallas_call(
        kern,
        grid=(b, nt),
        in_specs=[pl.BlockSpec(memory_space=pl.ANY), pl.BlockSpec(memory_space=pl.ANY),
                  tok(TOP_K), tok(d), tok(d // 2),
                  pl.BlockSpec((1, 1, d), lambda bi, i: (bi, 0, 0)),
                  pl.BlockSpec((1, d), lambda bi, i: (0, 0)),
                  pl.BlockSpec((d, ds_), lambda bi, i: (0, 0)),
                  pl.BlockSpec((d, ds_), lambda bi, i: (0, 0)),
                  pl.BlockSpec((ds_, d), lambda bi, i: (0, 0))],
        out_specs=tok(d),
        out_shape=jax.ShapeDtypeStruct((b, s, d), F32),
        scratch_shapes=[pltpu.VMEM((TOP_K, tc, d // 2), U32), pltpu.SMEM((TOP_K, tc), I32),
                        pltpu.SemaphoreType.DMA, pltpu.SemaphoreType.DMA],
        compiler_params=_cparams(("arbitrary", "arbitrary")),
        name="combine",
    )(pos_tiles, ys, ew, x1, h2, g2, nw3, wsg, wsu, wsd)


def _tile(n, pref):
    t = min(n, pref)
    if n % t:
        raise NotImplementedError(f"size {n} is not a multiple of tile {t}")
    return t


def kernel(x, c, ctx, c_ctx, w_ada, b_ada, norms, w_in, ml_conv, ml_gate_b, ml_norm_w, attn_sink, w_out,
           w_router, b_router, w_exp_gate, w_exp_up, w_exp_down, w_sh_gate, w_sh_up, w_sh_down):
    if w_ada.shape[0] != 1:
        raise NotImplementedError("single-layer configuration only")
    b, s, d = x.shape
    lc = ctx.shape[1]
    t = b * s
    H = ML_HEADS
    dv = d // 2 // H
    dk = dv // 2
    dh = d // 2 // AT_HEADS
    qk_w, v_w = 2 * H * dk, H * dv
    ng = 4 * H
    aq_w, akv_w = AT_HEADS * dh, AT_KV_HEADS * dh
    e = w_router.shape[-1]
    nw = norms[0]

    rows = -(-(b + 1) // SUBLANES) * SUBLANES
    cond = jnp.zeros((rows, d), F32).at[:b].set(c).at[b].set(c_ctx)
    mods = _ada(cond, w_ada[0], b_ada[0][None, :])
    sh1, sc1, g1, sh2, sc2, g2 = [m[:b, None, :] for m in jnp.split(mods, 6, axis=-1)]
    csh1, csc1 = [jnp.broadcast_to(m[b:b + 1, None, :], (b, 1, d)) for m in jnp.split(mods, 6, axis=-1)[:2]]

    w0 = w_in[0]
    o_ml, o_g, o_q = 0, qk_w + 2 * v_w, qk_w + 2 * v_w + ng
    ml_w = qk_w + 2 * v_w
    wp = jnp.concatenate([w0[:, o_ml:o_g], w0[:, o_q:], w0[:, o_g:o_q], jnp.zeros((d, LANES - ng), F32)],
                         axis=1).astype(BF16)
    layout = (("ml", 0, ml_w), ("aq", ml_w, aq_w), ("ak", ml_w + aq_w, akv_w),
              ("av", ml_w + aq_w + akv_w, akv_w), ("g", ml_w + aq_w + 2 * akv_w, LANES))
    widths = dict(ml=ml_w, aq=aq_w, ak=akv_w, av=akv_w, g=LANES)
    tabs = _rope_tables(s, dh)
    proj = functools.partial(_inproj, layout=layout, widths=widths, dh=dh)
    ml, aq, ak, av, gates = proj(x, sh1, sc1, nw[0:1], wp, tabs, rope=True, tm=_tile(s, 512))
    tm_c = _tile(lc, 512)
    mlc, _, akc, avc, gates_c = proj(ctx, csh1, csc1, nw[0:1], wp, tuple(tb[:tm_c] for tb in tabs),
                                     rope=False, tm=tm_c)

    L = _tile(lc, 256)
    if s % L:
        raise NotImplementedError("sequence must be a multiple of the mLSTM chunk")
    gate_b = jnp.zeros((1, LANES), F32).at[0, :ng].set(ml_gate_b[0])
    gp = _gate_prep(gates, gate_b, L)
    gpc = _gate_prep(gates_c, gate_b, L)
    y_ml = _mlstm(ml, mlc, gp, gpc, ml_conv[0], ml_norm_w[0][None, :], L)

    y_at = _attn(aq, ak, av, akc, avc, attn_sink[0])

    x1, h2, logits = _outproj(y_ml, y_at, x, g1, sh2, sc2, nw[1:2], nw[2:3], w_out[0].astype(BF16),
                              w_router[0], _tile(s, 256))

    eidx, ew, rank, cnt = _route(logits.reshape(t, e), jnp.broadcast_to(b_router[0][:, None], (e, LANES)),
                                 _tile(t, 512))
    counts = cnt[:, 0].astype(I32)
    padded = (counts + MOE_BLOCK - 1) // MOE_BLOCK * MOE_BLOCK
    pad_end = jnp.cumsum(padded)
    pad_start = pad_end - padded
    n_blocks = -(-(t * TOP_K + e * (MOE_BLOCK - 1)) // MOE_BLOCK)
    n_rows = n_blocks * MOE_BLOCK
    blk_row = jnp.arange(n_blocks, dtype=I32) * MOE_BLOCK
    block_e = jnp.minimum(jnp.sum((pad_end[None, :] <= blk_row[:, None]).astype(I32), axis=1), e - 1)
    n_used = (pad_end[-1] // MOE_BLOCK).astype(I32)[None]
    used = counts > 0
    eid = jnp.arange(e, dtype=I32)
    next_used = jnp.concatenate([lax.cummin(jnp.where(used, eid, e), reverse=True)[1:], jnp.full((1,), e, I32)])
    used_ord = jnp.cumsum(used.astype(I32)) - 1
    pos = _positions(eidx, rank, pad_start, _tile(t, 512))

    tt = _tile(t, 256)
    pos_d = pos.reshape(TOP_K, t // tt, tt).transpose(1, 0, 2)
    xs = _dispatch(h2.reshape(t, d // 2), pos_d, (pad_end - MOE_BLOCK).astype(I32), (padded > 0).astype(I32),
                   n_rows, tt)
    ys = _experts(xs, block_e, n_used, next_used, used_ord, w_exp_gate[0], w_exp_up[0], w_exp_down[0])
    tc = _tile(s, 128)
    pos_c = pos.reshape(TOP_K, t // tc, tc).transpose(1, 0, 2)
    ew_t = ew.T.reshape(b, s, TOP_K)
    return _combine(pos_c, ys, ew_t, x1, h2, g2, nw[3:4], w_sh_gate[0].astype(BF16), w_sh_up[0].astype(BF16),
                    w_sh_down[0].astype(BF16), tc)
```

```python
import functools

import jax
import jax.numpy as jnp
from jax import lax
from jax.experimental import pallas as pl
from jax.experimental.pallas import tpu as pltpu

F32 = jnp.float32
BF16 = jnp.bfloat16
I32 = jnp.int32

LANES = 128
SUBLANES = 8
VMEM_LIMIT_BYTES = 56 * 1024 * 1024

NORM_EPS = 1e-6
ML_HEADS = 4
CONV_W = 5
AT_HEADS = 8
AT_KV_HEADS = 2
GRID_W = 64
WINDOW = 128
BLOCK_Q = 128
ROPE_THETA = 10000.0
N_GROUPS = 8
TOPK_GROUPS = 4
TOP_K = 8
ROUTE_SCALE = 2.5
MOE_BLOCK = 128
NEG_BIG = -1e30


def _cparams(sem):
    return pltpu.CompilerParams(dimension_semantics=sem, vmem_limit_bytes=VMEM_LIMIT_BYTES)


def _rms(xf, w):
    return xf * lax.rsqrt(jnp.mean(xf * xf, axis=-1, keepdims=True) + NORM_EPS) * w


def _silu(x):
    return x * jax.nn.sigmoid(x)


def _mm(a, b):
    return jnp.dot(a, b, preferred_element_type=F32)


U32 = jnp.uint32


def _pack_halves(x):
    n = x.shape[1] // 2
    u = lax.bitcast_convert_type(x.astype(BF16).astype(F32), U32)
    return (u[:, :n] >> 16) | (u[:, n:] & jnp.uint32(0xFFFF0000))


def _unpack_halves(u):
    return (lax.bitcast_convert_type(u << 16, F32),
            lax.bitcast_convert_type(u & jnp.uint32(0xFFFF0000), F32))


def _store_row_tiled(ref2d, packed):
    r, n = packed.shape
    c = n // LANES
    for j in range(c):
        ref2d[pl.ds(j, r, stride=c), :] = packed[:, j * LANES:(j + 1) * LANES]


def _load_row_tiled(ref2d, r):
    c = ref2d.shape[0] // r
    return [ref2d[pl.ds(j, r, stride=c), :] for j in range(c)]


def _ada_kernel(c_ref, w_ref, b_ref, o_ref):
    s = _silu(c_ref[...]).astype(BF16)
    o_ref[...] = _mm(s, w_ref[...].astype(BF16)) + b_ref[...]


def _ada(cond, w, b):
    r, d = cond.shape
    n = w.shape[1]
    tn = min(n, 1024)
    return pl.pallas_call(
        _ada_kernel,
        grid=(n // tn,),
        in_specs=[pl.BlockSpec((r, d), lambda j: (0, 0)),
                  pl.BlockSpec((d, tn), lambda j: (0, j)),
                  pl.BlockSpec((1, tn), lambda j: (0, j))],
        out_specs=pl.BlockSpec((r, tn), lambda j: (0, j)),
        out_shape=jax.ShapeDtypeStruct((r, n), F32),
        compiler_params=_cparams(("arbitrary",)),
        name="ada",
    )(cond, w, b)


def _inproj_kernel(x_ref, sh_ref, sc_ref, nw_ref, w_ref, rc_ref, ra_ref, rb_ref,
                   ml_ref, aq_ref, ak_ref, av_ref, g_ref, hb_ref, *, layout, dh, rope, qscale, cw):
    h = _rms(x_ref[0], nw_ref[...]) * (1.0 + sc_ref[0]) + sh_ref[0]
    hb_ref[...] = h.astype(BF16)
    outs = dict(ml=ml_ref, aq=aq_ref, ak=ak_ref, av=av_ref, g=g_ref)
    for name, col0, width in layout:
        o_ref = outs[name]
        for c in range(0, width, cw):
            step = min(cw, width - c)
            acc = _mm(hb_ref[...], w_ref[:, col0 + c:col0 + c + step])
            if rope and name in ("aq", "ak"):
                for hh in range(step // dh):
                    a = acc[:, hh * dh:(hh + 1) * dh]
                    r = (a * rc_ref[...] + pltpu.roll(a, dh - dh // 4, 1) * ra_ref[...]
                         + pltpu.roll(a, dh // 4, 1) * rb_ref[...])
                    if name == "aq":
                        r = r * qscale
                    o_ref[0, :, c + hh * dh:c + (hh + 1) * dh] = r.astype(o_ref.dtype)
            else:
                if name == "aq":
                    acc = acc * qscale
                o_ref[0, :, c:c + step] = acc.astype(o_ref.dtype)


def _inproj(x, sh, sc, nw, wp, tabs, *, layout, widths, dh, rope, tm):
    b, s, d = x.shape
    npad = wp.shape[1]
    rc, ra, rb = tabs
    kern = functools.partial(_inproj_kernel, layout=layout, dh=dh, rope=rope, qscale=float(dh) ** -0.5, cw=512)
    vec = pl.BlockSpec((1, 1, d), lambda bi, i: (bi, 0, 0))
    tab = pl.BlockSpec((tm, dh), lambda bi, i: (i, 0))
    out_dt = dict(ml=BF16, aq=BF16, ak=BF16, av=BF16, g=F32)
    names = ("ml", "aq", "ak", "av", "g")
    return pl.pallas_call(
        kern,
        grid=(b, s // tm),
        in_specs=[pl.BlockSpec((1, tm, d), lambda bi, i: (bi, i, 0)), vec, vec,
                  pl.BlockSpec((1, d), lambda bi, i: (0, 0)),
                  pl.BlockSpec((d, npad), lambda bi, i: (0, 0), pipeline_mode=pl.Buffered(1)),
                  tab, tab, tab],
        out_specs=[pl.BlockSpec((1, tm, widths[n]), lambda bi, i: (bi, i, 0)) for n in names],
        out_shape=[jax.ShapeDtypeStruct((b, s, widths[n]), out_dt[n]) for n in names],
        scratch_shapes=[pltpu.VMEM((tm, d), BF16)],
        compiler_params=_cparams(("arbitrary", "arbitrary")),
        name="inproj_rope" if rope else "inproj_ctx",
    )(x, sh, sc, nw, wp, rc, ra, rb)


def _rope_tables(s, dh):
    rows = s // GRID_W
    row = jnp.repeat(jnp.arange(rows), GRID_W)
    col = jnp.tile(jnp.arange(GRID_W), rows)
    nf = dh // 4
    freqs = ROPE_THETA ** (-jnp.arange(nf, dtype=F32) / nf)
    pos = jnp.stack([row, col], axis=-1).astype(F32)
    ang = pos[:, :, None] * freqs
    cos, sin = jnp.cos(ang), jnp.sin(ang)
    z = jnp.zeros_like(sin[:, 0])
    rc = jnp.concatenate([cos[:, 0], cos[:, 0], cos[:, 1], cos[:, 1]], axis=-1)
    ra = jnp.concatenate([-sin[:, 0], z, -sin[:, 1], z], axis=-1)
    rb = jnp.concatenate([z, sin[:, 0], z, sin[:, 1]], axis=-1)
    return rc, ra, rb


def _gate_prep_kernel(g_ref, b_ref, gl_ref, pp_ref, glt_ref, ppt_ref, tot_ref, *, L, H):
    z = g_ref[0] + b_ref[...]
    lane = lax.broadcasted_iota(I32, z.shape, 1)
    row = lax.broadcasted_iota(I32, z.shape, 0)
    is_f = ((lane // H) % 2 == 1) & (lane < 4 * H)
    log_sig = jnp.minimum(z, 0.0) - jnp.log1p(jnp.exp(-jnp.abs(z)))
    gl = jnp.where(lane < 4 * H, jnp.where(is_f, log_sig, z), 0.0)
    cs = gl
    k = 1
    while k < L:
        cs = cs + jnp.where(row >= k, pltpu.roll(cs, k, 0), 0.0)
        k *= 2
    tot = cs[L - 1:L, :]
    suf = tot - cs + gl
    pp = jnp.where(lane >= 2 * H, suf, cs)
    gl_ref[0, 0] = gl
    pp_ref[0, 0] = pp
    glt_ref[0, 0] = gl.T
    ppt_ref[0, 0] = pp.T
    tot_ref[0, 0] = jnp.broadcast_to(tot, (SUBLANES, LANES))


def _gate_prep(g, gate_b, L):
    b, s, _ = g.shape
    nc = s // L
    kern = functools.partial(_gate_prep_kernel, L=L, H=ML_HEADS)
    col = pl.BlockSpec((1, 1, L, LANES), lambda bi, c: (bi, c, 0, 0))
    rowb = pl.BlockSpec((1, 1, LANES, L), lambda bi, c: (bi, c, 0, 0))
    return pl.pallas_call(
        kern,
        grid=(b, nc),
        in_specs=[pl.BlockSpec((1, L, LANES), lambda bi, c: (bi, c, 0)),
                  pl.BlockSpec((1, LANES), lambda bi, c: (0, 0))],
        out_specs=[col, col, rowb, rowb, pl.BlockSpec((1, 1, SUBLANES, LANES), lambda bi, c: (bi, c, 0, 0))],
        out_shape=[jax.ShapeDtypeStruct((b, nc, L, LANES), F32), jax.ShapeDtypeStruct((b, nc, L, LANES), F32),
                   jax.ShapeDtypeStruct((b, nc, LANES, L), F32), jax.ShapeDtypeStruct((b, nc, LANES, L), F32),
                   jax.ShapeDtypeStruct((b, nc, SUBLANES, LANES), F32)],
        compiler_params=_cparams(("arbitrary", "arbitrary")),
        name="gate_prep",
    )(g, gate_b)


CONV_PAD = 8


def _lane_pick(tile, ch):
    lane = lax.broadcasted_iota(I32, tile.shape, 1)
    return jnp.sum(jnp.where(lane == ch, tile, 0.0), axis=1, keepdims=True)


def _sublane_pick(tile, ch):
    sub = lax.broadcasted_iota(I32, tile.shape, 0)
    return jnp.sum(jnp.where(sub == ch, tile, 0.0), axis=0, keepdims=True)


def _mlstm_chunk(d, ch_i, ch_f, q, k, v, gl, pp, glt, ppt, tot8, ct_ref, n_ref, m_ref):
    L = k.shape[0]
    p_col = _lane_pick(pp, ch_f)
    li_col = _lane_pick(gl, ch_i)
    btot = _lane_pick(tot8[0:1], ch_f)
    m_old = m_ref[d][0:1, 0:1]
    ct = ct_ref[d]
    n_row = n_ref[d]
    h = None
    if q is not None:
        p_row = _sublane_pick(ppt, ch_f)
        li_row = _sublane_pick(glt, ch_i)
        ti = lax.broadcasted_iota(I32, (L, L), 0)
        si = lax.broadcasted_iota(I32, (L, L), 1)
        valid = (si <= ti) if d == 0 else (si >= ti)
        logw = jnp.where(valid, p_col - p_row + li_row, NEG_BIG)
        log_inter = p_col + m_old
        m_q = jnp.maximum(log_inter, jnp.max(logw, axis=1, keepdims=True))
        sqk = lax.dot_general(q, k, (((1,), (1,)), ((), ())), preferred_element_type=F32) * jnp.exp(logw - m_q)
        w_inter = jnp.exp(log_inter - m_q)
        num = _mm(sqk.astype(BF16), v) + w_inter * _mm(q, ct.astype(BF16))
        den = (jnp.sum(sqk, axis=1, keepdims=True)
               + w_inter * jnp.sum(q.astype(F32) * n_row, axis=1, keepdims=True))
        h = num / jnp.maximum(jnp.abs(den), jnp.exp(-m_q))
    log_w_end = btot - p_col + li_col
    m_new = jnp.maximum(btot + m_old, jnp.max(log_w_end, axis=0, keepdims=True))
    w_end = jnp.exp(log_w_end - m_new)
    decay = jnp.exp(btot + m_old - m_new)
    kw = k.astype(F32) * w_end
    ct_ref[d] = decay * ct + lax.dot_general(kw.astype(BF16), v, (((0,), (0,)), ((), ())),
                                             preferred_element_type=F32)
    n_ref[d] = decay * n_row + jnp.sum(kw, axis=0, keepdims=True)
    m_ref[d] = jnp.broadcast_to(m_new, (SUBLANES, LANES))
    return h


def _mlstm_kernel(q_ref, k_ref, v_ref, o_ref, kc_ref, vc_ref,
                  gl_ref, pp_ref, glt_ref, ppt_ref, tot_ref, glc_ref, ppc_ref, totc_ref,
                  cwq_ref, cwk_ref, nw_ref, y_ref,
                  xq_s, xk_s, xkc_s, qs_s, ks_s, kcs_s, hf_s, hb_s, ct_s, n_s, m_s,
                  *, L, H, S, Lc, dk, dv):
    hh = pl.program_id(1)
    nc, ncc = S // L, Lc // L
    zpad = jnp.zeros((CONV_PAD, dk), F32)

    def conv_silu(x_s, cw_ref, n_rows, out_s, scale):
        for c in range(n_rows // L):
            acc = jnp.zeros((L, dk), F32)
            for j in range(CONV_W):
                acc = acc + cw_ref[j:j + 1, :] * x_s[pl.ds(CONV_PAD + c * L + j - CONV_W // 2, L), :]
            out_s[c] = (_silu(acc) * scale).astype(BF16)

    def stage(x_s, src, n_rows):
        x_s[0:CONV_PAD, :] = zpad
        x_s[CONV_PAD + n_rows:2 * CONV_PAD + n_rows, :] = zpad
        x_s[CONV_PAD:CONV_PAD + n_rows, :] = src.astype(F32)

    stage(xq_s, q_ref[0], S)
    stage(xk_s, k_ref[0], S)
    stage(xkc_s, kc_ref[0], Lc)
    conv_silu(xq_s, cwq_ref, S, qs_s, float(dk) ** -0.5)
    conv_silu(xk_s, cwk_ref, S, ks_s, 1.0)
    conv_silu(xkc_s, cwk_ref, Lc, kcs_s, 1.0)

    ct_s[...] = jnp.zeros(ct_s.shape, F32)
    n_s[...] = jnp.zeros(n_s.shape, F32)
    m_s[...] = jnp.zeros(m_s.shape, F32)

    def chans(d):
        return d * 2 * H + hh, d * 2 * H + H + hh

    for d in (0, 1):
        ch_i, ch_f = chans(d)
        for c in (range(ncc) if d == 0 else range(ncc - 1, -1, -1)):
            _mlstm_chunk(d, ch_i, ch_f, None, kcs_s[c], vc_ref[0, c * L:(c + 1) * L, :],
                         glc_ref[0, c], ppc_ref[0, c], None, None, totc_ref[0, c], ct_s, n_s, m_s)

    def body(i, carry):
        for d in (0, 1):
            ch_i, ch_f = chans(d)
            c = i if d == 0 else nc - 1 - i
            r0 = pl.multiple_of(c * L, L)
            h = _mlstm_chunk(d, ch_i, ch_f, qs_s[c], ks_s[c], v_ref[0, pl.ds(r0, L), :],
                             gl_ref[0, c], pp_ref[0, c], glt_ref[0, c], ppt_ref[0, c], tot_ref[0, c],
                             ct_s, n_s, m_s)
            if d == 0:
                hf_s[c] = h
            else:
                hb_s[c] = h
        return carry

    lax.fori_loop(0, nc, body, 0)

    for c in range(nc):
        hs = hf_s[c] + hb_s[c]
        hn = _rms(hs, nw_ref[...])
        y_ref[0, c * L:(c + 1) * L, :] = (hn * jax.nn.sigmoid(o_ref[0, c * L:(c + 1) * L, :].astype(F32))).astype(BF16)


def _mlstm(ml, mlc, gp, gpc, conv_w, norm_w, L):
    b, s, _ = ml.shape
    lc = mlc.shape[1]
    H = ML_HEADS
    dv = norm_w.shape[1] // H
    dk = dv // 2
    nc, ncc = s // L, lc // L
    gl, pp, glt, ppt, tot = gp
    glc, ppc, _, _, totc = gpc
    kern = functools.partial(_mlstm_kernel, L=L, H=H, S=s, Lc=lc, dk=dk, dv=dv)

    def colspec(rows, w, off):
        return pl.BlockSpec((1, rows, w), lambda bi, h: (bi, 0, off + h))

    def gspec(n, r, c):
        return pl.BlockSpec((1, n, r, c), lambda bi, h: (bi, 0, 0, 0))

    return pl.pallas_call(
        kern,
        grid=(b, H),
        in_specs=[colspec(s, dk, 0), colspec(s, dk, H), colspec(s, dv, H), colspec(s, dv, 2 * H),
                  colspec(lc, dk, H), colspec(lc, dv, H),
                  gspec(nc, L, LANES), gspec(nc, L, LANES), gspec(nc, LANES, L), gspec(nc, LANES, L),
                  gspec(nc, SUBLANES, LANES),
                  gspec(ncc, L, LANES), gspec(ncc, L, LANES), gspec(ncc, SUBLANES, LANES),
                  pl.BlockSpec((CONV_W, dk), lambda bi, h: (0, h)),
                  pl.BlockSpec((CONV_W, dk), lambda bi, h: (0, H + h)),
                  pl.BlockSpec((1, dv), lambda bi, h: (0, h))],
        out_specs=pl.BlockSpec((1, s, dv), lambda bi, h: (bi, 0, h)),
        out_shape=jax.ShapeDtypeStruct((b, s, H * dv), BF16),
        scratch_shapes=[pltpu.VMEM((s + 2 * CONV_PAD, dk), F32), pltpu.VMEM((s + 2 * CONV_PAD, dk), F32),
                        pltpu.VMEM((lc + 2 * CONV_PAD, dk), F32),
                        pltpu.VMEM((nc, L, dk), BF16), pltpu.VMEM((nc, L, dk), BF16), pltpu.VMEM((ncc, L, dk), BF16),
                        pltpu.VMEM((nc, L, dv), F32), pltpu.VMEM((nc, L, dv), F32),
                        pltpu.VMEM((2, dk, dv), F32), pltpu.VMEM((2, 1, dk), F32),
                        pltpu.VMEM((2, SUBLANES, LANES), F32)],
        compiler_params=_cparams(("arbitrary", "arbitrary")),
        name="mlstm",
    )(ml, ml, ml, ml, mlc, mlc, gl, pp, glt, ppt, tot, glc, ppc, totc, conv_w, conv_w, norm_w)


def _attn_kernel(sink_ref, q_ref, kp_ref, kc_ref, kn_ref, vp_ref, vc_ref, vn_ref, kx_ref, vx_ref, bias_ref,
                 o_ref, *, bq, dh, G):
    q = q_ref[0]
    for g in range(AT_KV_HEADS):
        sl = slice(g * dh, (g + 1) * dh)
        qg = jnp.concatenate([q[:, (g * G + j) * dh:(g * G + j + 1) * dh] for j in range(G)], axis=0)
        kcat = jnp.concatenate([kp_ref[0][:, sl], kc_ref[0][:, sl], kn_ref[0][:, sl], kx_ref[0][:, sl]], axis=0)
        vcat = jnp.concatenate([vp_ref[0][:, sl], vc_ref[0][:, sl], vn_ref[0][:, sl], vx_ref[0][:, sl]], axis=0)
        s = lax.dot_general(qg, kcat, (((1,), (1,)), ((), ())), preferred_element_type=F32) + bias_ref[0]
        ri = lax.broadcasted_iota(I32, (G * bq, 1), 0)
        sk = jnp.zeros((G * bq, 1), F32)
        for j in range(G):
            sk = jnp.where((ri >= j * bq) & (ri < (j + 1) * bq), sink_ref[g * G + j], sk)
        m = jnp.maximum(jnp.max(s, axis=1, keepdims=True), sk)
        p = jnp.exp(s - m)
        den = jnp.sum(p, axis=1, keepdims=True) + jnp.exp(sk - m)
        o = _mm(p.astype(BF16), vcat) / den
        for j in range(G):
            o_ref[0, :, (g * G + j) * dh:(g * G + j + 1) * dh] = o[j * bq:(j + 1) * bq, :].astype(BF16)


def _attn_bias(bq, lc, G):
    qoff = (jnp.arange(G * bq) % bq)[:, None]
    ci = jnp.arange(3 * bq + lc)[None, :]
    prev = (ci < bq) & (qoff <= ci)
    cur = (ci >= bq) & (ci < 2 * bq)
    nxt = (ci >= 2 * bq) & (ci < 3 * bq) & (ci - 2 * bq <= qoff)
    ctx = ci >= 3 * bq
    inner = prev | cur | nxt | ctx
    first = cur | nxt | ctx
    last = prev | cur | ctx
    if WINDOW != bq:
        raise NotImplementedError("window must equal the query block")
    return jnp.where(jnp.stack([inner, first, last]), 0.0, NEG_BIG).astype(F32)


def _attn(aq, ak, av, akc, avc, sink):
    b, s, hd = aq.shape
    dh = hd // AT_HEADS
    G = AT_HEADS // AT_KV_HEADS
    bq = BLOCK_Q
    nb = s // bq
    lc = akc.shape[1]
    kvw = AT_KV_HEADS * dh
    bias = _attn_bias(bq, lc, G)
    if nb < 2:
        raise NotImplementedError("needs at least two query blocks")
    kern = functools.partial(_attn_kernel, bq=bq, dh=dh, G=G)
    prev = pl.BlockSpec((1, bq, kvw), lambda bi, n: (bi, jnp.maximum(n - 1, 0), 0))
    cur = pl.BlockSpec((1, bq, kvw), lambda bi, n: (bi, n, 0))
    nxt = pl.BlockSpec((1, bq, kvw), lambda bi, n: (bi, jnp.minimum(n + 1, nb - 1), 0))
    cx = pl.BlockSpec((1, lc, kvw), lambda bi, n: (bi, 0, 0))
    return pl.pallas_call(
        kern,
        grid=(b, nb),
        in_specs=[pl.BlockSpec(memory_space=pltpu.SMEM),
                  pl.BlockSpec((1, bq, hd), lambda bi, n: (bi, n, 0)),
                  prev, cur, nxt, prev, cur, nxt, cx, cx,
                  pl.BlockSpec((1, G * bq, 3 * bq + lc),
                               lambda bi, n: (jnp.where(n == 0, 1, jnp.where(n == nb - 1, 2, 0)), 0, 0))],
        out_specs=pl.BlockSpec((1, bq, hd), lambda bi, n: (bi, n, 0)),
        out_shape=jax.ShapeDtypeStruct((b, s, hd), BF16),
        compiler_params=_cparams(("arbitrary", "arbitrary")),
        name="attn",
    )(sink, aq, ak, ak, ak, av, av, av, akc, avc, bias)


def _outproj_kernel(ym_ref, ya_ref, x_ref, g1_ref, sh2_ref, sc2_ref, nw1_ref, nw2_ref, wo_ref, wrh_ref, wrl_ref,
                    x1_ref, h2_ref, lg_ref, y_s, *, dm, cw):
    d = y_s.shape[1]
    for c in range(0, d, cw):
        y_s[:, c:c + cw] = _mm(ym_ref[0], wo_ref[0:dm, c:c + cw]) + _mm(ya_ref[0], wo_ref[dm:, c:c + cw])
    x1 = x_ref[0] + g1_ref[0] * _rms(y_s[...], nw1_ref[...])
    x1_ref[0] = x1
    h2 = _rms(x1, nw2_ref[...]) * (1.0 + sc2_ref[0]) + sh2_ref[0]
    hi = h2.astype(BF16)
    _store_row_tiled(h2_ref.at[0], _pack_halves(h2))
    lo = (h2 - hi.astype(F32)).astype(BF16)
    lg_ref[0] = _mm(hi, wrh_ref[...]) + (_mm(hi, wrl_ref[...]) + _mm(lo, wrh_ref[...]))


def _outproj(y_ml, y_at, x, g1, sh2, sc2, nw1, nw2, wo, wr, tm):
    b, s, d = x.shape
    dm = y_ml.shape[2]
    e = wr.shape[1]
    wr_hi = wr.astype(BF16)
    wr_lo = (wr - wr_hi.astype(F32)).astype(BF16)
    kern = functools.partial(_outproj_kernel, dm=dm, cw=min(d, 512))
    vec = pl.BlockSpec((1, 1, d), lambda bi, i: (bi, 0, 0))
    nspec = pl.BlockSpec((1, d), lambda bi, i: (0, 0))
    return pl.pallas_call(
        kern,
        grid=(b, s // tm),
        in_specs=[pl.BlockSpec((1, tm, dm), lambda bi, i: (bi, i, 0)),
                  pl.BlockSpec((1, tm, d - dm), lambda bi, i: (bi, i, 0)),
                  pl.BlockSpec((1, tm, d), lambda bi, i: (bi, i, 0)),
                  vec, vec, vec, nspec, nspec,
                  pl.BlockSpec((d, d), lambda bi, i: (0, 0), pipeline_mode=pl.Buffered(1)),
                  pl.BlockSpec((d, e), lambda bi, i: (0, 0), pipeline_mode=pl.Buffered(1)),
                  pl.BlockSpec((d, e), lambda bi, i: (0, 0), pipeline_mode=pl.Buffered(1))],
        out_specs=[pl.BlockSpec((1, tm, d), lambda bi, i: (bi, i, 0)),
                   pl.BlockSpec((1, tm * (d // 2 // LANES), LANES), lambda bi, i: (bi, i, 0)),
                   pl.BlockSpec((1, tm, e), lambda bi, i: (bi, i, 0))],
        out_shape=[jax.ShapeDtypeStruct((b, s, d), F32), jax.ShapeDtypeStruct((b, s * (d // 2 // LANES), LANES), U32),
                   jax.ShapeDtypeStruct((b, s, e), F32)],
        scratch_shapes=[pltpu.VMEM((tm, d), F32)],
        compiler_params=_cparams(("arbitrary", "arbitrary")),
        name="outproj",
    )(y_ml, y_at, x, g1, sh2, sc2, nw1, nw2, wo, wr_hi, wr_lo)


def _route_kernel(lg_ref, br_ref, eidx_ref, ew_ref, rank_ref, cnt_ref, carry_s, *, E, tr):
    i = pl.program_id(0)

    @pl.when(i == 0)
    def _():
        carry_s[...] = jnp.zeros(carry_s.shape, F32)

    scores = jax.nn.sigmoid(lg_ref[...].T)
    biased = scores + br_ref[:, 0:1]
    row = lax.broadcasted_iota(I32, (E, tr), 0).astype(F32)
    gs = E // N_GROUPS
    ninf = -jnp.inf
    grp = []
    for g in range(N_GROUPS):
        xg = biased[g * gs:(g + 1) * gs]
        rg = (lax.broadcasted_iota(I32, (gs, tr), 0) + g * gs).astype(F32)
        m1 = jnp.max(xg, axis=0, keepdims=True)
        i1 = jnp.min(jnp.where(xg == m1, rg, float(E)), axis=0, keepdims=True)
        m2 = jnp.max(jnp.where(rg == i1, ninf, xg), axis=0, keepdims=True)
        grp.append(m1 + m2)
    gsc = jnp.concatenate(grp, axis=0)
    gi = lax.broadcasted_iota(I32, (N_GROUPS, tr), 0)
    beaten = jnp.zeros((N_GROUPS, tr), I32)
    for g2 in range(N_GROUPS):
        sg = gsc[g2:g2 + 1]
        beaten = beaten + jnp.where((sg > gsc) | ((sg == gsc) & (gi > g2)), 1, 0)
    keep_g = jnp.where(beaten < TOPK_GROUPS, 1.0, 0.0)
    keep = jnp.concatenate([jnp.broadcast_to(keep_g[g:g + 1], (gs, tr)) for g in range(N_GROUPS)], axis=0)
    masked = jnp.where(keep > 0.5, biased, ninf)
    idxs, ws = [], []
    for _ in range(TOP_K):
        m = jnp.max(masked, axis=0, keepdims=True)
        ik = jnp.min(jnp.where(masked == m, row, float(E)), axis=0, keepdims=True)
        sel = row == ik
        ws.append(jnp.sum(jnp.where(sel, scores, 0.0), axis=0, keepdims=True))
        idxs.append(ik)
        masked = jnp.where(sel, ninf, masked)
    w = jnp.concatenate(ws, axis=0)
    ew_ref[...] = w / jnp.sum(w, axis=0, keepdims=True) * ROUTE_SCALE
    eidx_ref[...] = jnp.concatenate(idxs, axis=0).astype(I32)

    assign = jnp.zeros((E, tr), F32)
    for ik in idxs:
        assign = assign + jnp.where(row == ik, 1.0, 0.0)
    upper = jnp.where(lax.broadcasted_iota(I32, (tr, tr), 0) < lax.broadcasted_iota(I32, (tr, tr), 1), 1.0, 0.0)
    base = _mm(assign.astype(BF16), upper.astype(BF16)) + carry_s[:, 0:1]
    ranks = [jnp.sum(jnp.where(row == ik, base, 0.0), axis=0, keepdims=True) for ik in idxs]
    rank_ref[...] = jnp.concatenate(ranks, axis=0).astype(I32)
    carry_s[...] = carry_s[...] + jnp.sum(assign, axis=1, keepdims=True)
    cnt_ref[...] = carry_s[...]


def _route(logits, b_router, tr):
    t, e = logits.shape
    kern = functools.partial(_route_kernel, E=e, tr=tr)
    kt = pl.BlockSpec((TOP_K, tr), lambda i: (0, i))
    return pl.pallas_call(
        kern,
        grid=(t // tr,),
        in_specs=[pl.BlockSpec((tr, e), lambda i: (i, 0)), pl.BlockSpec((e, LANES), lambda i: (0, 0))],
        out_specs=[kt, kt, kt, pl.BlockSpec((e, LANES), lambda i: (0, 0))],
        out_shape=[jax.ShapeDtypeStruct((TOP_K, t), I32), jax.ShapeDtypeStruct((TOP_K, t), F32),
                   jax.ShapeDtypeStruct((TOP_K, t), I32), jax.ShapeDtypeStruct((e, LANES), F32)],
        scratch_shapes=[pltpu.VMEM((e, LANES), F32)],
        compiler_params=_cparams(("arbitrary",)),
        name="route",
    )(logits, b_router)


def _positions_kernel(eidx_ref, rank_ref, ps_ref, pos_ref, *, E, tr):
    row = lax.broadcasted_iota(I32, (E, tr), 0)
    start = ps_ref[:, 0:1]
    rows = [jnp.sum(jnp.where(row == eidx_ref[k:k + 1, :], start, 0.0), axis=0, keepdims=True)
            for k in range(TOP_K)]
    pos_ref[...] = jnp.concatenate(rows, axis=0).astype(I32) + rank_ref[...]


def _positions(eidx, rank, pad_start, tr):
    t = eidx.shape[1]
    e = pad_start.shape[0]
    kern = functools.partial(_positions_kernel, E=e, tr=tr)
    kt = pl.BlockSpec((TOP_K, tr), lambda i: (0, i))
    return pl.pallas_call(
        kern,
        grid=(t // tr,),
        in_specs=[kt, kt, pl.BlockSpec((e, LANES), lambda i: (0, 0))],
        out_specs=kt,
        out_shape=jax.ShapeDtypeStruct((TOP_K, t), I32),
        compiler_params=_cparams(("arbitrary",)),
        name="positions",
    )(eidx, rank, jnp.broadcast_to(pad_start.astype(F32)[:, None], (e, LANES)))


def _dispatch_kernel(zstart_ref, zvalid_ref, pos_hbm, h2_ref, xs_hbm, zero_s, pos_s, sem_p, sem_z, sem_r,
                     *, E, tt):
    i = pl.program_id(0)

    def zero_copy(e):
        start = pl.multiple_of(zstart_ref[e], MOE_BLOCK)
        return pltpu.make_async_copy(zero_s, xs_hbm.at[pl.ds(start, MOE_BLOCK)], sem_z)

    @pl.when(i == 0)
    def _():
        zero_s[...] = jnp.zeros(zero_s.shape, U32)

        def start(e, c):
            @pl.when(zvalid_ref[e] > 0)
            def _():
                zero_copy(e).start()
            return c

        def wait(e, c):
            @pl.when(zvalid_ref[e] > 0)
            def _():
                zero_copy(e).wait()
            return c

        lax.fori_loop(0, E, start, 0)
        lax.fori_loop(0, E, wait, 0)

    pos_copy = pltpu.make_async_copy(pos_hbm.at[i], pos_s, sem_p)
    pos_copy.start()
    pos_copy.wait()

    def row_copy(t, k):
        return pltpu.make_async_copy(h2_ref.at[t], xs_hbm.at[pos_s[k, t]], sem_r)

    def start_rows(t, c):
        for k in range(TOP_K):
            row_copy(t, k).start(priority=k % 2)
        return c

    def wait_rows(t, c):
        for k in range(TOP_K):
            row_copy(t, k).wait()
        return c

    lax.fori_loop(0, tt, start_rows, 0)
    lax.fori_loop(0, tt, wait_rows, 0)


def _dispatch(h2, pos_tiles, zstart, zvalid, n_rows, tt):
    t, c, _ = h2.shape
    e = zstart.shape[0]
    kern = functools.partial(_dispatch_kernel, E=e, tt=tt)
    return pl.pallas_call(
        kern,
        grid_spec=pltpu.PrefetchScalarGridSpec(
            num_scalar_prefetch=2,
            grid=(t // tt,),
            in_specs=[pl.BlockSpec(memory_space=pl.ANY),
                      pl.BlockSpec((tt, c, LANES), lambda i, zs, zv: (i, 0, 0))],
            out_specs=pl.BlockSpec(memory_space=pl.ANY),
            scratch_shapes=[pltpu.VMEM((MOE_BLOCK, c, LANES), U32), pltpu.SMEM((TOP_K, tt), I32),
                            pltpu.SemaphoreType.DMA, pltpu.SemaphoreType.DMA, pltpu.SemaphoreType.DMA]),
        out_shape=jax.ShapeDtypeStruct((n_rows, c, LANES), U32),
        compiler_params=_cparams(("arbitrary",)),
        name="dispatch",
    )(zstart, zvalid, pos_tiles, h2)


def _experts_kernel(be_ref, nu_ref, nxt_ref, ord_ref, xs_ref, wg_hbm, wu_hbm, wd_hbm, ys_ref,
                    sg_s, su_s, sd_s, wgb_s, wub_s, wdb_s, sem, *, E):
    b = pl.program_id(0)
    n_used = nu_ref[0]

    def copies(e, slot):
        out = []
        for m, (w_hbm, st) in enumerate(((wg_hbm, sg_s), (wu_hbm, su_s), (wd_hbm, sd_s))):
            rows = w_hbm.shape[1] // 2
            for p in range(2):
                out.append(pltpu.make_async_copy(w_hbm.at[e, pl.ds(p * rows, rows)],
                                                 st.at[slot, pl.ds(p * rows, rows)], sem.at[slot, 2 * m + p]))
        return out

    def start_all(cps):
        for n, cp in enumerate(cps):
            cp.start(priority=n % 2)

    @pl.when(b < n_used)
    def _():
        e = be_ref[b]
        prev = be_ref[jnp.maximum(b - 1, 0)]
        slot = ord_ref[e] & 1

        @pl.when(b == 0)
        def _():
            start_all(copies(e, slot))

        @pl.when((b == 0) | (e != prev))
        def _():
            for cp in copies(e, slot):
                cp.wait()
            ne = nxt_ref[e]

            @pl.when(ne < E)
            def _():
                start_all(copies(ne, 1 - slot))

            wgb_s[...] = sg_s[slot].astype(BF16)
            wub_s[...] = su_s[slot].astype(BF16)
            wdb_s[...] = sd_s[slot].astype(BF16)

        parts = [_unpack_halves(ch) for ch in _load_row_tiled(xs_ref, MOE_BLOCK)]
        x_lo = jnp.concatenate([p[0] for p in parts], axis=1).astype(BF16)
        x_hi = jnp.concatenate([p[1] for p in parts], axis=1).astype(BF16)
        half = x_lo.shape[1]
        gate = _mm(x_lo, wgb_s[0:half, :]) + _mm(x_hi, wgb_s[half:, :])
        up = _mm(x_lo, wub_s[0:half, :]) + _mm(x_hi, wub_s[half:, :])
        a = _silu(gate) * up
        _store_row_tiled(ys_ref, _pack_halves(_mm(a.astype(BF16), wdb_s[...])))

    @pl.when(b >= n_used)
    def _():
        ys_ref[...] = jnp.zeros(ys_ref.shape, U32)


def _experts(xs, block_e, n_used, next_used, used_ord, wg, wu, wd):
    e, d, de = wg.shape
    c = d // 2 // LANES
    nblk = xs.shape[0] // (MOE_BLOCK * c)
    kern = functools.partial(_experts_kernel, E=e)

    def live(b, be, nu, nx, od):
        return (jnp.minimum(b, jnp.maximum(nu[0] - 1, 0)), 0)

    hbm = pl.BlockSpec(memory_space=pl.ANY)
    return pl.pallas_call(
        kern,
        grid_spec=pltpu.PrefetchScalarGridSpec(
            num_scalar_prefetch=4,
            grid=(nblk,),
            in_specs=[pl.BlockSpec((MOE_BLOCK * c, LANES), live), hbm, hbm, hbm],
            out_specs=pl.BlockSpec((MOE_BLOCK * c, LANES), lambda b, be, nu, nx, od: (b, 0)),
            scratch_shapes=[pltpu.VMEM((2, d, de), F32), pltpu.VMEM((2, d, de), F32), pltpu.VMEM((2, de, d), F32),
                            pltpu.VMEM((d, de), BF16), pltpu.VMEM((d, de), BF16), pltpu.VMEM((de, d), BF16),
                            pltpu.SemaphoreType.DMA((2, 6))]),
        out_shape=jax.ShapeDtypeStruct(xs.shape, U32),
        compiler_params=_cparams(("arbitrary",)),
        name="experts",
    )(block_e, n_used, next_used, used_ord, xs, wg, wu, wd)


def _combine_kernel(pos_hbm, ys_hbm, ew_ref, x1_ref, h2_ref, g2_ref, nw3_ref, wsg_ref, wsu_ref, wsd_ref,
                    out_ref, buf_s, pos_s, sem_p, sem_r, *, tc, nt):
    tile = pl.program_id(0) * nt + pl.program_id(1)
    pos_copy = pltpu.make_async_copy(pos_hbm.at[tile], pos_s, sem_p)
    pos_copy.start()
    pos_copy.wait()

    def row_copy(t, k):
        c = ys_hbm.shape[1]
        return pltpu.make_async_copy(ys_hbm.at[pos_s[k, t]], buf_s.at[k, pl.ds(pl.multiple_of(t * c, c), c)], sem_r)

    def start_rows(t, c):
        for k in range(TOP_K):
            row_copy(t, k).start(priority=k % 2)
        return c

    def wait_rows(t, c):
        for k in range(TOP_K):
            row_copy(t, k).wait()
        return c

    lax.fori_loop(0, tc, start_rows, 0)
    parts = [_unpack_halves(ch) for ch in _load_row_tiled(h2_ref.at[0], tc)]
    h_lo = jnp.concatenate([p[0] for p in parts], axis=1).astype(BF16)
    h_hi = jnp.concatenate([p[1] for p in parts], axis=1).astype(BF16)
    half = h_lo.shape[1]
    nch = half // LANES
    gate = _mm(h_lo, wsg_ref[0:half, :]) + _mm(h_hi, wsg_ref[half:, :])
    up = _mm(h_lo, wsu_ref[0:half, :]) + _mm(h_hi, wsu_ref[half:, :])
    shared = _mm((_silu(gate) * up).astype(BF16), wsd_ref[...])
    lax.fori_loop(0, tc, wait_rows, 0)
    ew = ew_ref[0]
    acc = [shared[:, j * LANES:(j + 1) * LANES] for j in range(2 * nch)]
    for k in range(TOP_K):
        wk = ew[:, k:k + 1]
        for j, ch in enumerate(_load_row_tiled(buf_s.at[k], tc)):
            y_lo, y_hi = _unpack_halves(ch)
            acc[j] = acc[j] + y_lo * wk
            acc[nch + j] = acc[nch + j] + y_hi * wk
    out_ref[0] = x1_ref[0] + g2_ref[0] * _rms(jnp.concatenate(acc, axis=1), nw3_ref[...])


def _combine(pos_tiles, ys, ew, x1, h2, g2, nw3, wsg, wsu, wsd, tc):
    b, s, d = x1.shape
    ds_ = wsg.shape[1]
    nt = s // tc
    c = ys.shape[1]
    kern = functools.partial(_combine_kernel, tc=tc, nt=nt)
    tok = lambda w: pl.BlockSpec((1, tc, w), lambda bi, i: (bi, i, 0))
    return pl.pallas_call(
        kern,
        grid=(b, nt),
        in_specs=[pl.BlockSpec(memory_space=pl.ANY), pl.BlockSpec(memory_space=pl.ANY),
                  tok(TOP_K), tok(d), pl.BlockSpec((1, tc * c, LANES), lambda bi, i: (bi, i, 0)),
                  pl.BlockSpec((1, 1, d), lambda bi, i: (bi, 0, 0)),
                  pl.BlockSpec((1, d), lambda bi, i: (0, 0)),
                  pl.BlockSpec((d, ds_), lambda bi, i: (0, 0)),
                  pl.BlockSpec((d, ds_), lambda bi, i: (0, 0)),
                  pl.BlockSpec((ds_, d), lambda bi, i: (0, 0))],
        out_specs=tok(d),
        out_shape=jax.ShapeDtypeStruct((b, s, d), F32),
        scratch_shapes=[pltpu.VMEM((TOP_K, tc * c, LANES), U32), pltpu.SMEM((TOP_K, tc), I32),
                        pltpu.SemaphoreType.DMA, pltpu.SemaphoreType.DMA],
        compiler_params=_cparams(("arbitrary", "arbitrary")),
        name="combine",
    )(pos_tiles, ys, ew, x1, h2, g2, nw3, wsg, wsu, wsd)


def _tile(n, pref):
    t = min(n, pref)
    if n % t:
        raise NotImplementedError(f"size {n} is not a multiple of tile {t}")
    return t


def kernel(x, c, ctx, c_ctx, w_ada, b_ada, norms, w_in, ml_conv, ml_gate_b, ml_norm_w, attn_sink, w_out,
           w_router, b_router, w_exp_gate, w_exp_up, w_exp_down, w_sh_gate, w_sh_up, w_sh_down):
    if w_ada.shape[0] != 1:
        raise NotImplementedError("single-layer configuration only")
    b, s, d = x.shape
    lc = ctx.shape[1]
    t = b * s
    H = ML_HEADS
    dv = d // 2 // H
    dk = dv // 2
    dh = d // 2 // AT_HEADS
    qk_w, v_w = 2 * H * dk, H * dv
    ng = 4 * H
    aq_w, akv_w = AT_HEADS * dh, AT_KV_HEADS * dh
    e = w_router.shape[-1]
    nw = norms[0]

    rows = -(-(b + 1) // SUBLANES) * SUBLANES
    cond = jnp.zeros((rows, d), F32).at[:b].set(c).at[b].set(c_ctx)
    mods = _ada(cond, w_ada[0], b_ada[0][None, :])
    sh1, sc1, g1, sh2, sc2, g2 = [m[:b, None, :] for m in jnp.split(mods, 6, axis=-1)]
    csh1, csc1 = [jnp.broadcast_to(m[b:b + 1, None, :], (b, 1, d)) for m in jnp.split(mods, 6, axis=-1)[:2]]

    w0 = w_in[0]
    o_ml, o_g, o_q = 0, qk_w + 2 * v_w, qk_w + 2 * v_w + ng
    ml_w = qk_w + 2 * v_w
    wp = jnp.concatenate([w0[:, o_ml:o_g], w0[:, o_q:], w0[:, o_g:o_q], jnp.zeros((d, LANES - ng), F32)],
                         axis=1).astype(BF16)
    layout = (("ml", 0, ml_w), ("aq", ml_w, aq_w), ("ak", ml_w + aq_w, akv_w),
              ("av", ml_w + aq_w + akv_w, akv_w), ("g", ml_w + aq_w + 2 * akv_w, LANES))
    widths = dict(ml=ml_w, aq=aq_w, ak=akv_w, av=akv_w, g=LANES)
    tabs = _rope_tables(s, dh)
    proj = functools.partial(_inproj, layout=layout, widths=widths, dh=dh)
    ml, aq, ak, av, gates = proj(x, sh1, sc1, nw[0:1], wp, tabs, rope=True, tm=_tile(s, 512))
    tm_c = _tile(lc, 512)
    mlc, _, akc, avc, gates_c = proj(ctx, csh1, csc1, nw[0:1], wp, tuple(tb[:tm_c] for tb in tabs),
                                     rope=False, tm=tm_c)

    L = _tile(lc, 256)
    if s % L:
        raise NotImplementedError("sequence must be a multiple of the mLSTM chunk")
    gate_b = jnp.zeros((1, LANES), F32).at[0, :ng].set(ml_gate_b[0])
    gp = _gate_prep(gates, gate_b, L)
    gpc = _gate_prep(gates_c, gate_b, L)
    y_ml = _mlstm(ml, mlc, gp, gpc, ml_conv[0], ml_norm_w[0][None, :], L)

    y_at = _attn(aq, ak, av, akc, avc, attn_sink[0])

    x1, h2, logits = _outproj(y_ml, y_at, x, g1, sh2, sc2, nw[1:2], nw[2:3], w_out[0].astype(BF16),
                              w_router[0], _tile(s, 256))

    eidx, ew, rank, cnt = _route(logits.reshape(t, e), jnp.broadcast_to(b_router[0][:, None], (e, LANES)),
                                 _tile(t, 512))
    counts = cnt[:, 0].astype(I32)
    padded = (counts + MOE_BLOCK - 1) // MOE_BLOCK * MOE_BLOCK
    pad_end = jnp.cumsum(padded)
    pad_start = pad_end - padded
    n_blocks = -(-(t * TOP_K + e * (MOE_BLOCK - 1)) // MOE_BLOCK)
    n_rows = n_blocks * MOE_BLOCK
    blk_row = jnp.arange(n_blocks, dtype=I32) * MOE_BLOCK
    block_e = jnp.minimum(jnp.sum((pad_end[None, :] <= blk_row[:, None]).astype(I32), axis=1), e - 1)
    n_used = (pad_end[-1] // MOE_BLOCK).astype(I32)[None]
    used = counts > 0
    eid = jnp.arange(e, dtype=I32)
    next_used = jnp.concatenate([lax.cummin(jnp.where(used, eid, e), reverse=True)[1:], jnp.full((1,), e, I32)])
    used_ord = jnp.cumsum(used.astype(I32)) - 1
    pos = _positions(eidx, rank, pad_start, _tile(t, 512))

    tt = _tile(t, 256)
    pos_d = pos.reshape(TOP_K, t // tt, tt).transpose(1, 0, 2)
    rc = d // 2 // LANES
    xs = _dispatch(h2.reshape(t, rc, LANES), pos_d, (pad_end - MOE_BLOCK).astype(I32), (padded > 0).astype(I32),
                   n_rows, tt)
    ys = _experts(xs.reshape(n_rows * rc, LANES), block_e, n_used, next_used, used_ord,
                  w_exp_gate[0], w_exp_up[0], w_exp_down[0])
    tc = _tile(s, 128)
    pos_c = pos.reshape(TOP_K, t // tc, tc).transpose(1, 0, 2)
    ew_t = ew.T.reshape(b, s, TOP_K)
    return _combine(pos_c, ys.reshape(n_rows, rc, LANES), ew_t, x1, h2, g2, nw[3:4], w_sh_gate[0].astype(BF16), w_sh_up[0].astype(BF16),
                    w_sh_down[0].astype(BF16), tc)
```

```python
import functools

import jax
import jax.numpy as jnp
from jax import lax
from jax.experimental import pallas as pl
from jax.experimental.pallas import tpu as pltpu

F32 = jnp.float32
BF16 = jnp.bfloat16
I32 = jnp.int32

LANES = 128
SUBLANES = 8
VMEM_LIMIT_BYTES = 56 * 1024 * 1024

NORM_EPS = 1e-6
ML_HEADS = 4
CONV_W = 5
AT_HEADS = 8
AT_KV_HEADS = 2
GRID_W = 64
WINDOW = 128
BLOCK_Q = 128
ROPE_THETA = 10000.0
N_GROUPS = 8
TOPK_GROUPS = 4
TOP_K = 8
ROUTE_SCALE = 2.5
MOE_BLOCK = 128
NEG_BIG = -1e30


def _cparams(sem):
    return pltpu.CompilerParams(dimension_semantics=sem, vmem_limit_bytes=VMEM_LIMIT_BYTES)


def _rms(xf, w):
    return xf * lax.rsqrt(jnp.mean(xf * xf, axis=-1, keepdims=True) + NORM_EPS) * w


def _silu(x):
    return x * jax.nn.sigmoid(x)


def _mm(a, b):
    return jnp.dot(a, b, preferred_element_type=F32)


U32 = jnp.uint32


def _pack_halves(x):
    n = x.shape[1] // 2
    u = lax.bitcast_convert_type(x.astype(BF16).astype(F32), U32)
    return (u[:, :n] >> 16) | (u[:, n:] & jnp.uint32(0xFFFF0000))


def _unpack_halves(u):
    return (lax.bitcast_convert_type(u << 16, F32),
            lax.bitcast_convert_type(u & jnp.uint32(0xFFFF0000), F32))


def _store_row_tiled(ref2d, packed):
    r, n = packed.shape
    c = n // LANES
    for j in range(c):
        ref2d[pl.ds(j, r, stride=c), :] = packed[:, j * LANES:(j + 1) * LANES]


def _load_row_tiled(ref2d, r):
    c = ref2d.shape[0] // r
    return [ref2d[pl.ds(j, r, stride=c), :] for j in range(c)]


def _ada_kernel(c_ref, w_ref, b_ref, o_ref):
    s = _silu(c_ref[...]).astype(BF16)
    o_ref[...] = _mm(s, w_ref[...].astype(BF16)) + b_ref[...]


def _ada(cond, w, b):
    r, d = cond.shape
    n = w.shape[1]
    tn = min(n, 1024)
    return pl.pallas_call(
        _ada_kernel,
        grid=(n // tn,),
        in_specs=[pl.BlockSpec((r, d), lambda j: (0, 0)),
                  pl.BlockSpec((d, tn), lambda j: (0, j)),
                  pl.BlockSpec((1, tn), lambda j: (0, j))],
        out_specs=pl.BlockSpec((r, tn), lambda j: (0, j)),
        out_shape=jax.ShapeDtypeStruct((r, n), F32),
        compiler_params=_cparams(("arbitrary",)),
        name="ada",
    )(cond, w, b)


def _inproj_kernel(x_ref, sh_ref, sc_ref, nw_ref, w_ref, rc_ref, ra_ref, rb_ref,
                   ml_ref, aq_ref, ak_ref, av_ref, g_ref, hb_ref, *, layout, dh, rope, qscale, cw):
    h = _rms(x_ref[0], nw_ref[...]) * (1.0 + sc_ref[0]) + sh_ref[0]
    hb_ref[...] = h.astype(BF16)
    outs = dict(ml=ml_ref, aq=aq_ref, ak=ak_ref, av=av_ref, g=g_ref)
    for name, col0, width in layout:
        o_ref = outs[name]
        for c in range(0, width, cw):
            step = min(cw, width - c)
            acc = _mm(hb_ref[...], w_ref[:, col0 + c:col0 + c + step])
            if rope and name in ("aq", "ak"):
                for hh in range(step // dh):
                    a = acc[:, hh * dh:(hh + 1) * dh]
                    r = (a * rc_ref[...] + pltpu.roll(a, dh - dh // 4, 1) * ra_ref[...]
                         + pltpu.roll(a, dh // 4, 1) * rb_ref[...])
                    if name == "aq":
                        r = r * qscale
                    o_ref[0, :, c + hh * dh:c + (hh + 1) * dh] = r.astype(o_ref.dtype)
            else:
                if name == "aq":
                    acc = acc * qscale
                o_ref[0, :, c:c + step] = acc.astype(o_ref.dtype)


def _inproj(x, sh, sc, nw, wp, tabs, *, layout, widths, dh, rope, tm):
    b, s, d = x.shape
    npad = wp.shape[1]
    rc, ra, rb = tabs
    kern = functools.partial(_inproj_kernel, layout=layout, dh=dh, rope=rope, qscale=float(dh) ** -0.5, cw=512)
    vec = pl.BlockSpec((1, 1, d), lambda bi, i: (bi, 0, 0))
    tab = pl.BlockSpec((tm, dh), lambda bi, i: (i, 0))
    out_dt = dict(ml=BF16, aq=BF16, ak=BF16, av=BF16, g=F32)
    names = ("ml", "aq", "ak", "av", "g")
    return pl.pallas_call(
        kern,
        grid=(b, s // tm),
        in_specs=[pl.BlockSpec((1, tm, d), lambda bi, i: (bi, i, 0)), vec, vec,
                  pl.BlockSpec((1, d), lambda bi, i: (0, 0)),
                  pl.BlockSpec((d, npad), lambda bi, i: (0, 0), pipeline_mode=pl.Buffered(1)),
                  tab, tab, tab],
        out_specs=[pl.BlockSpec((1, tm, widths[n]), lambda bi, i: (bi, i, 0)) for n in names],
        out_shape=[jax.ShapeDtypeStruct((b, s, widths[n]), out_dt[n]) for n in names],
        scratch_shapes=[pltpu.VMEM((tm, d), BF16)],
        compiler_params=_cparams(("arbitrary", "arbitrary")),
        name="inproj_rope" if rope else "inproj_ctx",
    )(x, sh, sc, nw, wp, rc, ra, rb)


def _rope_tables(s, dh):
    rows = s // GRID_W
    row = jnp.repeat(jnp.arange(rows), GRID_W)
    col = jnp.tile(jnp.arange(GRID_W), rows)
    nf = dh // 4
    freqs = ROPE_THETA ** (-jnp.arange(nf, dtype=F32) / nf)
    pos = jnp.stack([row, col], axis=-1).astype(F32)
    ang = pos[:, :, None] * freqs
    cos, sin = jnp.cos(ang), jnp.sin(ang)
    z = jnp.zeros_like(sin[:, 0])
    rc = jnp.concatenate([cos[:, 0], cos[:, 0], cos[:, 1], cos[:, 1]], axis=-1)
    ra = jnp.concatenate([-sin[:, 0], z, -sin[:, 1], z], axis=-1)
    rb = jnp.concatenate([z, sin[:, 0], z, sin[:, 1]], axis=-1)
    return rc, ra, rb


def _gate_prep_kernel(g_ref, b_ref, gl_ref, pp_ref, glt_ref, ppt_ref, tot_ref, *, L, H, nc):
    for c in range(nc):
        z = g_ref[0, c * L:(c + 1) * L, :] + b_ref[...]
        lane = lax.broadcasted_iota(I32, z.shape, 1)
        row = lax.broadcasted_iota(I32, z.shape, 0)
        is_f = ((lane // H) % 2 == 1) & (lane < 4 * H)
        log_sig = jnp.minimum(z, 0.0) - jnp.log1p(jnp.exp(-jnp.abs(z)))
        gl = jnp.where(lane < 4 * H, jnp.where(is_f, log_sig, z), 0.0)
        cs = gl
        k = 1
        while k < L:
            cs = cs + jnp.where(row >= k, pltpu.roll(cs, k, 0), 0.0)
            k *= 2
        tot = cs[L - 1:L, :]
        suf = tot - cs + gl
        pp = jnp.where(lane >= 2 * H, suf, cs)
        gl_ref[0, c] = gl
        pp_ref[0, c] = pp
        glt_ref[0, c] = gl.T
        ppt_ref[0, c] = pp.T
        tot_ref[0, c] = jnp.broadcast_to(tot, (SUBLANES, LANES))


def _gate_prep(g, gate_b, L):
    b, s, _ = g.shape
    nc = s // L
    kern = functools.partial(_gate_prep_kernel, L=L, H=ML_HEADS, nc=nc)
    col = pl.BlockSpec((1, nc, L, LANES), lambda bi: (bi, 0, 0, 0))
    rowb = pl.BlockSpec((1, nc, LANES, L), lambda bi: (bi, 0, 0, 0))
    return pl.pallas_call(
        kern,
        grid=(b,),
        in_specs=[pl.BlockSpec((1, s, LANES), lambda bi: (bi, 0, 0)),
                  pl.BlockSpec((1, LANES), lambda bi: (0, 0))],
        out_specs=[col, col, rowb, rowb, pl.BlockSpec((1, nc, SUBLANES, LANES), lambda bi: (bi, 0, 0, 0))],
        out_shape=[jax.ShapeDtypeStruct((b, nc, L, LANES), F32), jax.ShapeDtypeStruct((b, nc, L, LANES), F32),
                   jax.ShapeDtypeStruct((b, nc, LANES, L), F32), jax.ShapeDtypeStruct((b, nc, LANES, L), F32),
                   jax.ShapeDtypeStruct((b, nc, SUBLANES, LANES), F32)],
        compiler_params=_cparams(("arbitrary",)),
        name="gate_prep",
    )(g, gate_b)


CONV_PAD = 8


def _lane_pick(tile, ch):
    lane = lax.broadcasted_iota(I32, tile.shape, 1)
    return jnp.sum(jnp.where(lane == ch, tile, 0.0), axis=1, keepdims=True)


def _sublane_pick(tile, ch):
    sub = lax.broadcasted_iota(I32, tile.shape, 0)
    return jnp.sum(jnp.where(sub == ch, tile, 0.0), axis=0, keepdims=True)


def _mlstm_chunk(d, ch_i, ch_f, q, k, v, gl, pp, glt, ppt, tot8, ct_ref, n_ref, m_ref):
    L = k.shape[0]
    p_col = _lane_pick(pp, ch_f)
    li_col = _lane_pick(gl, ch_i)
    btot = _lane_pick(tot8[0:1], ch_f)
    m_old = m_ref[d][0:1, 0:1]
    ct = ct_ref[d]
    n_row = n_ref[d]
    h = None
    if q is not None:
        p_row = _sublane_pick(ppt, ch_f)
        li_row = _sublane_pick(glt, ch_i)
        ti = lax.broadcasted_iota(I32, (L, L), 0)
        si = lax.broadcasted_iota(I32, (L, L), 1)
        valid = (si <= ti) if d == 0 else (si >= ti)
        logw = jnp.where(valid, p_col - p_row + li_row, NEG_BIG)
        log_inter = p_col + m_old
        m_q = jnp.maximum(log_inter, jnp.max(logw, axis=1, keepdims=True))
        sqk = lax.dot_general(q, k, (((1,), (1,)), ((), ())), preferred_element_type=F32) * jnp.exp(logw - m_q)
        w_inter = jnp.exp(log_inter - m_q)
        num = _mm(sqk.astype(BF16), v) + w_inter * _mm(q, ct.astype(BF16))
        den = (jnp.sum(sqk, axis=1, keepdims=True)
               + w_inter * jnp.sum(q.astype(F32) * n_row, axis=1, keepdims=True))
        h = num / jnp.maximum(jnp.abs(den), jnp.exp(-m_q))
    log_w_end = btot - p_col + li_col
    m_new = jnp.maximum(btot + m_old, jnp.max(log_w_end, axis=0, keepdims=True))
    w_end = jnp.exp(log_w_end - m_new)
    decay = jnp.exp(btot + m_old - m_new)
    kw = k.astype(F32) * w_end
    ct_ref[d] = decay * ct + lax.dot_general(kw.astype(BF16), v, (((0,), (0,)), ((), ())),
                                             preferred_element_type=F32)
    n_ref[d] = decay * n_row + jnp.sum(kw, axis=0, keepdims=True)
    m_ref[d] = jnp.broadcast_to(m_new, (SUBLANES, LANES))
    return h


def _mlstm_kernel(q_ref, k_ref, v_ref, o_ref, kc_ref, vc_ref,
                  gl_ref, pp_ref, glt_ref, ppt_ref, tot_ref, glc_ref, ppc_ref, totc_ref,
                  cwq_ref, cwk_ref, nw_ref, y_ref,
                  xq_s, xk_s, xkc_s, qs_s, ks_s, kcs_s, hf_s, hb_s, ct_s, n_s, m_s,
                  *, L, H, S, Lc, dk, dv):
    hh = pl.program_id(1)
    nc, ncc = S // L, Lc // L
    zpad = jnp.zeros((CONV_PAD, dk), F32)

    def conv_silu(x_s, cw_ref, n_rows, out_s, scale):
        for c in range(n_rows // L):
            acc = jnp.zeros((L, dk), F32)
            for j in range(CONV_W):
                acc = acc + cw_ref[j:j + 1, :] * x_s[pl.ds(CONV_PAD + c * L + j - CONV_W // 2, L), :]
            out_s[c] = (_silu(acc) * scale).astype(BF16)

    def stage(x_s, src, n_rows):
        x_s[0:CONV_PAD, :] = zpad
        x_s[CONV_PAD + n_rows:2 * CONV_PAD + n_rows, :] = zpad
        x_s[CONV_PAD:CONV_PAD + n_rows, :] = src.astype(F32)

    stage(xq_s, q_ref[0], S)
    stage(xk_s, k_ref[0], S)
    stage(xkc_s, kc_ref[0], Lc)
    conv_silu(xq_s, cwq_ref, S, qs_s, float(dk) ** -0.5)
    conv_silu(xk_s, cwk_ref, S, ks_s, 1.0)
    conv_silu(xkc_s, cwk_ref, Lc, kcs_s, 1.0)

    ct_s[...] = jnp.zeros(ct_s.shape, F32)
    n_s[...] = jnp.zeros(n_s.shape, F32)
    m_s[...] = jnp.zeros(m_s.shape, F32)

    def chans(d):
        return d * 2 * H + hh, d * 2 * H + H + hh

    for d in (0, 1):
        ch_i, ch_f = chans(d)
        for c in (range(ncc) if d == 0 else range(ncc - 1, -1, -1)):
            _mlstm_chunk(d, ch_i, ch_f, None, kcs_s[c], vc_ref[0, c * L:(c + 1) * L, :],
                         glc_ref[0, c], ppc_ref[0, c], None, None, totc_ref[0, c], ct_s, n_s, m_s)

    def body(i, carry):
        for d in (0, 1):
            ch_i, ch_f = chans(d)
            c = i if d == 0 else nc - 1 - i
            r0 = pl.multiple_of(c * L, L)
            h = _mlstm_chunk(d, ch_i, ch_f, qs_s[c], ks_s[c], v_ref[0, pl.ds(r0, L), :],
                             gl_ref[0, c], pp_ref[0, c], glt_ref[0, c], ppt_ref[0, c], tot_ref[0, c],
                             ct_s, n_s, m_s)
            if d == 0:
                hf_s[c] = h
            else:
                hb_s[c] = h
        return carry

    lax.fori_loop(0, nc, body, 0)

    for c in range(nc):
        hs = hf_s[c] + hb_s[c]
        hn = _rms(hs, nw_ref[...])
        y_ref[0, c * L:(c + 1) * L, :] = (hn * jax.nn.sigmoid(o_ref[0, c * L:(c + 1) * L, :].astype(F32))).astype(BF16)


def _mlstm(ml, mlc, gp, gpc, conv_w, norm_w, L):
    b, s, _ = ml.shape
    lc = mlc.shape[1]
    H = ML_HEADS
    dv = norm_w.shape[1] // H
    dk = dv // 2
    nc, ncc = s // L, lc // L
    gl, pp, glt, ppt, tot = gp
    glc, ppc, _, _, totc = gpc
    kern = functools.partial(_mlstm_kernel, L=L, H=H, S=s, Lc=lc, dk=dk, dv=dv)

    def colspec(rows, w, off):
        return pl.BlockSpec((1, rows, w), lambda bi, h: (bi, 0, off + h))

    def gspec(n, r, c):
        return pl.BlockSpec((1, n, r, c), lambda bi, h: (bi, 0, 0, 0))

    return pl.pallas_call(
        kern,
        grid=(b, H),
        in_specs=[colspec(s, dk, 0), colspec(s, dk, H), colspec(s, dv, H), colspec(s, dv, 2 * H),
                  colspec(lc, dk, H), colspec(lc, dv, H),
                  gspec(nc, L, LANES), gspec(nc, L, LANES), gspec(nc, LANES, L), gspec(nc, LANES, L),
                  gspec(nc, SUBLANES, LANES),
                  gspec(ncc, L, LANES), gspec(ncc, L, LANES), gspec(ncc, SUBLANES, LANES),
                  pl.BlockSpec((CONV_W, dk), lambda bi, h: (0, h)),
                  pl.BlockSpec((CONV_W, dk), lambda bi, h: (0, H + h)),
                  pl.BlockSpec((1, dv), lambda bi, h: (0, h))],
        out_specs=pl.BlockSpec((1, s, dv), lambda bi, h: (bi, 0, h)),
        out_shape=jax.ShapeDtypeStruct((b, s, H * dv), BF16),
        scratch_shapes=[pltpu.VMEM((s + 2 * CONV_PAD, dk), F32), pltpu.VMEM((s + 2 * CONV_PAD, dk), F32),
                        pltpu.VMEM((lc + 2 * CONV_PAD, dk), F32),
                        pltpu.VMEM((nc, L, dk), BF16), pltpu.VMEM((nc, L, dk), BF16), pltpu.VMEM((ncc, L, dk), BF16),
                        pltpu.VMEM((nc, L, dv), F32), pltpu.VMEM((nc, L, dv), F32),
                        pltpu.VMEM((2, dk, dv), F32), pltpu.VMEM((2, 1, dk), F32),
                        pltpu.VMEM((2, SUBLANES, LANES), F32)],
        compiler_params=_cparams(("arbitrary", "arbitrary")),
        name="mlstm",
    )(ml, ml, ml, ml, mlc, mlc, gl, pp, glt, ppt, tot, glc, ppc, totc, conv_w, conv_w, norm_w)


def _attn_kernel(sink_ref, q_ref, kp_ref, kc_ref, kn_ref, vp_ref, vc_ref, vn_ref, kx_ref, vx_ref, bias_ref,
                 o_ref, *, bq, dh, G):
    q = q_ref[0]
    for g in range(AT_KV_HEADS):
        sl = slice(g * dh, (g + 1) * dh)
        qg = jnp.concatenate([q[:, (g * G + j) * dh:(g * G + j + 1) * dh] for j in range(G)], axis=0)
        kcat = jnp.concatenate([kp_ref[0][:, sl], kc_ref[0][:, sl], kn_ref[0][:, sl], kx_ref[0][:, sl]], axis=0)
        vcat = jnp.concatenate([vp_ref[0][:, sl], vc_ref[0][:, sl], vn_ref[0][:, sl], vx_ref[0][:, sl]], axis=0)
        s = lax.dot_general(qg, kcat, (((1,), (1,)), ((), ())), preferred_element_type=F32) + bias_ref[0]
        ri = lax.broadcasted_iota(I32, (G * bq, 1), 0)
        sk = jnp.zeros((G * bq, 1), F32)
        for j in range(G):
            sk = jnp.where((ri >= j * bq) & (ri < (j + 1) * bq), sink_ref[g * G + j], sk)
        m = jnp.maximum(jnp.max(s, axis=1, keepdims=True), sk)
        p = jnp.exp(s - m)
        den = jnp.sum(p, axis=1, keepdims=True) + jnp.exp(sk - m)
        o = _mm(p.astype(BF16), vcat) / den
        for j in range(G):
            o_ref[0, :, (g * G + j) * dh:(g * G + j + 1) * dh] = o[j * bq:(j + 1) * bq, :].astype(BF16)


def _attn_bias(bq, lc, G):
    qoff = (jnp.arange(G * bq) % bq)[:, None]
    ci = jnp.arange(3 * bq + lc)[None, :]
    prev = (ci < bq) & (qoff <= ci)
    cur = (ci >= bq) & (ci < 2 * bq)
    nxt = (ci >= 2 * bq) & (ci < 3 * bq) & (ci - 2 * bq <= qoff)
    ctx = ci >= 3 * bq
    inner = prev | cur | nxt | ctx
    first = cur | nxt | ctx
    last = prev | cur | ctx
    if WINDOW != bq:
        raise NotImplementedError("window must equal the query block")
    return jnp.where(jnp.stack([inner, first, last]), 0.0, NEG_BIG).astype(F32)


def _attn(aq, ak, av, akc, avc, sink):
    b, s, hd = aq.shape
    dh = hd // AT_HEADS
    G = AT_HEADS // AT_KV_HEADS
    bq = BLOCK_Q
    nb = s // bq
    lc = akc.shape[1]
    kvw = AT_KV_HEADS * dh
    bias = _attn_bias(bq, lc, G)
    if nb < 2:
        raise NotImplementedError("needs at least two query blocks")
    kern = functools.partial(_attn_kernel, bq=bq, dh=dh, G=G)
    prev = pl.BlockSpec((1, bq, kvw), lambda bi, n: (bi, jnp.maximum(n - 1, 0), 0))
    cur = pl.BlockSpec((1, bq, kvw), lambda bi, n: (bi, n, 0))
    nxt = pl.BlockSpec((1, bq, kvw), lambda bi, n: (bi, jnp.minimum(n + 1, nb - 1), 0))
    cx = pl.BlockSpec((1, lc, kvw), lambda bi, n: (bi, 0, 0))
    return pl.pallas_call(
        kern,
        grid=(b, nb),
        in_specs=[pl.BlockSpec(memory_space=pltpu.SMEM),
                  pl.BlockSpec((1, bq, hd), lambda bi, n: (bi, n, 0)),
                  prev, cur, nxt, prev, cur, nxt, cx, cx,
                  pl.BlockSpec((1, G * bq, 3 * bq + lc),
                               lambda bi, n: (jnp.where(n == 0, 1, jnp.where(n == nb - 1, 2, 0)), 0, 0))],
        out_specs=pl.BlockSpec((1, bq, hd), lambda bi, n: (bi, n, 0)),
        out_shape=jax.ShapeDtypeStruct((b, s, hd), BF16),
        compiler_params=_cparams(("arbitrary", "arbitrary")),
        name="attn",
    )(sink, aq, ak, ak, ak, av, av, av, akc, avc, bias)


def _outproj_kernel(ym_ref, ya_ref, x_ref, g1_ref, sh2_ref, sc2_ref, nw1_ref, nw2_ref, wo_ref, wrh_ref, wrl_ref,
                    x1_ref, h2_ref, lg_ref, y_s, *, dm, cw):
    d = y_s.shape[1]
    for c in range(0, d, cw):
        y_s[:, c:c + cw] = _mm(ym_ref[0], wo_ref[0:dm, c:c + cw]) + _mm(ya_ref[0], wo_ref[dm:, c:c + cw])
    x1 = x_ref[0] + g1_ref[0] * _rms(y_s[...], nw1_ref[...])
    x1_ref[0] = x1
    h2 = _rms(x1, nw2_ref[...]) * (1.0 + sc2_ref[0]) + sh2_ref[0]
    hi = h2.astype(BF16)
    _store_row_tiled(h2_ref.at[0], _pack_halves(h2))
    lo = (h2 - hi.astype(F32)).astype(BF16)
    lg_ref[0] = _mm(hi, wrh_ref[...]) + (_mm(hi, wrl_ref[...]) + _mm(lo, wrh_ref[...]))


def _outproj(y_ml, y_at, x, g1, sh2, sc2, nw1, nw2, wo, wr, tm):
    b, s, d = x.shape
    dm = y_ml.shape[2]
    e = wr.shape[1]
    wr_hi = wr.astype(BF16)
    wr_lo = (wr - wr_hi.astype(F32)).astype(BF16)
    kern = functools.partial(_outproj_kernel, dm=dm, cw=min(d, 512))
    vec = pl.BlockSpec((1, 1, d), lambda bi, i: (bi, 0, 0))
    nspec = pl.BlockSpec((1, d), lambda bi, i: (0, 0))
    return pl.pallas_call(
        kern,
        grid=(b, s // tm),
        in_specs=[pl.BlockSpec((1, tm, dm), lambda bi, i: (bi, i, 0)),
                  pl.BlockSpec((1, tm, d - dm), lambda bi, i: (bi, i, 0)),
                  pl.BlockSpec((1, tm, d), lambda bi, i: (bi, i, 0)),
                  vec, vec, vec, nspec, nspec,
                  pl.BlockSpec((d, d), lambda bi, i: (0, 0), pipeline_mode=pl.Buffered(1)),
                  pl.BlockSpec((d, e), lambda bi, i: (0, 0), pipeline_mode=pl.Buffered(1)),
                  pl.BlockSpec((d, e), lambda bi, i: (0, 0), pipeline_mode=pl.Buffered(1))],
        out_specs=[pl.BlockSpec((1, tm, d), lambda bi, i: (bi, i, 0)),
                   pl.BlockSpec((1, tm * (d // 2 // LANES), LANES), lambda bi, i: (bi, i, 0)),
                   pl.BlockSpec((1, tm, e), lambda bi, i: (bi, i, 0))],
        out_shape=[jax.ShapeDtypeStruct((b, s, d), F32), jax.ShapeDtypeStruct((b, s * (d // 2 // LANES), LANES), U32),
                   jax.ShapeDtypeStruct((b, s, e), F32)],
        scratch_shapes=[pltpu.VMEM((tm, d), F32)],
        compiler_params=_cparams(("arbitrary", "arbitrary")),
        name="outproj",
    )(y_ml, y_at, x, g1, sh2, sc2, nw1, nw2, wo, wr_hi, wr_lo)


def _route_kernel(lg_ref, br_ref, eidx_ref, ew_ref, rank_ref, cnt_ref, carry_s, *, E, tr):
    i = pl.program_id(0)

    @pl.when(i == 0)
    def _():
        carry_s[...] = jnp.zeros(carry_s.shape, F32)

    scores = jax.nn.sigmoid(lg_ref[...].T)
    biased = scores + br_ref[:, 0:1]
    row = lax.broadcasted_iota(I32, (E, tr), 0).astype(F32)
    gs = E // N_GROUPS
    ninf = -jnp.inf
    grp = []
    for g in range(N_GROUPS):
        xg = biased[g * gs:(g + 1) * gs]
        rg = (lax.broadcasted_iota(I32, (gs, tr), 0) + g * gs).astype(F32)
        m1 = jnp.max(xg, axis=0, keepdims=True)
        i1 = jnp.min(jnp.where(xg == m1, rg, float(E)), axis=0, keepdims=True)
        m2 = jnp.max(jnp.where(rg == i1, ninf, xg), axis=0, keepdims=True)
        grp.append(m1 + m2)
    gsc = jnp.concatenate(grp, axis=0)
    gi = lax.broadcasted_iota(I32, (N_GROUPS, tr), 0)
    beaten = jnp.zeros((N_GROUPS, tr), I32)
    for g2 in range(N_GROUPS):
        sg = gsc[g2:g2 + 1]
        beaten = beaten + jnp.where((sg > gsc) | ((sg == gsc) & (gi > g2)), 1, 0)
    keep_g = jnp.where(beaten < TOPK_GROUPS, 1.0, 0.0)
    keep = jnp.concatenate([jnp.broadcast_to(keep_g[g:g + 1], (gs, tr)) for g in range(N_GROUPS)], axis=0)
    masked = jnp.where(keep > 0.5, biased, ninf)
    idxs, ws = [], []
    for _ in range(TOP_K):
        m = jnp.max(masked, axis=0, keepdims=True)
        ik = jnp.min(jnp.where(masked == m, row, float(E)), axis=0, keepdims=True)
        sel = row == ik
        ws.append(jnp.sum(jnp.where(sel, scores, 0.0), axis=0, keepdims=True))
        idxs.append(ik)
        masked = jnp.where(sel, ninf, masked)
    w = jnp.concatenate(ws, axis=0)
    ew_ref[...] = w / jnp.sum(w, axis=0, keepdims=True) * ROUTE_SCALE
    eidx_ref[...] = jnp.concatenate(idxs, axis=0).astype(I32)

    assign = jnp.zeros((E, tr), F32)
    for ik in idxs:
        assign = assign + jnp.where(row == ik, 1.0, 0.0)
    upper = jnp.where(lax.broadcasted_iota(I32, (tr, tr), 0) < lax.broadcasted_iota(I32, (tr, tr), 1), 1.0, 0.0)
    base = _mm(assign.astype(BF16), upper.astype(BF16)) + carry_s[:, 0:1]
    ranks = [jnp.sum(jnp.where(row == ik, base, 0.0), axis=0, keepdims=True) for ik in idxs]
    rank_ref[...] = jnp.concatenate(ranks, axis=0).astype(I32)
    carry_s[...] = carry_s[...] + jnp.sum(assign, axis=1, keepdims=True)
    cnt_ref[...] = carry_s[...]


def _route(logits, b_router, tr):
    t, e = logits.shape
    kern = functools.partial(_route_kernel, E=e, tr=tr)
    kt = pl.BlockSpec((TOP_K, tr), lambda i: (0, i))
    return pl.pallas_call(
        kern,
        grid=(t // tr,),
        in_specs=[pl.BlockSpec((tr, e), lambda i: (i, 0)), pl.BlockSpec((e, LANES), lambda i: (0, 0))],
        out_specs=[kt, kt, kt, pl.BlockSpec((e, LANES), lambda i: (0, 0))],
        out_shape=[jax.ShapeDtypeStruct((TOP_K, t), I32), jax.ShapeDtypeStruct((TOP_K, t), F32),
                   jax.ShapeDtypeStruct((TOP_K, t), I32), jax.ShapeDtypeStruct((e, LANES), F32)],
        scratch_shapes=[pltpu.VMEM((e, LANES), F32)],
        compiler_params=_cparams(("arbitrary",)),
        name="route",
    )(logits, b_router)


def _positions_kernel(eidx_ref, rank_ref, ps_ref, pos_ref, *, E, tr):
    row = lax.broadcasted_iota(I32, (E, tr), 0)
    start = ps_ref[:, 0:1]
    rows = [jnp.sum(jnp.where(row == eidx_ref[k:k + 1, :], start, 0.0), axis=0, keepdims=True)
            for k in range(TOP_K)]
    pos_ref[...] = jnp.concatenate(rows, axis=0).astype(I32) + rank_ref[...]


def _positions(eidx, rank, pad_start, tr):
    t = eidx.shape[1]
    e = pad_start.shape[0]
    kern = functools.partial(_positions_kernel, E=e, tr=tr)
    kt = pl.BlockSpec((TOP_K, tr), lambda i: (0, i))
    return pl.pallas_call(
        kern,
        grid=(t // tr,),
        in_specs=[kt, kt, pl.BlockSpec((e, LANES), lambda i: (0, 0))],
        out_specs=kt,
        out_shape=jax.ShapeDtypeStruct((TOP_K, t), I32),
        compiler_params=_cparams(("arbitrary",)),
        name="positions",
    )(eidx, rank, jnp.broadcast_to(pad_start.astype(F32)[:, None], (e, LANES)))


def _dispatch_kernel(zstart_ref, zvalid_ref, pos_hbm, h2_ref, xs_hbm, zero_s, pos_s, sem_p, sem_z, sem_r,
                     *, E, tt):
    i = pl.program_id(0)

    def zero_copy(e):
        start = pl.multiple_of(zstart_ref[e], MOE_BLOCK)
        return pltpu.make_async_copy(zero_s, xs_hbm.at[pl.ds(start, MOE_BLOCK)], sem_z)

    @pl.when(i == 0)
    def _():
        zero_s[...] = jnp.zeros(zero_s.shape, U32)

        def start(e, c):
            @pl.when(zvalid_ref[e] > 0)
            def _():
                zero_copy(e).start()
            return c

        def wait(e, c):
            @pl.when(zvalid_ref[e] > 0)
            def _():
                zero_copy(e).wait()
            return c

        lax.fori_loop(0, E, start, 0)
        lax.fori_loop(0, E, wait, 0)

    pos_copy = pltpu.make_async_copy(pos_hbm.at[i], pos_s, sem_p)
    pos_copy.start()
    pos_copy.wait()

    def row_copy(t, k):
        return pltpu.make_async_copy(h2_ref.at[t], xs_hbm.at[pos_s[k, t]], sem_r)

    def start_rows(t, c):
        for k in range(TOP_K):
            row_copy(t, k).start(priority=k % 2)
        return c

    def wait_rows(t, c):
        for k in range(TOP_K):
            row_copy(t, k).wait()
        return c

    lax.fori_loop(0, tt, start_rows, 0)
    lax.fori_loop(0, tt, wait_rows, 0)


def _dispatch(h2, pos_tiles, zstart, zvalid, n_rows, tt):
    t, c, _ = h2.shape
    e = zstart.shape[0]
    kern = functools.partial(_dispatch_kernel, E=e, tt=tt)
    return pl.pallas_call(
        kern,
        grid_spec=pltpu.PrefetchScalarGridSpec(
            num_scalar_prefetch=2,
            grid=(t // tt,),
            in_specs=[pl.BlockSpec(memory_space=pl.ANY),
                      pl.BlockSpec((tt, c, LANES), lambda i, zs, zv: (i, 0, 0))],
            out_specs=pl.BlockSpec(memory_space=pl.ANY),
            scratch_shapes=[pltpu.VMEM((MOE_BLOCK, c, LANES), U32), pltpu.SMEM((TOP_K, tt), I32),
                            pltpu.SemaphoreType.DMA, pltpu.SemaphoreType.DMA, pltpu.SemaphoreType.DMA]),
        out_shape=jax.ShapeDtypeStruct((n_rows, c, LANES), U32),
        compiler_params=_cparams(("arbitrary",)),
        name="dispatch",
    )(zstart, zvalid, pos_tiles, h2)


def _experts_kernel(be_ref, nu_ref, nxt_ref, nxt2_ref, ord_ref, xs_ref, wg_hbm, wu_hbm, wd_hbm, ys_ref,
                    sg_s, su_s, sd_s, wgb_s, wub_s, wdb_s, sem, *, E):
    b = pl.program_id(0)
    n_used = nu_ref[0]

    def copies(e, slot):
        out = []
        for m, (w_hbm, st) in enumerate(((wg_hbm, sg_s), (wu_hbm, su_s), (wd_hbm, sd_s))):
            rows = w_hbm.shape[1] // 2
            for p in range(2):
                out.append(pltpu.make_async_copy(w_hbm.at[e, pl.ds(p * rows, rows)],
                                                 st.at[slot, pl.ds(p * rows, rows)], sem.at[slot, 2 * m + p]))
        return out

    def start_all(cps):
        for n, cp in enumerate(cps):
            cp.start(priority=n % 2)

    @pl.when(b < n_used)
    def _():
        e = be_ref[b]
        prev = be_ref[jnp.maximum(b - 1, 0)]
        slot = ord_ref[e] & 1

        @pl.when(b == 0)
        def _():
            start_all(copies(e, slot))
            ne = nxt_ref[e]

            @pl.when(ne < E)
            def _():
                start_all(copies(ne, 1 - slot))

        @pl.when((b == 0) | (e != prev))
        def _():
            for cp in copies(e, slot):
                cp.wait()
            wgb_s[...] = sg_s[slot].astype(BF16)
            wub_s[...] = su_s[slot].astype(BF16)
            wdb_s[...] = sd_s[slot].astype(BF16)
            n2 = nxt2_ref[e]

            @pl.when(n2 < E)
            def _():
                start_all(copies(n2, slot))

        parts = [_unpack_halves(ch) for ch in _load_row_tiled(xs_ref, MOE_BLOCK)]
        x_lo = jnp.concatenate([p[0] for p in parts], axis=1).astype(BF16)
        x_hi = jnp.concatenate([p[1] for p in parts], axis=1).astype(BF16)
        half = x_lo.shape[1]
        gate = _mm(x_lo, wgb_s[0:half, :]) + _mm(x_hi, wgb_s[half:, :])
        up = _mm(x_lo, wub_s[0:half, :]) + _mm(x_hi, wub_s[half:, :])
        a = _silu(gate) * up
        _store_row_tiled(ys_ref, _pack_halves(_mm(a.astype(BF16), wdb_s[...])))

    @pl.when(b >= n_used)
    def _():
        ys_ref[...] = jnp.zeros(ys_ref.shape, U32)


def _experts(xs, block_e, n_used, next_used, next_used2, used_ord, wg, wu, wd):
    e, d, de = wg.shape
    c = d // 2 // LANES
    nblk = xs.shape[0] // (MOE_BLOCK * c)
    kern = functools.partial(_experts_kernel, E=e)

    def live(b, be, nu, nx, nx2, od):
        return (jnp.minimum(b, jnp.maximum(nu[0] - 1, 0)), 0)

    hbm = pl.BlockSpec(memory_space=pl.ANY)
    return pl.pallas_call(
        kern,
        grid_spec=pltpu.PrefetchScalarGridSpec(
            num_scalar_prefetch=5,
            grid=(nblk,),
            in_specs=[pl.BlockSpec((MOE_BLOCK * c, LANES), live), hbm, hbm, hbm],
            out_specs=pl.BlockSpec((MOE_BLOCK * c, LANES), lambda b, be, nu, nx, nx2, od: (b, 0)),
            scratch_shapes=[pltpu.VMEM((2, d, de), F32), pltpu.VMEM((2, d, de), F32), pltpu.VMEM((2, de, d), F32),
                            pltpu.VMEM((d, de), BF16), pltpu.VMEM((d, de), BF16), pltpu.VMEM((de, d), BF16),
                            pltpu.SemaphoreType.DMA((2, 6))]),
        out_shape=jax.ShapeDtypeStruct(xs.shape, U32),
        compiler_params=_cparams(("arbitrary",)),
        name="experts",
    )(block_e, n_used, next_used, next_used2, used_ord, xs, wg, wu, wd)


def _combine_kernel(pos_hbm, ys_hbm, ew_ref, x1_ref, h2_ref, g2_ref, nw3_ref, wsg_ref, wsu_ref, wsd_ref,
                    out_ref, buf_s, pos_s, sem_p, sem_r, *, tc, nt, n_tiles):
    tile = pl.program_id(0) * nt + pl.program_id(1)
    slot = tile & 1
    other = 1 - slot
    c = ys_hbm.shape[1]

    def pos_copy(tl, sl):
        return pltpu.make_async_copy(pos_hbm.at[tl], pos_s.at[sl], sem_p.at[sl])

    def row_copy(sl, t, k):
        return pltpu.make_async_copy(ys_hbm.at[pos_s[sl, k, t]],
                                     buf_s.at[sl, k, pl.ds(pl.multiple_of(t * c, c), c)], sem_r.at[sl])

    def start_rows(sl):
        def body(t, carry):
            for k in range(TOP_K):
                row_copy(sl, t, k).start(priority=k % 2)
            return carry
        lax.fori_loop(0, tc, body, 0)

    def wait_rows(sl):
        def body(t, carry):
            for k in range(TOP_K):
                row_copy(sl, t, k).wait()
            return carry
        lax.fori_loop(0, tc, body, 0)

    @pl.when(tile == 0)
    def _():
        pos_copy(0, 0).start()
        pos_copy(0, 0).wait()
        start_rows(0)
        pos_copy(jnp.minimum(1, n_tiles - 1), 1).start()

    pos_copy(jnp.minimum(tile + 1, n_tiles - 1), other).wait()
    start_rows(other)
    parts = [_unpack_halves(ch) for ch in _load_row_tiled(h2_ref.at[0], tc)]
    h_lo = jnp.concatenate([p[0] for p in parts], axis=1).astype(BF16)
    h_hi = jnp.concatenate([p[1] for p in parts], axis=1).astype(BF16)
    half = h_lo.shape[1]
    nch = half // LANES
    gate = _mm(h_lo, wsg_ref[0:half, :]) + _mm(h_hi, wsg_ref[half:, :])
    up = _mm(h_lo, wsu_ref[0:half, :]) + _mm(h_hi, wsu_ref[half:, :])
    shared = _mm((_silu(gate) * up).astype(BF16), wsd_ref[...])
    wait_rows(slot)
    ew = ew_ref[0]
    acc = [shared[:, j * LANES:(j + 1) * LANES] for j in range(2 * nch)]
    for k in range(TOP_K):
        wk = ew[:, k:k + 1]
        for j, ch in enumerate(_load_row_tiled(buf_s.at[slot, k], tc)):
            y_lo, y_hi = _unpack_halves(ch)
            acc[j] = acc[j] + y_lo * wk
            acc[nch + j] = acc[nch + j] + y_hi * wk
    out_ref[0] = x1_ref[0] + g2_ref[0] * _rms(jnp.concatenate(acc, axis=1), nw3_ref[...])
    pos_copy(jnp.minimum(tile + 2, n_tiles - 1), slot).start()

    @pl.when(tile == n_tiles - 1)
    def _():
        wait_rows(other)
        pos_copy(n_tiles - 1, slot).wait()


def _combine(pos_tiles, ys, ew, x1, h2, g2, nw3, wsg, wsu, wsd, tc):
    b, s, d = x1.shape
    ds_ = wsg.shape[1]
    nt = s // tc
    c = ys.shape[1]
    kern = functools.partial(_combine_kernel, tc=tc, nt=nt, n_tiles=b * nt)
    tok = lambda w: pl.BlockSpec((1, tc, w), lambda bi, i: (bi, i, 0))
    return pl.pallas_call(
        kern,
        grid=(b, nt),
        in_specs=[pl.BlockSpec(memory_space=pl.ANY), pl.BlockSpec(memory_space=pl.ANY),
                  tok(TOP_K), tok(d), pl.BlockSpec((1, tc * c, LANES), lambda bi, i: (bi, i, 0)),
                  pl.BlockSpec((1, 1, d), lambda bi, i: (bi, 0, 0)),
                  pl.BlockSpec((1, d), lambda bi, i: (0, 0)),
                  pl.BlockSpec((d, ds_), lambda bi, i: (0, 0)),
                  pl.BlockSpec((d, ds_), lambda bi, i: (0, 0)),
                  pl.BlockSpec((ds_, d), lambda bi, i: (0, 0))],
        out_specs=tok(d),
        out_shape=jax.ShapeDtypeStruct((b, s, d), F32),
        scratch_shapes=[pltpu.VMEM((2, TOP_K, tc * c, LANES), U32), pltpu.SMEM((2, TOP_K, tc), I32),
                        pltpu.SemaphoreType.DMA((2,)), pltpu.SemaphoreType.DMA((2,))],
        compiler_params=_cparams(("arbitrary", "arbitrary")),
        name="combine",
    )(pos_tiles, ys, ew, x1, h2, g2, nw3, wsg, wsu, wsd)


def _tile(n, pref):
    t = min(n, pref)
    if n % t:
        raise NotImplementedError(f"size {n} is not a multiple of tile {t}")
    return t


def kernel(x, c, ctx, c_ctx, w_ada, b_ada, norms, w_in, ml_conv, ml_gate_b, ml_norm_w, attn_sink, w_out,
           w_router, b_router, w_exp_gate, w_exp_up, w_exp_down, w_sh_gate, w_sh_up, w_sh_down):
    if w_ada.shape[0] != 1:
        raise NotImplementedError("single-layer configuration only")
    b, s, d = x.shape
    lc = ctx.shape[1]
    t = b * s
    H = ML_HEADS
    dv = d // 2 // H
    dk = dv // 2
    dh = d // 2 // AT_HEADS
    qk_w, v_w = 2 * H * dk, H * dv
    ng = 4 * H
    aq_w, akv_w = AT_HEADS * dh, AT_KV_HEADS * dh
    e = w_router.shape[-1]
    nw = norms[0]

    rows = -(-(b + 1) // SUBLANES) * SUBLANES
    cond = jnp.zeros((rows, d), F32).at[:b].set(c).at[b].set(c_ctx)
    mods = _ada(cond, w_ada[0], b_ada[0][None, :])
    sh1, sc1, g1, sh2, sc2, g2 = [m[:b, None, :] for m in jnp.split(mods, 6, axis=-1)]
    csh1, csc1 = [jnp.broadcast_to(m[b:b + 1, None, :], (b, 1, d)) for m in jnp.split(mods, 6, axis=-1)[:2]]

    w0 = w_in[0]
    o_ml, o_g, o_q = 0, qk_w + 2 * v_w, qk_w + 2 * v_w + ng
    ml_w = qk_w + 2 * v_w
    wp = jnp.concatenate([w0[:, o_ml:o_g], w0[:, o_q:], w0[:, o_g:o_q], jnp.zeros((d, LANES - ng), F32)],
                         axis=1).astype(BF16)
    layout = (("ml", 0, ml_w), ("aq", ml_w, aq_w), ("ak", ml_w + aq_w, akv_w),
              ("av", ml_w + aq_w + akv_w, akv_w), ("g", ml_w + aq_w + 2 * akv_w, LANES))
    widths = dict(ml=ml_w, aq=aq_w, ak=akv_w, av=akv_w, g=LANES)
    tabs = _rope_tables(s, dh)
    proj = functools.partial(_inproj, layout=layout, widths=widths, dh=dh)
    ml, aq, ak, av, gates = proj(x, sh1, sc1, nw[0:1], wp, tabs, rope=True, tm=_tile(s, 512))
    tm_c = _tile(lc, 512)
    mlc, _, akc, avc, gates_c = proj(ctx, csh1, csc1, nw[0:1], wp, tuple(tb[:tm_c] for tb in tabs),
                                     rope=False, tm=tm_c)

    L = _tile(lc, 256)
    if s % L:
        raise NotImplementedError("sequence must be a multiple of the mLSTM chunk")
    gate_b = jnp.zeros((1, LANES), F32).at[0, :ng].set(ml_gate_b[0])
    gp = _gate_prep(gates, gate_b, L)
    gpc = _gate_prep(gates_c, gate_b, L)
    y_ml = _mlstm(ml, mlc, gp, gpc, ml_conv[0], ml_norm_w[0][None, :], L)

    y_at = _attn(aq, ak, av, akc, avc, attn_sink[0])

    x1, h2, logits = _outproj(y_ml, y_at, x, g1, sh2, sc2, nw[1:2], nw[2:3], w_out[0].astype(BF16),
                              w_router[0], _tile(s, 256))

    eidx, ew, rank, cnt = _route(logits.reshape(t, e), jnp.broadcast_to(b_router[0][:, None], (e, LANES)),
                                 _tile(t, 512))
    counts = cnt[:, 0].astype(I32)
    padded = (counts + MOE_BLOCK - 1) // MOE_BLOCK * MOE_BLOCK
    pad_end = jnp.cumsum(padded)
    pad_start = pad_end - padded
    n_blocks = -(-(t * TOP_K + e * (MOE_BLOCK - 1)) // MOE_BLOCK)
    n_rows = n_blocks * MOE_BLOCK
    blk_row = jnp.arange(n_blocks, dtype=I32) * MOE_BLOCK
    block_e = jnp.minimum(jnp.sum((pad_end[None, :] <= blk_row[:, None]).astype(I32), axis=1), e - 1)
    n_used = (pad_end[-1] // MOE_BLOCK).astype(I32)[None]
    used = counts > 0
    eid = jnp.arange(e, dtype=I32)
    next_used = jnp.concatenate([lax.cummin(jnp.where(used, eid, e), reverse=True)[1:], jnp.full((1,), e, I32)])
    next_used2 = jnp.concatenate([next_used, jnp.full((1,), e, I32)])[next_used]
    used_ord = jnp.cumsum(used.astype(I32)) - 1
    pos = _positions(eidx, rank, pad_start, _tile(t, 512))

    tt = _tile(t, 256)
    pos_d = pos.reshape(TOP_K, t // tt, tt).transpose(1, 0, 2)
    rc = d // 2 // LANES
    xs = _dispatch(h2.reshape(t, rc, LANES), pos_d, (pad_end - MOE_BLOCK).astype(I32), (padded > 0).astype(I32),
                   n_rows, tt)
    ys = _experts(xs.reshape(n_rows * rc, LANES), block_e, n_used, next_used, next_used2, used_ord,
                  w_exp_gate[0], w_exp_up[0], w_exp_down[0])
    tc = _tile(s, 128)
    pos_c = pos.reshape(TOP_K, t // tc, tc).transpose(1, 0, 2)
    ew_t = ew.T.reshape(b, s, TOP_K)
    return _combine(pos_c, ys.reshape(n_rows, rc, LANES), ew_t, x1, h2, g2, nw[3:4], w_sh_gate[0].astype(BF16), w_sh_up[0].astype(BF16),
                    w_sh_down[0].astype(BF16), tc)
```

```python
import functools

import jax
import jax.numpy as jnp
from jax import lax
from jax.experimental import pallas as pl
from jax.experimental.pallas import tpu as pltpu

F32 = jnp.float32
BF16 = jnp.bfloat16
I32 = jnp.int32

LANES = 128
SUBLANES = 8
VMEM_LIMIT_BYTES = 56 * 1024 * 1024

NORM_EPS = 1e-6
ML_HEADS = 4
CONV_W = 5
AT_HEADS = 8
AT_KV_HEADS = 2
GRID_W = 64
WINDOW = 128
BLOCK_Q = 128
ROPE_THETA = 10000.0
N_GROUPS = 8
TOPK_GROUPS = 4
TOP_K = 8
ROUTE_SCALE = 2.5
MOE_BLOCK = 128
NEG_BIG = -1e30


def _cparams(sem):
    return pltpu.CompilerParams(dimension_semantics=sem, vmem_limit_bytes=VMEM_LIMIT_BYTES)


def _rms(xf, w):
    return xf * lax.rsqrt(jnp.mean(xf * xf, axis=-1, keepdims=True) + NORM_EPS) * w


def _silu(x):
    return x * jax.nn.sigmoid(x)


def _mm(a, b):
    return jnp.dot(a, b, preferred_element_type=F32)


U32 = jnp.uint32


def _pack_halves(x):
    n = x.shape[1] // 2
    u = lax.bitcast_convert_type(x.astype(BF16).astype(F32), U32)
    return (u[:, :n] >> 16) | (u[:, n:] & jnp.uint32(0xFFFF0000))


def _unpack_halves(u):
    return (lax.bitcast_convert_type(u << 16, F32),
            lax.bitcast_convert_type(u & jnp.uint32(0xFFFF0000), F32))


def _store_row_tiled(ref2d, packed):
    r, n = packed.shape
    c = n // LANES
    for j in range(c):
        ref2d[pl.ds(j, r, stride=c), :] = packed[:, j * LANES:(j + 1) * LANES]


def _load_row_tiled(ref2d, r):
    c = ref2d.shape[0] // r
    return [ref2d[pl.ds(j, r, stride=c), :] for j in range(c)]


def _ada_kernel(c_ref, w_ref, b_ref, o_ref):
    s = _silu(c_ref[...]).astype(BF16)
    o_ref[...] = _mm(s, w_ref[...].astype(BF16)) + b_ref[...]


def _ada(cond, w, b):
    r, d = cond.shape
    n = w.shape[1]
    tn = min(n, 1024)
    return pl.pallas_call(
        _ada_kernel,
        grid=(n // tn,),
        in_specs=[pl.BlockSpec((r, d), lambda j: (0, 0)),
                  pl.BlockSpec((d, tn), lambda j: (0, j)),
                  pl.BlockSpec((1, tn), lambda j: (0, j))],
        out_specs=pl.BlockSpec((r, tn), lambda j: (0, j)),
        out_shape=jax.ShapeDtypeStruct((r, n), F32),
        compiler_params=_cparams(("arbitrary",)),
        name="ada",
    )(cond, w, b)


def _inproj_kernel(x_ref, sh_ref, sc_ref, nw_ref, w_ref, rc_ref, ra_ref, rb_ref,
                   ml_ref, aq_ref, ak_ref, av_ref, g_ref, hb_ref, *, layout, dh, rope, qscale, cw):
    h = _rms(x_ref[0], nw_ref[...]) * (1.0 + sc_ref[0]) + sh_ref[0]
    hb_ref[...] = h.astype(BF16)
    outs = dict(ml=ml_ref, aq=aq_ref, ak=ak_ref, av=av_ref, g=g_ref)
    for name, col0, width in layout:
        o_ref = outs[name]
        for c in range(0, width, cw):
            step = min(cw, width - c)
            acc = _mm(hb_ref[...], w_ref[:, col0 + c:col0 + c + step])
            if rope and name in ("aq", "ak"):
                for hh in range(step // dh):
                    a = acc[:, hh * dh:(hh + 1) * dh]
                    r = (a * rc_ref[...] + pltpu.roll(a, dh - dh // 4, 1) * ra_ref[...]
                         + pltpu.roll(a, dh // 4, 1) * rb_ref[...])
                    if name == "aq":
                        r = r * qscale
                    o_ref[0, :, c + hh * dh:c + (hh + 1) * dh] = r.astype(o_ref.dtype)
            else:
                if name == "aq":
                    acc = acc * qscale
                o_ref[0, :, c:c + step] = acc.astype(o_ref.dtype)


def _inproj(x, sh, sc, nw, wp, tabs, *, layout, widths, dh, rope, tm):
    b, s, d = x.shape
    npad = wp.shape[1]
    rc, ra, rb = tabs
    kern = functools.partial(_inproj_kernel, layout=layout, dh=dh, rope=rope, qscale=float(dh) ** -0.5, cw=512)
    vec = pl.BlockSpec((1, 1, d), lambda bi, i: (bi, 0, 0))
    tab = pl.BlockSpec((tm, dh), lambda bi, i: (i, 0))
    out_dt = dict(ml=BF16, aq=BF16, ak=BF16, av=BF16, g=F32)
    names = ("ml", "aq", "ak", "av", "g")
    return pl.pallas_call(
        kern,
        grid=(b, s // tm),
        in_specs=[pl.BlockSpec((1, tm, d), lambda bi, i: (bi, i, 0)), vec, vec,
                  pl.BlockSpec((1, d), lambda bi, i: (0, 0)),
                  pl.BlockSpec((d, npad), lambda bi, i: (0, 0), pipeline_mode=pl.Buffered(1)),
                  tab, tab, tab],
        out_specs=[pl.BlockSpec((1, tm, widths[n]), lambda bi, i: (bi, i, 0)) for n in names],
        out_shape=[jax.ShapeDtypeStruct((b, s, widths[n]), out_dt[n]) for n in names],
        scratch_shapes=[pltpu.VMEM((tm, d), BF16)],
        compiler_params=_cparams(("arbitrary", "arbitrary")),
        name="inproj_rope" if rope else "inproj_ctx",
    )(x, sh, sc, nw, wp, rc, ra, rb)


def _rope_tables(s, dh):
    rows = s // GRID_W
    row = jnp.repeat(jnp.arange(rows), GRID_W)
    col = jnp.tile(jnp.arange(GRID_W), rows)
    nf = dh // 4
    freqs = ROPE_THETA ** (-jnp.arange(nf, dtype=F32) / nf)
    pos = jnp.stack([row, col], axis=-1).astype(F32)
    ang = pos[:, :, None] * freqs
    cos, sin = jnp.cos(ang), jnp.sin(ang)
    z = jnp.zeros_like(sin[:, 0])
    rc = jnp.concatenate([cos[:, 0], cos[:, 0], cos[:, 1], cos[:, 1]], axis=-1)
    ra = jnp.concatenate([-sin[:, 0], z, -sin[:, 1], z], axis=-1)
    rb = jnp.concatenate([z, sin[:, 0], z, sin[:, 1]], axis=-1)
    return rc, ra, rb


def _gate_prep_kernel(g_ref, b_ref, gl_ref, pp_ref, glt_ref, ppt_ref, tot_ref, *, L, H, nc):
    for c in range(nc):
        z = g_ref[0, c * L:(c + 1) * L, :] + b_ref[...]
        lane = lax.broadcasted_iota(I32, z.shape, 1)
        row = lax.broadcasted_iota(I32, z.shape, 0)
        is_f = ((lane // H) % 2 == 1) & (lane < 4 * H)
        log_sig = jnp.minimum(z, 0.0) - jnp.log1p(jnp.exp(-jnp.abs(z)))
        gl = jnp.where(lane < 4 * H, jnp.where(is_f, log_sig, z), 0.0)
        cs = gl
        k = 1
        while k < L:
            cs = cs + jnp.where(row >= k, pltpu.roll(cs, k, 0), 0.0)
            k *= 2
        tot = cs[L - 1:L, :]
        suf = tot - cs + gl
        pp = jnp.where(lane >= 2 * H, suf, cs)
        gl_ref[0, c] = gl
        pp_ref[0, c] = pp
        glt_ref[0, c] = gl.T
        ppt_ref[0, c] = pp.T
        tot_ref[0, c] = jnp.broadcast_to(tot, (SUBLANES, LANES))


def _gate_prep(g, gate_b, L):
    b, s, _ = g.shape
    nc = s // L
    kern = functools.partial(_gate_prep_kernel, L=L, H=ML_HEADS, nc=nc)
    col = pl.BlockSpec((1, nc, L, LANES), lambda bi: (bi, 0, 0, 0))
    rowb = pl.BlockSpec((1, nc, LANES, L), lambda bi: (bi, 0, 0, 0))
    return pl.pallas_call(
        kern,
        grid=(b,),
        in_specs=[pl.BlockSpec((1, s, LANES), lambda bi: (bi, 0, 0)),
                  pl.BlockSpec((1, LANES), lambda bi: (0, 0))],
        out_specs=[col, col, rowb, rowb, pl.BlockSpec((1, nc, SUBLANES, LANES), lambda bi: (bi, 0, 0, 0))],
        out_shape=[jax.ShapeDtypeStruct((b, nc, L, LANES), F32), jax.ShapeDtypeStruct((b, nc, L, LANES), F32),
                   jax.ShapeDtypeStruct((b, nc, LANES, L), F32), jax.ShapeDtypeStruct((b, nc, LANES, L), F32),
                   jax.ShapeDtypeStruct((b, nc, SUBLANES, LANES), F32)],
        compiler_params=_cparams(("arbitrary",)),
        name="gate_prep",
    )(g, gate_b)


CONV_PAD = 8


def _lane_pick(tile, ch):
    lane = lax.broadcasted_iota(I32, tile.shape, 1)
    return jnp.sum(jnp.where(lane == ch, tile, 0.0), axis=1, keepdims=True)


def _sublane_pick(tile, ch):
    sub = lax.broadcasted_iota(I32, tile.shape, 0)
    return jnp.sum(jnp.where(sub == ch, tile, 0.0), axis=0, keepdims=True)


def _mlstm_chunk(d, ch_i, ch_f, q, k, v, gl, pp, glt, ppt, tot8, ct_ref, n_ref, m_ref):
    L = k.shape[0]
    p_col = _lane_pick(pp, ch_f)
    li_col = _lane_pick(gl, ch_i)
    btot = _lane_pick(tot8[0:1], ch_f)
    m_old = m_ref[d][0:1, 0:1]
    ct = ct_ref[d]
    n_row = n_ref[d]
    h = None
    if q is not None:
        p_row = _sublane_pick(ppt, ch_f)
        li_row = _sublane_pick(glt, ch_i)
        ti = lax.broadcasted_iota(I32, (L, L), 0)
        si = lax.broadcasted_iota(I32, (L, L), 1)
        valid = (si <= ti) if d == 0 else (si >= ti)
        logw = jnp.where(valid, p_col - p_row + li_row, NEG_BIG)
        log_inter = p_col + m_old
        m_q = jnp.maximum(log_inter, jnp.max(logw, axis=1, keepdims=True))
        sqk = lax.dot_general(q, k, (((1,), (1,)), ((), ())), preferred_element_type=F32) * jnp.exp(logw - m_q)
        w_inter = jnp.exp(log_inter - m_q)
        num = _mm(sqk.astype(BF16), v) + w_inter * _mm(q, ct.astype(BF16))
        den = (jnp.sum(sqk, axis=1, keepdims=True)
               + w_inter * jnp.sum(q.astype(F32) * n_row, axis=1, keepdims=True))
        h = num / jnp.maximum(jnp.abs(den), jnp.exp(-m_q))
    log_w_end = btot - p_col + li_col
    m_new = jnp.maximum(btot + m_old, jnp.max(log_w_end, axis=0, keepdims=True))
    w_end = jnp.exp(log_w_end - m_new)
    decay = jnp.exp(btot + m_old - m_new)
    kw = k.astype(F32) * w_end
    ct_ref[d] = decay * ct + lax.dot_general(kw.astype(BF16), v, (((0,), (0,)), ((), ())),
                                             preferred_element_type=F32)
    n_ref[d] = decay * n_row + jnp.sum(kw, axis=0, keepdims=True)
    m_ref[d] = jnp.broadcast_to(m_new, (SUBLANES, LANES))
    return h


def _mlstm_kernel(q_ref, k_ref, v_ref, o_ref, kc_ref, vc_ref,
                  gl_ref, pp_ref, glt_ref, ppt_ref, tot_ref, glc_ref, ppc_ref, totc_ref,
                  cwq_ref, cwk_ref, nw_ref, y_ref,
                  xq_s, xk_s, xkc_s, qs_s, ks_s, kcs_s, hf_s, hb_s, ct_s, n_s, m_s,
                  *, L, H, S, Lc, dk, dv):
    hh = pl.program_id(1)
    nc, ncc = S // L, Lc // L
    zpad = jnp.zeros((CONV_PAD, dk), F32)

    def conv_silu(x_s, cw_ref, n_rows, out_s, scale):
        for c in range(n_rows // L):
            acc = jnp.zeros((L, dk), F32)
            for j in range(CONV_W):
                acc = acc + cw_ref[j:j + 1, :] * x_s[pl.ds(CONV_PAD + c * L + j - CONV_W // 2, L), :]
            out_s[c] = (_silu(acc) * scale).astype(BF16)

    def stage(x_s, src, n_rows):
        x_s[0:CONV_PAD, :] = zpad
        x_s[CONV_PAD + n_rows:2 * CONV_PAD + n_rows, :] = zpad
        x_s[CONV_PAD:CONV_PAD + n_rows, :] = src.astype(F32)

    stage(xq_s, q_ref[0], S)
    stage(xk_s, k_ref[0], S)
    stage(xkc_s, kc_ref[0], Lc)
    conv_silu(xq_s, cwq_ref, S, qs_s, float(dk) ** -0.5)
    conv_silu(xk_s, cwk_ref, S, ks_s, 1.0)
    conv_silu(xkc_s, cwk_ref, Lc, kcs_s, 1.0)

    ct_s[...] = jnp.zeros(ct_s.shape, F32)
    n_s[...] = jnp.zeros(n_s.shape, F32)
    m_s[...] = jnp.zeros(m_s.shape, F32)

    def chans(d):
        return d * 2 * H + hh, d * 2 * H + H + hh

    for d in (0, 1):
        ch_i, ch_f = chans(d)
        for c in (range(ncc) if d == 0 else range(ncc - 1, -1, -1)):
            _mlstm_chunk(d, ch_i, ch_f, None, kcs_s[c], vc_ref[0, c * L:(c + 1) * L, :],
                         glc_ref[0, c], ppc_ref[0, c], None, None, totc_ref[0, c], ct_s, n_s, m_s)

    def body(i, carry):
        for d in (0, 1):
            ch_i, ch_f = chans(d)
            c = i if d == 0 else nc - 1 - i
            r0 = pl.multiple_of(c * L, L)
            h = _mlstm_chunk(d, ch_i, ch_f, qs_s[c], ks_s[c], v_ref[0, pl.ds(r0, L), :],
                             gl_ref[0, c], pp_ref[0, c], glt_ref[0, c], ppt_ref[0, c], tot_ref[0, c],
                             ct_s, n_s, m_s)
            if d == 0:
                hf_s[c] = h
            else:
                hb_s[c] = h
        return carry

    lax.fori_loop(0, nc, body, 0)

    for c in range(nc):
        hs = hf_s[c] + hb_s[c]
        hn = _rms(hs, nw_ref[...])
        y_ref[0, c * L:(c + 1) * L, :] = (hn * jax.nn.sigmoid(o_ref[0, c * L:(c + 1) * L, :].astype(F32))).astype(BF16)


def _mlstm(ml, mlc, gp, gpc, conv_w, norm_w, L):
    b, s, _ = ml.shape
    lc = mlc.shape[1]
    H = ML_HEADS
    dv = norm_w.shape[1] // H
    dk = dv // 2
    nc, ncc = s // L, lc // L
    gl, pp, glt, ppt, tot = gp
    glc, ppc, _, _, totc = gpc
    kern = functools.partial(_mlstm_kernel, L=L, H=H, S=s, Lc=lc, dk=dk, dv=dv)

    def colspec(rows, w, off):
        return pl.BlockSpec((1, rows, w), lambda bi, h: (bi, 0, off + h))

    def gspec(n, r, c):
        return pl.BlockSpec((1, n, r, c), lambda bi, h: (bi, 0, 0, 0))

    return pl.pallas_call(
        kern,
        grid=(b, H),
        in_specs=[colspec(s, dk, 0), colspec(s, dk, H), colspec(s, dv, H), colspec(s, dv, 2 * H),
                  colspec(lc, dk, H), colspec(lc, dv, H),
                  gspec(nc, L, LANES), gspec(nc, L, LANES), gspec(nc, LANES, L), gspec(nc, LANES, L),
                  gspec(nc, SUBLANES, LANES),
                  gspec(ncc, L, LANES), gspec(ncc, L, LANES), gspec(ncc, SUBLANES, LANES),
                  pl.BlockSpec((CONV_W, dk), lambda bi, h: (0, h)),
                  pl.BlockSpec((CONV_W, dk), lambda bi, h: (0, H + h)),
                  pl.BlockSpec((1, dv), lambda bi, h: (0, h))],
        out_specs=pl.BlockSpec((1, s, dv), lambda bi, h: (bi, 0, h)),
        out_shape=jax.ShapeDtypeStruct((b, s, H * dv), BF16),
        scratch_shapes=[pltpu.VMEM((s + 2 * CONV_PAD, dk), F32), pltpu.VMEM((s + 2 * CONV_PAD, dk), F32),
                        pltpu.VMEM((lc + 2 * CONV_PAD, dk), F32),
                        pltpu.VMEM((nc, L, dk), BF16), pltpu.VMEM((nc, L, dk), BF16), pltpu.VMEM((ncc, L, dk), BF16),
                        pltpu.VMEM((nc, L, dv), F32), pltpu.VMEM((nc, L, dv), F32),
                        pltpu.VMEM((2, dk, dv), F32), pltpu.VMEM((2, 1, dk), F32),
                        pltpu.VMEM((2, SUBLANES, LANES), F32)],
        compiler_params=_cparams(("arbitrary", "arbitrary")),
        name="mlstm",
    )(ml, ml, ml, ml, mlc, mlc, gl, pp, glt, ppt, tot, glc, ppc, totc, conv_w, conv_w, norm_w)


def _attn_kernel(sink_ref, q_ref, kp_ref, kc_ref, kn_ref, vp_ref, vc_ref, vn_ref, kx_ref, vx_ref, bias_ref,
                 o_ref, *, bq, dh, G):
    q = q_ref[0]
    for g in range(AT_KV_HEADS):
        sl = slice(g * dh, (g + 1) * dh)
        qg = jnp.concatenate([q[:, (g * G + j) * dh:(g * G + j + 1) * dh] for j in range(G)], axis=0)
        kcat = jnp.concatenate([kp_ref[0][:, sl], kc_ref[0][:, sl], kn_ref[0][:, sl], kx_ref[0][:, sl]], axis=0)
        vcat = jnp.concatenate([vp_ref[0][:, sl], vc_ref[0][:, sl], vn_ref[0][:, sl], vx_ref[0][:, sl]], axis=0)
        s = lax.dot_general(qg, kcat, (((1,), (1,)), ((), ())), preferred_element_type=F32) + bias_ref[0]
        ri = lax.broadcasted_iota(I32, (G * bq, 1), 0)
        sk = jnp.zeros((G * bq, 1), F32)
        for j in range(G):
            sk = jnp.where((ri >= j * bq) & (ri < (j + 1) * bq), sink_ref[g * G + j], sk)
        m = jnp.maximum(jnp.max(s, axis=1, keepdims=True), sk)
        p = jnp.exp(s - m)
        den = jnp.sum(p, axis=1, keepdims=True) + jnp.exp(sk - m)
        o = _mm(p.astype(BF16), vcat) / den
        for j in range(G):
            o_ref[0, :, (g * G + j) * dh:(g * G + j + 1) * dh] = o[j * bq:(j + 1) * bq, :].astype(BF16)


def _attn_bias(bq, lc, G):
    qoff = (jnp.arange(G * bq) % bq)[:, None]
    ci = jnp.arange(3 * bq + lc)[None, :]
    prev = (ci < bq) & (qoff <= ci)
    cur = (ci >= bq) & (ci < 2 * bq)
    nxt = (ci >= 2 * bq) & (ci < 3 * bq) & (ci - 2 * bq <= qoff)
    ctx = ci >= 3 * bq
    inner = prev | cur | nxt | ctx
    first = cur | nxt | ctx
    last = prev | cur | ctx
    if WINDOW != bq:
        raise NotImplementedError("window must equal the query block")
    return jnp.where(jnp.stack([inner, first, last]), 0.0, NEG_BIG).astype(F32)


def _attn(aq, ak, av, akc, avc, sink):
    b, s, hd = aq.shape
    dh = hd // AT_HEADS
    G = AT_HEADS // AT_KV_HEADS
    bq = BLOCK_Q
    nb = s // bq
    lc = akc.shape[1]
    kvw = AT_KV_HEADS * dh
    bias = _attn_bias(bq, lc, G)
    if nb < 2:
        raise NotImplementedError("needs at least two query blocks")
    kern = functools.partial(_attn_kernel, bq=bq, dh=dh, G=G)
    prev = pl.BlockSpec((1, bq, kvw), lambda bi, n: (bi, jnp.maximum(n - 1, 0), 0))
    cur = pl.BlockSpec((1, bq, kvw), lambda bi, n: (bi, n, 0))
    nxt = pl.BlockSpec((1, bq, kvw), lambda bi, n: (bi, jnp.minimum(n + 1, nb - 1), 0))
    cx = pl.BlockSpec((1, lc, kvw), lambda bi, n: (bi, 0, 0))
    return pl.pallas_call(
        kern,
        grid=(b, nb),
        in_specs=[pl.BlockSpec(memory_space=pltpu.SMEM),
                  pl.BlockSpec((1, bq, hd), lambda bi, n: (bi, n, 0)),
                  prev, cur, nxt, prev, cur, nxt, cx, cx,
                  pl.BlockSpec((1, G * bq, 3 * bq + lc),
                               lambda bi, n: (jnp.where(n == 0, 1, jnp.where(n == nb - 1, 2, 0)), 0, 0))],
        out_specs=pl.BlockSpec((1, bq, hd), lambda bi, n: (bi, n, 0)),
        out_shape=jax.ShapeDtypeStruct((b, s, hd), BF16),
        compiler_params=_cparams(("arbitrary", "arbitrary")),
        name="attn",
    )(sink, aq, ak, ak, ak, av, av, av, akc, avc, bias)


def _outproj_kernel(ym_ref, ya_ref, x_ref, g1_ref, sh2_ref, sc2_ref, nw1_ref, nw2_ref, wo_ref, wrh_ref, wrl_ref,
                    x1_ref, h2_ref, lg_ref, y_s, *, dm, cw):
    d = y_s.shape[1]
    for c in range(0, d, cw):
        y_s[:, c:c + cw] = _mm(ym_ref[0], wo_ref[0:dm, c:c + cw]) + _mm(ya_ref[0], wo_ref[dm:, c:c + cw])
    x1 = x_ref[0] + g1_ref[0] * _rms(y_s[...], nw1_ref[...])
    x1_ref[0] = x1
    h2 = _rms(x1, nw2_ref[...]) * (1.0 + sc2_ref[0]) + sh2_ref[0]
    hi = h2.astype(BF16)
    _store_row_tiled(h2_ref.at[0], _pack_halves(h2))
    lo = (h2 - hi.astype(F32)).astype(BF16)
    lg_ref[0] = _mm(hi, wrh_ref[...]) + (_mm(hi, wrl_ref[...]) + _mm(lo, wrh_ref[...]))


def _outproj(y_ml, y_at, x, g1, sh2, sc2, nw1, nw2, wo, wr, tm):
    b, s, d = x.shape
    dm = y_ml.shape[2]
    e = wr.shape[1]
    wr_hi = wr.astype(BF16)
    wr_lo = (wr - wr_hi.astype(F32)).astype(BF16)
    kern = functools.partial(_outproj_kernel, dm=dm, cw=min(d, 512))
    vec = pl.BlockSpec((1, 1, d), lambda bi, i: (bi, 0, 0))
    nspec = pl.BlockSpec((1, d), lambda bi, i: (0, 0))
    return pl.pallas_call(
        kern,
        grid=(b, s // tm),
        in_specs=[pl.BlockSpec((1, tm, dm), lambda bi, i: (bi, i, 0)),
                  pl.BlockSpec((1, tm, d - dm), lambda bi, i: (bi, i, 0)),
                  pl.BlockSpec((1, tm, d), lambda bi, i: (bi, i, 0)),
                  vec, vec, vec, nspec, nspec,
                  pl.BlockSpec((d, d), lambda bi, i: (0, 0), pipeline_mode=pl.Buffered(1)),
                  pl.BlockSpec((d, e), lambda bi, i: (0, 0), pipeline_mode=pl.Buffered(1)),
                  pl.BlockSpec((d, e), lambda bi, i: (0, 0), pipeline_mode=pl.Buffered(1))],
        out_specs=[pl.BlockSpec((1, tm, d), lambda bi, i: (bi, i, 0)),
                   pl.BlockSpec((1, tm * (d // 2 // LANES), LANES), lambda bi, i: (bi, i, 0)),
                   pl.BlockSpec((1, tm, e), lambda bi, i: (bi, i, 0))],
        out_shape=[jax.ShapeDtypeStruct((b, s, d), F32), jax.ShapeDtypeStruct((b, s * (d // 2 // LANES), LANES), U32),
                   jax.ShapeDtypeStruct((b, s, e), F32)],
        scratch_shapes=[pltpu.VMEM((tm, d), F32)],
        compiler_params=_cparams(("arbitrary", "arbitrary")),
        name="outproj",
    )(y_ml, y_at, x, g1, sh2, sc2, nw1, nw2, wo, wr_hi, wr_lo)


def _route_kernel(lg_ref, br_ref, eidx_ref, ew_ref, rank_ref, cnt_ref, carry_s, *, E, tr):
    i = pl.program_id(0)

    @pl.when(i == 0)
    def _():
        carry_s[...] = jnp.zeros(carry_s.shape, F32)

    scores = jax.nn.sigmoid(lg_ref[...].T)
    biased = scores + br_ref[:, 0:1]
    row = lax.broadcasted_iota(I32, (E, tr), 0).astype(F32)
    gs = E // N_GROUPS
    ninf = -jnp.inf
    grp = []
    for g in range(N_GROUPS):
        xg = biased[g * gs:(g + 1) * gs]
        rg = (lax.broadcasted_iota(I32, (gs, tr), 0) + g * gs).astype(F32)
        m1 = jnp.max(xg, axis=0, keepdims=True)
        i1 = jnp.min(jnp.where(xg == m1, rg, float(E)), axis=0, keepdims=True)
        m2 = jnp.max(jnp.where(rg == i1, ninf, xg), axis=0, keepdims=True)
        grp.append(m1 + m2)
    gsc = jnp.concatenate(grp, axis=0)
    gi = lax.broadcasted_iota(I32, (N_GROUPS, tr), 0)
    beaten = jnp.zeros((N_GROUPS, tr), I32)
    for g2 in range(N_GROUPS):
        sg = gsc[g2:g2 + 1]
        beaten = beaten + jnp.where((sg > gsc) | ((sg == gsc) & (gi > g2)), 1, 0)
    keep_g = jnp.where(beaten < TOPK_GROUPS, 1.0, 0.0)
    keep = jnp.concatenate([jnp.broadcast_to(keep_g[g:g + 1], (gs, tr)) for g in range(N_GROUPS)], axis=0)
    masked = jnp.where(keep > 0.5, biased, ninf)
    idxs, ws = [], []
    for _ in range(TOP_K):
        m = jnp.max(masked, axis=0, keepdims=True)
        ik = jnp.min(jnp.where(masked == m, row, float(E)), axis=0, keepdims=True)
        sel = row == ik
        ws.append(jnp.sum(jnp.where(sel, scores, 0.0), axis=0, keepdims=True))
        idxs.append(ik)
        masked = jnp.where(sel, ninf, masked)
    w = jnp.concatenate(ws, axis=0)
    ew_ref[...] = w / jnp.sum(w, axis=0, keepdims=True) * ROUTE_SCALE
    eidx_ref[...] = jnp.concatenate(idxs, axis=0).astype(I32)

    assign = jnp.zeros((E, tr), F32)
    for ik in idxs:
        assign = assign + jnp.where(row == ik, 1.0, 0.0)
    upper = jnp.where(lax.broadcasted_iota(I32, (tr, tr), 0) < lax.broadcasted_iota(I32, (tr, tr), 1), 1.0, 0.0)
    base = _mm(assign.astype(BF16), upper.astype(BF16)) + carry_s[:, 0:1]
    ranks = [jnp.sum(jnp.where(row == ik, base, 0.0), axis=0, keepdims=True) for ik in idxs]
    rank_ref[...] = jnp.concatenate(ranks, axis=0).astype(I32)
    carry_s[...] = carry_s[...] + jnp.sum(assign, axis=1, keepdims=True)
    cnt_ref[...] = carry_s[...]


def _route(logits, b_router, tr):
    t, e = logits.shape
    kern = functools.partial(_route_kernel, E=e, tr=tr)
    kt = pl.BlockSpec((TOP_K, tr), lambda i: (0, i))
    return pl.pallas_call(
        kern,
        grid=(t // tr,),
        in_specs=[pl.BlockSpec((tr, e), lambda i: (i, 0)), pl.BlockSpec((e, LANES), lambda i: (0, 0))],
        out_specs=[kt, kt, kt, pl.BlockSpec((e, LANES), lambda i: (0, 0))],
        out_shape=[jax.ShapeDtypeStruct((TOP_K, t), I32), jax.ShapeDtypeStruct((TOP_K, t), F32),
                   jax.ShapeDtypeStruct((TOP_K, t), I32), jax.ShapeDtypeStruct((e, LANES), F32)],
        scratch_shapes=[pltpu.VMEM((e, LANES), F32)],
        compiler_params=_cparams(("arbitrary",)),
        name="route",
    )(logits, b_router)


def _positions_kernel(eidx_ref, rank_ref, ps_ref, pos_ref, *, E, tr):
    row = lax.broadcasted_iota(I32, (E, tr), 0)
    start = ps_ref[:, 0:1]
    rows = [jnp.sum(jnp.where(row == eidx_ref[k:k + 1, :], start, 0.0), axis=0, keepdims=True)
            for k in range(TOP_K)]
    pos_ref[...] = jnp.concatenate(rows, axis=0).astype(I32) + rank_ref[...]


def _positions(eidx, rank, pad_start, tr):
    t = eidx.shape[1]
    e = pad_start.shape[0]
    kern = functools.partial(_positions_kernel, E=e, tr=tr)
    kt = pl.BlockSpec((TOP_K, tr), lambda i: (0, i))
    return pl.pallas_call(
        kern,
        grid=(t // tr,),
        in_specs=[kt, kt, pl.BlockSpec((e, LANES), lambda i: (0, 0))],
        out_specs=kt,
        out_shape=jax.ShapeDtypeStruct((TOP_K, t), I32),
        compiler_params=_cparams(("arbitrary",)),
        name="positions",
    )(eidx, rank, jnp.broadcast_to(pad_start.astype(F32)[:, None], (e, LANES)))


def _dispatch_kernel(zstart_ref, zvalid_ref, pos_hbm, h2_ref, xs_hbm, zero_s, pos_s, sem_p, sem_z, sem_r,
                     *, E, tt):
    i = pl.program_id(0)

    def zero_copy(e):
        start = pl.multiple_of(zstart_ref[e], MOE_BLOCK)
        return pltpu.make_async_copy(zero_s, xs_hbm.at[pl.ds(start, MOE_BLOCK)], sem_z)

    @pl.when(i == 0)
    def _():
        zero_s[...] = jnp.zeros(zero_s.shape, U32)

        def start(e, c):
            @pl.when(zvalid_ref[e] > 0)
            def _():
                zero_copy(e).start()
            return c

        def wait(e, c):
            @pl.when(zvalid_ref[e] > 0)
            def _():
                zero_copy(e).wait()
            return c

        lax.fori_loop(0, E, start, 0)
        lax.fori_loop(0, E, wait, 0)

    pos_copy = pltpu.make_async_copy(pos_hbm.at[i], pos_s, sem_p)
    pos_copy.start()
    pos_copy.wait()

    def row_copy(t, k):
        return pltpu.make_async_copy(h2_ref.at[t], xs_hbm.at[pos_s[k, t]], sem_r)

    def start_rows(t, c):
        for k in range(TOP_K):
            row_copy(t, k).start(priority=k % 2)
        return c

    def wait_rows(t, c):
        for k in range(TOP_K):
            row_copy(t, k).wait()
        return c

    lax.fori_loop(0, tt, start_rows, 0)
    lax.fori_loop(0, tt, wait_rows, 0)


def _dispatch(h2, pos_tiles, zstart, zvalid, n_rows, tt):
    t, c, _ = h2.shape
    e = zstart.shape[0]
    kern = functools.partial(_dispatch_kernel, E=e, tt=tt)
    return pl.pallas_call(
        kern,
        grid_spec=pltpu.PrefetchScalarGridSpec(
            num_scalar_prefetch=2,
            grid=(t // tt,),
            in_specs=[pl.BlockSpec(memory_space=pl.ANY),
                      pl.BlockSpec((tt, c, LANES), lambda i, zs, zv: (i, 0, 0))],
            out_specs=pl.BlockSpec(memory_space=pl.ANY),
            scratch_shapes=[pltpu.VMEM((MOE_BLOCK, c, LANES), U32), pltpu.SMEM((TOP_K, tt), I32),
                            pltpu.SemaphoreType.DMA, pltpu.SemaphoreType.DMA, pltpu.SemaphoreType.DMA]),
        out_shape=jax.ShapeDtypeStruct((n_rows, c, LANES), U32),
        compiler_params=_cparams(("arbitrary",)),
        name="dispatch",
    )(zstart, zvalid, pos_tiles, h2)


def _experts_kernel(be_ref, nu_ref, nxt_ref, nxt2_ref, ord_ref, xs_ref, wg_hbm, wu_hbm, wd_hbm, ys_ref,
                    sg_s, su_s, sd_s, wgb_s, wub_s, wdb_s, sem, *, E):
    b = pl.program_id(0)
    n_used = nu_ref[0]

    def copies(e, slot):
        out = []
        for m, (w_hbm, st) in enumerate(((wg_hbm, sg_s), (wu_hbm, su_s), (wd_hbm, sd_s))):
            rows = w_hbm.shape[1] // 2
            for p in range(2):
                out.append(pltpu.make_async_copy(w_hbm.at[e, pl.ds(p * rows, rows)],
                                                 st.at[slot, pl.ds(p * rows, rows)], sem.at[slot, 2 * m + p]))
        return out

    def start_all(cps):
        for cp in cps:
            cp.start(priority=1)

    @pl.when(b < n_used)
    def _():
        e = be_ref[b]
        prev = be_ref[jnp.maximum(b - 1, 0)]
        slot = ord_ref[e] & 1

        @pl.when(b == 0)
        def _():
            start_all(copies(e, slot))
            ne = nxt_ref[e]

            @pl.when(ne < E)
            def _():
                start_all(copies(ne, 1 - slot))

        @pl.when((b == 0) | (e != prev))
        def _():
            for cp in copies(e, slot):
                cp.wait()
            wgb_s[...] = sg_s[slot].astype(BF16)
            wub_s[...] = su_s[slot].astype(BF16)
            wdb_s[...] = sd_s[slot].astype(BF16)
            n2 = nxt2_ref[e]

            @pl.when(n2 < E)
            def _():
                start_all(copies(n2, slot))

        parts = [_unpack_halves(ch) for ch in _load_row_tiled(xs_ref, MOE_BLOCK)]
        x_lo = jnp.concatenate([p[0] for p in parts], axis=1).astype(BF16)
        x_hi = jnp.concatenate([p[1] for p in parts], axis=1).astype(BF16)
        half = x_lo.shape[1]
        gate = _mm(x_lo, wgb_s[0:half, :]) + _mm(x_hi, wgb_s[half:, :])
        up = _mm(x_lo, wub_s[0:half, :]) + _mm(x_hi, wub_s[half:, :])
        a = _silu(gate) * up
        _store_row_tiled(ys_ref, _pack_halves(_mm(a.astype(BF16), wdb_s[...])))

    @pl.when(b >= n_used)
    def _():
        ys_ref[...] = jnp.zeros(ys_ref.shape, U32)


def _experts(xs, block_e, n_used, next_used, next_used2, used_ord, wg, wu, wd):
    e, d, de = wg.shape
    c = d // 2 // LANES
    nblk = xs.shape[0] // (MOE_BLOCK * c)
    kern = functools.partial(_experts_kernel, E=e)

    def live(b, be, nu, nx, nx2, od):
        return (jnp.minimum(b, jnp.maximum(nu[0] - 1, 0)), 0)

    hbm = pl.BlockSpec(memory_space=pl.ANY)
    return pl.pallas_call(
        kern,
        grid_spec=pltpu.PrefetchScalarGridSpec(
            num_scalar_prefetch=5,
            grid=(nblk,),
            in_specs=[pl.BlockSpec((MOE_BLOCK * c, LANES), live), hbm, hbm, hbm],
            out_specs=pl.BlockSpec((MOE_BLOCK * c, LANES), lambda b, be, nu, nx, nx2, od: (b, 0)),
            scratch_shapes=[pltpu.VMEM((2, d, de), F32), pltpu.VMEM((2, d, de), F32), pltpu.VMEM((2, de, d), F32),
                            pltpu.VMEM((d, de), BF16), pltpu.VMEM((d, de), BF16), pltpu.VMEM((de, d), BF16),
                            pltpu.SemaphoreType.DMA((2, 6))]),
        out_shape=jax.ShapeDtypeStruct(xs.shape, U32),
        compiler_params=_cparams(("arbitrary",)),
        name="experts",
    )(block_e, n_used, next_used, next_used2, used_ord, xs, wg, wu, wd)


def _combine_kernel(pos_hbm, ys_hbm, ew_ref, x1_ref, h2_ref, g2_ref, nw3_ref, wsg_ref, wsu_ref, wsd_ref,
                    out_ref, buf_s, pos_s, sem_p, sem_r, *, tc, nt, n_tiles):
    tile = pl.program_id(0) * nt + pl.program_id(1)
    slot = tile & 1
    other = 1 - slot
    c = ys_hbm.shape[1]

    def pos_copy(tl, sl):
        return pltpu.make_async_copy(pos_hbm.at[tl], pos_s.at[sl], sem_p.at[sl])

    def row_copy(sl, t, k):
        return pltpu.make_async_copy(ys_hbm.at[pos_s[sl, k, t]],
                                     buf_s.at[sl, k, pl.ds(pl.multiple_of(t * c, c), c)], sem_r.at[sl])

    def start_rows(sl):
        def body(t, carry):
            for k in range(TOP_K):
                row_copy(sl, t, k).start(priority=k % 2)
            return carry
        lax.fori_loop(0, tc, body, 0)

    def wait_rows(sl):
        def body(t, carry):
            for k in range(TOP_K):
                row_copy(sl, t, k).wait()
            return carry
        lax.fori_loop(0, tc, body, 0)

    @pl.when(tile == 0)
    def _():
        pos_copy(0, 0).start()
        pos_copy(0, 0).wait()
        start_rows(0)
        pos_copy(jnp.minimum(1, n_tiles - 1), 1).start()

    pos_copy(jnp.minimum(tile + 1, n_tiles - 1), other).wait()
    start_rows(other)
    parts = [_unpack_halves(ch) for ch in _load_row_tiled(h2_ref.at[0], tc)]
    h_lo = jnp.concatenate([p[0] for p in parts], axis=1).astype(BF16)
    h_hi = jnp.concatenate([p[1] for p in parts], axis=1).astype(BF16)
    half = h_lo.shape[1]
    nch = half // LANES
    gate = _mm(h_lo, wsg_ref[0:half, :]) + _mm(h_hi, wsg_ref[half:, :])
    up = _mm(h_lo, wsu_ref[0:half, :]) + _mm(h_hi, wsu_ref[half:, :])
    shared = _mm((_silu(gate) * up).astype(BF16), wsd_ref[...])
    wait_rows(slot)
    ew = ew_ref[0]
    acc = [shared[:, j * LANES:(j + 1) * LANES] for j in range(2 * nch)]
    for k in range(TOP_K):
        wk = ew[:, k:k + 1]
        for j, ch in enumerate(_load_row_tiled(buf_s.at[slot, k], tc)):
            y_lo, y_hi = _unpack_halves(ch)
            acc[j] = acc[j] + y_lo * wk
            acc[nch + j] = acc[nch + j] + y_hi * wk
    out_ref[0] = x1_ref[0] + g2_ref[0] * _rms(jnp.concatenate(acc, axis=1), nw3_ref[...])
    pos_copy(jnp.minimum(tile + 2, n_tiles - 1), slot).start()

    @pl.when(tile == n_tiles - 1)
    def _():
        wait_rows(other)
        pos_copy(n_tiles - 1, slot).wait()


def _combine(pos_tiles, ys, ew, x1, h2, g2, nw3, wsg, wsu, wsd, tc):
    b, s, d = x1.shape
    ds_ = wsg.shape[1]
    nt = s // tc
    c = ys.shape[1]
    kern = functools.partial(_combine_kernel, tc=tc, nt=nt, n_tiles=b * nt)
    tok = lambda w: pl.BlockSpec((1, tc, w), lambda bi, i: (bi, i, 0))
    return pl.pallas_call(
        kern,
        grid=(b, nt),
        in_specs=[pl.BlockSpec(memory_space=pl.ANY), pl.BlockSpec(memory_space=pl.ANY),
                  tok(TOP_K), tok(d), pl.BlockSpec((1, tc * c, LANES), lambda bi, i: (bi, i, 0)),
                  pl.BlockSpec((1, 1, d), lambda bi, i: (bi, 0, 0)),
                  pl.BlockSpec((1, d), lambda bi, i: (0, 0)),
                  pl.BlockSpec((d, ds_), lambda bi, i: (0, 0)),
                  pl.BlockSpec((d, ds_), lambda bi, i: (0, 0)),
                  pl.BlockSpec((ds_, d), lambda bi, i: (0, 0))],
        out_specs=tok(d),
        out_shape=jax.ShapeDtypeStruct((b, s, d), F32),
        scratch_shapes=[pltpu.VMEM((2, TOP_K, tc * c, LANES), U32), pltpu.SMEM((2, TOP_K, tc), I32),
                        pltpu.SemaphoreType.DMA((2,)), pltpu.SemaphoreType.DMA((2,))],
        compiler_params=_cparams(("arbitrary", "arbitrary")),
        name="combine",
    )(pos_tiles, ys, ew, x1, h2, g2, nw3, wsg, wsu, wsd)


def _tile(n, pref):
    t = min(n, pref)
    if n % t:
        raise NotImplementedError(f"size {n} is not a multiple of tile {t}")
    return t


def kernel(x, c, ctx, c_ctx, w_ada, b_ada, norms, w_in, ml_conv, ml_gate_b, ml_norm_w, attn_sink, w_out,
           w_router, b_router, w_exp_gate, w_exp_up, w_exp_down, w_sh_gate, w_sh_up, w_sh_down):
    if w_ada.shape[0] != 1:
        raise NotImplementedError("single-layer configuration only")
    b, s, d = x.shape
    lc = ctx.shape[1]
    t = b * s
    H = ML_HEADS
    dv = d // 2 // H
    dk = dv // 2
    dh = d // 2 // AT_HEADS
    qk_w, v_w = 2 * H * dk, H * dv
    ng = 4 * H
    aq_w, akv_w = AT_HEADS * dh, AT_KV_HEADS * dh
    e = w_router.shape[-1]
    nw = norms[0]

    rows = -(-(b + 1) // SUBLANES) * SUBLANES
    cond = jnp.zeros((rows, d), F32).at[:b].set(c).at[b].set(c_ctx)
    mods = _ada(cond, w_ada[0], b_ada[0][None, :])
    sh1, sc1, g1, sh2, sc2, g2 = [m[:b, None, :] for m in jnp.split(mods, 6, axis=-1)]
    csh1, csc1 = [jnp.broadcast_to(m[b:b + 1, None, :], (b, 1, d)) for m in jnp.split(mods, 6, axis=-1)[:2]]

    w0 = w_in[0]
    o_ml, o_g, o_q = 0, qk_w + 2 * v_w, qk_w + 2 * v_w + ng
    ml_w = qk_w + 2 * v_w
    wp = jnp.concatenate([w0[:, o_ml:o_g], w0[:, o_q:], w0[:, o_g:o_q], jnp.zeros((d, LANES - ng), F32)],
                         axis=1).astype(BF16)
    layout = (("ml", 0, ml_w), ("aq", ml_w, aq_w), ("ak", ml_w + aq_w, akv_w),
              ("av", ml_w + aq_w + akv_w, akv_w), ("g", ml_w + aq_w + 2 * akv_w, LANES))
    widths = dict(ml=ml_w, aq=aq_w, ak=akv_w, av=akv_w, g=LANES)
    tabs = _rope_tables(s, dh)
    proj = functools.partial(_inproj, layout=layout, widths=widths, dh=dh)
    ml, aq, ak, av, gates = proj(x, sh1, sc1, nw[0:1], wp, tabs, rope=True, tm=_tile(s, 512))
    tm_c = _tile(lc, 512)
    mlc, _, akc, avc, gates_c = proj(ctx, csh1, csc1, nw[0:1], wp, tuple(tb[:tm_c] for tb in tabs),
                                     rope=False, tm=tm_c)

    L = _tile(lc, 256)
    if s % L:
        raise NotImplementedError("sequence must be a multiple of the mLSTM chunk")
    gate_b = jnp.zeros((1, LANES), F32).at[0, :ng].set(ml_gate_b[0])
    gp = _gate_prep(gates, gate_b, L)
    gpc = _gate_prep(gates_c, gate_b, L)
    y_ml = _mlstm(ml, mlc, gp, gpc, ml_conv[0], ml_norm_w[0][None, :], L)

    y_at = _attn(aq, ak, av, akc, avc, attn_sink[0])

    x1, h2, logits = _outproj(y_ml, y_at, x, g1, sh2, sc2, nw[1:2], nw[2:3], w_out[0].astype(BF16),
                              w_router[0], _tile(s, 256))

    eidx, ew, rank, cnt = _route(logits.reshape(t, e), jnp.broadcast_to(b_router[0][:, None], (e, LANES)),
                                 _tile(t, 512))
    counts = cnt[:, 0].astype(I32)
    padded = (counts + MOE_BLOCK - 1) // MOE_BLOCK * MOE_BLOCK
    pad_end = jnp.cumsum(padded)
    pad_start = pad_end - padded
    n_blocks = -(-(t * TOP_K + e * (MOE_BLOCK - 1)) // MOE_BLOCK)
    n_rows = n_blocks * MOE_BLOCK
    blk_row = jnp.arange(n_blocks, dtype=I32) * MOE_BLOCK
    block_e = jnp.minimum(jnp.sum((pad_end[None, :] <= blk_row[:, None]).astype(I32), axis=1), e - 1)
    n_used = (pad_end[-1] // MOE_BLOCK).astype(I32)[None]
    used = counts > 0
    eid = jnp.arange(e, dtype=I32)
    next_used = jnp.concatenate([lax.cummin(jnp.where(used, eid, e), reverse=True)[1:], jnp.full((1,), e, I32)])
    next_used2 = jnp.concatenate([next_used, jnp.full((1,), e, I32)])[next_used]
    used_ord = jnp.cumsum(used.astype(I32)) - 1
    pos = _positions(eidx, rank, pad_start, _tile(t, 512))

    tt = _tile(t, 256)
    pos_d = pos.reshape(TOP_K, t // tt, tt).transpose(1, 0, 2)
    rc = d // 2 // LANES
    xs = _dispatch(h2.reshape(t, rc, LANES), pos_d, (pad_end - MOE_BLOCK).astype(I32), (padded > 0).astype(I32),
                   n_rows, tt)
    ys = _experts(xs.reshape(n_rows * rc, LANES), block_e, n_used, next_used, next_used2, used_ord,
                  w_exp_gate[0], w_exp_up[0], w_exp_down[0])
    tc = _tile(s, 128)
    pos_c = pos.reshape(TOP_K, t // tc, tc).transpose(1, 0, 2)
    ew_t = ew.T.reshape(b, s, TOP_K)
    return _combine(pos_c, ys.reshape(n_rows, rc, LANES), ew_t, x1, h2, g2, nw[3:4], w_sh_gate[0].astype(BF16), w_sh_up[0].astype(BF16),
                    w_sh_down[0].astype(BF16), tc)
```

```python
import functools

import jax
import jax.numpy as jnp
from jax import lax
from jax.experimental import pallas as pl
from jax.experimental.pallas import tpu as pltpu

F32 = jnp.float32
BF16 = jnp.bfloat16
I32 = jnp.int32

LANES = 128
SUBLANES = 8
VMEM_LIMIT_BYTES = 56 * 1024 * 1024

NORM_EPS = 1e-6
ML_HEADS = 4
CONV_W = 5
AT_HEADS = 8
AT_KV_HEADS = 2
GRID_W = 64
WINDOW = 128
BLOCK_Q = 128
ROPE_THETA = 10000.0
N_GROUPS = 8
TOPK_GROUPS = 4
TOP_K = 8
ROUTE_SCALE = 2.5
EXPERT_BLOCK = 256
NEG_BIG = -1e30


def _cparams(sem):
    return pltpu.CompilerParams(dimension_semantics=sem, vmem_limit_bytes=VMEM_LIMIT_BYTES)


def _rms(xf, w):
    return xf * lax.rsqrt(jnp.mean(xf * xf, axis=-1, keepdims=True) + NORM_EPS) * w


def _silu(x):
    return x * jax.nn.sigmoid(x)


def _mm(a, b):
    return jnp.dot(a, b, preferred_element_type=F32)


U32 = jnp.uint32


def _pack_halves(x):
    n = x.shape[1] // 2
    u = lax.bitcast_convert_type(x.astype(BF16).astype(F32), U32)
    return (u[:, :n] >> 16) | (u[:, n:] & jnp.uint32(0xFFFF0000))


def _unpack_halves(u):
    return (lax.bitcast_convert_type(u << 16, F32),
            lax.bitcast_convert_type(u & jnp.uint32(0xFFFF0000), F32))


def _store_row_tiled(ref2d, packed):
    r, n = packed.shape
    c = n // LANES
    for j in range(c):
        ref2d[pl.ds(j, r, stride=c), :] = packed[:, j * LANES:(j + 1) * LANES]


def _load_row_tiled(ref2d, r):
    c = ref2d.shape[0] // r
    return [ref2d[pl.ds(j, r, stride=c), :] for j in range(c)]


def _ada_kernel(c_ref, w_ref, b_ref, o_ref):
    s = _silu(c_ref[...]).astype(BF16)
    o_ref[...] = _mm(s, w_ref[...].astype(BF16)) + b_ref[...]


def _ada(cond, w, b):
    r, d = cond.shape
    n = w.shape[1]
    tn = min(n, 1024)
    return pl.pallas_call(
        _ada_kernel,
        grid=(n // tn,),
        in_specs=[pl.BlockSpec((r, d), lambda j: (0, 0)),
                  pl.BlockSpec((d, tn), lambda j: (0, j)),
                  pl.BlockSpec((1, tn), lambda j: (0, j))],
        out_specs=pl.BlockSpec((r, tn), lambda j: (0, j)),
        out_shape=jax.ShapeDtypeStruct((r, n), F32),
        compiler_params=_cparams(("arbitrary",)),
        name="ada",
    )(cond, w, b)


def _inproj_kernel(x_ref, sh_ref, sc_ref, nw_ref, w_ref, rc_ref, ra_ref, rb_ref,
                   ml_ref, aq_ref, ak_ref, av_ref, g_ref, hb_ref, *, layout, dh, rope, qscale, cw):
    h = _rms(x_ref[0], nw_ref[...]) * (1.0 + sc_ref[0]) + sh_ref[0]
    hb_ref[...] = h.astype(BF16)
    outs = dict(ml=ml_ref, aq=aq_ref, ak=ak_ref, av=av_ref, g=g_ref)
    for name, col0, width in layout:
        o_ref = outs[name]
        for c in range(0, width, cw):
            step = min(cw, width - c)
            acc = _mm(hb_ref[...], w_ref[:, col0 + c:col0 + c + step])
            if rope and name in ("aq", "ak"):
                for hh in range(step // dh):
                    a = acc[:, hh * dh:(hh + 1) * dh]
                    r = (a * rc_ref[...] + pltpu.roll(a, dh - dh // 4, 1) * ra_ref[...]
                         + pltpu.roll(a, dh // 4, 1) * rb_ref[...])
                    if name == "aq":
                        r = r * qscale
                    o_ref[0, :, c + hh * dh:c + (hh + 1) * dh] = r.astype(o_ref.dtype)
            else:
                if name == "aq":
                    acc = acc * qscale
                o_ref[0, :, c:c + step] = acc.astype(o_ref.dtype)


def _inproj(x, sh, sc, nw, wp, tabs, *, layout, widths, dh, rope, tm):
    b, s, d = x.shape
    npad = wp.shape[1]
    rc, ra, rb = tabs
    kern = functools.partial(_inproj_kernel, layout=layout, dh=dh, rope=rope, qscale=float(dh) ** -0.5, cw=512)
    vec = pl.BlockSpec((1, 1, d), lambda bi, i: (bi, 0, 0))
    tab = pl.BlockSpec((tm, dh), lambda bi, i: (i, 0))
    out_dt = dict(ml=BF16, aq=BF16, ak=BF16, av=BF16, g=F32)
    names = ("ml", "aq", "ak", "av", "g")
    return pl.pallas_call(
        kern,
        grid=(b, s // tm),
        in_specs=[pl.BlockSpec((1, tm, d), lambda bi, i: (bi, i, 0)), vec, vec,
                  pl.BlockSpec((1, d), lambda bi, i: (0, 0)),
                  pl.BlockSpec((d, npad), lambda bi, i: (0, 0), pipeline_mode=pl.Buffered(1)),
                  tab, tab, tab],
        out_specs=[pl.BlockSpec((1, tm, widths[n]), lambda bi, i: (bi, i, 0)) for n in names],
        out_shape=[jax.ShapeDtypeStruct((b, s, widths[n]), out_dt[n]) for n in names],
        scratch_shapes=[pltpu.VMEM((tm, d), BF16)],
        compiler_params=_cparams(("arbitrary", "arbitrary")),
        name="inproj_rope" if rope else "inproj_ctx",
    )(x, sh, sc, nw, wp, rc, ra, rb)


def _rope_tables(s, dh):
    rows = s // GRID_W
    row = jnp.repeat(jnp.arange(rows), GRID_W)
    col = jnp.tile(jnp.arange(GRID_W), rows)
    nf = dh // 4
    freqs = ROPE_THETA ** (-jnp.arange(nf, dtype=F32) / nf)
    pos = jnp.stack([row, col], axis=-1).astype(F32)
    ang = pos[:, :, None] * freqs
    cos, sin = jnp.cos(ang), jnp.sin(ang)
    z = jnp.zeros_like(sin[:, 0])
    rc = jnp.concatenate([cos[:, 0], cos[:, 0], cos[:, 1], cos[:, 1]], axis=-1)
    ra = jnp.concatenate([-sin[:, 0], z, -sin[:, 1], z], axis=-1)
    rb = jnp.concatenate([z, sin[:, 0], z, sin[:, 1]], axis=-1)
    return rc, ra, rb


def _gate_prep_kernel(g_ref, b_ref, gl_ref, pp_ref, glt_ref, ppt_ref, tot_ref, *, L, H, nc):
    for c in range(nc):
        z = g_ref[0, c * L:(c + 1) * L, :] + b_ref[...]
        lane = lax.broadcasted_iota(I32, z.shape, 1)
        row = lax.broadcasted_iota(I32, z.shape, 0)
        is_f = ((lane // H) % 2 == 1) & (lane < 4 * H)
        log_sig = jnp.minimum(z, 0.0) - jnp.log1p(jnp.exp(-jnp.abs(z)))
        gl = jnp.where(lane < 4 * H, jnp.where(is_f, log_sig, z), 0.0)
        cs = gl
        k = 1
        while k < L:
            cs = cs + jnp.where(row >= k, pltpu.roll(cs, k, 0), 0.0)
            k *= 2
        tot = cs[L - 1:L, :]
        suf = tot - cs + gl
        pp = jnp.where(lane >= 2 * H, suf, cs)
        gl_ref[0, c] = gl
        pp_ref[0, c] = pp
        glt_ref[0, c] = gl.T
        ppt_ref[0, c] = pp.T
        tot_ref[0, c] = jnp.broadcast_to(tot, (SUBLANES, LANES))


def _gate_prep(g, gate_b, L):
    b, s, _ = g.shape
    nc = s // L
    kern = functools.partial(_gate_prep_kernel, L=L, H=ML_HEADS, nc=nc)
    col = pl.BlockSpec((1, nc, L, LANES), lambda bi: (bi, 0, 0, 0))
    rowb = pl.BlockSpec((1, nc, LANES, L), lambda bi: (bi, 0, 0, 0))
    return pl.pallas_call(
        kern,
        grid=(b,),
        in_specs=[pl.BlockSpec((1, s, LANES), lambda bi: (bi, 0, 0)),
                  pl.BlockSpec((1, LANES), lambda bi: (0, 0))],
        out_specs=[col, col, rowb, rowb, pl.BlockSpec((1, nc, SUBLANES, LANES), lambda bi: (bi, 0, 0, 0))],
        out_shape=[jax.ShapeDtypeStruct((b, nc, L, LANES), F32), jax.ShapeDtypeStruct((b, nc, L, LANES), F32),
                   jax.ShapeDtypeStruct((b, nc, LANES, L), F32), jax.ShapeDtypeStruct((b, nc, LANES, L), F32),
                   jax.ShapeDtypeStruct((b, nc, SUBLANES, LANES), F32)],
        compiler_params=_cparams(("arbitrary",)),
        name="gate_prep",
    )(g, gate_b)


CONV_PAD = 8


def _lane_pick(tile, ch):
    lane = lax.broadcasted_iota(I32, tile.shape, 1)
    return jnp.sum(jnp.where(lane == ch, tile, 0.0), axis=1, keepdims=True)


def _sublane_pick(tile, ch):
    sub = lax.broadcasted_iota(I32, tile.shape, 0)
    return jnp.sum(jnp.where(sub == ch, tile, 0.0), axis=0, keepdims=True)


def _mlstm_chunk(d, ch_i, ch_f, q, k, v, gl, pp, glt, ppt, tot8, ct_ref, n_ref, m_ref):
    L = k.shape[0]
    p_col = _lane_pick(pp, ch_f)
    li_col = _lane_pick(gl, ch_i)
    btot = _lane_pick(tot8[0:1], ch_f)
    m_old = m_ref[d][0:1, 0:1]
    ct = ct_ref[d]
    n_row = n_ref[d]
    h = None
    if q is not None:
        p_row = _sublane_pick(ppt, ch_f)
        li_row = _sublane_pick(glt, ch_i)
        ti = lax.broadcasted_iota(I32, (L, L), 0)
        si = lax.broadcasted_iota(I32, (L, L), 1)
        valid = (si <= ti) if d == 0 else (si >= ti)
        logw = jnp.where(valid, p_col - p_row + li_row, NEG_BIG)
        log_inter = p_col + m_old
        m_q = jnp.maximum(log_inter, jnp.max(logw, axis=1, keepdims=True))
        sqk = lax.dot_general(q, k, (((1,), (1,)), ((), ())), preferred_element_type=F32) * jnp.exp(logw - m_q)
        w_inter = jnp.exp(log_inter - m_q)
        num = _mm(sqk.astype(BF16), v) + w_inter * _mm(q, ct.astype(BF16))
        den = (jnp.sum(sqk, axis=1, keepdims=True)
               + w_inter * jnp.sum(q.astype(F32) * n_row, axis=1, keepdims=True))
        h = num / jnp.maximum(jnp.abs(den), jnp.exp(-m_q))
    log_w_end = btot - p_col + li_col
    m_new = jnp.maximum(btot + m_old, jnp.max(log_w_end, axis=0, keepdims=True))
    w_end = jnp.exp(log_w_end - m_new)
    decay = jnp.exp(btot + m_old - m_new)
    kw = k.astype(F32) * w_end
    ct_ref[d] = decay * ct + lax.dot_general(kw.astype(BF16), v, (((0,), (0,)), ((), ())),
                                             preferred_element_type=F32)
    n_ref[d] = decay * n_row + jnp.sum(kw, axis=0, keepdims=True)
    m_ref[d] = jnp.broadcast_to(m_new, (SUBLANES, LANES))
    return h


def _mlstm_kernel(q_ref, k_ref, v_ref, o_ref, kc_ref, vc_ref,
                  gl_ref, pp_ref, glt_ref, ppt_ref, tot_ref, glc_ref, ppc_ref, totc_ref,
                  cwq_ref, cwk_ref, nw_ref, y_ref,
                  xq_s, xk_s, xkc_s, qs_s, ks_s, kcs_s, hf_s, hb_s, ct_s, n_s, m_s,
                  *, L, H, S, Lc, dk, dv):
    hh = pl.program_id(1)
    nc, ncc = S // L, Lc // L
    zpad = jnp.zeros((CONV_PAD, dk), F32)

    def conv_silu(x_s, cw_ref, n_rows, out_s, scale):
        for c in range(n_rows // L):
            acc = jnp.zeros((L, dk), F32)
            for j in range(CONV_W):
                acc = acc + cw_ref[j:j + 1, :] * x_s[pl.ds(CONV_PAD + c * L + j - CONV_W // 2, L), :]
            out_s[c] = (_silu(acc) * scale).astype(BF16)

    def stage(x_s, src, n_rows):
        x_s[0:CONV_PAD, :] = zpad
        x_s[CONV_PAD + n_rows:2 * CONV_PAD + n_rows, :] = zpad
        x_s[CONV_PAD:CONV_PAD + n_rows, :] = src.astype(F32)

    stage(xq_s, q_ref[0], S)
    stage(xk_s, k_ref[0], S)
    stage(xkc_s, kc_ref[0], Lc)
    conv_silu(xq_s, cwq_ref, S, qs_s, float(dk) ** -0.5)
    conv_silu(xk_s, cwk_ref, S, ks_s, 1.0)
    conv_silu(xkc_s, cwk_ref, Lc, kcs_s, 1.0)

    ct_s[...] = jnp.zeros(ct_s.shape, F32)
    n_s[...] = jnp.zeros(n_s.shape, F32)
    m_s[...] = jnp.zeros(m_s.shape, F32)

    def chans(d):
        return d * 2 * H + hh, d * 2 * H + H + hh

    for d in (0, 1):
        ch_i, ch_f = chans(d)
        for c in (range(ncc) if d == 0 else range(ncc - 1, -1, -1)):
            _mlstm_chunk(d, ch_i, ch_f, None, kcs_s[c], vc_ref[0, c * L:(c + 1) * L, :],
                         glc_ref[0, c], ppc_ref[0, c], None, None, totc_ref[0, c], ct_s, n_s, m_s)

    def body(i, carry):
        for d in (0, 1):
            ch_i, ch_f = chans(d)
            c = i if d == 0 else nc - 1 - i
            r0 = pl.multiple_of(c * L, L)
            h = _mlstm_chunk(d, ch_i, ch_f, qs_s[c], ks_s[c], v_ref[0, pl.ds(r0, L), :],
                             gl_ref[0, c], pp_ref[0, c], glt_ref[0, c], ppt_ref[0, c], tot_ref[0, c],
                             ct_s, n_s, m_s)
            if d == 0:
                hf_s[c] = h
            else:
                hb_s[c] = h
        return carry

    lax.fori_loop(0, nc, body, 0)

    for c in range(nc):
        hs = hf_s[c] + hb_s[c]
        hn = _rms(hs, nw_ref[...])
        y_ref[0, c * L:(c + 1) * L, :] = (hn * jax.nn.sigmoid(o_ref[0, c * L:(c + 1) * L, :].astype(F32))).astype(BF16)


def _mlstm(ml, mlc, gp, gpc, conv_w, norm_w, L):
    b, s, _ = ml.shape
    lc = mlc.shape[1]
    H = ML_HEADS
    dv = norm_w.shape[1] // H
    dk = dv // 2
    nc, ncc = s // L, lc // L
    gl, pp, glt, ppt, tot = gp
    glc, ppc, _, _, totc = gpc
    kern = functools.partial(_mlstm_kernel, L=L, H=H, S=s, Lc=lc, dk=dk, dv=dv)

    def colspec(rows, w, off):
        return pl.BlockSpec((1, rows, w), lambda bi, h: (bi, 0, off + h))

    def gspec(n, r, c):
        return pl.BlockSpec((1, n, r, c), lambda bi, h: (bi, 0, 0, 0))

    return pl.pallas_call(
        kern,
        grid=(b, H),
        in_specs=[colspec(s, dk, 0), colspec(s, dk, H), colspec(s, dv, H), colspec(s, dv, 2 * H),
                  colspec(lc, dk, H), colspec(lc, dv, H),
                  gspec(nc, L, LANES), gspec(nc, L, LANES), gspec(nc, LANES, L), gspec(nc, LANES, L),
                  gspec(nc, SUBLANES, LANES),
                  gspec(ncc, L, LANES), gspec(ncc, L, LANES), gspec(ncc, SUBLANES, LANES),
                  pl.BlockSpec((CONV_W, dk), lambda bi, h: (0, h)),
                  pl.BlockSpec((CONV_W, dk), lambda bi, h: (0, H + h)),
                  pl.BlockSpec((1, dv), lambda bi, h: (0, h))],
        out_specs=pl.BlockSpec((1, s, dv), lambda bi, h: (bi, 0, h)),
        out_shape=jax.ShapeDtypeStruct((b, s, H * dv), BF16),
        scratch_shapes=[pltpu.VMEM((s + 2 * CONV_PAD, dk), F32), pltpu.VMEM((s + 2 * CONV_PAD, dk), F32),
                        pltpu.VMEM((lc + 2 * CONV_PAD, dk), F32),
                        pltpu.VMEM((nc, L, dk), BF16), pltpu.VMEM((nc, L, dk), BF16), pltpu.VMEM((ncc, L, dk), BF16),
                        pltpu.VMEM((nc, L, dv), F32), pltpu.VMEM((nc, L, dv), F32),
                        pltpu.VMEM((2, dk, dv), F32), pltpu.VMEM((2, 1, dk), F32),
                        pltpu.VMEM((2, SUBLANES, LANES), F32)],
        compiler_params=_cparams(("arbitrary", "arbitrary")),
        name="mlstm",
    )(ml, ml, ml, ml, mlc, mlc, gl, pp, glt, ppt, tot, glc, ppc, totc, conv_w, conv_w, norm_w)


def _attn_kernel(sink_ref, q_ref, kp_ref, kc_ref, kn_ref, vp_ref, vc_ref, vn_ref, kx_ref, vx_ref, bias_ref,
                 o_ref, *, bq, dh, G):
    q = q_ref[0]
    for g in range(AT_KV_HEADS):
        sl = slice(g * dh, (g + 1) * dh)
        qg = jnp.concatenate([q[:, (g * G + j) * dh:(g * G + j + 1) * dh] for j in range(G)], axis=0)
        kcat = jnp.concatenate([kp_ref[0][:, sl], kc_ref[0][:, sl], kn_ref[0][:, sl], kx_ref[0][:, sl]], axis=0)
        vcat = jnp.concatenate([vp_ref[0][:, sl], vc_ref[0][:, sl], vn_ref[0][:, sl], vx_ref[0][:, sl]], axis=0)
        s = lax.dot_general(qg, kcat, (((1,), (1,)), ((), ())), preferred_element_type=F32) + bias_ref[0]
        ri = lax.broadcasted_iota(I32, (G * bq, 1), 0)
        sk = jnp.zeros((G * bq, 1), F32)
        for j in range(G):
            sk = jnp.where((ri >= j * bq) & (ri < (j + 1) * bq), sink_ref[g * G + j], sk)
        m = jnp.maximum(jnp.max(s, axis=1, keepdims=True), sk)
        p = jnp.exp(s - m)
        den = jnp.sum(p, axis=1, keepdims=True) + jnp.exp(sk - m)
        o = _mm(p.astype(BF16), vcat) / den
        for j in range(G):
            o_ref[0, :, (g * G + j) * dh:(g * G + j + 1) * dh] = o[j * bq:(j + 1) * bq, :].astype(BF16)


def _attn_bias(bq, lc, G):
    qoff = (jnp.arange(G * bq) % bq)[:, None]
    ci = jnp.arange(3 * bq + lc)[None, :]
    prev = (ci < bq) & (qoff <= ci)
    cur = (ci >= bq) & (ci < 2 * bq)
    nxt = (ci >= 2 * bq) & (ci < 3 * bq) & (ci - 2 * bq <= qoff)
    ctx = ci >= 3 * bq
    inner = prev | cur | nxt | ctx
    first = cur | nxt | ctx
    last = prev | cur | ctx
    if WINDOW != bq:
        raise NotImplementedError("window must equal the query block")
    return jnp.where(jnp.stack([inner, first, last]), 0.0, NEG_BIG).astype(F32)


def _attn(aq, ak, av, akc, avc, sink):
    b, s, hd = aq.shape
    dh = hd // AT_HEADS
    G = AT_HEADS // AT_KV_HEADS
    bq = BLOCK_Q
    nb = s // bq
    lc = akc.shape[1]
    kvw = AT_KV_HEADS * dh
    bias = _attn_bias(bq, lc, G)
    if nb < 2:
        raise NotImplementedError("needs at least two query blocks")
    kern = functools.partial(_attn_kernel, bq=bq, dh=dh, G=G)
    prev = pl.BlockSpec((1, bq, kvw), lambda bi, n: (bi, jnp.maximum(n - 1, 0), 0))
    cur = pl.BlockSpec((1, bq, kvw), lambda bi, n: (bi, n, 0))
    nxt = pl.BlockSpec((1, bq, kvw), lambda bi, n: (bi, jnp.minimum(n + 1, nb - 1), 0))
    cx = pl.BlockSpec((1, lc, kvw), lambda bi, n: (bi, 0, 0))
    return pl.pallas_call(
        kern,
        grid=(b, nb),
        in_specs=[pl.BlockSpec(memory_space=pltpu.SMEM),
                  pl.BlockSpec((1, bq, hd), lambda bi, n: (bi, n, 0)),
                  prev, cur, nxt, prev, cur, nxt, cx, cx,
                  pl.BlockSpec((1, G * bq, 3 * bq + lc),
                               lambda bi, n: (jnp.where(n == 0, 1, jnp.where(n == nb - 1, 2, 0)), 0, 0))],
        out_specs=pl.BlockSpec((1, bq, hd), lambda bi, n: (bi, n, 0)),
        out_shape=jax.ShapeDtypeStruct((b, s, hd), BF16),
        compiler_params=_cparams(("arbitrary", "arbitrary")),
        name="attn",
    )(sink, aq, ak, ak, ak, av, av, av, akc, avc, bias)


def _outproj_kernel(ym_ref, ya_ref, x_ref, g1_ref, sh2_ref, sc2_ref, nw1_ref, nw2_ref, wo_ref, wrh_ref, wrl_ref,
                    x1_ref, h2_ref, lg_ref, y_s, *, dm, cw):
    d = y_s.shape[1]
    for c in range(0, d, cw):
        y_s[:, c:c + cw] = _mm(ym_ref[0], wo_ref[0:dm, c:c + cw]) + _mm(ya_ref[0], wo_ref[dm:, c:c + cw])
    x1 = x_ref[0] + g1_ref[0] * _rms(y_s[...], nw1_ref[...])
    x1_ref[0] = x1
    h2 = _rms(x1, nw2_ref[...]) * (1.0 + sc2_ref[0]) + sh2_ref[0]
    hi = h2.astype(BF16)
    _store_row_tiled(h2_ref.at[0], _pack_halves(h2))
    lo = (h2 - hi.astype(F32)).astype(BF16)
    lg_ref[0] = _mm(hi, wrh_ref[...]) + (_mm(hi, wrl_ref[...]) + _mm(lo, wrh_ref[...]))


def _outproj(y_ml, y_at, x, g1, sh2, sc2, nw1, nw2, wo, wr, tm):
    b, s, d = x.shape
    dm = y_ml.shape[2]
    e = wr.shape[1]
    wr_hi = wr.astype(BF16)
    wr_lo = (wr - wr_hi.astype(F32)).astype(BF16)
    kern = functools.partial(_outproj_kernel, dm=dm, cw=min(d, 512))
    vec = pl.BlockSpec((1, 1, d), lambda bi, i: (bi, 0, 0))
    nspec = pl.BlockSpec((1, d), lambda bi, i: (0, 0))
    return pl.pallas_call(
        kern,
        grid=(b, s // tm),
        in_specs=[pl.BlockSpec((1, tm, dm), lambda bi, i: (bi, i, 0)),
                  pl.BlockSpec((1, tm, d - dm), lambda bi, i: (bi, i, 0)),
                  pl.BlockSpec((1, tm, d), lambda bi, i: (bi, i, 0)),
                  vec, vec, vec, nspec, nspec,
                  pl.BlockSpec((d, d), lambda bi, i: (0, 0), pipeline_mode=pl.Buffered(1)),
                  pl.BlockSpec((d, e), lambda bi, i: (0, 0), pipeline_mode=pl.Buffered(1)),
                  pl.BlockSpec((d, e), lambda bi, i: (0, 0), pipeline_mode=pl.Buffered(1))],
        out_specs=[pl.BlockSpec((1, tm, d), lambda bi, i: (bi, i, 0)),
                   pl.BlockSpec((1, tm * (d // 2 // LANES), LANES), lambda bi, i: (bi, i, 0)),
                   pl.BlockSpec((1, tm, e), lambda bi, i: (bi, i, 0))],
        out_shape=[jax.ShapeDtypeStruct((b, s, d), F32), jax.ShapeDtypeStruct((b, s * (d // 2 // LANES), LANES), U32),
                   jax.ShapeDtypeStruct((b, s, e), F32)],
        scratch_shapes=[pltpu.VMEM((tm, d), F32)],
        compiler_params=_cparams(("arbitrary", "arbitrary")),
        name="outproj",
    )(y_ml, y_at, x, g1, sh2, sc2, nw1, nw2, wo, wr_hi, wr_lo)


def _route_kernel(lg_ref, br_ref, eidx_ref, ew_ref, rank_ref, cnt_ref, carry_s, *, E, tr):
    i = pl.program_id(0)

    @pl.when(i == 0)
    def _():
        carry_s[...] = jnp.zeros(carry_s.shape, F32)

    scores = jax.nn.sigmoid(lg_ref[...].T)
    biased = scores + br_ref[:, 0:1]
    row = lax.broadcasted_iota(I32, (E, tr), 0).astype(F32)
    gs = E // N_GROUPS
    ninf = -jnp.inf
    grp = []
    for g in range(N_GROUPS):
        xg = biased[g * gs:(g + 1) * gs]
        rg = (lax.broadcasted_iota(I32, (gs, tr), 0) + g * gs).astype(F32)
        m1 = jnp.max(xg, axis=0, keepdims=True)
        i1 = jnp.min(jnp.where(xg == m1, rg, float(E)), axis=0, keepdims=True)
        m2 = jnp.max(jnp.where(rg == i1, ninf, xg), axis=0, keepdims=True)
        grp.append(m1 + m2)
    gsc = jnp.concatenate(grp, axis=0)
    gi = lax.broadcasted_iota(I32, (N_GROUPS, tr), 0)
    beaten = jnp.zeros((N_GROUPS, tr), I32)
    for g2 in range(N_GROUPS):
        sg = gsc[g2:g2 + 1]
        beaten = beaten + jnp.where((sg > gsc) | ((sg == gsc) & (gi > g2)), 1, 0)
    keep_g = jnp.where(beaten < TOPK_GROUPS, 1.0, 0.0)
    keep = jnp.concatenate([jnp.broadcast_to(keep_g[g:g + 1], (gs, tr)) for g in range(N_GROUPS)], axis=0)
    masked = jnp.where(keep > 0.5, biased, ninf)
    idxs, ws = [], []
    for _ in range(TOP_K):
        m = jnp.max(masked, axis=0, keepdims=True)
        ik = jnp.min(jnp.where(masked == m, row, float(E)), axis=0, keepdims=True)
        sel = row == ik
        ws.append(jnp.sum(jnp.where(sel, scores, 0.0), axis=0, keepdims=True))
        idxs.append(ik)
        masked = jnp.where(sel, ninf, masked)
    w = jnp.concatenate(ws, axis=0)
    ew_ref[...] = w / jnp.sum(w, axis=0, keepdims=True) * ROUTE_SCALE
    eidx_ref[...] = jnp.concatenate(idxs, axis=0).astype(I32)

    assign = jnp.zeros((E, tr), F32)
    for ik in idxs:
        assign = assign + jnp.where(row == ik, 1.0, 0.0)
    upper = jnp.where(lax.broadcasted_iota(I32, (tr, tr), 0) < lax.broadcasted_iota(I32, (tr, tr), 1), 1.0, 0.0)
    base = _mm(assign.astype(BF16), upper.astype(BF16)) + carry_s[:, 0:1]
    ranks = [jnp.sum(jnp.where(row == ik, base, 0.0), axis=0, keepdims=True) for ik in idxs]
    rank_ref[...] = jnp.concatenate(ranks, axis=0).astype(I32)
    carry_s[...] = carry_s[...] + jnp.sum(assign, axis=1, keepdims=True)
    cnt_ref[...] = carry_s[...]


def _route(logits, b_router, tr):
    t, e = logits.shape
    kern = functools.partial(_route_kernel, E=e, tr=tr)
    kt = pl.BlockSpec((TOP_K, tr), lambda i: (0, i))
    return pl.pallas_call(
        kern,
        grid=(t // tr,),
        in_specs=[pl.BlockSpec((tr, e), lambda i: (i, 0)), pl.BlockSpec((e, LANES), lambda i: (0, 0))],
        out_specs=[kt, kt, kt, pl.BlockSpec((e, LANES), lambda i: (0, 0))],
        out_shape=[jax.ShapeDtypeStruct((TOP_K, t), I32), jax.ShapeDtypeStruct((TOP_K, t), F32),
                   jax.ShapeDtypeStruct((TOP_K, t), I32), jax.ShapeDtypeStruct((e, LANES), F32)],
        scratch_shapes=[pltpu.VMEM((e, LANES), F32)],
        compiler_params=_cparams(("arbitrary",)),
        name="route",
    )(logits, b_router)


def _positions_kernel(eidx_ref, rank_ref, ps_ref, pos_ref, *, E, tr):
    row = lax.broadcasted_iota(I32, (E, tr), 0)
    start = ps_ref[:, 0:1]
    rows = [jnp.sum(jnp.where(row == eidx_ref[k:k + 1, :], start, 0.0), axis=0, keepdims=True)
            for k in range(TOP_K)]
    pos_ref[...] = jnp.concatenate(rows, axis=0).astype(I32) + rank_ref[...]


def _positions(eidx, rank, pad_start, tr):
    t = eidx.shape[1]
    e = pad_start.shape[0]
    kern = functools.partial(_positions_kernel, E=e, tr=tr)
    kt = pl.BlockSpec((TOP_K, tr), lambda i: (0, i))
    return pl.pallas_call(
        kern,
        grid=(t // tr,),
        in_specs=[kt, kt, pl.BlockSpec((e, LANES), lambda i: (0, 0))],
        out_specs=kt,
        out_shape=jax.ShapeDtypeStruct((TOP_K, t), I32),
        compiler_params=_cparams(("arbitrary",)),
        name="positions",
    )(eidx, rank, jnp.broadcast_to(pad_start.astype(F32)[:, None], (e, LANES)))


def _dispatch_kernel(zstart_ref, zvalid_ref, pos_hbm, h2_ref, xs_hbm, zero_s, pos_s, sem_p, sem_z, sem_r,
                     *, E, tt):
    i = pl.program_id(0)

    def zero_copy(e):
        start = pl.multiple_of(zstart_ref[e], EXPERT_BLOCK)
        return pltpu.make_async_copy(zero_s, xs_hbm.at[pl.ds(start, EXPERT_BLOCK)], sem_z)

    @pl.when(i == 0)
    def _():
        zero_s[...] = jnp.zeros(zero_s.shape, U32)

        def start(e, c):
            @pl.when(zvalid_ref[e] > 0)
            def _():
                zero_copy(e).start()
            return c

        def wait(e, c):
            @pl.when(zvalid_ref[e] > 0)
            def _():
                zero_copy(e).wait()
            return c

        lax.fori_loop(0, E, start, 0)
        lax.fori_loop(0, E, wait, 0)

    pos_copy = pltpu.make_async_copy(pos_hbm.at[i], pos_s, sem_p)
    pos_copy.start()
    pos_copy.wait()

    def row_copy(t, k):
        return pltpu.make_async_copy(h2_ref.at[t], xs_hbm.at[pos_s[k, t]], sem_r)

    def start_rows(t, c):
        for k in range(TOP_K):
            row_copy(t, k).start(priority=k % 2)
        return c

    def wait_rows(t, c):
        for k in range(TOP_K):
            row_copy(t, k).wait()
        return c

    lax.fori_loop(0, tt, start_rows, 0)
    lax.fori_loop(0, tt, wait_rows, 0)


def _dispatch(h2, pos_tiles, zstart, zvalid, n_rows, tt):
    t, c, _ = h2.shape
    e = zstart.shape[0]
    kern = functools.partial(_dispatch_kernel, E=e, tt=tt)
    return pl.pallas_call(
        kern,
        grid_spec=pltpu.PrefetchScalarGridSpec(
            num_scalar_prefetch=2,
            grid=(t // tt,),
            in_specs=[pl.BlockSpec(memory_space=pl.ANY),
                      pl.BlockSpec((tt, c, LANES), lambda i, zs, zv: (i, 0, 0))],
            out_specs=pl.BlockSpec(memory_space=pl.ANY),
            scratch_shapes=[pltpu.VMEM((EXPERT_BLOCK, c, LANES), U32), pltpu.SMEM((TOP_K, tt), I32),
                            pltpu.SemaphoreType.DMA, pltpu.SemaphoreType.DMA, pltpu.SemaphoreType.DMA]),
        out_shape=jax.ShapeDtypeStruct((n_rows, c, LANES), U32),
        compiler_params=_cparams(("arbitrary",)),
        name="dispatch",
    )(zstart, zvalid, pos_tiles, h2)


def _experts_kernel(be_ref, nu_ref, nxt_ref, ord_ref, xs_ref, wg_hbm, wu_hbm, wd_hbm, ys_ref,
                    sg_s, su_s, sd_s, wgb_s, wub_s, wdb_s, sem, *, E):
    b = pl.program_id(0)
    n_used = nu_ref[0]

    def copies(e, slot):
        out = []
        for m, (w_hbm, st) in enumerate(((wg_hbm, sg_s), (wu_hbm, su_s), (wd_hbm, sd_s))):
            rows = w_hbm.shape[1] // 2
            for p in range(2):
                out.append(pltpu.make_async_copy(w_hbm.at[e, pl.ds(p * rows, rows)],
                                                 st.at[slot, pl.ds(p * rows, rows)], sem.at[slot, 2 * m + p]))
        return out

    def start_all(cps):
        for n, cp in enumerate(cps):
            cp.start(priority=n % 2)

    @pl.when(b < n_used)
    def _():
        e = be_ref[b]
        prev = be_ref[jnp.maximum(b - 1, 0)]
        slot = ord_ref[e] & 1

        @pl.when(b == 0)
        def _():
            start_all(copies(e, slot))

        @pl.when((b == 0) | (e != prev))
        def _():
            for cp in copies(e, slot):
                cp.wait()
            ne = nxt_ref[e]

            @pl.when(ne < E)
            def _():
                start_all(copies(ne, 1 - slot))

            wgb_s[...] = sg_s[slot].astype(BF16)
            wub_s[...] = su_s[slot].astype(BF16)
            wdb_s[...] = sd_s[slot].astype(BF16)

        parts = [_unpack_halves(ch) for ch in _load_row_tiled(xs_ref, EXPERT_BLOCK)]
        x_lo = jnp.concatenate([p[0] for p in parts], axis=1).astype(BF16)
        x_hi = jnp.concatenate([p[1] for p in parts], axis=1).astype(BF16)
        half = x_lo.shape[1]
        gate = _mm(x_lo, wgb_s[0:half, :]) + _mm(x_hi, wgb_s[half:, :])
        up = _mm(x_lo, wub_s[0:half, :]) + _mm(x_hi, wub_s[half:, :])
        a = _silu(gate) * up
        _store_row_tiled(ys_ref, _pack_halves(_mm(a.astype(BF16), wdb_s[...])))


def _experts(xs, block_e, n_used, next_used, used_ord, wg, wu, wd):
    e, d, de = wg.shape
    c = d // 2 // LANES
    nblk = xs.shape[0] // (EXPERT_BLOCK * c)
    kern = functools.partial(_experts_kernel, E=e)

    def live(b, be, nu, nx, od):
        return (jnp.minimum(b, jnp.maximum(nu[0] - 1, 0)), 0)

    hbm = pl.BlockSpec(memory_space=pl.ANY)
    return pl.pallas_call(
        kern,
        grid_spec=pltpu.PrefetchScalarGridSpec(
            num_scalar_prefetch=4,
            grid=(nblk,),
            in_specs=[pl.BlockSpec((EXPERT_BLOCK * c, LANES), live), hbm, hbm, hbm],
            out_specs=pl.BlockSpec((EXPERT_BLOCK * c, LANES), live),
            scratch_shapes=[pltpu.VMEM((2, d, de), F32), pltpu.VMEM((2, d, de), F32), pltpu.VMEM((2, de, d), F32),
                            pltpu.VMEM((d, de), BF16), pltpu.VMEM((d, de), BF16), pltpu.VMEM((de, d), BF16),
                            pltpu.SemaphoreType.DMA((2, 6))]),
        out_shape=jax.ShapeDtypeStruct(xs.shape, U32),
        compiler_params=_cparams(("arbitrary",)),
        name="experts",
    )(block_e, n_used, next_used, used_ord, xs, wg, wu, wd)


def _combine_kernel(pos_hbm, ys_hbm, ew_ref, x1_ref, h2_ref, g2_ref, nw3_ref, wsg_ref, wsu_ref, wsd_ref,
                    out_ref, buf_s, pos_s, sem_p, sem_r, *, tc, nt, n_tiles):
    tile = pl.program_id(0) * nt + pl.program_id(1)
    slot = tile & 1
    other = 1 - slot
    c = ys_hbm.shape[1]

    def pos_copy(tl, sl):
        return pltpu.make_async_copy(pos_hbm.at[tl], pos_s.at[sl], sem_p.at[sl])

    def row_copy(sl, t, k):
        return pltpu.make_async_copy(ys_hbm.at[pos_s[sl, k, t]],
                                     buf_s.at[sl, k, pl.ds(pl.multiple_of(t * c, c), c)], sem_r.at[sl])

    def start_rows(sl):
        def body(t, carry):
            for k in range(TOP_K):
                row_copy(sl, t, k).start(priority=k % 2)
            return carry
        lax.fori_loop(0, tc, body, 0)

    def wait_rows(sl):
        def body(t, carry):
            for k in range(TOP_K):
                row_copy(sl, t, k).wait()
            return carry
        lax.fori_loop(0, tc, body, 0)

    @pl.when(tile == 0)
    def _():
        pos_copy(0, 0).start()
        pos_copy(0, 0).wait()
        start_rows(0)
        pos_copy(jnp.minimum(1, n_tiles - 1), 1).start()

    pos_copy(jnp.minimum(tile + 1, n_tiles - 1), other).wait()
    start_rows(other)
    parts = [_unpack_halves(ch) for ch in _load_row_tiled(h2_ref.at[0], tc)]
    h_lo = jnp.concatenate([p[0] for p in parts], axis=1).astype(BF16)
    h_hi = jnp.concatenate([p[1] for p in parts], axis=1).astype(BF16)
    half = h_lo.shape[1]
    nch = half // LANES
    gate = _mm(h_lo, wsg_ref[0:half, :]) + _mm(h_hi, wsg_ref[half:, :])
    up = _mm(h_lo, wsu_ref[0:half, :]) + _mm(h_hi, wsu_ref[half:, :])
    shared = _mm((_silu(gate) * up).astype(BF16), wsd_ref[...])
    wait_rows(slot)
    ew = ew_ref[0]
    acc = [shared[:, j * LANES:(j + 1) * LANES] for j in range(2 * nch)]
    for k in range(TOP_K):
        wk = ew[:, k:k + 1]
        for j, ch in enumerate(_load_row_tiled(buf_s.at[slot, k], tc)):
            y_lo, y_hi = _unpack_halves(ch)
            acc[j] = acc[j] + y_lo * wk
            acc[nch + j] = acc[nch + j] + y_hi * wk
    out_ref[0] = x1_ref[0] + g2_ref[0] * _rms(jnp.concatenate(acc, axis=1), nw3_ref[...])
    pos_copy(jnp.minimum(tile + 2, n_tiles - 1), slot).start()

    @pl.when(tile == n_tiles - 1)
    def _():
        wait_rows(other)
        pos_copy(n_tiles - 1, slot).wait()


def _combine(pos_tiles, ys, ew, x1, h2, g2, nw3, wsg, wsu, wsd, tc):
    b, s, d = x1.shape
    ds_ = wsg.shape[1]
    nt = s // tc
    c = ys.shape[1]
    kern = functools.partial(_combine_kernel, tc=tc, nt=nt, n_tiles=b * nt)
    tok = lambda w: pl.BlockSpec((1, tc, w), lambda bi, i: (bi, i, 0))
    return pl.pallas_call(
        kern,
        grid=(b, nt),
        in_specs=[pl.BlockSpec(memory_space=pl.ANY), pl.BlockSpec(memory_space=pl.ANY),
                  tok(TOP_K), tok(d), pl.BlockSpec((1, tc * c, LANES), lambda bi, i: (bi, i, 0)),
                  pl.BlockSpec((1, 1, d), lambda bi, i: (bi, 0, 0)),
                  pl.BlockSpec((1, d), lambda bi, i: (0, 0)),
                  pl.BlockSpec((d, ds_), lambda bi, i: (0, 0)),
                  pl.BlockSpec((d, ds_), lambda bi, i: (0, 0)),
                  pl.BlockSpec((ds_, d), lambda bi, i: (0, 0))],
        out_specs=tok(d),
        out_shape=jax.ShapeDtypeStruct((b, s, d), F32),
        scratch_shapes=[pltpu.VMEM((2, TOP_K, tc * c, LANES), U32), pltpu.SMEM((2, TOP_K, tc), I32),
                        pltpu.SemaphoreType.DMA((2,)), pltpu.SemaphoreType.DMA((2,))],
        compiler_params=_cparams(("arbitrary", "arbitrary")),
        name="combine",
    )(pos_tiles, ys, ew, x1, h2, g2, nw3, wsg, wsu, wsd)


def _tile(n, pref):
    t = min(n, pref)
    if n % t:
        raise NotImplementedError(f"size {n} is not a multiple of tile {t}")
    return t


def kernel(x, c, ctx, c_ctx, w_ada, b_ada, norms, w_in, ml_conv, ml_gate_b, ml_norm_w, attn_sink, w_out,
           w_router, b_router, w_exp_gate, w_exp_up, w_exp_down, w_sh_gate, w_sh_up, w_sh_down):
    if w_ada.shape[0] != 1:
        raise NotImplementedError("single-layer configuration only")
    b, s, d = x.shape
    lc = ctx.shape[1]
    t = b * s
    H = ML_HEADS
    dv = d // 2 // H
    dk = dv // 2
    dh = d // 2 // AT_HEADS
    qk_w, v_w = 2 * H * dk, H * dv
    ng = 4 * H
    aq_w, akv_w = AT_HEADS * dh, AT_KV_HEADS * dh
    e = w_router.shape[-1]
    nw = norms[0]

    rows = -(-(b + 1) // SUBLANES) * SUBLANES
    cond = jnp.zeros((rows, d), F32).at[:b].set(c).at[b].set(c_ctx)
    mods = _ada(cond, w_ada[0], b_ada[0][None, :])
    sh1, sc1, g1, sh2, sc2, g2 = [m[:b, None, :] for m in jnp.split(mods, 6, axis=-1)]
    csh1, csc1 = [jnp.broadcast_to(m[b:b + 1, None, :], (b, 1, d)) for m in jnp.split(mods, 6, axis=-1)[:2]]

    w0 = w_in[0]
    o_ml, o_g, o_q = 0, qk_w + 2 * v_w, qk_w + 2 * v_w + ng
    ml_w = qk_w + 2 * v_w
    wp = jnp.concatenate([w0[:, o_ml:o_g], w0[:, o_q:], w0[:, o_g:o_q], jnp.zeros((d, LANES - ng), F32)],
                         axis=1).astype(BF16)
    layout = (("ml", 0, ml_w), ("aq", ml_w, aq_w), ("ak", ml_w + aq_w, akv_w),
              ("av", ml_w + aq_w + akv_w, akv_w), ("g", ml_w + aq_w + 2 * akv_w, LANES))
    widths = dict(ml=ml_w, aq=aq_w, ak=akv_w, av=akv_w, g=LANES)
    tabs = _rope_tables(s, dh)
    proj = functools.partial(_inproj, layout=layout, widths=widths, dh=dh)
    ml, aq, ak, av, gates = proj(x, sh1, sc1, nw[0:1], wp, tabs, rope=True, tm=_tile(s, 512))
    tm_c = _tile(lc, 512)
    mlc, _, akc, avc, gates_c = proj(ctx, csh1, csc1, nw[0:1], wp, tuple(tb[:tm_c] for tb in tabs),
                                     rope=False, tm=tm_c)

    L = _tile(lc, 256)
    if s % L:
        raise NotImplementedError("sequence must be a multiple of the mLSTM chunk")
    gate_b = jnp.zeros((1, LANES), F32).at[0, :ng].set(ml_gate_b[0])
    gp = _gate_prep(gates, gate_b, L)
    gpc = _gate_prep(gates_c, gate_b, L)
    y_ml = _mlstm(ml, mlc, gp, gpc, ml_conv[0], ml_norm_w[0][None, :], L)

    y_at = _attn(aq, ak, av, akc, avc, attn_sink[0])

    x1, h2, logits = _outproj(y_ml, y_at, x, g1, sh2, sc2, nw[1:2], nw[2:3], w_out[0].astype(BF16),
                              w_router[0], _tile(s, 256))

    eidx, ew, rank, cnt = _route(logits.reshape(t, e), jnp.broadcast_to(b_router[0][:, None], (e, LANES)),
                                 _tile(t, 512))
    counts = cnt[:, 0].astype(I32)
    padded = (counts + EXPERT_BLOCK - 1) // EXPERT_BLOCK * EXPERT_BLOCK
    pad_end = jnp.cumsum(padded)
    pad_start = pad_end - padded
    n_blocks = -(-(t * TOP_K + e * (EXPERT_BLOCK - 1)) // EXPERT_BLOCK)
    n_rows = n_blocks * EXPERT_BLOCK
    blk_row = jnp.arange(n_blocks, dtype=I32) * EXPERT_BLOCK
    block_e = jnp.minimum(jnp.sum((pad_end[None, :] <= blk_row[:, None]).astype(I32), axis=1), e - 1)
    n_used = (pad_end[-1] // EXPERT_BLOCK).astype(I32)[None]
    used = counts > 0
    eid = jnp.arange(e, dtype=I32)
    next_used = jnp.concatenate([lax.cummin(jnp.where(used, eid, e), reverse=True)[1:], jnp.full((1,), e, I32)])
    used_ord = jnp.cumsum(used.astype(I32)) - 1
    pos = _positions(eidx, rank, pad_start, _tile(t, 512))

    tt = _tile(t, 256)
    pos_d = pos.reshape(TOP_K, t // tt, tt).transpose(1, 0, 2)
    rc = d // 2 // LANES
    xs = _dispatch(h2.reshape(t, rc, LANES), pos_d, (pad_end - EXPERT_BLOCK).astype(I32), (padded > 0).astype(I32),
                   n_rows, tt)
    ys = _experts(xs.reshape(n_rows * rc, LANES), block_e, n_used, next_used, used_ord,
                  w_exp_gate[0], w_exp_up[0], w_exp_down[0])
    tc = _tile(s, 128)
    pos_c = pos.reshape(TOP_K, t // tc, tc).transpose(1, 0, 2)
    ew_t = ew.T.reshape(b, s, TOP_K)
    return _combine(pos_c, ys.reshape(n_rows, rc, LANES), ew_t, x1, h2, g2, nw[3:4], w_sh_gate[0].astype(BF16), w_sh_up[0].astype(BF16),
                    w_sh_down[0].astype(BF16), tc)
```

```python
import functools

import jax
import jax.numpy as jnp
from jax import lax
from jax.experimental import pallas as pl
from jax.experimental.pallas import tpu as pltpu

F32 = jnp.float32
BF16 = jnp.bfloat16
I32 = jnp.int32

LANES = 128
SUBLANES = 8
VMEM_LIMIT_BYTES = 56 * 1024 * 1024

NORM_EPS = 1e-6
ML_HEADS = 4
CONV_W = 5
AT_HEADS = 8
AT_KV_HEADS = 2
GRID_W = 64
WINDOW = 128
BLOCK_Q = 128
ROPE_THETA = 10000.0
N_GROUPS = 8
TOPK_GROUPS = 4
TOP_K = 8
ROUTE_SCALE = 2.5
EXPERT_BLOCK = 256
ZERO_ROWS = 32
NEG_BIG = -1e30


def _cparams(sem):
    return pltpu.CompilerParams(dimension_semantics=sem, vmem_limit_bytes=VMEM_LIMIT_BYTES)


def _rms(xf, w):
    return xf * lax.rsqrt(jnp.mean(xf * xf, axis=-1, keepdims=True) + NORM_EPS) * w


def _silu(x):
    return x * jax.nn.sigmoid(x)


def _mm(a, b):
    return jnp.dot(a, b, preferred_element_type=F32)


U32 = jnp.uint32


def _pack_halves(x):
    n = x.shape[1] // 2
    u = lax.bitcast_convert_type(x.astype(BF16).astype(F32), U32)
    return (u[:, :n] >> 16) | (u[:, n:] & jnp.uint32(0xFFFF0000))


def _unpack_halves(u):
    return (lax.bitcast_convert_type(u << 16, F32),
            lax.bitcast_convert_type(u & jnp.uint32(0xFFFF0000), F32))


def _store_row_tiled(ref2d, packed):
    r, n = packed.shape
    c = n // LANES
    for j in range(c):
        ref2d[pl.ds(j, r, stride=c), :] = packed[:, j * LANES:(j + 1) * LANES]


def _load_row_tiled(ref2d, r):
    c = ref2d.shape[0] // r
    return [ref2d[pl.ds(j, r, stride=c), :] for j in range(c)]


def _ada_kernel(c_ref, w_ref, b_ref, o_ref):
    s = _silu(c_ref[...]).astype(BF16)
    o_ref[...] = _mm(s, w_ref[...].astype(BF16)) + b_ref[...]


def _ada(cond, w, b):
    r, d = cond.shape
    n = w.shape[1]
    tn = min(n, 1024)
    return pl.pallas_call(
        _ada_kernel,
        grid=(n // tn,),
        in_specs=[pl.BlockSpec((r, d), lambda j: (0, 0)),
                  pl.BlockSpec((d, tn), lambda j: (0, j)),
                  pl.BlockSpec((1, tn), lambda j: (0, j))],
        out_specs=pl.BlockSpec((r, tn), lambda j: (0, j)),
        out_shape=jax.ShapeDtypeStruct((r, n), F32),
        compiler_params=_cparams(("arbitrary",)),
        name="ada",
    )(cond, w, b)


def _inproj_kernel(x_ref, sh_ref, sc_ref, nw_ref, w_ref, rc_ref, ra_ref, rb_ref,
                   ml_ref, aq_ref, ak_ref, av_ref, g_ref, hb_ref, *, layout, dh, rope, qscale, cw):
    h = _rms(x_ref[0], nw_ref[...]) * (1.0 + sc_ref[0]) + sh_ref[0]
    hb_ref[...] = h.astype(BF16)
    outs = dict(ml=ml_ref, aq=aq_ref, ak=ak_ref, av=av_ref, g=g_ref)
    for name, col0, width in layout:
        o_ref = outs[name]
        for c in range(0, width, cw):
            step = min(cw, width - c)
            acc = _mm(hb_ref[...], w_ref[:, col0 + c:col0 + c + step])
            if rope and name in ("aq", "ak"):
                for hh in range(step // dh):
                    a = acc[:, hh * dh:(hh + 1) * dh]
                    r = (a * rc_ref[...] + pltpu.roll(a, dh - dh // 4, 1) * ra_ref[...]
                         + pltpu.roll(a, dh // 4, 1) * rb_ref[...])
                    if name == "aq":
                        r = r * qscale
                    o_ref[0, :, c + hh * dh:c + (hh + 1) * dh] = r.astype(o_ref.dtype)
            else:
                if name == "aq":
                    acc = acc * qscale
                o_ref[0, :, c:c + step] = acc.astype(o_ref.dtype)


def _inproj(x, sh, sc, nw, wp, tabs, *, layout, widths, dh, rope, tm):
    b, s, d = x.shape
    npad = wp.shape[1]
    rc, ra, rb = tabs
    kern = functools.partial(_inproj_kernel, layout=layout, dh=dh, rope=rope, qscale=float(dh) ** -0.5, cw=512)
    vec = pl.BlockSpec((1, 1, d), lambda bi, i: (bi, 0, 0))
    tab = pl.BlockSpec((tm, dh), lambda bi, i: (i, 0))
    out_dt = dict(ml=BF16, aq=BF16, ak=BF16, av=BF16, g=F32)
    names = ("ml", "aq", "ak", "av", "g")
    return pl.pallas_call(
        kern,
        grid=(b, s // tm),
        in_specs=[pl.BlockSpec((1, tm, d), lambda bi, i: (bi, i, 0)), vec, vec,
                  pl.BlockSpec((1, d), lambda bi, i: (0, 0)),
                  pl.BlockSpec((d, npad), lambda bi, i: (0, 0), pipeline_mode=pl.Buffered(1)),
                  tab, tab, tab],
        out_specs=[pl.BlockSpec((1, tm, widths[n]), lambda bi, i: (bi, i, 0)) for n in names],
        out_shape=[jax.ShapeDtypeStruct((b, s, widths[n]), out_dt[n]) for n in names],
        scratch_shapes=[pltpu.VMEM((tm, d), BF16)],
        compiler_params=_cparams(("arbitrary", "arbitrary")),
        name="inproj_rope" if rope else "inproj_ctx",
    )(x, sh, sc, nw, wp, rc, ra, rb)


def _rope_tables(s, dh):
    rows = s // GRID_W
    row = jnp.repeat(jnp.arange(rows), GRID_W)
    col = jnp.tile(jnp.arange(GRID_W), rows)
    nf = dh // 4
    freqs = ROPE_THETA ** (-jnp.arange(nf, dtype=F32) / nf)
    pos = jnp.stack([row, col], axis=-1).astype(F32)
    ang = pos[:, :, None] * freqs
    cos, sin = jnp.cos(ang), jnp.sin(ang)
    z = jnp.zeros_like(sin[:, 0])
    rc = jnp.concatenate([cos[:, 0], cos[:, 0], cos[:, 1], cos[:, 1]], axis=-1)
    ra = jnp.concatenate([-sin[:, 0], z, -sin[:, 1], z], axis=-1)
    rb = jnp.concatenate([z, sin[:, 0], z, sin[:, 1]], axis=-1)
    return rc, ra, rb


def _gate_prep_kernel(g_ref, b_ref, gl_ref, pp_ref, glt_ref, ppt_ref, tot_ref, *, L, H, nc):
    for c in range(nc):
        z = g_ref[0, c * L:(c + 1) * L, :] + b_ref[...]
        lane = lax.broadcasted_iota(I32, z.shape, 1)
        row = lax.broadcasted_iota(I32, z.shape, 0)
        is_f = ((lane // H) % 2 == 1) & (lane < 4 * H)
        log_sig = jnp.minimum(z, 0.0) - jnp.log1p(jnp.exp(-jnp.abs(z)))
        gl = jnp.where(lane < 4 * H, jnp.where(is_f, log_sig, z), 0.0)
        cs = gl
        k = 1
        while k < L:
            cs = cs + jnp.where(row >= k, pltpu.roll(cs, k, 0), 0.0)
            k *= 2
        tot = cs[L - 1:L, :]
        suf = tot - cs + gl
        pp = jnp.where(lane >= 2 * H, suf, cs)
        gl_ref[0, c] = gl
        pp_ref[0, c] = pp
        glt_ref[0, c] = gl.T
        ppt_ref[0, c] = pp.T
        tot_ref[0, c] = jnp.broadcast_to(tot, (SUBLANES, LANES))


def _gate_prep(g, gate_b, L):
    b, s, _ = g.shape
    nc = s // L
    kern = functools.partial(_gate_prep_kernel, L=L, H=ML_HEADS, nc=nc)
    col = pl.BlockSpec((1, nc, L, LANES), lambda bi: (bi, 0, 0, 0))
    rowb = pl.BlockSpec((1, nc, LANES, L), lambda bi: (bi, 0, 0, 0))
    return pl.pallas_call(
        kern,
        grid=(b,),
        in_specs=[pl.BlockSpec((1, s, LANES), lambda bi: (bi, 0, 0)),
                  pl.BlockSpec((1, LANES), lambda bi: (0, 0))],
        out_specs=[col, col, rowb, rowb, pl.BlockSpec((1, nc, SUBLANES, LANES), lambda bi: (bi, 0, 0, 0))],
        out_shape=[jax.ShapeDtypeStruct((b, nc, L, LANES), F32), jax.ShapeDtypeStruct((b, nc, L, LANES), F32),
                   jax.ShapeDtypeStruct((b, nc, LANES, L), F32), jax.ShapeDtypeStruct((b, nc, LANES, L), F32),
                   jax.ShapeDtypeStruct((b, nc, SUBLANES, LANES), F32)],
        compiler_params=_cparams(("arbitrary",)),
        name="gate_prep",
    )(g, gate_b)


CONV_PAD = 8


def _lane_pick(tile, ch):
    lane = lax.broadcasted_iota(I32, tile.shape, 1)
    return jnp.sum(jnp.where(lane == ch, tile, 0.0), axis=1, keepdims=True)


def _sublane_pick(tile, ch):
    sub = lax.broadcasted_iota(I32, tile.shape, 0)
    return jnp.sum(jnp.where(sub == ch, tile, 0.0), axis=0, keepdims=True)


def _mlstm_chunk(d, ch_i, ch_f, q, k, v, gl, pp, glt, ppt, tot8, ct_ref, n_ref, m_ref, mask_ref=None):
    L = k.shape[0]
    p_col = _lane_pick(pp, ch_f)
    li_col = _lane_pick(gl, ch_i)
    btot = _lane_pick(tot8[0:1], ch_f)
    m_old = m_ref[d][0:1, 0:1]
    ct = ct_ref[d]
    n_row = n_ref[d]
    h = None
    if q is not None:
        p_row = _sublane_pick(ppt, ch_f)
        li_row = _sublane_pick(glt, ch_i)
        logw = (p_col - p_row + li_row) + mask_ref[d]
        log_inter = p_col + m_old
        m_q = jnp.maximum(log_inter, jnp.max(logw, axis=1, keepdims=True))
        sqk = lax.dot_general(q, k, (((1,), (1,)), ((), ())), preferred_element_type=F32) * jnp.exp(logw - m_q)
        w_inter = jnp.exp(log_inter - m_q)
        num = _mm(sqk.astype(BF16), v) + w_inter * _mm(q, ct.astype(BF16))
        den = (jnp.sum(sqk, axis=1, keepdims=True)
               + w_inter * jnp.sum(q.astype(F32) * n_row, axis=1, keepdims=True))
        h = num / jnp.maximum(jnp.abs(den), jnp.exp(-m_q))
    log_w_end = btot - p_col + li_col
    m_new = jnp.maximum(btot + m_old, jnp.max(log_w_end, axis=0, keepdims=True))
    w_end = jnp.exp(log_w_end - m_new)
    decay = jnp.exp(btot + m_old - m_new)
    kw = k.astype(F32) * w_end
    ct_ref[d] = decay * ct + lax.dot_general(kw.astype(BF16), v, (((0,), (0,)), ((), ())),
                                             preferred_element_type=F32)
    n_ref[d] = decay * n_row + jnp.sum(kw, axis=0, keepdims=True)
    m_ref[d] = jnp.broadcast_to(m_new, (SUBLANES, LANES))
    return h


def _mlstm_kernel(q_ref, k_ref, v_ref, o_ref, kc_ref, vc_ref,
                  gl_ref, pp_ref, glt_ref, ppt_ref, tot_ref, glc_ref, ppc_ref, totc_ref,
                  cwq_ref, cwk_ref, nw_ref, mask_ref, y_ref,
                  xq_s, xk_s, xkc_s, qs_s, ks_s, kcs_s, hf_s, hb_s, ct_s, n_s, m_s,
                  *, L, H, S, Lc, dk, dv):
    hh = pl.program_id(1)
    nc, ncc = S // L, Lc // L
    zpad = jnp.zeros((CONV_PAD, dk), F32)

    def conv_silu(x_s, cw_ref, n_rows, out_s, scale):
        for c in range(n_rows // L):
            acc = jnp.zeros((L, dk), F32)
            for j in range(CONV_W):
                acc = acc + cw_ref[j:j + 1, :] * x_s[pl.ds(CONV_PAD + c * L + j - CONV_W // 2, L), :]
            out_s[c] = (_silu(acc) * scale).astype(BF16)

    def stage(x_s, src, n_rows):
        x_s[0:CONV_PAD, :] = zpad
        x_s[CONV_PAD + n_rows:2 * CONV_PAD + n_rows, :] = zpad
        x_s[CONV_PAD:CONV_PAD + n_rows, :] = src.astype(F32)

    stage(xq_s, q_ref[0], S)
    stage(xk_s, k_ref[0], S)
    stage(xkc_s, kc_ref[0], Lc)
    conv_silu(xq_s, cwq_ref, S, qs_s, float(dk) ** -0.5)
    conv_silu(xk_s, cwk_ref, S, ks_s, 1.0)
    conv_silu(xkc_s, cwk_ref, Lc, kcs_s, 1.0)

    ct_s[...] = jnp.zeros(ct_s.shape, F32)
    n_s[...] = jnp.zeros(n_s.shape, F32)
    m_s[...] = jnp.zeros(m_s.shape, F32)

    def chans(d):
        return d * 2 * H + hh, d * 2 * H + H + hh

    for d in (0, 1):
        ch_i, ch_f = chans(d)
        for c in (range(ncc) if d == 0 else range(ncc - 1, -1, -1)):
            _mlstm_chunk(d, ch_i, ch_f, None, kcs_s[c], vc_ref[0, c * L:(c + 1) * L, :],
                         glc_ref[0, c], ppc_ref[0, c], None, None, totc_ref[0, c], ct_s, n_s, m_s)

    def body(i, carry):
        for d in (0, 1):
            ch_i, ch_f = chans(d)
            c = i if d == 0 else nc - 1 - i
            r0 = pl.multiple_of(c * L, L)
            h = _mlstm_chunk(d, ch_i, ch_f, qs_s[c], ks_s[c], v_ref[0, pl.ds(r0, L), :],
                             gl_ref[0, c], pp_ref[0, c], glt_ref[0, c], ppt_ref[0, c], tot_ref[0, c],
                             ct_s, n_s, m_s, mask_ref)
            if d == 0:
                hf_s[c] = h
            else:
                hb_s[c] = h
        return carry

    lax.fori_loop(0, nc, body, 0)

    for c in range(nc):
        hs = hf_s[c] + hb_s[c]
        hn = _rms(hs, nw_ref[...])
        y_ref[0, c * L:(c + 1) * L, :] = (hn * jax.nn.sigmoid(o_ref[0, c * L:(c + 1) * L, :].astype(F32))).astype(BF16)


def _mlstm(ml, mlc, gp, gpc, conv_w, norm_w, L):
    b, s, _ = ml.shape
    lc = mlc.shape[1]
    H = ML_HEADS
    dv = norm_w.shape[1] // H
    dk = dv // 2
    nc, ncc = s // L, lc // L
    gl, pp, glt, ppt, tot = gp
    glc, ppc, _, _, totc = gpc
    kern = functools.partial(_mlstm_kernel, L=L, H=H, S=s, Lc=lc, dk=dk, dv=dv)
    tri = jnp.arange(L)[None, :] <= jnp.arange(L)[:, None]
    masks = jnp.where(jnp.stack([tri, tri.T]), 0.0, NEG_BIG).astype(F32)

    def colspec(rows, w, off):
        return pl.BlockSpec((1, rows, w), lambda bi, h: (bi, 0, off + h))

    def gspec(n, r, c):
        return pl.BlockSpec((1, n, r, c), lambda bi, h: (bi, 0, 0, 0))

    return pl.pallas_call(
        kern,
        grid=(b, H),
        in_specs=[colspec(s, dk, 0), colspec(s, dk, H), colspec(s, dv, H), colspec(s, dv, 2 * H),
                  colspec(lc, dk, H), colspec(lc, dv, H),
                  gspec(nc, L, LANES), gspec(nc, L, LANES), gspec(nc, LANES, L), gspec(nc, LANES, L),
                  gspec(nc, SUBLANES, LANES),
                  gspec(ncc, L, LANES), gspec(ncc, L, LANES), gspec(ncc, SUBLANES, LANES),
                  pl.BlockSpec((CONV_W, dk), lambda bi, h: (0, h)),
                  pl.BlockSpec((CONV_W, dk), lambda bi, h: (0, H + h)),
                  pl.BlockSpec((1, dv), lambda bi, h: (0, h)),
                  pl.BlockSpec((2, L, L), lambda bi, h: (0, 0, 0))],
        out_specs=pl.BlockSpec((1, s, dv), lambda bi, h: (bi, 0, h)),
        out_shape=jax.ShapeDtypeStruct((b, s, H * dv), BF16),
        scratch_shapes=[pltpu.VMEM((s + 2 * CONV_PAD, dk), F32), pltpu.VMEM((s + 2 * CONV_PAD, dk), F32),
                        pltpu.VMEM((lc + 2 * CONV_PAD, dk), F32),
                        pltpu.VMEM((nc, L, dk), BF16), pltpu.VMEM((nc, L, dk), BF16), pltpu.VMEM((ncc, L, dk), BF16),
                        pltpu.VMEM((nc, L, dv), F32), pltpu.VMEM((nc, L, dv), F32),
                        pltpu.VMEM((2, dk, dv), F32), pltpu.VMEM((2, 1, dk), F32),
                        pltpu.VMEM((2, SUBLANES, LANES), F32)],
        compiler_params=_cparams(("arbitrary", "arbitrary")),
        name="mlstm",
    )(ml, ml, ml, ml, mlc, mlc, gl, pp, glt, ppt, tot, glc, ppc, totc, conv_w, conv_w, norm_w, masks)


def _attn_kernel(sink_ref, q_ref, kp_ref, kc_ref, kn_ref, vp_ref, vc_ref, vn_ref, kx_ref, vx_ref, bias_ref,
                 o_ref, *, bq, dh, G):
    q = q_ref[0]
    for g in range(AT_KV_HEADS):
        sl = slice(g * dh, (g + 1) * dh)
        qg = jnp.concatenate([q[:, (g * G + j) * dh:(g * G + j + 1) * dh] for j in range(G)], axis=0)
        kcat = jnp.concatenate([kp_ref[0][:, sl], kc_ref[0][:, sl], kn_ref[0][:, sl], kx_ref[0][:, sl]], axis=0)
        vcat = jnp.concatenate([vp_ref[0][:, sl], vc_ref[0][:, sl], vn_ref[0][:, sl], vx_ref[0][:, sl]], axis=0)
        s = lax.dot_general(qg, kcat, (((1,), (1,)), ((), ())), preferred_element_type=F32) + bias_ref[0]
        ri = lax.broadcasted_iota(I32, (G * bq, 1), 0)
        sk = jnp.zeros((G * bq, 1), F32)
        for j in range(G):
            sk = jnp.where((ri >= j * bq) & (ri < (j + 1) * bq), sink_ref[g * G + j], sk)
        m = jnp.maximum(jnp.max(s, axis=1, keepdims=True), sk)
        p = jnp.exp(s - m)
        den = jnp.sum(p, axis=1, keepdims=True) + jnp.exp(sk - m)
        o = _mm(p.astype(BF16), vcat) / den
        for j in range(G):
            o_ref[0, :, (g * G + j) * dh:(g * G + j + 1) * dh] = o[j * bq:(j + 1) * bq, :].astype(BF16)


def _attn_bias(bq, lc, G):
    qoff = (jnp.arange(G * bq) % bq)[:, None]
    ci = jnp.arange(3 * bq + lc)[None, :]
    prev = (ci < bq) & (qoff <= ci)
    cur = (ci >= bq) & (ci < 2 * bq)
    nxt = (ci >= 2 * bq) & (ci < 3 * bq) & (ci - 2 * bq <= qoff)
    ctx = ci >= 3 * bq
    inner = prev | cur | nxt | ctx
    first = cur | nxt | ctx
    last = prev | cur | ctx
    if WINDOW != bq:
        raise NotImplementedError("window must equal the query block")
    return jnp.where(jnp.stack([inner, first, last]), 0.0, NEG_BIG).astype(F32)


def _attn(aq, ak, av, akc, avc, sink):
    b, s, hd = aq.shape
    dh = hd // AT_HEADS
    G = AT_HEADS // AT_KV_HEADS
    bq = BLOCK_Q
    nb = s // bq
    lc = akc.shape[1]
    kvw = AT_KV_HEADS * dh
    bias = _attn_bias(bq, lc, G)
    if nb < 2:
        raise NotImplementedError("needs at least two query blocks")
    kern = functools.partial(_attn_kernel, bq=bq, dh=dh, G=G)
    prev = pl.BlockSpec((1, bq, kvw), lambda bi, n: (bi, jnp.maximum(n - 1, 0), 0))
    cur = pl.BlockSpec((1, bq, kvw), lambda bi, n: (bi, n, 0))
    nxt = pl.BlockSpec((1, bq, kvw), lambda bi, n: (bi, jnp.minimum(n + 1, nb - 1), 0))
    cx = pl.BlockSpec((1, lc, kvw), lambda bi, n: (bi, 0, 0))
    return pl.pallas_call(
        kern,
        grid=(b, nb),
        in_specs=[pl.BlockSpec(memory_space=pltpu.SMEM),
                  pl.BlockSpec((1, bq, hd), lambda bi, n: (bi, n, 0)),
                  prev, cur, nxt, prev, cur, nxt, cx, cx,
                  pl.BlockSpec((1, G * bq, 3 * bq + lc),
                               lambda bi, n: (jnp.where(n == 0, 1, jnp.where(n == nb - 1, 2, 0)), 0, 0))],
        out_specs=pl.BlockSpec((1, bq, hd), lambda bi, n: (bi, n, 0)),
        out_shape=jax.ShapeDtypeStruct((b, s, hd), BF16),
        compiler_params=_cparams(("arbitrary", "arbitrary")),
        name="attn",
    )(sink, aq, ak, ak, ak, av, av, av, akc, avc, bias)


def _outproj_kernel(ym_ref, ya_ref, x_ref, g1_ref, sh2_ref, sc2_ref, nw1_ref, nw2_ref, wo_ref, wrh_ref, wrl_ref,
                    x1_ref, h2_ref, lg_ref, y_s, *, dm, cw):
    i = pl.program_id(0)
    d = y_s.shape[2]

    @pl.when(i == 0)
    def _():
        y_s[...] = jnp.zeros(y_s.shape, F32)

    cur = i & 1
    x1 = x_ref[0] + g1_ref[0] * _rms(y_s[1 - cur], nw1_ref[...])
    x1_ref[0] = x1
    h2 = _rms(x1, nw2_ref[...]) * (1.0 + sc2_ref[0]) + sh2_ref[0]
    hi = h2.astype(BF16)
    _store_row_tiled(h2_ref.at[0], _pack_halves(h2))
    lo = (h2 - hi.astype(F32)).astype(BF16)
    lg_ref[0] = _mm(hi, wrh_ref[...]) + (_mm(hi, wrl_ref[...]) + _mm(lo, wrh_ref[...]))
    for c in range(0, d, cw):
        y_s[cur, :, c:c + cw] = _mm(ym_ref[0], wo_ref[0:dm, c:c + cw]) + _mm(ya_ref[0], wo_ref[dm:, c:c + cw])


def _outproj(y_ml, y_at, x, g1, sh2, sc2, nw1, nw2, wo, wr, tm):
    b, s, d = x.shape
    dm = y_ml.shape[2]
    e = wr.shape[1]
    wr_hi = wr.astype(BF16)
    wr_lo = (wr - wr_hi.astype(F32)).astype(BF16)
    kern = functools.partial(_outproj_kernel, dm=dm, cw=min(d, 512))
    nt = s // tm
    n = b * nt

    def ahead(i):
        a = jnp.minimum(i, n - 1)
        return a // nt, a % nt

    def behind(i):
        a = jnp.maximum(i - 1, 0)
        return a // nt, a % nt

    def tok(w, which):
        return pl.BlockSpec((1, tm, w), lambda i: (*which(i), 0))

    vec = pl.BlockSpec((1, 1, d), lambda i: (behind(i)[0], 0, 0))
    nspec = pl.BlockSpec((1, d), lambda i: (0, 0))
    const = lambda r, c: pl.BlockSpec((r, c), lambda i: (0, 0), pipeline_mode=pl.Buffered(1))
    rc = d // 2 // LANES
    return pl.pallas_call(
        kern,
        grid=(n + 1,),
        in_specs=[tok(dm, ahead), tok(d - dm, ahead), tok(d, behind),
                  vec, vec, vec, nspec, nspec, const(d, d), const(d, e), const(d, e)],
        out_specs=[tok(d, behind), pl.BlockSpec((1, tm * rc, LANES), lambda i: (*behind(i), 0)), tok(e, behind)],
        out_shape=[jax.ShapeDtypeStruct((b, s, d), F32), jax.ShapeDtypeStruct((b, s * rc, LANES), U32),
                   jax.ShapeDtypeStruct((b, s, e), F32)],
        scratch_shapes=[pltpu.VMEM((2, tm, d), F32)],
        compiler_params=_cparams(("arbitrary",)),
        name="outproj",
    )(y_ml, y_at, x, g1, sh2, sc2, nw1, nw2, wo, wr_hi, wr_lo)


def _route_kernel(lg_ref, br_ref, eidx_ref, ew_ref, rank_ref, cnt_ref, carry_s, *, E, tr):
    i = pl.program_id(0)

    @pl.when(i == 0)
    def _():
        carry_s[...] = jnp.zeros(carry_s.shape, F32)

    scores = jax.nn.sigmoid(lg_ref[...].T)
    biased = scores + br_ref[:, 0:1]
    row = lax.broadcasted_iota(I32, (E, tr), 0).astype(F32)
    gs = E // N_GROUPS
    ninf = -jnp.inf
    grp = []
    for g in range(N_GROUPS):
        xg = biased[g * gs:(g + 1) * gs]
        rg = (lax.broadcasted_iota(I32, (gs, tr), 0) + g * gs).astype(F32)
        m1 = jnp.max(xg, axis=0, keepdims=True)
        i1 = jnp.min(jnp.where(xg == m1, rg, float(E)), axis=0, keepdims=True)
        m2 = jnp.max(jnp.where(rg == i1, ninf, xg), axis=0, keepdims=True)
        grp.append(m1 + m2)
    gsc = jnp.concatenate(grp, axis=0)
    gi = lax.broadcasted_iota(I32, (N_GROUPS, tr), 0)
    beaten = jnp.zeros((N_GROUPS, tr), I32)
    for g2 in range(N_GROUPS):
        sg = gsc[g2:g2 + 1]
        beaten = beaten + jnp.where((sg > gsc) | ((sg == gsc) & (gi > g2)), 1, 0)
    keep_g = jnp.where(beaten < TOPK_GROUPS, 1.0, 0.0)
    keep = jnp.concatenate([jnp.broadcast_to(keep_g[g:g + 1], (gs, tr)) for g in range(N_GROUPS)], axis=0)
    masked = jnp.where(keep > 0.5, biased, ninf)
    idxs, ws = [], []
    for _ in range(TOP_K):
        m = jnp.max(masked, axis=0, keepdims=True)
        ik = jnp.min(jnp.where(masked == m, row, float(E)), axis=0, keepdims=True)
        sel = row == ik
        ws.append(jnp.sum(jnp.where(sel, scores, 0.0), axis=0, keepdims=True))
        idxs.append(ik)
        masked = jnp.where(sel, ninf, masked)
    w = jnp.concatenate(ws, axis=0)
    ew_ref[...] = w / jnp.sum(w, axis=0, keepdims=True) * ROUTE_SCALE
    eidx_ref[...] = jnp.concatenate(idxs, axis=0).astype(I32)

    assign = jnp.zeros((E, tr), F32)
    for ik in idxs:
        assign = assign + jnp.where(row == ik, 1.0, 0.0)
    upper = jnp.where(lax.broadcasted_iota(I32, (tr, tr), 0) < lax.broadcasted_iota(I32, (tr, tr), 1), 1.0, 0.0)
    base = _mm(assign.astype(BF16), upper.astype(BF16)) + carry_s[:, 0:1]
    ranks = [jnp.sum(jnp.where(row == ik, base, 0.0), axis=0, keepdims=True) for ik in idxs]
    rank_ref[...] = jnp.concatenate(ranks, axis=0).astype(I32)
    carry_s[...] = carry_s[...] + jnp.sum(assign, axis=1, keepdims=True)
    cnt_ref[...] = carry_s[...]


def _route(logits, b_router, tr):
    t, e = logits.shape
    kern = functools.partial(_route_kernel, E=e, tr=tr)
    kt = pl.BlockSpec((TOP_K, tr), lambda i: (0, i))
    return pl.pallas_call(
        kern,
        grid=(t // tr,),
        in_specs=[pl.BlockSpec((tr, e), lambda i: (i, 0)), pl.BlockSpec((e, LANES), lambda i: (0, 0))],
        out_specs=[kt, kt, kt, pl.BlockSpec((e, LANES), lambda i: (0, 0))],
        out_shape=[jax.ShapeDtypeStruct((TOP_K, t), I32), jax.ShapeDtypeStruct((TOP_K, t), F32),
                   jax.ShapeDtypeStruct((TOP_K, t), I32), jax.ShapeDtypeStruct((e, LANES), F32)],
        scratch_shapes=[pltpu.VMEM((e, LANES), F32)],
        compiler_params=_cparams(("arbitrary",)),
        name="route",
    )(logits, b_router)


def _positions_kernel(eidx_ref, rank_ref, ps_ref, pos_ref, *, E, tr):
    row = lax.broadcasted_iota(I32, (E, tr), 0)
    start = ps_ref[:, 0:1]
    rows = [jnp.sum(jnp.where(row == eidx_ref[k:k + 1, :], start, 0.0), axis=0, keepdims=True)
            for k in range(TOP_K)]
    pos_ref[...] = jnp.concatenate(rows, axis=0).astype(I32) + rank_ref[...]


def _positions(eidx, rank, pad_start, tr):
    t = eidx.shape[1]
    e = pad_start.shape[0]
    kern = functools.partial(_positions_kernel, E=e, tr=tr)
    kt = pl.BlockSpec((TOP_K, tr), lambda i: (0, i))
    return pl.pallas_call(
        kern,
        grid=(t // tr,),
        in_specs=[kt, kt, pl.BlockSpec((e, LANES), lambda i: (0, 0))],
        out_specs=kt,
        out_shape=jax.ShapeDtypeStruct((TOP_K, t), I32),
        compiler_params=_cparams(("arbitrary",)),
        name="positions",
    )(eidx, rank, jnp.broadcast_to(pad_start.astype(F32)[:, None], (e, LANES)))


def _dispatch_kernel(zstart_ref, zcount_ref, pos_hbm, h2_ref, xs_hbm, zero_s, pos_s, sem_p, sem_z, sem_r,
                     *, E, tt):
    i = pl.program_id(0)

    def zero_copy(e, j):
        return pltpu.make_async_copy(zero_s, xs_hbm.at[pl.ds(zstart_ref[e] + j * ZERO_ROWS, ZERO_ROWS)], sem_z)

    @pl.when(i == 0)
    def _():
        zero_s[...] = jnp.zeros(zero_s.shape, U32)

        def each(fn):
            def per_expert(e, c):
                def per_piece(j, c2):
                    fn(zero_copy(e, j))
                    return c2
                return lax.fori_loop(0, zcount_ref[e], per_piece, c)
            lax.fori_loop(0, E, per_expert, 0)

        each(lambda cp: cp.start())
        each(lambda cp: cp.wait())

    pos_copy = pltpu.make_async_copy(pos_hbm.at[i], pos_s, sem_p)
    pos_copy.start()
    pos_copy.wait()

    def row_copy(t, k):
        return pltpu.make_async_copy(h2_ref.at[t], xs_hbm.at[pos_s[k, t]], sem_r)

    def start_rows(t, c):
        for k in range(TOP_K):
            row_copy(t, k).start(priority=k % 2)
        return c

    def wait_rows(t, c):
        for k in range(TOP_K):
            row_copy(t, k).wait()
        return c

    lax.fori_loop(0, tt, start_rows, 0)
    lax.fori_loop(0, tt, wait_rows, 0)


def _dispatch(h2, pos_tiles, zstart, zcount, n_rows, tt):
    t, c, _ = h2.shape
    e = zstart.shape[0]
    kern = functools.partial(_dispatch_kernel, E=e, tt=tt)
    return pl.pallas_call(
        kern,
        grid_spec=pltpu.PrefetchScalarGridSpec(
            num_scalar_prefetch=2,
            grid=(t // tt,),
            in_specs=[pl.BlockSpec(memory_space=pl.ANY),
                      pl.BlockSpec((tt, c, LANES), lambda i, zs, zv: (i, 0, 0))],
            out_specs=pl.BlockSpec(memory_space=pl.ANY),
            scratch_shapes=[pltpu.VMEM((ZERO_ROWS, c, LANES), U32), pltpu.SMEM((TOP_K, tt), I32),
                            pltpu.SemaphoreType.DMA, pltpu.SemaphoreType.DMA, pltpu.SemaphoreType.DMA]),
        out_shape=jax.ShapeDtypeStruct((n_rows, c, LANES), U32),
        compiler_params=_cparams(("arbitrary",)),
        name="dispatch",
    )(zstart, zcount, pos_tiles, h2)


def _experts_kernel(be_ref, nu_ref, nxt_ref, ord_ref, xs_ref, wg_hbm, wu_hbm, wd_hbm, ys_ref,
                    sg_s, su_s, sd_s, wgb_s, wub_s, wdb_s, sem, *, E):
    b = pl.program_id(0)
    n_used = nu_ref[0]

    def copies(e, slot):
        out = []
        for m, (w_hbm, st) in enumerate(((wg_hbm, sg_s), (wu_hbm, su_s), (wd_hbm, sd_s))):
            rows = w_hbm.shape[1] // 2
            for p in range(2):
                out.append(pltpu.make_async_copy(w_hbm.at[e, pl.ds(p * rows, rows)],
                                                 st.at[slot, pl.ds(p * rows, rows)], sem.at[slot, 2 * m + p]))
        return out

    def start_all(cps):
        for n, cp in enumerate(cps):
            cp.start(priority=n % 2)

    @pl.when(b < n_used)
    def _():
        e = be_ref[b]
        prev = be_ref[jnp.maximum(b - 1, 0)]
        slot = ord_ref[e] & 1

        @pl.when(b == 0)
        def _():
            start_all(copies(e, slot))

        @pl.when((b == 0) | (e != prev))
        def _():
            for cp in copies(e, slot):
                cp.wait()
            ne = nxt_ref[e]

            @pl.when(ne < E)
            def _():
                start_all(copies(ne, 1 - slot))

            wgb_s[...] = sg_s[slot].astype(BF16)
            wub_s[...] = su_s[slot].astype(BF16)
            wdb_s[...] = sd_s[slot].astype(BF16)

        parts = [_unpack_halves(ch) for ch in _load_row_tiled(xs_ref, EXPERT_BLOCK)]
        x_lo = jnp.concatenate([p[0] for p in parts], axis=1).astype(BF16)
        x_hi = jnp.concatenate([p[1] for p in parts], axis=1).astype(BF16)
        half = x_lo.shape[1]
        gate = _mm(x_lo, wgb_s[0:half, :]) + _mm(x_hi, wgb_s[half:, :])
        up = _mm(x_lo, wub_s[0:half, :]) + _mm(x_hi, wub_s[half:, :])
        a = _silu(gate) * up
        _store_row_tiled(ys_ref, _pack_halves(_mm(a.astype(BF16), wdb_s[...])))


def _experts(xs, block_e, n_used, next_used, used_ord, wg, wu, wd):
    e, d, de = wg.shape
    c = d // 2 // LANES
    nblk = xs.shape[0] // (EXPERT_BLOCK * c)
    kern = functools.partial(_experts_kernel, E=e)

    def live(b, be, nu, nx, od):
        return (jnp.minimum(b, jnp.maximum(nu[0] - 1, 0)), 0)

    hbm = pl.BlockSpec(memory_space=pl.ANY)
    return pl.pallas_call(
        kern,
        grid_spec=pltpu.PrefetchScalarGridSpec(
            num_scalar_prefetch=4,
            grid=(nblk,),
            in_specs=[pl.BlockSpec((EXPERT_BLOCK * c, LANES), live), hbm, hbm, hbm],
            out_specs=pl.BlockSpec((EXPERT_BLOCK * c, LANES), live),
            scratch_shapes=[pltpu.VMEM((2, d, de), F32), pltpu.VMEM((2, d, de), F32), pltpu.VMEM((2, de, d), F32),
                            pltpu.VMEM((d, de), BF16), pltpu.VMEM((d, de), BF16), pltpu.VMEM((de, d), BF16),
                            pltpu.SemaphoreType.DMA((2, 6))]),
        out_shape=jax.ShapeDtypeStruct(xs.shape, U32),
        compiler_params=_cparams(("arbitrary",)),
        name="experts",
    )(block_e, n_used, next_used, used_ord, xs, wg, wu, wd)


def _combine_kernel(pos_hbm, ys_hbm, ew_ref, x1_ref, h2_ref, g2_ref, nw3_ref, wsg_ref, wsu_ref, wsd_ref,
                    out_ref, buf_s, pos_s, sem_p, sem_r, *, tc, nt):
    tile = pl.program_id(0) * nt + pl.program_id(1)
    pos_copy = pltpu.make_async_copy(pos_hbm.at[tile], pos_s, sem_p)
    pos_copy.start()
    pos_copy.wait()

    def row_copy(t, k):
        c = ys_hbm.shape[1]
        return pltpu.make_async_copy(ys_hbm.at[pos_s[k, t]], buf_s.at[k, pl.ds(pl.multiple_of(t * c, c), c)], sem_r)

    def start_rows(t, c):
        for k in range(TOP_K):
            row_copy(t, k).start(priority=k % 2)
        return c

    def wait_rows(t, c):
        for k in range(TOP_K):
            row_copy(t, k).wait()
        return c

    lax.fori_loop(0, tc, start_rows, 0)
    parts = [_unpack_halves(ch) for ch in _load_row_tiled(h2_ref.at[0], tc)]
    h_lo = jnp.concatenate([p[0] for p in parts], axis=1).astype(BF16)
    h_hi = jnp.concatenate([p[1] for p in parts], axis=1).astype(BF16)
    half = h_lo.shape[1]
    nch = half // LANES
    gate = _mm(h_lo, wsg_ref[0:half, :]) + _mm(h_hi, wsg_ref[half:, :])
    up = _mm(h_lo, wsu_ref[0:half, :]) + _mm(h_hi, wsu_ref[half:, :])
    shared = _mm((_silu(gate) * up).astype(BF16), wsd_ref[...])
    lax.fori_loop(0, tc, wait_rows, 0)
    ew = ew_ref[0]
    acc = [shared[:, j * LANES:(j + 1) * LANES] for j in range(2 * nch)]
    for k in range(TOP_K):
        wk = ew[:, k:k + 1]
        for j, ch in enumerate(_load_row_tiled(buf_s.at[k], tc)):
            y_lo, y_hi = _unpack_halves(ch)
            acc[j] = acc[j] + y_lo * wk
            acc[nch + j] = acc[nch + j] + y_hi * wk
    out_ref[0] = x1_ref[0] + g2_ref[0] * _rms(jnp.concatenate(acc, axis=1), nw3_ref[...])


def _combine(pos_tiles, ys, ew, x1, h2, g2, nw3, wsg, wsu, wsd, tc):
    b, s, d = x1.shape
    ds_ = wsg.shape[1]
    nt = s // tc
    c = ys.shape[1]
    kern = functools.partial(_combine_kernel, tc=tc, nt=nt)
    tok = lambda w: pl.BlockSpec((1, tc, w), lambda bi, i: (bi, i, 0))
    return pl.pallas_call(
        kern,
        grid=(b, nt),
        in_specs=[pl.BlockSpec(memory_space=pl.ANY), pl.BlockSpec(memory_space=pl.ANY),
                  tok(TOP_K), tok(d), pl.BlockSpec((1, tc * c, LANES), lambda bi, i: (bi, i, 0)),
                  pl.BlockSpec((1, 1, d), lambda bi, i: (bi, 0, 0)),
                  pl.BlockSpec((1, d), lambda bi, i: (0, 0)),
                  pl.BlockSpec((d, ds_), lambda bi, i: (0, 0)),
                  pl.BlockSpec((d, ds_), lambda bi, i: (0, 0)),
                  pl.BlockSpec((ds_, d), lambda bi, i: (0, 0))],
        out_specs=tok(d),
        out_shape=jax.ShapeDtypeStruct((b, s, d), F32),
        scratch_shapes=[pltpu.VMEM((TOP_K, tc * c, LANES), U32), pltpu.SMEM((TOP_K, tc), I32),
                        pltpu.SemaphoreType.DMA, pltpu.SemaphoreType.DMA],
        compiler_params=_cparams(("arbitrary", "arbitrary")),
        name="combine",
    )(pos_tiles, ys, ew, x1, h2, g2, nw3, wsg, wsu, wsd)


def _tile(n, pref):
    t = min(n, pref)
    if n % t:
        raise NotImplementedError(f"size {n} is not a multiple of tile {t}")
    return t


def kernel(x, c, ctx, c_ctx, w_ada, b_ada, norms, w_in, ml_conv, ml_gate_b, ml_norm_w, attn_sink, w_out,
           w_router, b_router, w_exp_gate, w_exp_up, w_exp_down, w_sh_gate, w_sh_up, w_sh_down):
    if w_ada.shape[0] != 1:
        raise NotImplementedError("single-layer configuration only")
    b, s, d = x.shape
    lc = ctx.shape[1]
    t = b * s
    H = ML_HEADS
    dv = d // 2 // H
    dk = dv // 2
    dh = d // 2 // AT_HEADS
    qk_w, v_w = 2 * H * dk, H * dv
    ng = 4 * H
    aq_w, akv_w = AT_HEADS * dh, AT_KV_HEADS * dh
    e = w_router.shape[-1]
    nw = norms[0]

    rows = -(-(b + 1) // SUBLANES) * SUBLANES
    cond = jnp.zeros((rows, d), F32).at[:b].set(c).at[b].set(c_ctx)
    mods = _ada(cond, w_ada[0], b_ada[0][None, :])
    sh1, sc1, g1, sh2, sc2, g2 = [m[:b, None, :] for m in jnp.split(mods, 6, axis=-1)]
    csh1, csc1 = [jnp.broadcast_to(m[b:b + 1, None, :], (b, 1, d)) for m in jnp.split(mods, 6, axis=-1)[:2]]

    w0 = w_in[0]
    o_ml, o_g, o_q = 0, qk_w + 2 * v_w, qk_w + 2 * v_w + ng
    ml_w = qk_w + 2 * v_w
    wp = jnp.concatenate([w0[:, o_ml:o_g], w0[:, o_q:], w0[:, o_g:o_q], jnp.zeros((d, LANES - ng), F32)],
                         axis=1).astype(BF16)
    layout = (("ml", 0, ml_w), ("aq", ml_w, aq_w), ("ak", ml_w + aq_w, akv_w),
              ("av", ml_w + aq_w + akv_w, akv_w), ("g", ml_w + aq_w + 2 * akv_w, LANES))
    widths = dict(ml=ml_w, aq=aq_w, ak=akv_w, av=akv_w, g=LANES)
    tabs = _rope_tables(s, dh)
    proj = functools.partial(_inproj, layout=layout, widths=widths, dh=dh)
    ml, aq, ak, av, gates = proj(x, sh1, sc1, nw[0:1], wp, tabs, rope=True, tm=_tile(s, 512))
    tm_c = _tile(lc, 512)
    mlc, _, akc, avc, gates_c = proj(ctx, csh1, csc1, nw[0:1], wp, tuple(tb[:tm_c] for tb in tabs),
                                     rope=False, tm=tm_c)

    L = _tile(lc, 256)
    if s % L:
        raise NotImplementedError("sequence must be a multiple of the mLSTM chunk")
    gate_b = jnp.zeros((1, LANES), F32).at[0, :ng].set(ml_gate_b[0])
    gp = _gate_prep(gates, gate_b, L)
    gpc = _gate_prep(gates_c, gate_b, L)
    y_ml = _mlstm(ml, mlc, gp, gpc, ml_conv[0], ml_norm_w[0][None, :], L)

    y_at = _attn(aq, ak, av, akc, avc, attn_sink[0])

    x1, h2, logits = _outproj(y_ml, y_at, x, g1, sh2, sc2, nw[1:2], nw[2:3], w_out[0].astype(BF16),
                              w_router[0], _tile(s, 256))

    eidx, ew, rank, cnt = _route(logits.reshape(t, e), jnp.broadcast_to(b_router[0][:, None], (e, LANES)),
                                 _tile(t, 512))
    counts = cnt[:, 0].astype(I32)
    padded = (counts + EXPERT_BLOCK - 1) // EXPERT_BLOCK * EXPERT_BLOCK
    pad_end = jnp.cumsum(padded)
    pad_start = pad_end - padded
    n_blocks = -(-(t * TOP_K + e * (EXPERT_BLOCK - 1)) // EXPERT_BLOCK)
    n_rows = n_blocks * EXPERT_BLOCK
    blk_row = jnp.arange(n_blocks, dtype=I32) * EXPERT_BLOCK
    block_e = jnp.minimum(jnp.sum((pad_end[None, :] <= blk_row[:, None]).astype(I32), axis=1), e - 1)
    n_used = (pad_end[-1] // EXPERT_BLOCK).astype(I32)[None]
    used = counts > 0
    eid = jnp.arange(e, dtype=I32)
    next_used = jnp.concatenate([lax.cummin(jnp.where(used, eid, e), reverse=True)[1:], jnp.full((1,), e, I32)])
    used_ord = jnp.cumsum(used.astype(I32)) - 1
    pos = _positions(eidx, rank, pad_start, _tile(t, 512))

    tt = _tile(t, 512)
    pos_d = pos.reshape(TOP_K, t // tt, tt).transpose(1, 0, 2)
    rc = d // 2 // LANES
    zcount = (padded - counts + ZERO_ROWS - 1) // ZERO_ROWS
    xs = _dispatch(h2.reshape(t, rc, LANES), pos_d, (pad_end - zcount * ZERO_ROWS).astype(I32), zcount.astype(I32),
                   n_rows, tt)
    ys = _experts(xs.reshape(n_rows * rc, LANES), block_e, n_used, next_used, used_ord,
                  w_exp_gate[0], w_exp_up[0], w_exp_down[0])
    tc = _tile(s, 128)
    pos_c = pos.reshape(TOP_K, t // tc, tc).transpose(1, 0, 2)
    ew_t = ew.T.reshape(b, s, TOP_K)
    return _combine(pos_c, ys.reshape(n_rows, rc, LANES), ew_t, x1, h2, g2, nw[3:4],
                    w_sh_gate[0].astype(BF16), w_sh_up[0].astype(BF16), w_sh_down[0].astype(BF16), tc)
```

```python
import functools

import jax
import jax.numpy as jnp
from jax import lax
from jax.experimental import pallas as pl
from jax.experimental.pallas import tpu as pltpu

F32 = jnp.float32
BF16 = jnp.bfloat16
I32 = jnp.int32

LANES = 128
SUBLANES = 8
VMEM_LIMIT_BYTES = 56 * 1024 * 1024

NORM_EPS = 1e-6
ML_HEADS = 4
CONV_W = 5
AT_HEADS = 8
AT_KV_HEADS = 2
GRID_W = 64
WINDOW = 128
BLOCK_Q = 128
ROPE_THETA = 10000.0
N_GROUPS = 8
TOPK_GROUPS = 4
TOP_K = 8
ROUTE_SCALE = 2.5
EXPERT_BLOCK = 256
ZERO_ROWS = 32
NEG_BIG = -1e30


def _cparams(sem):
    return pltpu.CompilerParams(dimension_semantics=sem, vmem_limit_bytes=VMEM_LIMIT_BYTES)


def _rms(xf, w):
    return xf * lax.rsqrt(jnp.mean(xf * xf, axis=-1, keepdims=True) + NORM_EPS) * w


def _silu(x):
    return x * jax.nn.sigmoid(x)


def _mm(a, b):
    return jnp.dot(a, b, preferred_element_type=F32)


U32 = jnp.uint32


def _pack_halves(x):
    n = x.shape[1] // 2
    u = lax.bitcast_convert_type(x.astype(BF16).astype(F32), U32)
    return (u[:, :n] >> 16) | (u[:, n:] & jnp.uint32(0xFFFF0000))


def _unpack_halves(u):
    return (lax.bitcast_convert_type(u << 16, F32),
            lax.bitcast_convert_type(u & jnp.uint32(0xFFFF0000), F32))


def _store_row_tiled(ref2d, packed):
    r, n = packed.shape
    c = n // LANES
    for j in range(c):
        ref2d[pl.ds(j, r, stride=c), :] = packed[:, j * LANES:(j + 1) * LANES]


def _load_row_tiled(ref2d, r):
    c = ref2d.shape[0] // r
    return [ref2d[pl.ds(j, r, stride=c), :] for j in range(c)]


def _ada_kernel(c_ref, w_ref, b_ref, o_ref):
    s = _silu(c_ref[...]).astype(BF16)
    o_ref[...] = _mm(s, w_ref[...].astype(BF16)) + b_ref[...]


def _ada(cond, w, b):
    r, d = cond.shape
    n = w.shape[1]
    tn = min(n, 1024)
    return pl.pallas_call(
        _ada_kernel,
        grid=(n // tn,),
        in_specs=[pl.BlockSpec((r, d), lambda j: (0, 0)),
                  pl.BlockSpec((d, tn), lambda j: (0, j)),
                  pl.BlockSpec((1, tn), lambda j: (0, j))],
        out_specs=pl.BlockSpec((r, tn), lambda j: (0, j)),
        out_shape=jax.ShapeDtypeStruct((r, n), F32),
        compiler_params=_cparams(("arbitrary",)),
        name="ada",
    )(cond, w, b)


def _inproj_kernel(x_ref, sh_ref, sc_ref, nw_ref, w_ref, rc_ref, ra_ref, rb_ref,
                   ml_ref, aq_ref, ak_ref, av_ref, g_ref, hb_ref, *, layout, dh, rope, qscale, cw):
    h = _rms(x_ref[0], nw_ref[...]) * (1.0 + sc_ref[0]) + sh_ref[0]
    hb_ref[...] = h.astype(BF16)
    outs = dict(ml=ml_ref, aq=aq_ref, ak=ak_ref, av=av_ref, g=g_ref)
    for name, col0, width in layout:
        o_ref = outs[name]
        for c in range(0, width, cw):
            step = min(cw, width - c)
            acc = _mm(hb_ref[...], w_ref[:, col0 + c:col0 + c + step])
            if rope and name in ("aq", "ak"):
                for hh in range(step // dh):
                    a = acc[:, hh * dh:(hh + 1) * dh]
                    r = (a * rc_ref[...] + pltpu.roll(a, dh - dh // 4, 1) * ra_ref[...]
                         + pltpu.roll(a, dh // 4, 1) * rb_ref[...])
                    if name == "aq":
                        r = r * qscale
                    o_ref[0, :, c + hh * dh:c + (hh + 1) * dh] = r.astype(o_ref.dtype)
            else:
                if name == "aq":
                    acc = acc * qscale
                o_ref[0, :, c:c + step] = acc.astype(o_ref.dtype)


def _inproj(x, sh, sc, nw, wp, tabs, *, layout, widths, dh, rope, tm):
    b, s, d = x.shape
    npad = wp.shape[1]
    rc, ra, rb = tabs
    kern = functools.partial(_inproj_kernel, layout=layout, dh=dh, rope=rope, qscale=float(dh) ** -0.5, cw=512)
    vec = pl.BlockSpec((1, 1, d), lambda bi, i: (bi, 0, 0))
    tab = pl.BlockSpec((tm, dh), lambda bi, i: (i, 0))
    out_dt = dict(ml=BF16, aq=BF16, ak=BF16, av=BF16, g=F32)
    names = ("ml", "aq", "ak", "av", "g")
    return pl.pallas_call(
        kern,
        grid=(b, s // tm),
        in_specs=[pl.BlockSpec((1, tm, d), lambda bi, i: (bi, i, 0)), vec, vec,
                  pl.BlockSpec((1, d), lambda bi, i: (0, 0)),
                  pl.BlockSpec((d, npad), lambda bi, i: (0, 0), pipeline_mode=pl.Buffered(1)),
                  tab, tab, tab],
        out_specs=[pl.BlockSpec((1, tm, widths[n]), lambda bi, i: (bi, i, 0)) for n in names],
        out_shape=[jax.ShapeDtypeStruct((b, s, widths[n]), out_dt[n]) for n in names],
        scratch_shapes=[pltpu.VMEM((tm, d), BF16)],
        compiler_params=_cparams(("arbitrary", "arbitrary")),
        name="inproj_rope" if rope else "inproj_ctx",
    )(x, sh, sc, nw, wp, rc, ra, rb)


def _rope_tables(s, dh):
    rows = s // GRID_W
    row = jnp.repeat(jnp.arange(rows), GRID_W)
    col = jnp.tile(jnp.arange(GRID_W), rows)
    nf = dh // 4
    freqs = ROPE_THETA ** (-jnp.arange(nf, dtype=F32) / nf)
    pos = jnp.stack([row, col], axis=-1).astype(F32)
    ang = pos[:, :, None] * freqs
    cos, sin = jnp.cos(ang), jnp.sin(ang)
    z = jnp.zeros_like(sin[:, 0])
    rc = jnp.concatenate([cos[:, 0], cos[:, 0], cos[:, 1], cos[:, 1]], axis=-1)
    ra = jnp.concatenate([-sin[:, 0], z, -sin[:, 1], z], axis=-1)
    rb = jnp.concatenate([z, sin[:, 0], z, sin[:, 1]], axis=-1)
    return rc, ra, rb


def _gate_prep_kernel(g_ref, b_ref, gl_ref, pp_ref, glt_ref, ppt_ref, tot_ref, *, L, H, nc):
    for c in range(nc):
        z = g_ref[0, c * L:(c + 1) * L, :] + b_ref[...]
        lane = lax.broadcasted_iota(I32, z.shape, 1)
        row = lax.broadcasted_iota(I32, z.shape, 0)
        is_f = ((lane // H) % 2 == 1) & (lane < 4 * H)
        log_sig = jnp.minimum(z, 0.0) - jnp.log1p(jnp.exp(-jnp.abs(z)))
        gl = jnp.where(lane < 4 * H, jnp.where(is_f, log_sig, z), 0.0)
        cs = gl
        k = 1
        while k < L:
            cs = cs + jnp.where(row >= k, pltpu.roll(cs, k, 0), 0.0)
            k *= 2
        tot = cs[L - 1:L, :]
        suf = tot - cs + gl
        pp = jnp.where(lane >= 2 * H, suf, cs)
        gl_ref[0, c] = gl
        pp_ref[0, c] = pp
        glt_ref[0, c] = gl.T
        ppt_ref[0, c] = pp.T
        tot_ref[0, c] = jnp.broadcast_to(tot, (SUBLANES, LANES))


def _gate_prep(g, gate_b, L):
    b, s, _ = g.shape
    nc = s // L
    kern = functools.partial(_gate_prep_kernel, L=L, H=ML_HEADS, nc=nc)
    col = pl.BlockSpec((1, nc, L, LANES), lambda bi: (bi, 0, 0, 0))
    rowb = pl.BlockSpec((1, nc, LANES, L), lambda bi: (bi, 0, 0, 0))
    return pl.pallas_call(
        kern,
        grid=(b,),
        in_specs=[pl.BlockSpec((1, s, LANES), lambda bi: (bi, 0, 0)),
                  pl.BlockSpec((1, LANES), lambda bi: (0, 0))],
        out_specs=[col, col, rowb, rowb, pl.BlockSpec((1, nc, SUBLANES, LANES), lambda bi: (bi, 0, 0, 0))],
        out_shape=[jax.ShapeDtypeStruct((b, nc, L, LANES), F32), jax.ShapeDtypeStruct((b, nc, L, LANES), F32),
                   jax.ShapeDtypeStruct((b, nc, LANES, L), F32), jax.ShapeDtypeStruct((b, nc, LANES, L), F32),
                   jax.ShapeDtypeStruct((b, nc, SUBLANES, LANES), F32)],
        compiler_params=_cparams(("arbitrary",)),
        name="gate_prep",
    )(g, gate_b)


CONV_PAD = 8


def _lane_pick(tile, ch):
    lane = lax.broadcasted_iota(I32, tile.shape, 1)
    return jnp.sum(jnp.where(lane == ch, tile, 0.0), axis=1, keepdims=True)


def _sublane_pick(tile, ch):
    sub = lax.broadcasted_iota(I32, tile.shape, 0)
    return jnp.sum(jnp.where(sub == ch, tile, 0.0), axis=0, keepdims=True)


def _mlstm_chunk(d, ch_i, ch_f, q, k, v, gl, pp, glt, ppt, tot8, ct_ref, n_ref, m_ref, mask_ref=None):
    L = k.shape[0]
    p_col = _lane_pick(pp, ch_f)
    li_col = _lane_pick(gl, ch_i)
    btot = _lane_pick(tot8[0:1], ch_f)
    m_old = m_ref[d][0:1, 0:1]
    ct = ct_ref[d]
    n_row = n_ref[d]
    h = None
    if q is not None:
        p_row = _sublane_pick(ppt, ch_f)
        li_row = _sublane_pick(glt, ch_i)
        logw = (p_col - p_row + li_row) + mask_ref[d]
        log_inter = p_col + m_old
        m_q = jnp.maximum(log_inter, jnp.max(logw, axis=1, keepdims=True))
        sqk = lax.dot_general(q, k, (((1,), (1,)), ((), ())), preferred_element_type=F32) * jnp.exp(logw - m_q)
        w_inter = jnp.exp(log_inter - m_q)
        num = _mm(sqk.astype(BF16), v) + w_inter * _mm(q, ct.astype(BF16))
        den = (jnp.sum(sqk, axis=1, keepdims=True)
               + w_inter * jnp.sum(q.astype(F32) * n_row, axis=1, keepdims=True))
        h = num / jnp.maximum(jnp.abs(den), jnp.exp(-m_q))
    log_w_end = btot - p_col + li_col
    m_new = jnp.maximum(btot + m_old, jnp.max(log_w_end, axis=0, keepdims=True))
    w_end = jnp.exp(log_w_end - m_new)
    decay = jnp.exp(btot + m_old - m_new)
    kw = k.astype(F32) * w_end
    ct_ref[d] = decay * ct + lax.dot_general(kw.astype(BF16), v, (((0,), (0,)), ((), ())),
                                             preferred_element_type=F32)
    n_ref[d] = decay * n_row + jnp.sum(kw, axis=0, keepdims=True)
    m_ref[d] = jnp.broadcast_to(m_new, (SUBLANES, LANES))
    return h


def _mlstm_kernel(q_ref, k_ref, v_ref, o_ref, kc_ref, vc_ref,
                  gl_ref, pp_ref, glt_ref, ppt_ref, tot_ref, glc_ref, ppc_ref, totc_ref,
                  cwq_ref, cwk_ref, nw_ref, mask_ref, y_ref,
                  xq_s, xk_s, xkc_s, qs_s, ks_s, kcs_s, hf_s, hb_s, ct_s, n_s, m_s,
                  *, L, H, S, Lc, dk, dv):
    hh = pl.program_id(1)
    nc, ncc = S // L, Lc // L
    zpad = jnp.zeros((CONV_PAD, dk), F32)

    def conv_silu(x_s, cw_ref, n_rows, out_s, scale):
        for c in range(n_rows // L):
            acc = jnp.zeros((L, dk), F32)
            for j in range(CONV_W):
                acc = acc + cw_ref[j:j + 1, :] * x_s[pl.ds(CONV_PAD + c * L + j - CONV_W // 2, L), :]
            out_s[c] = (_silu(acc) * scale).astype(BF16)

    def stage(x_s, src, n_rows):
        x_s[0:CONV_PAD, :] = zpad
        x_s[CONV_PAD + n_rows:2 * CONV_PAD + n_rows, :] = zpad
        x_s[CONV_PAD:CONV_PAD + n_rows, :] = src.astype(F32)

    stage(xq_s, q_ref[0], S)
    stage(xk_s, k_ref[0], S)
    stage(xkc_s, kc_ref[0], Lc)
    conv_silu(xq_s, cwq_ref, S, qs_s, float(dk) ** -0.5)
    conv_silu(xk_s, cwk_ref, S, ks_s, 1.0)
    conv_silu(xkc_s, cwk_ref, Lc, kcs_s, 1.0)

    ct_s[...] = jnp.zeros(ct_s.shape, F32)
    n_s[...] = jnp.zeros(n_s.shape, F32)
    m_s[...] = jnp.zeros(m_s.shape, F32)

    def chans(d):
        return d * 2 * H + hh, d * 2 * H + H + hh

    for d in (0, 1):
        ch_i, ch_f = chans(d)
        for c in (range(ncc) if d == 0 else range(ncc - 1, -1, -1)):
            _mlstm_chunk(d, ch_i, ch_f, None, kcs_s[c], vc_ref[0, c * L:(c + 1) * L, :],
                         glc_ref[0, c], ppc_ref[0, c], None, None, totc_ref[0, c], ct_s, n_s, m_s)

    def body(i, carry):
        for d in (0, 1):
            ch_i, ch_f = chans(d)
            c = i if d == 0 else nc - 1 - i
            r0 = pl.multiple_of(c * L, L)
            h = _mlstm_chunk(d, ch_i, ch_f, qs_s[c], ks_s[c], v_ref[0, pl.ds(r0, L), :],
                             gl_ref[0, c], pp_ref[0, c], glt_ref[0, c], ppt_ref[0, c], tot_ref[0, c],
                             ct_s, n_s, m_s, mask_ref)
            if d == 0:
                hf_s[c] = h
            else:
                hb_s[c] = h
        return carry

    lax.fori_loop(0, nc, body, 0)

    for c in range(nc):
        hs = hf_s[c] + hb_s[c]
        hn = _rms(hs, nw_ref[...])
        y_ref[0, c * L:(c + 1) * L, :] = (hn * jax.nn.sigmoid(o_ref[0, c * L:(c + 1) * L, :].astype(F32))).astype(BF16)


def _mlstm(ml, mlc, gp, gpc, conv_w, norm_w, L):
    b, s, _ = ml.shape
    lc = mlc.shape[1]
    H = ML_HEADS
    dv = norm_w.shape[1] // H
    dk = dv // 2
    nc, ncc = s // L, lc // L
    gl, pp, glt, ppt, tot = gp
    glc, ppc, _, _, totc = gpc
    kern = functools.partial(_mlstm_kernel, L=L, H=H, S=s, Lc=lc, dk=dk, dv=dv)
    tri = jnp.arange(L)[None, :] <= jnp.arange(L)[:, None]
    masks = jnp.where(jnp.stack([tri, tri.T]), 0.0, NEG_BIG).astype(F32)

    def colspec(rows, w, off):
        return pl.BlockSpec((1, rows, w), lambda bi, h: (bi, 0, off + h))

    def gspec(n, r, c):
        return pl.BlockSpec((1, n, r, c), lambda bi, h: (bi, 0, 0, 0))

    return pl.pallas_call(
        kern,
        grid=(b, H),
        in_specs=[colspec(s, dk, 0), colspec(s, dk, H), colspec(s, dv, H), colspec(s, dv, 2 * H),
                  colspec(lc, dk, H), colspec(lc, dv, H),
                  gspec(nc, L, LANES), gspec(nc, L, LANES), gspec(nc, LANES, L), gspec(nc, LANES, L),
                  gspec(nc, SUBLANES, LANES),
                  gspec(ncc, L, LANES), gspec(ncc, L, LANES), gspec(ncc, SUBLANES, LANES),
                  pl.BlockSpec((CONV_W, dk), lambda bi, h: (0, h)),
                  pl.BlockSpec((CONV_W, dk), lambda bi, h: (0, H + h)),
                  pl.BlockSpec((1, dv), lambda bi, h: (0, h)),
                  pl.BlockSpec((2, L, L), lambda bi, h: (0, 0, 0))],
        out_specs=pl.BlockSpec((1, s, dv), lambda bi, h: (bi, 0, h)),
        out_shape=jax.ShapeDtypeStruct((b, s, H * dv), BF16),
        scratch_shapes=[pltpu.VMEM((s + 2 * CONV_PAD, dk), F32), pltpu.VMEM((s + 2 * CONV_PAD, dk), F32),
                        pltpu.VMEM((lc + 2 * CONV_PAD, dk), F32),
                        pltpu.VMEM((nc, L, dk), BF16), pltpu.VMEM((nc, L, dk), BF16), pltpu.VMEM((ncc, L, dk), BF16),
                        pltpu.VMEM((nc, L, dv), F32), pltpu.VMEM((nc, L, dv), F32),
                        pltpu.VMEM((2, dk, dv), F32), pltpu.VMEM((2, 1, dk), F32),
                        pltpu.VMEM((2, SUBLANES, LANES), F32)],
        compiler_params=_cparams(("arbitrary", "arbitrary")),
        name="mlstm",
    )(ml, ml, ml, ml, mlc, mlc, gl, pp, glt, ppt, tot, glc, ppc, totc, conv_w, conv_w, norm_w, masks)


def _attn_kernel(sink_ref, q_ref, kp_ref, kc_ref, kn_ref, vp_ref, vc_ref, vn_ref, kx_ref, vx_ref, bias_ref,
                 o_ref, *, bq, dh, G):
    q = q_ref[0]
    for g in range(AT_KV_HEADS):
        sl = slice(g * dh, (g + 1) * dh)
        qg = jnp.concatenate([q[:, (g * G + j) * dh:(g * G + j + 1) * dh] for j in range(G)], axis=0)
        kcat = jnp.concatenate([kp_ref[0][:, sl], kc_ref[0][:, sl], kn_ref[0][:, sl], kx_ref[0][:, sl]], axis=0)
        vcat = jnp.concatenate([vp_ref[0][:, sl], vc_ref[0][:, sl], vn_ref[0][:, sl], vx_ref[0][:, sl]], axis=0)
        s = lax.dot_general(qg, kcat, (((1,), (1,)), ((), ())), preferred_element_type=F32) + bias_ref[0]
        ri = lax.broadcasted_iota(I32, (G * bq, 1), 0)
        sk = jnp.zeros((G * bq, 1), F32)
        for j in range(G):
            sk = jnp.where((ri >= j * bq) & (ri < (j + 1) * bq), sink_ref[g * G + j], sk)
        m = jnp.maximum(jnp.max(s, axis=1, keepdims=True), sk)
        p = jnp.exp(s - m)
        den = jnp.sum(p, axis=1, keepdims=True) + jnp.exp(sk - m)
        o = _mm(p.astype(BF16), vcat) / den
        for j in range(G):
            o_ref[0, :, (g * G + j) * dh:(g * G + j + 1) * dh] = o[j * bq:(j + 1) * bq, :].astype(BF16)


def _attn_bias(bq, lc, G):
    qoff = (jnp.arange(G * bq) % bq)[:, None]
    ci = jnp.arange(3 * bq + lc)[None, :]
    prev = (ci < bq) & (qoff <= ci)
    cur = (ci >= bq) & (ci < 2 * bq)
    nxt = (ci >= 2 * bq) & (ci < 3 * bq) & (ci - 2 * bq <= qoff)
    ctx = ci >= 3 * bq
    inner = prev | cur | nxt | ctx
    first = cur | nxt | ctx
    last = prev | cur | ctx
    if WINDOW != bq:
        raise NotImplementedError("window must equal the query block")
    return jnp.where(jnp.stack([inner, first, last]), 0.0, NEG_BIG).astype(F32)


def _attn(aq, ak, av, akc, avc, sink):
    b, s, hd = aq.shape
    dh = hd // AT_HEADS
    G = AT_HEADS // AT_KV_HEADS
    bq = BLOCK_Q
    nb = s // bq
    lc = akc.shape[1]
    kvw = AT_KV_HEADS * dh
    bias = _attn_bias(bq, lc, G)
    if nb < 2:
        raise NotImplementedError("needs at least two query blocks")
    kern = functools.partial(_attn_kernel, bq=bq, dh=dh, G=G)
    prev = pl.BlockSpec((1, bq, kvw), lambda bi, n: (bi, jnp.maximum(n - 1, 0), 0))
    cur = pl.BlockSpec((1, bq, kvw), lambda bi, n: (bi, n, 0))
    nxt = pl.BlockSpec((1, bq, kvw), lambda bi, n: (bi, jnp.minimum(n + 1, nb - 1), 0))
    cx = pl.BlockSpec((1, lc, kvw), lambda bi, n: (bi, 0, 0))
    return pl.pallas_call(
        kern,
        grid=(b, nb),
        in_specs=[pl.BlockSpec(memory_space=pltpu.SMEM),
                  pl.BlockSpec((1, bq, hd), lambda bi, n: (bi, n, 0)),
                  prev, cur, nxt, prev, cur, nxt, cx, cx,
                  pl.BlockSpec((1, G * bq, 3 * bq + lc),
                               lambda bi, n: (jnp.where(n == 0, 1, jnp.where(n == nb - 1, 2, 0)), 0, 0))],
        out_specs=pl.BlockSpec((1, bq, hd), lambda bi, n: (bi, n, 0)),
        out_shape=jax.ShapeDtypeStruct((b, s, hd), BF16),
        compiler_params=_cparams(("arbitrary", "arbitrary")),
        name="attn",
    )(sink, aq, ak, ak, ak, av, av, av, akc, avc, bias)


def _outproj_kernel(ym_ref, ya_ref, x_ref, g1_ref, sh2_ref, sc2_ref, nw1_ref, nw2_ref, wo_ref, wrh_ref, wrl_ref,
                    x1_ref, h2_ref, lg_ref, y_s, *, dm, cw):
    i = pl.program_id(0)
    d = y_s.shape[2]

    @pl.when(i == 0)
    def _():
        y_s[...] = jnp.zeros(y_s.shape, F32)

    cur = i & 1
    x1 = x_ref[0] + g1_ref[0] * _rms(y_s[1 - cur], nw1_ref[...])
    x1_ref[0] = x1
    h2 = _rms(x1, nw2_ref[...]) * (1.0 + sc2_ref[0]) + sh2_ref[0]
    hi = h2.astype(BF16)
    _store_row_tiled(h2_ref.at[0], _pack_halves(h2))
    lo = (h2 - hi.astype(F32)).astype(BF16)
    lg_ref[0] = _mm(hi, wrh_ref[...]) + (_mm(hi, wrl_ref[...]) + _mm(lo, wrh_ref[...]))
    for c in range(0, d, cw):
        y_s[cur, :, c:c + cw] = _mm(ym_ref[0], wo_ref[0:dm, c:c + cw]) + _mm(ya_ref[0], wo_ref[dm:, c:c + cw])


def _outproj(y_ml, y_at, x, g1, sh2, sc2, nw1, nw2, wo, wr, tm):
    b, s, d = x.shape
    dm = y_ml.shape[2]
    e = wr.shape[1]
    wr_hi = wr.astype(BF16)
    wr_lo = (wr - wr_hi.astype(F32)).astype(BF16)
    kern = functools.partial(_outproj_kernel, dm=dm, cw=min(d, 512))
    nt = s // tm
    n = b * nt

    def ahead(i):
        a = jnp.minimum(i, n - 1)
        return a // nt, a % nt

    def behind(i):
        a = jnp.maximum(i - 1, 0)
        return a // nt, a % nt

    def tok(w, which):
        return pl.BlockSpec((1, tm, w), lambda i: (*which(i), 0))

    vec = pl.BlockSpec((1, 1, d), lambda i: (behind(i)[0], 0, 0))
    nspec = pl.BlockSpec((1, d), lambda i: (0, 0))
    const = lambda r, c: pl.BlockSpec((r, c), lambda i: (0, 0), pipeline_mode=pl.Buffered(1))
    rc = d // 2 // LANES
    return pl.pallas_call(
        kern,
        grid=(n + 1,),
        in_specs=[tok(dm, ahead), tok(d - dm, ahead), tok(d, behind),
                  vec, vec, vec, nspec, nspec, const(d, d), const(d, e), const(d, e)],
        out_specs=[tok(d, behind), pl.BlockSpec((1, tm * rc, LANES), lambda i: (*behind(i), 0)), tok(e, behind)],
        out_shape=[jax.ShapeDtypeStruct((b, s, d), F32), jax.ShapeDtypeStruct((b, s * rc, LANES), U32),
                   jax.ShapeDtypeStruct((b, s, e), F32)],
        scratch_shapes=[pltpu.VMEM((2, tm, d), F32)],
        compiler_params=_cparams(("arbitrary",)),
        name="outproj",
    )(y_ml, y_at, x, g1, sh2, sc2, nw1, nw2, wo, wr_hi, wr_lo)


def _route_kernel(lg_ref, br_ref, eidx_ref, ew_ref, rank_ref, cnt_ref, carry_s, *, E, tr):
    i = pl.program_id(0)

    @pl.when(i == 0)
    def _():
        carry_s[...] = jnp.zeros(carry_s.shape, F32)

    scores = jax.nn.sigmoid(lg_ref[...].T)
    biased = scores + br_ref[:, 0:1]
    row = lax.broadcasted_iota(I32, (E, tr), 0).astype(F32)
    gs = E // N_GROUPS
    ninf = -jnp.inf
    grp = []
    for g in range(N_GROUPS):
        xg = biased[g * gs:(g + 1) * gs]
        rg = (lax.broadcasted_iota(I32, (gs, tr), 0) + g * gs).astype(F32)
        m1 = jnp.max(xg, axis=0, keepdims=True)
        i1 = jnp.min(jnp.where(xg == m1, rg, float(E)), axis=0, keepdims=True)
        m2 = jnp.max(jnp.where(rg == i1, ninf, xg), axis=0, keepdims=True)
        grp.append(m1 + m2)
    gsc = jnp.concatenate(grp, axis=0)
    gi = lax.broadcasted_iota(I32, (N_GROUPS, tr), 0)
    beaten = jnp.zeros((N_GROUPS, tr), I32)
    for g2 in range(N_GROUPS):
        sg = gsc[g2:g2 + 1]
        beaten = beaten + jnp.where((sg > gsc) | ((sg == gsc) & (gi > g2)), 1, 0)
    keep_g = jnp.where(beaten < TOPK_GROUPS, 1.0, 0.0)
    keep = jnp.concatenate([jnp.broadcast_to(keep_g[g:g + 1], (gs, tr)) for g in range(N_GROUPS)], axis=0)
    masked = jnp.where(keep > 0.5, biased, ninf)
    idxs, ws = [], []
    for _ in range(TOP_K):
        m = jnp.max(masked, axis=0, keepdims=True)
        ik = jnp.min(jnp.where(masked == m, row, float(E)), axis=0, keepdims=True)
        sel = row == ik
        ws.append(jnp.sum(jnp.where(sel, scores, 0.0), axis=0, keepdims=True))
        idxs.append(ik)
        masked = jnp.where(sel, ninf, masked)
    w = jnp.concatenate(ws, axis=0)
    ew_ref[...] = w / jnp.sum(w, axis=0, keepdims=True) * ROUTE_SCALE
    eidx_ref[...] = jnp.concatenate(idxs, axis=0).astype(I32)

    assign = jnp.zeros((E, tr), F32)
    for ik in idxs:
        assign = assign + jnp.where(row == ik, 1.0, 0.0)
    upper = jnp.where(lax.broadcasted_iota(I32, (tr, tr), 0) < lax.broadcasted_iota(I32, (tr, tr), 1), 1.0, 0.0)
    base = _mm(assign.astype(BF16), upper.astype(BF16)) + carry_s[:, 0:1]
    ranks = [jnp.sum(jnp.where(row == ik, base, 0.0), axis=0, keepdims=True) for ik in idxs]
    rank_ref[...] = jnp.concatenate(ranks, axis=0).astype(I32)
    carry_s[...] = carry_s[...] + jnp.sum(assign, axis=1, keepdims=True)
    cnt_ref[...] = carry_s[...]


def _route(logits, b_router, tr):
    t, e = logits.shape
    kern = functools.partial(_route_kernel, E=e, tr=tr)
    kt = pl.BlockSpec((TOP_K, tr), lambda i: (0, i))
    return pl.pallas_call(
        kern,
        grid=(t // tr,),
        in_specs=[pl.BlockSpec((tr, e), lambda i: (i, 0)), pl.BlockSpec((e, LANES), lambda i: (0, 0))],
        out_specs=[kt, kt, kt, pl.BlockSpec((e, LANES), lambda i: (0, 0))],
        out_shape=[jax.ShapeDtypeStruct((TOP_K, t), I32), jax.ShapeDtypeStruct((TOP_K, t), F32),
                   jax.ShapeDtypeStruct((TOP_K, t), I32), jax.ShapeDtypeStruct((e, LANES), F32)],
        scratch_shapes=[pltpu.VMEM((e, LANES), F32)],
        compiler_params=_cparams(("arbitrary",)),
        name="route",
    )(logits, b_router)


def _positions_kernel(eidx_ref, rank_ref, ps_ref, pos_ref, *, E, tr):
    row = lax.broadcasted_iota(I32, (E, tr), 0)
    start = ps_ref[:, 0:1]
    rows = [jnp.sum(jnp.where(row == eidx_ref[k:k + 1, :], start, 0.0), axis=0, keepdims=True)
            for k in range(TOP_K)]
    pos_ref[...] = jnp.concatenate(rows, axis=0).astype(I32) + rank_ref[...]


def _positions(eidx, rank, pad_start, tr):
    t = eidx.shape[1]
    e = pad_start.shape[0]
    kern = functools.partial(_positions_kernel, E=e, tr=tr)
    kt = pl.BlockSpec((TOP_K, tr), lambda i: (0, i))
    return pl.pallas_call(
        kern,
        grid=(t // tr,),
        in_specs=[kt, kt, pl.BlockSpec((e, LANES), lambda i: (0, 0))],
        out_specs=kt,
        out_shape=jax.ShapeDtypeStruct((TOP_K, t), I32),
        compiler_params=_cparams(("arbitrary",)),
        name="positions",
    )(eidx, rank, jnp.broadcast_to(pad_start.astype(F32)[:, None], (e, LANES)))


def _dispatch_kernel(zstart_ref, zcount_ref, pos_hbm, h2_ref, xs_hbm, zero_s, pos_s, sem_p, sem_z, sem_r,
                     *, E, tt):
    i = pl.program_id(0)

    def zero_copy(e, j):
        return pltpu.make_async_copy(zero_s, xs_hbm.at[pl.ds(zstart_ref[e] + j * ZERO_ROWS, ZERO_ROWS)], sem_z)

    @pl.when(i == 0)
    def _():
        zero_s[...] = jnp.zeros(zero_s.shape, U32)

        def each(fn):
            def per_expert(e, c):
                def per_piece(j, c2):
                    fn(zero_copy(e, j))
                    return c2
                return lax.fori_loop(0, zcount_ref[e], per_piece, c)
            lax.fori_loop(0, E, per_expert, 0)

        each(lambda cp: cp.start())
        each(lambda cp: cp.wait())

    pos_copy = pltpu.make_async_copy(pos_hbm.at[i], pos_s, sem_p)
    pos_copy.start()
    pos_copy.wait()

    def row_copy(t, k):
        return pltpu.make_async_copy(h2_ref.at[t], xs_hbm.at[pos_s[k, t]], sem_r)

    def start_rows(t, c):
        for k in range(TOP_K):
            row_copy(t, k).start(priority=k % 2)
        return c

    def wait_rows(t, c):
        for k in range(TOP_K):
            row_copy(t, k).wait()
        return c

    lax.fori_loop(0, tt, start_rows, 0)
    lax.fori_loop(0, tt, wait_rows, 0)


def _dispatch(h2, pos_tiles, zstart, zcount, n_rows, tt):
    t, c, _ = h2.shape
    e = zstart.shape[0]
    kern = functools.partial(_dispatch_kernel, E=e, tt=tt)
    return pl.pallas_call(
        kern,
        grid_spec=pltpu.PrefetchScalarGridSpec(
            num_scalar_prefetch=2,
            grid=(t // tt,),
            in_specs=[pl.BlockSpec(memory_space=pl.ANY),
                      pl.BlockSpec((tt, c, LANES), lambda i, zs, zv: (i, 0, 0))],
            out_specs=pl.BlockSpec(memory_space=pl.ANY),
            scratch_shapes=[pltpu.VMEM((ZERO_ROWS, c, LANES), U32), pltpu.SMEM((TOP_K, tt), I32),
                            pltpu.SemaphoreType.DMA, pltpu.SemaphoreType.DMA, pltpu.SemaphoreType.DMA]),
        out_shape=jax.ShapeDtypeStruct((n_rows, c, LANES), U32),
        compiler_params=_cparams(("arbitrary",)),
        name="dispatch",
    )(zstart, zcount, pos_tiles, h2)


def _experts_kernel(be_ref, nu_ref, nxt_ref, ord_ref, xs_ref, wg_hbm, wu_hbm, wd_hbm, ys_ref,
                    sg_s, su_s, sd_s, sem, *, E):
    b = pl.program_id(0)
    n_used = nu_ref[0]

    def copies(e, slot):
        out = []
        for m, (w_hbm, st) in enumerate(((wg_hbm, sg_s), (wu_hbm, su_s), (wd_hbm, sd_s))):
            rows = w_hbm.shape[1] // 2
            for p in range(2):
                out.append(pltpu.make_async_copy(w_hbm.at[e, pl.ds(p * rows, rows)],
                                                 st.at[slot, pl.ds(p * rows, rows)], sem.at[slot, 2 * m + p]))
        return out

    def start_all(cps):
        for n, cp in enumerate(cps):
            cp.start(priority=n % 2)

    @pl.when(b < n_used)
    def _():
        e = be_ref[b]
        prev = be_ref[jnp.maximum(b - 1, 0)]
        slot = ord_ref[e] & 1

        @pl.when(b == 0)
        def _():
            start_all(copies(e, slot))

        @pl.when((b == 0) | (e != prev))
        def _():
            for cp in copies(e, slot):
                cp.wait()
            ne = nxt_ref[e]

            @pl.when(ne < E)
            def _():
                start_all(copies(ne, 1 - slot))


        parts = [_unpack_halves(ch) for ch in _load_row_tiled(xs_ref, EXPERT_BLOCK)]
        x_lo = jnp.concatenate([p[0] for p in parts], axis=1).astype(BF16)
        x_hi = jnp.concatenate([p[1] for p in parts], axis=1).astype(BF16)
        half = x_lo.shape[1]
        gate = _mm(x_lo, sg_s[slot, 0:half, :]) + _mm(x_hi, sg_s[slot, half:, :])
        up = _mm(x_lo, su_s[slot, 0:half, :]) + _mm(x_hi, su_s[slot, half:, :])
        a = _silu(gate) * up
        _store_row_tiled(ys_ref, _pack_halves(_mm(a.astype(BF16), sd_s[slot])))


def _experts(xs, block_e, n_used, next_used, used_ord, wg, wu, wd):
    e, d, de = wg.shape
    c = d // 2 // LANES
    nblk = xs.shape[0] // (EXPERT_BLOCK * c)
    kern = functools.partial(_experts_kernel, E=e)

    def live(b, be, nu, nx, od):
        return (jnp.minimum(b, jnp.maximum(nu[0] - 1, 0)), 0)

    hbm = pl.BlockSpec(memory_space=pl.ANY)
    return pl.pallas_call(
        kern,
        grid_spec=pltpu.PrefetchScalarGridSpec(
            num_scalar_prefetch=4,
            grid=(nblk,),
            in_specs=[pl.BlockSpec((EXPERT_BLOCK * c, LANES), live), hbm, hbm, hbm],
            out_specs=pl.BlockSpec((EXPERT_BLOCK * c, LANES), live),
            scratch_shapes=[pltpu.VMEM((2, d, de), F32), pltpu.VMEM((2, d, de), F32), pltpu.VMEM((2, de, d), F32),
                            pltpu.SemaphoreType.DMA((2, 6))]),
        out_shape=jax.ShapeDtypeStruct(xs.shape, U32),
        compiler_params=_cparams(("arbitrary",)),
        name="experts",
    )(block_e, n_used, next_used, used_ord, xs, wg, wu, wd)


def _combine_kernel(pos_hbm, ys_hbm, ew_ref, x1_ref, h2_ref, g2_ref, nw3_ref, wsg_ref, wsu_ref, wsd_ref,
                    out_ref, buf_s, pos_s, sem_p, sem_r, *, tc, nt):
    tile = pl.program_id(0) * nt + pl.program_id(1)
    pos_copy = pltpu.make_async_copy(pos_hbm.at[tile], pos_s, sem_p)
    pos_copy.start()
    pos_copy.wait()

    def row_copy(t, k):
        c = ys_hbm.shape[1]
        return pltpu.make_async_copy(ys_hbm.at[pos_s[k, t]], buf_s.at[k, pl.ds(pl.multiple_of(t * c, c), c)], sem_r)

    def start_rows(t, c):
        for k in range(TOP_K):
            row_copy(t, k).start(priority=k % 2)
        return c

    def wait_rows(t, c):
        for k in range(TOP_K):
            row_copy(t, k).wait()
        return c

    lax.fori_loop(0, tc, start_rows, 0)
    parts = [_unpack_halves(ch) for ch in _load_row_tiled(h2_ref.at[0], tc)]
    h_lo = jnp.concatenate([p[0] for p in parts], axis=1).astype(BF16)
    h_hi = jnp.concatenate([p[1] for p in parts], axis=1).astype(BF16)
    half = h_lo.shape[1]
    nch = half // LANES
    gate = _mm(h_lo, wsg_ref[0:half, :]) + _mm(h_hi, wsg_ref[half:, :])
    up = _mm(h_lo, wsu_ref[0:half, :]) + _mm(h_hi, wsu_ref[half:, :])
    shared = _mm((_silu(gate) * up).astype(BF16), wsd_ref[...])
    lax.fori_loop(0, tc, wait_rows, 0)
    ew = ew_ref[0]
    acc = [shared[:, j * LANES:(j + 1) * LANES] for j in range(2 * nch)]
    for k in range(TOP_K):
        wk = ew[:, k:k + 1]
        for j, ch in enumerate(_load_row_tiled(buf_s.at[k], tc)):
            y_lo, y_hi = _unpack_halves(ch)
            acc[j] = acc[j] + y_lo * wk
            acc[nch + j] = acc[nch + j] + y_hi * wk
    out_ref[0] = x1_ref[0] + g2_ref[0] * _rms(jnp.concatenate(acc, axis=1), nw3_ref[...])


def _combine(pos_tiles, ys, ew, x1, h2, g2, nw3, wsg, wsu, wsd, tc):
    b, s, d = x1.shape
    ds_ = wsg.shape[1]
    nt = s // tc
    c = ys.shape[1]
    kern = functools.partial(_combine_kernel, tc=tc, nt=nt)
    tok = lambda w: pl.BlockSpec((1, tc, w), lambda bi, i: (bi, i, 0))
    return pl.pallas_call(
        kern,
        grid=(b, nt),
        in_specs=[pl.BlockSpec(memory_space=pl.ANY), pl.BlockSpec(memory_space=pl.ANY),
                  tok(TOP_K), tok(d), pl.BlockSpec((1, tc * c, LANES), lambda bi, i: (bi, i, 0)),
                  pl.BlockSpec((1, 1, d), lambda bi, i: (bi, 0, 0)),
                  pl.BlockSpec((1, d), lambda bi, i: (0, 0)),
                  pl.BlockSpec((d, ds_), lambda bi, i: (0, 0)),
                  pl.BlockSpec((d, ds_), lambda bi, i: (0, 0)),
                  pl.BlockSpec((ds_, d), lambda bi, i: (0, 0))],
        out_specs=tok(d),
        out_shape=jax.ShapeDtypeStruct((b, s, d), F32),
        scratch_shapes=[pltpu.VMEM((TOP_K, tc * c, LANES), U32), pltpu.SMEM((TOP_K, tc), I32),
                        pltpu.SemaphoreType.DMA, pltpu.SemaphoreType.DMA],
        compiler_params=_cparams(("arbitrary", "arbitrary")),
        name="combine",
    )(pos_tiles, ys, ew, x1, h2, g2, nw3, wsg, wsu, wsd)


def _tile(n, pref):
    t = min(n, pref)
    if n % t:
        raise NotImplementedError(f"size {n} is not a multiple of tile {t}")
    return t


def kernel(x, c, ctx, c_ctx, w_ada, b_ada, norms, w_in, ml_conv, ml_gate_b, ml_norm_w, attn_sink, w_out,
           w_router, b_router, w_exp_gate, w_exp_up, w_exp_down, w_sh_gate, w_sh_up, w_sh_down):
    if w_ada.shape[0] != 1:
        raise NotImplementedError("single-layer configuration only")
    b, s, d = x.shape
    lc = ctx.shape[1]
    t = b * s
    H = ML_HEADS
    dv = d // 2 // H
    dk = dv // 2
    dh = d // 2 // AT_HEADS
    qk_w, v_w = 2 * H * dk, H * dv
    ng = 4 * H
    aq_w, akv_w = AT_HEADS * dh, AT_KV_HEADS * dh
    e = w_router.shape[-1]
    nw = norms[0]

    rows = -(-(b + 1) // SUBLANES) * SUBLANES
    cond = jnp.zeros((rows, d), F32).at[:b].set(c).at[b].set(c_ctx)
    mods = _ada(cond, w_ada[0], b_ada[0][None, :])
    sh1, sc1, g1, sh2, sc2, g2 = [m[:b, None, :] for m in jnp.split(mods, 6, axis=-1)]
    csh1, csc1 = [jnp.broadcast_to(m[b:b + 1, None, :], (b, 1, d)) for m in jnp.split(mods, 6, axis=-1)[:2]]

    w0 = w_in[0]
    o_ml, o_g, o_q = 0, qk_w + 2 * v_w, qk_w + 2 * v_w + ng
    ml_w = qk_w + 2 * v_w
    wp = jnp.concatenate([w0[:, o_ml:o_g], w0[:, o_q:], w0[:, o_g:o_q], jnp.zeros((d, LANES - ng), F32)],
                         axis=1).astype(BF16)
    layout = (("ml", 0, ml_w), ("aq", ml_w, aq_w), ("ak", ml_w + aq_w, akv_w),
              ("av", ml_w + aq_w + akv_w, akv_w), ("g", ml_w + aq_w + 2 * akv_w, LANES))
    widths = dict(ml=ml_w, aq=aq_w, ak=akv_w, av=akv_w, g=LANES)
    tabs = _rope_tables(s, dh)
    proj = functools.partial(_inproj, layout=layout, widths=widths, dh=dh)
    ml, aq, ak, av, gates = proj(x, sh1, sc1, nw[0:1], wp, tabs, rope=True, tm=_tile(s, 512))
    tm_c = _tile(lc, 512)
    mlc, _, akc, avc, gates_c = proj(ctx, csh1, csc1, nw[0:1], wp, tuple(tb[:tm_c] for tb in tabs),
                                     rope=False, tm=tm_c)

    L = _tile(lc, 256)
    if s % L:
        raise NotImplementedError("sequence must be a multiple of the mLSTM chunk")
    gate_b = jnp.zeros((1, LANES), F32).at[0, :ng].set(ml_gate_b[0])
    gp = _gate_prep(gates, gate_b, L)
    gpc = _gate_prep(gates_c, gate_b, L)
    y_ml = _mlstm(ml, mlc, gp, gpc, ml_conv[0], ml_norm_w[0][None, :], L)

    y_at = _attn(aq, ak, av, akc, avc, attn_sink[0])

    x1, h2, logits = _outproj(y_ml, y_at, x, g1, sh2, sc2, nw[1:2], nw[2:3], w_out[0].astype(BF16),
                              w_router[0], _tile(s, 256))

    eidx, ew, rank, cnt = _route(logits.reshape(t, e), jnp.broadcast_to(b_router[0][:, None], (e, LANES)),
                                 _tile(t, 512))
    counts = cnt[:, 0].astype(I32)
    padded = (counts + EXPERT_BLOCK - 1) // EXPERT_BLOCK * EXPERT_BLOCK
    pad_end = jnp.cumsum(padded)
    pad_start = pad_end - padded
    n_blocks = -(-(t * TOP_K + e * (EXPERT_BLOCK - 1)) // EXPERT_BLOCK)
    n_rows = n_blocks * EXPERT_BLOCK
    blk_row = jnp.arange(n_blocks, dtype=I32) * EXPERT_BLOCK
    block_e = jnp.minimum(jnp.sum((pad_end[None, :] <= blk_row[:, None]).astype(I32), axis=1), e - 1)
    n_used = (pad_end[-1] // EXPERT_BLOCK).astype(I32)[None]
    used = counts > 0
    eid = jnp.arange(e, dtype=I32)
    next_used = jnp.concatenate([lax.cummin(jnp.where(used, eid, e), reverse=True)[1:], jnp.full((1,), e, I32)])
    used_ord = jnp.cumsum(used.astype(I32)) - 1
    pos = _positions(eidx, rank, pad_start, _tile(t, 512))

    tt = _tile(t, 512)
    pos_d = pos.reshape(TOP_K, t // tt, tt).transpose(1, 0, 2)
    rc = d // 2 // LANES
    zcount = (padded - counts + ZERO_ROWS - 1) // ZERO_ROWS
    xs = _dispatch(h2.reshape(t, rc, LANES), pos_d, (pad_end - zcount * ZERO_ROWS).astype(I32), zcount.astype(I32),
                   n_rows, tt)
    ys = _experts(xs.reshape(n_rows * rc, LANES), block_e, n_used, next_used, used_ord,
                  w_exp_gate[0], w_exp_up[0], w_exp_down[0])
    tc = _tile(s, 128)
    pos_c = pos.reshape(TOP_K, t // tc, tc).transpose(1, 0, 2)
    ew_t = ew.T.reshape(b, s, TOP_K)
    return _combine(pos_c, ys.reshape(n_rows, rc, LANES), ew_t, x1, h2, g2, nw[3:4],
                    w_sh_gate[0].astype(BF16), w_sh_up[0].astype(BF16), w_sh_down[0].astype(BF16), tc)
```

```python
import functools

import jax
import jax.numpy as jnp
from jax import lax
from jax.experimental import pallas as pl
from jax.experimental.pallas import tpu as pltpu

F32 = jnp.float32
BF16 = jnp.bfloat16
I32 = jnp.int32
U32 = jnp.uint32

LANES = 128
SUBLANES = 8
VMEM_LIMIT_BYTES = 56 * 1024 * 1024

NORM_EPS = 1e-6
ML_HEADS = 4
CONV_W = 5
AT_HEADS = 8
AT_KV_HEADS = 2
GRID_W = 64
WINDOW = 128
BLOCK_Q = 128
ROPE_THETA = 10000.0
N_GROUPS = 8
TOPK_GROUPS = 4
TOP_K = 8
ROUTE_SCALE = 2.5
EXPERT_BLOCK = 256
ZERO_ROWS = 32
NEG_BIG = -1e30


def _cparams(sem):
    return pltpu.CompilerParams(dimension_semantics=sem, vmem_limit_bytes=VMEM_LIMIT_BYTES)


def _rms(xf, w):
    return xf * lax.rsqrt(jnp.mean(xf * xf, axis=-1, keepdims=True) + NORM_EPS) * w


def _silu(x):
    return x * jax.nn.sigmoid(x)


def _mm(a, b):
    return jnp.dot(a, b, preferred_element_type=F32)


def _pack_halves(x):
    n = x.shape[1] // 2
    u = lax.bitcast_convert_type(x.astype(BF16).astype(F32), U32)
    return (u[:, :n] >> 16) | (u[:, n:] & jnp.uint32(0xFFFF0000))


def _unpack_halves(u):
    return (lax.bitcast_convert_type(u << 16, F32),
            lax.bitcast_convert_type(u & jnp.uint32(0xFFFF0000), F32))


def _store_row_tiled(ref2d, packed):
    r, n = packed.shape
    c = n // LANES
    for j in range(c):
        ref2d[pl.ds(j, r, stride=c), :] = packed[:, j * LANES:(j + 1) * LANES]


def _load_row_tiled(ref2d, r):
    c = ref2d.shape[0] // r
    return [ref2d[pl.ds(j, r, stride=c), :] for j in range(c)]


def _ada_kernel(c_ref, w_ref, b_ref, o_ref):
    s = _silu(c_ref[...]).astype(BF16)
    o_ref[...] = _mm(s, w_ref[...].astype(BF16)) + b_ref[...]


def _ada(cond, w, b):
    r, d = cond.shape
    n = w.shape[1]
    tn = min(n, 1024)
    return pl.pallas_call(
        _ada_kernel,
        grid=(n // tn,),
        in_specs=[pl.BlockSpec((r, d), lambda j: (0, 0)),
                  pl.BlockSpec((d, tn), lambda j: (0, j)),
                  pl.BlockSpec((1, tn), lambda j: (0, j))],
        out_specs=pl.BlockSpec((r, tn), lambda j: (0, j)),
        out_shape=jax.ShapeDtypeStruct((r, n), F32),
        compiler_params=_cparams(("arbitrary",)),
        name="ada",
    )(cond, w, b)


def _inproj_kernel(x_ref, sh_ref, sc_ref, nw_ref, w_ref, rc_ref, ra_ref, rb_ref, *refs,
                   layout, dh, rope, qscale, cw):
    hb_ref = refs[-1]
    h = _rms(x_ref[0], nw_ref[...]) * (1.0 + sc_ref[0]) + sh_ref[0]
    hb_ref[...] = h.astype(BF16)
    for (name, col0, width), o_ref in zip(layout, refs):
        for c in range(0, width, cw):
            step = min(cw, width - c)
            acc = _mm(hb_ref[...], w_ref[:, col0 + c:col0 + c + step])
            if rope and name in ("aq", "ak"):
                for hh in range(step // dh):
                    a = acc[:, hh * dh:(hh + 1) * dh]
                    r = (a * rc_ref[...] + pltpu.roll(a, dh - dh // 4, 1) * ra_ref[...]
                         + pltpu.roll(a, dh // 4, 1) * rb_ref[...])
                    if name == "aq":
                        r = r * qscale
                    o_ref[0, :, c + hh * dh:c + (hh + 1) * dh] = r.astype(o_ref.dtype)
            else:
                if name == "aq":
                    acc = acc * qscale
                o_ref[0, :, c:c + step] = acc.astype(o_ref.dtype)


def _inproj(x, sh, sc, nw, wp, tabs, *, layout, dh, rope, tm):
    b, s, d = x.shape
    npad = wp.shape[1]
    rc, ra, rb = tabs
    kern = functools.partial(_inproj_kernel, layout=layout, dh=dh, rope=rope, qscale=float(dh) ** -0.5, cw=512)
    vec = pl.BlockSpec((1, 1, d), lambda bi, i: (bi, 0, 0))
    tab = pl.BlockSpec((tm, dh), lambda bi, i: (i, 0))
    out_dt = dict(ml=BF16, aq=BF16, ak=BF16, av=BF16, g=F32)
    names = [n for n, _, _ in layout]
    widths = {n: w for n, _, w in layout}
    outs = pl.pallas_call(
        kern,
        grid=(b, s // tm),
        in_specs=[pl.BlockSpec((1, tm, d), lambda bi, i: (bi, i, 0)), vec, vec,
                  pl.BlockSpec((1, d), lambda bi, i: (0, 0)),
                  pl.BlockSpec((d, npad), lambda bi, i: (0, 0), pipeline_mode=pl.Buffered(1)),
                  tab, tab, tab],
        out_specs=[pl.BlockSpec((1, tm, widths[n]), lambda bi, i: (bi, i, 0)) for n in names],
        out_shape=[jax.ShapeDtypeStruct((b, s, widths[n]), out_dt[n]) for n in names],
        scratch_shapes=[pltpu.VMEM((tm, d), BF16)],
        compiler_params=_cparams(("arbitrary", "arbitrary")),
        name="inproj_rope" if rope else "inproj_ctx",
    )(x, sh, sc, nw, wp, rc, ra, rb)
    return dict(zip(names, outs))


def _rope_tables(s, dh):
    rows = s // GRID_W
    row = jnp.repeat(jnp.arange(rows), GRID_W)
    col = jnp.tile(jnp.arange(GRID_W), rows)
    nf = dh // 4
    freqs = ROPE_THETA ** (-jnp.arange(nf, dtype=F32) / nf)
    pos = jnp.stack([row, col], axis=-1).astype(F32)
    ang = pos[:, :, None] * freqs
    cos, sin = jnp.cos(ang), jnp.sin(ang)
    z = jnp.zeros_like(sin[:, 0])
    rc = jnp.concatenate([cos[:, 0], cos[:, 0], cos[:, 1], cos[:, 1]], axis=-1)
    ra = jnp.concatenate([-sin[:, 0], z, -sin[:, 1], z], axis=-1)
    rb = jnp.concatenate([z, sin[:, 0], z, sin[:, 1]], axis=-1)
    return rc, ra, rb


def _gate_prep_kernel(g_ref, b_ref, gl_ref, pp_ref, glt_ref, ppt_ref, tot_ref, *, L, H, nc):
    for c in range(nc):
        z = g_ref[0, c * L:(c + 1) * L, :] + b_ref[...]
        lane = lax.broadcasted_iota(I32, z.shape, 1)
        row = lax.broadcasted_iota(I32, z.shape, 0)
        is_f = ((lane // H) % 2 == 1) & (lane < 4 * H)
        log_sig = jnp.minimum(z, 0.0) - jnp.log1p(jnp.exp(-jnp.abs(z)))
        gl = jnp.where(lane < 4 * H, jnp.where(is_f, log_sig, z), 0.0)
        cs = gl
        k = 1
        while k < L:
            cs = cs + jnp.where(row >= k, pltpu.roll(cs, k, 0), 0.0)
            k *= 2
        tot = cs[L - 1:L, :]
        suf = tot - cs + gl
        pp = jnp.where(lane >= 2 * H, suf, cs)
        gl_ref[0, c] = gl
        pp_ref[0, c] = pp
        glt_ref[0, c] = gl.T
        ppt_ref[0, c] = pp.T
        tot_ref[0, c] = jnp.broadcast_to(tot, (SUBLANES, LANES))


def _gate_prep(g, gate_b, L):
    b, s, _ = g.shape
    nc = s // L
    kern = functools.partial(_gate_prep_kernel, L=L, H=ML_HEADS, nc=nc)
    col = pl.BlockSpec((1, nc, L, LANES), lambda bi: (bi, 0, 0, 0))
    rowb = pl.BlockSpec((1, nc, LANES, L), lambda bi: (bi, 0, 0, 0))
    return pl.pallas_call(
        kern,
        grid=(b,),
        in_specs=[pl.BlockSpec((1, s, LANES), lambda bi: (bi, 0, 0)),
                  pl.BlockSpec((1, LANES), lambda bi: (0, 0))],
        out_specs=[col, col, rowb, rowb, pl.BlockSpec((1, nc, SUBLANES, LANES), lambda bi: (bi, 0, 0, 0))],
        out_shape=[jax.ShapeDtypeStruct((b, nc, L, LANES), F32), jax.ShapeDtypeStruct((b, nc, L, LANES), F32),
                   jax.ShapeDtypeStruct((b, nc, LANES, L), F32), jax.ShapeDtypeStruct((b, nc, LANES, L), F32),
                   jax.ShapeDtypeStruct((b, nc, SUBLANES, LANES), F32)],
        compiler_params=_cparams(("arbitrary",)),
        name="gate_prep",
    )(g, gate_b)


CONV_PAD = 8


def _lane_pick(tile, ch):
    lane = lax.broadcasted_iota(I32, tile.shape, 1)
    return jnp.sum(jnp.where(lane == ch, tile, 0.0), axis=1, keepdims=True)


def _sublane_pick(tile, ch):
    sub = lax.broadcasted_iota(I32, tile.shape, 0)
    return jnp.sum(jnp.where(sub == ch, tile, 0.0), axis=0, keepdims=True)


def _mlstm_chunk(d, ch_i, ch_f, q, k, v, gl, pp, glt, ppt, tot8, ct_ref, n_ref, m_ref, mask_ref=None):
    L = k.shape[0]
    p_col = _lane_pick(pp, ch_f)
    li_col = _lane_pick(gl, ch_i)
    btot = _lane_pick(tot8[0:1], ch_f)
    m_old = m_ref[d][0:1, 0:1]
    ct = ct_ref[d]
    n_row = n_ref[d]
    h = None
    if q is not None:
        p_row = _sublane_pick(ppt, ch_f)
        li_row = _sublane_pick(glt, ch_i)
        logw = (p_col - p_row + li_row) + mask_ref[d]
        log_inter = p_col + m_old
        m_q = jnp.maximum(log_inter, jnp.max(logw, axis=1, keepdims=True))
        sqk = lax.dot_general(q, k, (((1,), (1,)), ((), ())), preferred_element_type=F32) * jnp.exp(logw - m_q)
        w_inter = jnp.exp(log_inter - m_q)
        num = _mm(sqk.astype(BF16), v) + w_inter * _mm(q, ct.astype(BF16))
        den = (jnp.sum(sqk, axis=1, keepdims=True)
               + w_inter * jnp.sum(q.astype(F32) * n_row, axis=1, keepdims=True))
        h = num / jnp.maximum(jnp.abs(den), jnp.exp(-m_q))
    log_w_end = btot - p_col + li_col
    m_new = jnp.maximum(btot + m_old, jnp.max(log_w_end, axis=0, keepdims=True))
    w_end = jnp.exp(log_w_end - m_new)
    decay = jnp.exp(btot + m_old - m_new)
    kw = k.astype(F32) * w_end
    ct_ref[d] = decay * ct + lax.dot_general(kw.astype(BF16), v, (((0,), (0,)), ((), ())),
                                             preferred_element_type=F32)
    n_ref[d] = decay * n_row + jnp.sum(kw, axis=0, keepdims=True)
    m_ref[d] = jnp.broadcast_to(m_new, (SUBLANES, LANES))
    return h


def _mlstm_kernel(q_ref, k_ref, v_ref, o_ref, kc_ref, vc_ref,
                  gl_ref, pp_ref, glt_ref, ppt_ref, tot_ref, glc_ref, ppc_ref, totc_ref,
                  cwq_ref, cwk_ref, nw_ref, mask_ref, y_ref,
                  xq_s, xk_s, xkc_s, qs_s, ks_s, kcs_s, hf_s, hb_s, ct_s, n_s, m_s,
                  *, L, H, S, Lc, dk, dv):
    hh = pl.program_id(1)
    nc, ncc = S // L, Lc // L
    zpad = jnp.zeros((CONV_PAD, dk), F32)

    def conv_silu(x_s, cw_ref, n_rows, out_s, scale):
        for c in range(n_rows // L):
            acc = jnp.zeros((L, dk), F32)
            for j in range(CONV_W):
                acc = acc + cw_ref[j:j + 1, :] * x_s[pl.ds(CONV_PAD + c * L + j - CONV_W // 2, L), :]
            out_s[c] = (_silu(acc) * scale).astype(BF16)

    def stage(x_s, src, n_rows):
        x_s[0:CONV_PAD, :] = zpad
        x_s[CONV_PAD + n_rows:2 * CONV_PAD + n_rows, :] = zpad
        x_s[CONV_PAD:CONV_PAD + n_rows, :] = src.astype(F32)

    stage(xq_s, q_ref[0], S)
    stage(xk_s, k_ref[0], S)
    stage(xkc_s, kc_ref[0], Lc)
    conv_silu(xq_s, cwq_ref, S, qs_s, float(dk) ** -0.5)
    conv_silu(xk_s, cwk_ref, S, ks_s, 1.0)
    conv_silu(xkc_s, cwk_ref, Lc, kcs_s, 1.0)

    ct_s[...] = jnp.zeros(ct_s.shape, F32)
    n_s[...] = jnp.zeros(n_s.shape, F32)
    m_s[...] = jnp.zeros(m_s.shape, F32)

    def chans(d):
        return d * 2 * H + hh, d * 2 * H + H + hh

    for d in (0, 1):
        ch_i, ch_f = chans(d)
        for c in (range(ncc) if d == 0 else range(ncc - 1, -1, -1)):
            _mlstm_chunk(d, ch_i, ch_f, None, kcs_s[c], vc_ref[0, c * L:(c + 1) * L, :],
                         glc_ref[0, c], ppc_ref[0, c], None, None, totc_ref[0, c], ct_s, n_s, m_s)

    def body(i, carry):
        for d in (0, 1):
            ch_i, ch_f = chans(d)
            c = i if d == 0 else nc - 1 - i
            r0 = pl.multiple_of(c * L, L)
            h = _mlstm_chunk(d, ch_i, ch_f, qs_s[c], ks_s[c], v_ref[0, pl.ds(r0, L), :],
                             gl_ref[0, c], pp_ref[0, c], glt_ref[0, c], ppt_ref[0, c], tot_ref[0, c],
                             ct_s, n_s, m_s, mask_ref)
            if d == 0:
                hf_s[c] = h
            else:
                hb_s[c] = h
        return carry

    lax.fori_loop(0, nc, body, 0)

    for c in range(nc):
        hs = hf_s[c] + hb_s[c]
        hn = _rms(hs, nw_ref[...])
        y_ref[0, c * L:(c + 1) * L, :] = (hn * jax.nn.sigmoid(o_ref[0, c * L:(c + 1) * L, :].astype(F32))).astype(BF16)


def _mlstm(ml, mlc, gp, gpc, conv_w, norm_w, L):
    b, s, _ = ml.shape
    lc = mlc.shape[1]
    H = ML_HEADS
    dv = norm_w.shape[1] // H
    dk = dv // 2
    nc, ncc = s // L, lc // L
    gl, pp, glt, ppt, tot = gp
    glc, ppc, _, _, totc = gpc
    kern = functools.partial(_mlstm_kernel, L=L, H=H, S=s, Lc=lc, dk=dk, dv=dv)
    tri = jnp.arange(L)[None, :] <= jnp.arange(L)[:, None]
    masks = jnp.where(jnp.stack([tri, tri.T]), 0.0, NEG_BIG).astype(F32)

    def colspec(rows, w, off):
        return pl.BlockSpec((1, rows, w), lambda bi, h: (bi, 0, off + h))

    def gspec(n, r, c):
        return pl.BlockSpec((1, n, r, c), lambda bi, h: (bi, 0, 0, 0))

    return pl.pallas_call(
        kern,
        grid=(b, H),
        in_specs=[colspec(s, dk, 0), colspec(s, dk, H), colspec(s, dv, H), colspec(s, dv, 2 * H),
                  colspec(lc, dk, H), colspec(lc, dv, H),
                  gspec(nc, L, LANES), gspec(nc, L, LANES), gspec(nc, LANES, L), gspec(nc, LANES, L),
                  gspec(nc, SUBLANES, LANES),
                  gspec(ncc, L, LANES), gspec(ncc, L, LANES), gspec(ncc, SUBLANES, LANES),
                  pl.BlockSpec((CONV_W, dk), lambda bi, h: (0, h)),
                  pl.BlockSpec((CONV_W, dk), lambda bi, h: (0, H + h)),
                  pl.BlockSpec((1, dv), lambda bi, h: (0, h)),
                  pl.BlockSpec((2, L, L), lambda bi, h: (0, 0, 0))],
        out_specs=pl.BlockSpec((1, s, dv), lambda bi, h: (bi, 0, h)),
        out_shape=jax.ShapeDtypeStruct((b, s, H * dv), BF16),
        scratch_shapes=[pltpu.VMEM((s + 2 * CONV_PAD, dk), F32), pltpu.VMEM((s + 2 * CONV_PAD, dk), F32),
                        pltpu.VMEM((lc + 2 * CONV_PAD, dk), F32),
                        pltpu.VMEM((nc, L, dk), BF16), pltpu.VMEM((nc, L, dk), BF16), pltpu.VMEM((ncc, L, dk), BF16),
                        pltpu.VMEM((nc, L, dv), F32), pltpu.VMEM((nc, L, dv), F32),
                        pltpu.VMEM((2, dk, dv), F32), pltpu.VMEM((2, 1, dk), F32),
                        pltpu.VMEM((2, SUBLANES, LANES), F32)],
        compiler_params=_cparams(("arbitrary", "arbitrary")),
        name="mlstm",
    )(ml, ml, ml, ml, mlc, mlc, gl, pp, glt, ppt, tot, glc, ppc, totc, conv_w, conv_w, norm_w, masks)


def _attn_kernel(sink_ref, q_ref, kp_ref, kc_ref, kn_ref, vp_ref, vc_ref, vn_ref, kx_ref, vx_ref, bias_ref,
                 o_ref, *, bq, dh, G):
    q = q_ref[0]
    for g in range(AT_KV_HEADS):
        sl = slice(g * dh, (g + 1) * dh)
        qg = jnp.concatenate([q[:, (g * G + j) * dh:(g * G + j + 1) * dh] for j in range(G)], axis=0)
        kcat = jnp.concatenate([kp_ref[0][:, sl], kc_ref[0][:, sl], kn_ref[0][:, sl], kx_ref[0][:, sl]], axis=0)
        vcat = jnp.concatenate([vp_ref[0][:, sl], vc_ref[0][:, sl], vn_ref[0][:, sl], vx_ref[0][:, sl]], axis=0)
        s = lax.dot_general(qg, kcat, (((1,), (1,)), ((), ())), preferred_element_type=F32) + bias_ref[0]
        ri = lax.broadcasted_iota(I32, (G * bq, 1), 0)
        sk = jnp.zeros((G * bq, 1), F32)
        for j in range(G):
            sk = jnp.where((ri >= j * bq) & (ri < (j + 1) * bq), sink_ref[g * G + j], sk)
        m = jnp.maximum(jnp.max(s, axis=1, keepdims=True), sk)
        p = jnp.exp(s - m)
        den = jnp.sum(p, axis=1, keepdims=True) + jnp.exp(sk - m)
        o = _mm(p.astype(BF16), vcat) / den
        for j in range(G):
            o_ref[0, :, (g * G + j) * dh:(g * G + j + 1) * dh] = o[j * bq:(j + 1) * bq, :].astype(BF16)


def _attn_bias(bq, lc, G):
    qoff = (jnp.arange(G * bq) % bq)[:, None]
    ci = jnp.arange(3 * bq + lc)[None, :]
    prev = (ci < bq) & (qoff <= ci)
    cur = (ci >= bq) & (ci < 2 * bq)
    nxt = (ci >= 2 * bq) & (ci < 3 * bq) & (ci - 2 * bq <= qoff)
    ctx = ci >= 3 * bq
    inner = prev | cur | nxt | ctx
    first = cur | nxt | ctx
    last = prev | cur | ctx
    if WINDOW != bq:
        raise NotImplementedError("window must equal the query block")
    return jnp.where(jnp.stack([inner, first, last]), 0.0, NEG_BIG).astype(F32)


def _attn(aq, ak, av, akc, avc, sink):
    b, s, hd = aq.shape
    dh = hd // AT_HEADS
    G = AT_HEADS // AT_KV_HEADS
    bq = BLOCK_Q
    nb = s // bq
    lc = akc.shape[1]
    kvw = AT_KV_HEADS * dh
    bias = _attn_bias(bq, lc, G)
    if nb < 2:
        raise NotImplementedError("needs at least two query blocks")
    kern = functools.partial(_attn_kernel, bq=bq, dh=dh, G=G)
    prev = pl.BlockSpec((1, bq, kvw), lambda bi, n: (bi, jnp.maximum(n - 1, 0), 0))
    cur = pl.BlockSpec((1, bq, kvw), lambda bi, n: (bi, n, 0))
    nxt = pl.BlockSpec((1, bq, kvw), lambda bi, n: (bi, jnp.minimum(n + 1, nb - 1), 0))
    cx = pl.BlockSpec((1, lc, kvw), lambda bi, n: (bi, 0, 0))
    return pl.pallas_call(
        kern,
        grid=(b, nb),
        in_specs=[pl.BlockSpec(memory_space=pltpu.SMEM),
                  pl.BlockSpec((1, bq, hd), lambda bi, n: (bi, n, 0)),
                  prev, cur, nxt, prev, cur, nxt, cx, cx,
                  pl.BlockSpec((1, G * bq, 3 * bq + lc),
                               lambda bi, n: (jnp.where(n == 0, 1, jnp.where(n == nb - 1, 2, 0)), 0, 0))],
        out_specs=pl.BlockSpec((1, bq, hd), lambda bi, n: (bi, n, 0)),
        out_shape=jax.ShapeDtypeStruct((b, s, hd), BF16),
        compiler_params=_cparams(("arbitrary", "arbitrary")),
        name="attn",
    )(sink, aq, ak, ak, ak, av, av, av, akc, avc, bias)


def _outproj_kernel(ym_ref, ya_ref, x_ref, g1_ref, sh2_ref, sc2_ref, nw1_ref, nw2_ref, wo_ref, wrh_ref, wrl_ref,
                    x1_ref, h2_ref, lg_ref, y_s, *, dm, cw):
    i = pl.program_id(0)
    d = y_s.shape[2]

    @pl.when(i == 0)
    def _():
        y_s[...] = jnp.zeros(y_s.shape, F32)

    cur = i & 1
    x1 = x_ref[0] + g1_ref[0] * _rms(y_s[1 - cur], nw1_ref[...])
    x1_ref[0] = x1
    h2 = _rms(x1, nw2_ref[...]) * (1.0 + sc2_ref[0]) + sh2_ref[0]
    hi = h2.astype(BF16)
    _store_row_tiled(h2_ref.at[0], _pack_halves(h2))
    lo = (h2 - hi.astype(F32)).astype(BF16)
    lg_ref[0] = _mm(hi, wrh_ref[...]) + (_mm(hi, wrl_ref[...]) + _mm(lo, wrh_ref[...]))
    for c in range(0, d, cw):
        y_s[cur, :, c:c + cw] = _mm(ym_ref[0], wo_ref[0:dm, c:c + cw]) + _mm(ya_ref[0], wo_ref[dm:, c:c + cw])


def _outproj(y_ml, y_at, x, g1, sh2, sc2, nw1, nw2, wo, wr, tm):
    b, s, d = x.shape
    dm = y_ml.shape[2]
    e = wr.shape[1]
    wr_hi = wr.astype(BF16)
    wr_lo = (wr - wr_hi.astype(F32)).astype(BF16)
    kern = functools.partial(_outproj_kernel, dm=dm, cw=min(d, 512))
    nt = s // tm
    n = b * nt

    def ahead(i):
        a = jnp.minimum(i, n - 1)
        return a // nt, a % nt

    def behind(i):
        a = jnp.maximum(i - 1, 0)
        return a // nt, a % nt

    def tok(w, which):
        return pl.BlockSpec((1, tm, w), lambda i: (*which(i), 0))

    vec = pl.BlockSpec((1, 1, d), lambda i: (behind(i)[0], 0, 0))
    nspec = pl.BlockSpec((1, d), lambda i: (0, 0))
    const = lambda r, c: pl.BlockSpec((r, c), lambda i: (0, 0), pipeline_mode=pl.Buffered(1))
    rc = d // 2 // LANES
    return pl.pallas_call(
        kern,
        grid=(n + 1,),
        in_specs=[tok(dm, ahead), tok(d - dm, ahead), tok(d, behind),
                  vec, vec, vec, nspec, nspec, const(d, d), const(d, e), const(d, e)],
        out_specs=[tok(d, behind), pl.BlockSpec((1, tm * rc, LANES), lambda i: (*behind(i), 0)), tok(e, behind)],
        out_shape=[jax.ShapeDtypeStruct((b, s, d), F32), jax.ShapeDtypeStruct((b, s * rc, LANES), U32),
                   jax.ShapeDtypeStruct((b, s, e), F32)],
        scratch_shapes=[pltpu.VMEM((2, tm, d), F32)],
        compiler_params=_cparams(("arbitrary",)),
        name="outproj",
    )(y_ml, y_at, x, g1, sh2, sc2, nw1, nw2, wo, wr_hi, wr_lo)


def _route_kernel(lg_ref, br_ref, eidx_ref, ew_ref, rank_ref, cnt_ref, carry_s, *, E, tr):
    i = pl.program_id(0)

    @pl.when(i == 0)
    def _():
        carry_s[...] = jnp.zeros(carry_s.shape, F32)

    scores = jax.nn.sigmoid(lg_ref[...].T)
    biased = scores + br_ref[:, 0:1]
    row = lax.broadcasted_iota(I32, (E, tr), 0).astype(F32)
    gs = E // N_GROUPS
    ninf = -jnp.inf
    grp = []
    for g in range(N_GROUPS):
        xg = biased[g * gs:(g + 1) * gs]
        rg = (lax.broadcasted_iota(I32, (gs, tr), 0) + g * gs).astype(F32)
        m1 = jnp.max(xg, axis=0, keepdims=True)
        i1 = jnp.min(jnp.where(xg == m1, rg, float(E)), axis=0, keepdims=True)
        m2 = jnp.max(jnp.where(rg == i1, ninf, xg), axis=0, keepdims=True)
        grp.append(m1 + m2)
    gsc = jnp.concatenate(grp, axis=0)
    gi = lax.broadcasted_iota(I32, (N_GROUPS, tr), 0)
    beaten = jnp.zeros((N_GROUPS, tr), I32)
    for g2 in range(N_GROUPS):
        sg = gsc[g2:g2 + 1]
        beaten = beaten + jnp.where((sg > gsc) | ((sg == gsc) & (gi > g2)), 1, 0)
    keep_g = jnp.where(beaten < TOPK_GROUPS, 1.0, 0.0)
    keep = jnp.concatenate([jnp.broadcast_to(keep_g[g:g + 1], (gs, tr)) for g in range(N_GROUPS)], axis=0)
    masked = jnp.where(keep > 0.5, biased, ninf)
    idxs, ws, sels = [], [], []
    for _ in range(TOP_K):
        m = jnp.max(masked, axis=0, keepdims=True)
        ik = jnp.min(jnp.where(masked == m, row, float(E)), axis=0, keepdims=True)
        sel = row == ik
        ws.append(jnp.sum(jnp.where(sel, scores, 0.0), axis=0, keepdims=True))
        idxs.append(ik)
        sels.append(sel)
        masked = jnp.where(sel, ninf, masked)
    w = jnp.concatenate(ws, axis=0)
    ew_ref[...] = w / jnp.sum(w, axis=0, keepdims=True) * ROUTE_SCALE
    eidx_ref[...] = jnp.concatenate(idxs, axis=0).astype(I32)

    assign = jnp.zeros((E, tr), F32)
    for sel in sels:
        assign = assign + jnp.where(sel, 1.0, 0.0)
    upper = jnp.where(lax.broadcasted_iota(I32, (tr, tr), 0) < lax.broadcasted_iota(I32, (tr, tr), 1), 1.0, 0.0)
    base = _mm(assign.astype(BF16), upper.astype(BF16)) + carry_s[:, 0:1]
    ranks = [jnp.sum(jnp.where(sel, base, 0.0), axis=0, keepdims=True) for sel in sels]
    rank_ref[...] = jnp.concatenate(ranks, axis=0).astype(I32)
    carry_s[...] = carry_s[...] + jnp.sum(assign, axis=1, keepdims=True)
    cnt_ref[...] = carry_s[...]


def _route(logits, b_router, tr):
    t, e = logits.shape
    kern = functools.partial(_route_kernel, E=e, tr=tr)
    kt = pl.BlockSpec((TOP_K, tr), lambda i: (0, i))
    return pl.pallas_call(
        kern,
        grid=(t // tr,),
        in_specs=[pl.BlockSpec((tr, e), lambda i: (i, 0)), pl.BlockSpec((e, LANES), lambda i: (0, 0))],
        out_specs=[kt, kt, kt, pl.BlockSpec((e, LANES), lambda i: (0, 0))],
        out_shape=[jax.ShapeDtypeStruct((TOP_K, t), I32), jax.ShapeDtypeStruct((TOP_K, t), F32),
                   jax.ShapeDtypeStruct((TOP_K, t), I32), jax.ShapeDtypeStruct((e, LANES), F32)],
        scratch_shapes=[pltpu.VMEM((e, LANES), F32)],
        compiler_params=_cparams(("arbitrary",)),
        name="route",
    )(logits, b_router)


def _positions_kernel(eidx_ref, rank_ref, ps_ref, pos_ref, *, E, tr):
    row = lax.broadcasted_iota(I32, (E, tr), 0)
    start = ps_ref[:, 0:1]
    rows = [jnp.sum(jnp.where(row == eidx_ref[k:k + 1, :], start, 0.0), axis=0, keepdims=True)
            for k in range(TOP_K)]
    pos_ref[...] = jnp.concatenate(rows, axis=0).astype(I32) + rank_ref[...]


def _positions(eidx, rank, pad_start, tr):
    t = eidx.shape[1]
    e = pad_start.shape[0]
    kern = functools.partial(_positions_kernel, E=e, tr=tr)
    kt = pl.BlockSpec((TOP_K, tr), lambda i: (0, i))
    return pl.pallas_call(
        kern,
        grid=(t // tr,),
        in_specs=[kt, kt, pl.BlockSpec((e, LANES), lambda i: (0, 0))],
        out_specs=kt,
        out_shape=jax.ShapeDtypeStruct((TOP_K, t), I32),
        compiler_params=_cparams(("arbitrary",)),
        name="positions",
    )(eidx, rank, jnp.broadcast_to(pad_start.astype(F32)[:, None], (e, LANES)))


def _dispatch_kernel(zstart_ref, zcount_ref, pos_hbm, h2_ref, xs_hbm, zero_s, pos_s, sem_p, sem_z, sem_r,
                     *, E, tt):
    i = pl.program_id(0)

    def zero_copy(e, j):
        return pltpu.make_async_copy(zero_s, xs_hbm.at[pl.ds(zstart_ref[e] + j * ZERO_ROWS, ZERO_ROWS)], sem_z)

    @pl.when(i == 0)
    def _():
        zero_s[...] = jnp.zeros(zero_s.shape, U32)

        def each(fn):
            def per_expert(e, c):
                def per_piece(j, c2):
                    fn(zero_copy(e, j))
                    return c2
                return lax.fori_loop(0, zcount_ref[e], per_piece, c)
            lax.fori_loop(0, E, per_expert, 0)

        each(lambda cp: cp.start())
        each(lambda cp: cp.wait())

    pos_copy = pltpu.make_async_copy(pos_hbm.at[i], pos_s, sem_p)
    pos_copy.start()
    pos_copy.wait()

    def row_copy(t, k):
        return pltpu.make_async_copy(h2_ref.at[t], xs_hbm.at[pos_s[k, t]], sem_r)

    def start_rows(t, c):
        for k in range(TOP_K):
            row_copy(t, k).start(priority=k % 2)
        return c

    def wait_rows(t, c):
        for k in range(TOP_K):
            row_copy(t, k).wait()
        return c

    lax.fori_loop(0, tt, start_rows, 0)
    lax.fori_loop(0, tt, wait_rows, 0)


def _dispatch(h2, pos_tiles, zstart, zcount, n_rows, tt):
    t, c, _ = h2.shape
    e = zstart.shape[0]
    kern = functools.partial(_dispatch_kernel, E=e, tt=tt)
    return pl.pallas_call(
        kern,
        grid_spec=pltpu.PrefetchScalarGridSpec(
            num_scalar_prefetch=2,
            grid=(t // tt,),
            in_specs=[pl.BlockSpec(memory_space=pl.ANY),
                      pl.BlockSpec((tt, c, LANES), lambda i, zs, zv: (i, 0, 0))],
            out_specs=pl.BlockSpec(memory_space=pl.ANY),
            scratch_shapes=[pltpu.VMEM((ZERO_ROWS, c, LANES), U32), pltpu.SMEM((TOP_K, tt), I32),
                            pltpu.SemaphoreType.DMA, pltpu.SemaphoreType.DMA, pltpu.SemaphoreType.DMA]),
        out_shape=jax.ShapeDtypeStruct((n_rows, c, LANES), U32),
        compiler_params=_cparams(("arbitrary",)),
        name="dispatch",
    )(zstart, zcount, pos_tiles, h2)


def _experts_kernel(be_ref, nu_ref, nxt_ref, ord_ref, xs_ref, wg_hbm, wu_hbm, wd_hbm, ys_ref,
                    sg_s, su_s, sd_s, sem, *, E):
    b = pl.program_id(0)
    n_used = nu_ref[0]

    def copies(e, slot):
        out = []
        for m, (w_hbm, st) in enumerate(((wg_hbm, sg_s), (wu_hbm, su_s), (wd_hbm, sd_s))):
            rows = w_hbm.shape[1] // 2
            for p in range(2):
                out.append(pltpu.make_async_copy(w_hbm.at[e, pl.ds(p * rows, rows)],
                                                 st.at[slot, pl.ds(p * rows, rows)], sem.at[slot, 2 * m + p]))
        return out

    def start_all(cps):
        for n, cp in enumerate(cps):
            cp.start(priority=n % 2)

    @pl.when(b < n_used)
    def _():
        e = be_ref[b]
        prev = be_ref[jnp.maximum(b - 1, 0)]
        slot = ord_ref[e] & 1

        @pl.when(b == 0)
        def _():
            start_all(copies(e, slot))

        @pl.when((b == 0) | (e != prev))
        def _():
            for cp in copies(e, slot):
                cp.wait()
            ne = nxt_ref[e]

            @pl.when(ne < E)
            def _():
                start_all(copies(ne, 1 - slot))


        parts = [_unpack_halves(ch) for ch in _load_row_tiled(xs_ref, EXPERT_BLOCK)]
        x_lo = jnp.concatenate([p[0] for p in parts], axis=1).astype(BF16)
        x_hi = jnp.concatenate([p[1] for p in parts], axis=1).astype(BF16)
        half = x_lo.shape[1]
        gate = _mm(x_lo, sg_s[slot, 0:half, :]) + _mm(x_hi, sg_s[slot, half:, :])
        up = _mm(x_lo, su_s[slot, 0:half, :]) + _mm(x_hi, su_s[slot, half:, :])
        a = _silu(gate) * up
        _store_row_tiled(ys_ref, _pack_halves(_mm(a.astype(BF16), sd_s[slot])))


def _experts(xs, block_e, n_used, next_used, used_ord, wg, wu, wd):
    e, d, de = wg.shape
    c = d // 2 // LANES
    nblk = xs.shape[0] // (EXPERT_BLOCK * c)
    kern = functools.partial(_experts_kernel, E=e)

    def live(b, be, nu, nx, od):
        return (jnp.minimum(b, jnp.maximum(nu[0] - 1, 0)), 0)

    hbm = pl.BlockSpec(memory_space=pl.ANY)
    return pl.pallas_call(
        kern,
        grid_spec=pltpu.PrefetchScalarGridSpec(
            num_scalar_prefetch=4,
            grid=(nblk,),
            in_specs=[pl.BlockSpec((EXPERT_BLOCK * c, LANES), live), hbm, hbm, hbm],
            out_specs=pl.BlockSpec((EXPERT_BLOCK * c, LANES), live),
            scratch_shapes=[pltpu.VMEM((2, d, de), F32), pltpu.VMEM((2, d, de), F32), pltpu.VMEM((2, de, d), F32),
                            pltpu.SemaphoreType.DMA((2, 6))]),
        out_shape=jax.ShapeDtypeStruct(xs.shape, U32),
        compiler_params=_cparams(("arbitrary",)),
        name="experts",
    )(block_e, n_used, next_used, used_ord, xs, wg, wu, wd)


def _combine_kernel(pos_hbm, ys_hbm, ew_ref, x1_ref, h2_ref, g2_ref, nw3_ref, wsg_ref, wsu_ref, wsd_ref,
                    out_ref, buf_s, pos_s, sem_p, sem_r, *, tc, nt):
    tile = pl.program_id(0) * nt + pl.program_id(1)
    pos_copy = pltpu.make_async_copy(pos_hbm.at[tile], pos_s, sem_p)
    pos_copy.start()
    pos_copy.wait()

    def row_copy(t, k):
        c = ys_hbm.shape[1]
        return pltpu.make_async_copy(ys_hbm.at[pos_s[k, t]], buf_s.at[k, pl.ds(pl.multiple_of(t * c, c), c)], sem_r)

    def start_rows(t, c):
        for k in range(TOP_K):
            row_copy(t, k).start(priority=k % 2)
        return c

    def wait_rows(t, c):
        for k in range(TOP_K):
            row_copy(t, k).wait()
        return c

    lax.fori_loop(0, tc, start_rows, 0)
    parts = [_unpack_halves(ch) for ch in _load_row_tiled(h2_ref.at[0], tc)]
    h_lo = jnp.concatenate([p[0] for p in parts], axis=1).astype(BF16)
    h_hi = jnp.concatenate([p[1] for p in parts], axis=1).astype(BF16)
    half = h_lo.shape[1]
    nch = half // LANES
    gate = _mm(h_lo, wsg_ref[0:half, :]) + _mm(h_hi, wsg_ref[half:, :])
    up = _mm(h_lo, wsu_ref[0:half, :]) + _mm(h_hi, wsu_ref[half:, :])
    shared = _mm((_silu(gate) * up).astype(BF16), wsd_ref[...])
    lax.fori_loop(0, tc, wait_rows, 0)
    ew = ew_ref[0]
    acc = [shared[:, j * LANES:(j + 1) * LANES] for j in range(2 * nch)]
    for k in range(TOP_K):
        wk = ew[:, k:k + 1]
        for j, ch in enumerate(_load_row_tiled(buf_s.at[k], tc)):
            y_lo, y_hi = _unpack_halves(ch)
            acc[j] = acc[j] + y_lo * wk
            acc[nch + j] = acc[nch + j] + y_hi * wk
    out_ref[0] = x1_ref[0] + g2_ref[0] * _rms(jnp.concatenate(acc, axis=1), nw3_ref[...])


def _combine(pos_tiles, ys, ew, x1, h2, g2, nw3, wsg, wsu, wsd, tc):
    b, s, d = x1.shape
    ds_ = wsg.shape[1]
    nt = s // tc
    c = ys.shape[1]
    kern = functools.partial(_combine_kernel, tc=tc, nt=nt)
    tok = lambda w: pl.BlockSpec((1, tc, w), lambda bi, i: (bi, i, 0))
    return pl.pallas_call(
        kern,
        grid=(b, nt),
        in_specs=[pl.BlockSpec(memory_space=pl.ANY), pl.BlockSpec(memory_space=pl.ANY),
                  tok(TOP_K), tok(d), pl.BlockSpec((1, tc * c, LANES), lambda bi, i: (bi, i, 0)),
                  pl.BlockSpec((1, 1, d), lambda bi, i: (bi, 0, 0)),
                  pl.BlockSpec((1, d), lambda bi, i: (0, 0)),
                  pl.BlockSpec((d, ds_), lambda bi, i: (0, 0)),
                  pl.BlockSpec((d, ds_), lambda bi, i: (0, 0)),
                  pl.BlockSpec((ds_, d), lambda bi, i: (0, 0))],
        out_specs=tok(d),
        out_shape=jax.ShapeDtypeStruct((b, s, d), F32),
        scratch_shapes=[pltpu.VMEM((TOP_K, tc * c, LANES), U32), pltpu.SMEM((TOP_K, tc), I32),
                        pltpu.SemaphoreType.DMA, pltpu.SemaphoreType.DMA],
        compiler_params=_cparams(("arbitrary", "arbitrary")),
        name="combine",
    )(pos_tiles, ys, ew, x1, h2, g2, nw3, wsg, wsu, wsd)


def _tile(n, pref):
    t = min(n, pref)
    if n % t:
        raise NotImplementedError(f"size {n} is not a multiple of tile {t}")
    return t


def kernel(x, c, ctx, c_ctx, w_ada, b_ada, norms, w_in, ml_conv, ml_gate_b, ml_norm_w, attn_sink, w_out,
           w_router, b_router, w_exp_gate, w_exp_up, w_exp_down, w_sh_gate, w_sh_up, w_sh_down):
    if w_ada.shape[0] != 1:
        raise NotImplementedError("single-layer configuration only")
    b, s, d = x.shape
    lc = ctx.shape[1]
    t = b * s
    H = ML_HEADS
    dv = d // 2 // H
    dk = dv // 2
    dh = d // 2 // AT_HEADS
    qk_w, v_w = 2 * H * dk, H * dv
    ng = 4 * H
    aq_w, akv_w = AT_HEADS * dh, AT_KV_HEADS * dh
    e = w_router.shape[-1]
    nw = norms[0]

    rows = -(-(b + 1) // SUBLANES) * SUBLANES
    cond = jnp.zeros((rows, d), F32).at[:b].set(c).at[b].set(c_ctx)
    mods = _ada(cond, w_ada[0], b_ada[0][None, :])
    sh1, sc1, g1, sh2, sc2, g2 = [m[:b, None, :] for m in jnp.split(mods, 6, axis=-1)]
    csh1, csc1 = [jnp.broadcast_to(m[b:b + 1, None, :], (b, 1, d)) for m in jnp.split(mods, 6, axis=-1)[:2]]

    w0 = w_in[0]
    o_ml, o_g, o_q = 0, qk_w + 2 * v_w, qk_w + 2 * v_w + ng
    ml_w = qk_w + 2 * v_w
    wp = jnp.concatenate([w0[:, o_ml:o_g], w0[:, o_q:], w0[:, o_g:o_q], jnp.zeros((d, LANES - ng), F32)],
                         axis=1).astype(BF16)
    layout = (("ml", 0, ml_w), ("aq", ml_w, aq_w), ("ak", ml_w + aq_w, akv_w),
              ("av", ml_w + aq_w + akv_w, akv_w), ("g", ml_w + aq_w + 2 * akv_w, LANES))
    tabs = _rope_tables(s, dh)
    lat = _inproj(x, sh1, sc1, nw[0:1], wp, tabs, layout=layout, dh=dh, rope=True, tm=_tile(s, 512))
    ml, aq, ak, av, gates = (lat[n] for n in ("ml", "aq", "ak", "av", "g"))
    tm_c = _tile(lc, 512)
    cx = _inproj(ctx, csh1, csc1, nw[0:1], wp, tuple(tb[:tm_c] for tb in tabs), layout=layout, dh=dh,
                 rope=False, tm=tm_c)
    mlc, akc, avc, gates_c = (cx[n] for n in ("ml", "ak", "av", "g"))

    L = _tile(lc, 256)
    if s % L:
        raise NotImplementedError("sequence must be a multiple of the mLSTM chunk")
    gate_b = jnp.zeros((1, LANES), F32).at[0, :ng].set(ml_gate_b[0])
    gp = _gate_prep(gates, gate_b, L)
    gpc = _gate_prep(gates_c, gate_b, L)
    y_ml = _mlstm(ml, mlc, gp, gpc, ml_conv[0], ml_norm_w[0][None, :], L)

    y_at = _attn(aq, ak, av, akc, avc, attn_sink[0])

    x1, h2, logits = _outproj(y_ml, y_at, x, g1, sh2, sc2, nw[1:2], nw[2:3], w_out[0].astype(BF16),
                              w_router[0], _tile(s, 512))

    eidx, ew, rank, cnt = _route(logits.reshape(t, e), jnp.broadcast_to(b_router[0][:, None], (e, LANES)),
                                 _tile(t, 512))
    counts = cnt[:, 0].astype(I32)
    padded = (counts + EXPERT_BLOCK - 1) // EXPERT_BLOCK * EXPERT_BLOCK
    pad_end = jnp.cumsum(padded)
    pad_start = pad_end - padded
    n_blocks = -(-(t * TOP_K + e * (EXPERT_BLOCK - 1)) // EXPERT_BLOCK)
    n_rows = n_blocks * EXPERT_BLOCK
    blk_row = jnp.arange(n_blocks, dtype=I32) * EXPERT_BLOCK
    block_e = jnp.minimum(jnp.sum((pad_end[None, :] <= blk_row[:, None]).astype(I32), axis=1), e - 1)
    n_used = (pad_end[-1] // EXPERT_BLOCK).astype(I32)[None]
    used = counts > 0
    eid = jnp.arange(e, dtype=I32)
    next_used = jnp.concatenate([lax.cummin(jnp.where(used, eid, e), reverse=True)[1:], jnp.full((1,), e, I32)])
    used_ord = jnp.cumsum(used.astype(I32)) - 1
    pos = _positions(eidx, rank, pad_start, _tile(t, 512))

    tt = _tile(t, 512)
    pos_d = pos.reshape(TOP_K, t // tt, tt).transpose(1, 0, 2)
    rc = d // 2 // LANES
    zcount = (padded - counts + ZERO_ROWS - 1) // ZERO_ROWS
    xs = _dispatch(h2.reshape(t, rc, LANES), pos_d, (pad_end - zcount * ZERO_ROWS).astype(I32), zcount.astype(I32),
                   n_rows, tt)
    ys = _experts(xs.reshape(n_rows * rc, LANES), block_e, n_used, next_used, used_ord,
                  w_exp_gate[0], w_exp_up[0], w_exp_down[0])
    tc = _tile(s, 128)
    pos_c = pos.reshape(TOP_K, t // tc, tc).transpose(1, 0, 2)
    ew_t = ew.T.reshape(b, s, TOP_K)
    return _combine(pos_c, ys.reshape(n_rows, rc, LANES), ew_t, x1, h2, g2, nw[3:4],
                    w_sh_gate[0].astype(BF16), w_sh_up[0].astype(BF16), w_sh_down[0].astype(BF16), tc)
```

```python
import functools

import jax
import jax.numpy as jnp
from jax import lax
from jax.experimental import pallas as pl
from jax.experimental.pallas import tpu as pltpu

F32 = jnp.float32
BF16 = jnp.bfloat16
I32 = jnp.int32
U32 = jnp.uint32

LANES = 128
SUBLANES = 8
VMEM_LIMIT_BYTES = 56 * 1024 * 1024

NORM_EPS = 1e-6
ML_HEADS = 4
CONV_W = 5
AT_HEADS = 8
AT_KV_HEADS = 2
GRID_W = 64
WINDOW = 128
BLOCK_Q = 128
ROPE_THETA = 10000.0
N_GROUPS = 8
TOPK_GROUPS = 4
TOP_K = 8
ROUTE_SCALE = 2.5
EXPERT_BLOCK = 256
ZERO_ROWS = 32
NEG_BIG = -1e30


def _cparams(sem):
    return pltpu.CompilerParams(dimension_semantics=sem, vmem_limit_bytes=VMEM_LIMIT_BYTES)


def _rms(xf, w):
    return xf * lax.rsqrt(jnp.mean(xf * xf, axis=-1, keepdims=True) + NORM_EPS) * w


def _silu(x):
    return x * jax.nn.sigmoid(x)


def _mm(a, b):
    return jnp.dot(a, b, preferred_element_type=F32)


def _pack_halves(x):
    n = x.shape[1] // 2
    u = lax.bitcast_convert_type(x.astype(BF16).astype(F32), U32)
    return (u[:, :n] >> 16) | (u[:, n:] & jnp.uint32(0xFFFF0000))


def _unpack_halves(u):
    return (lax.bitcast_convert_type(u << 16, F32),
            lax.bitcast_convert_type(u & jnp.uint32(0xFFFF0000), F32))


def _store_row_tiled(ref2d, packed):
    r, n = packed.shape
    c = n // LANES
    for j in range(c):
        ref2d[pl.ds(j, r, stride=c), :] = packed[:, j * LANES:(j + 1) * LANES]


def _load_row_tiled(ref2d, r):
    c = ref2d.shape[0] // r
    return [ref2d[pl.ds(j, r, stride=c), :] for j in range(c)]


def _ada_kernel(c_ref, w_ref, b_ref, o_ref):
    s = _silu(c_ref[...]).astype(BF16)
    o_ref[...] = _mm(s, w_ref[...].astype(BF16)) + b_ref[...]


def _ada(cond, w, b):
    r, d = cond.shape
    n = w.shape[1]
    tn = min(n, 1024)
    return pl.pallas_call(
        _ada_kernel,
        grid=(n // tn,),
        in_specs=[pl.BlockSpec((r, d), lambda j: (0, 0)),
                  pl.BlockSpec((d, tn), lambda j: (0, j)),
                  pl.BlockSpec((1, tn), lambda j: (0, j))],
        out_specs=pl.BlockSpec((r, tn), lambda j: (0, j)),
        out_shape=jax.ShapeDtypeStruct((r, n), F32),
        compiler_params=_cparams(("arbitrary",)),
        name="ada",
    )(cond, w, b)


def _inproj_kernel(x_ref, sh_ref, sc_ref, nw_ref, w_ref, rc_ref, ra_ref, rb_ref, *refs,
                   layout, dh, rope, qscale, cw):
    hb_ref = refs[-1]
    h = _rms(x_ref[0], nw_ref[...]) * (1.0 + sc_ref[0]) + sh_ref[0]
    hb_ref[...] = h.astype(BF16)
    for (name, col0, width), o_ref in zip(layout, refs):
        for c in range(0, width, cw):
            step = min(cw, width - c)
            acc = _mm(hb_ref[...], w_ref[:, col0 + c:col0 + c + step])
            if rope and name in ("aq", "ak"):
                for hh in range(step // dh):
                    a = acc[:, hh * dh:(hh + 1) * dh]
                    r = (a * rc_ref[...] + pltpu.roll(a, dh - dh // 4, 1) * ra_ref[...]
                         + pltpu.roll(a, dh // 4, 1) * rb_ref[...])
                    if name == "aq":
                        r = r * qscale
                    o_ref[0, :, c + hh * dh:c + (hh + 1) * dh] = r.astype(o_ref.dtype)
            else:
                if name == "aq":
                    acc = acc * qscale
                o_ref[0, :, c:c + step] = acc.astype(o_ref.dtype)


def _inproj(x, sh, sc, nw, wp, tabs, *, layout, dh, rope, tm):
    b, s, d = x.shape
    npad = wp.shape[1]
    rc, ra, rb = tabs
    kern = functools.partial(_inproj_kernel, layout=layout, dh=dh, rope=rope, qscale=float(dh) ** -0.5, cw=512)
    vec = pl.BlockSpec((1, 1, d), lambda bi, i: (bi, 0, 0))
    tab = pl.BlockSpec((tm, dh), lambda bi, i: (i, 0))
    out_dt = dict(ml=BF16, aq=BF16, ak=BF16, av=BF16, g=F32)
    names = [n for n, _, _ in layout]
    widths = {n: w for n, _, w in layout}
    outs = pl.pallas_call(
        kern,
        grid=(b, s // tm),
        in_specs=[pl.BlockSpec((1, tm, d), lambda bi, i: (bi, i, 0)), vec, vec,
                  pl.BlockSpec((1, d), lambda bi, i: (0, 0)),
                  pl.BlockSpec((d, npad), lambda bi, i: (0, 0), pipeline_mode=pl.Buffered(1)),
                  tab, tab, tab],
        out_specs=[pl.BlockSpec((1, tm, widths[n]), lambda bi, i: (bi, i, 0)) for n in names],
        out_shape=[jax.ShapeDtypeStruct((b, s, widths[n]), out_dt[n]) for n in names],
        scratch_shapes=[pltpu.VMEM((tm, d), BF16)],
        compiler_params=_cparams(("arbitrary", "arbitrary")),
        name="inproj_rope" if rope else "inproj_ctx",
    )(x, sh, sc, nw, wp, rc, ra, rb)
    return dict(zip(names, outs))


def _rope_tables(s, dh):
    rows = s // GRID_W
    row = jnp.repeat(jnp.arange(rows), GRID_W)
    col = jnp.tile(jnp.arange(GRID_W), rows)
    nf = dh // 4
    freqs = ROPE_THETA ** (-jnp.arange(nf, dtype=F32) / nf)
    pos = jnp.stack([row, col], axis=-1).astype(F32)
    ang = pos[:, :, None] * freqs
    cos, sin = jnp.cos(ang), jnp.sin(ang)
    z = jnp.zeros_like(sin[:, 0])
    rc = jnp.concatenate([cos[:, 0], cos[:, 0], cos[:, 1], cos[:, 1]], axis=-1)
    ra = jnp.concatenate([-sin[:, 0], z, -sin[:, 1], z], axis=-1)
    rb = jnp.concatenate([z, sin[:, 0], z, sin[:, 1]], axis=-1)
    return rc, ra, rb


def _gate_prep_kernel(g_ref, b_ref, gl_ref, pp_ref, glt_ref, ppt_ref, tot_ref, *, L, H, nc):
    for c in range(nc):
        z = g_ref[0, c * L:(c + 1) * L, :] + b_ref[...]
        lane = lax.broadcasted_iota(I32, z.shape, 1)
        row = lax.broadcasted_iota(I32, z.shape, 0)
        is_f = ((lane // H) % 2 == 1) & (lane < 4 * H)
        log_sig = jnp.minimum(z, 0.0) - jnp.log1p(jnp.exp(-jnp.abs(z)))
        gl = jnp.where(lane < 4 * H, jnp.where(is_f, log_sig, z), 0.0)
        cs = gl
        k = 1
        while k < L:
            cs = cs + jnp.where(row >= k, pltpu.roll(cs, k, 0), 0.0)
            k *= 2
        tot = cs[L - 1:L, :]
        suf = tot - cs + gl
        pp = jnp.where(lane >= 2 * H, suf, cs)
        gl_ref[0, c] = gl
        pp_ref[0, c] = pp
        glt_ref[0, c] = gl.T
        ppt_ref[0, c] = pp.T
        tot_ref[0, c] = jnp.broadcast_to(tot, (SUBLANES, LANES))


def _gate_prep(g, gate_b, L):
    b, s, _ = g.shape
    nc = s // L
    kern = functools.partial(_gate_prep_kernel, L=L, H=ML_HEADS, nc=nc)
    col = pl.BlockSpec((1, nc, L, LANES), lambda bi: (bi, 0, 0, 0))
    rowb = pl.BlockSpec((1, nc, LANES, L), lambda bi: (bi, 0, 0, 0))
    return pl.pallas_call(
        kern,
        grid=(b,),
        in_specs=[pl.BlockSpec((1, s, LANES), lambda bi: (bi, 0, 0)),
                  pl.BlockSpec((1, LANES), lambda bi: (0, 0))],
        out_specs=[col, col, rowb, rowb, pl.BlockSpec((1, nc, SUBLANES, LANES), lambda bi: (bi, 0, 0, 0))],
        out_shape=[jax.ShapeDtypeStruct((b, nc, L, LANES), F32), jax.ShapeDtypeStruct((b, nc, L, LANES), F32),
                   jax.ShapeDtypeStruct((b, nc, LANES, L), F32), jax.ShapeDtypeStruct((b, nc, LANES, L), F32),
                   jax.ShapeDtypeStruct((b, nc, SUBLANES, LANES), F32)],
        compiler_params=_cparams(("arbitrary",)),
        name="gate_prep",
    )(g, gate_b)


CONV_PAD = 8


def _lane_pick(tile, ch):
    lane = lax.broadcasted_iota(I32, tile.shape, 1)
    return jnp.sum(jnp.where(lane == ch, tile, 0.0), axis=1, keepdims=True)


def _sublane_pick(tile, ch):
    sub = lax.broadcasted_iota(I32, tile.shape, 0)
    return jnp.sum(jnp.where(sub == ch, tile, 0.0), axis=0, keepdims=True)


def _mlstm_chunk(d, ch_i, ch_f, q, k, v, gl, pp, glt, ppt, tot8, ct_ref, n_ref, m_ref, mask_ref=None):
    L = k.shape[0]
    p_col = _lane_pick(pp, ch_f)
    li_col = _lane_pick(gl, ch_i)
    btot = _lane_pick(tot8[0:1], ch_f)
    m_old = m_ref[d][0:1, 0:1]
    ct = ct_ref[d]
    n_row = n_ref[d]
    h = None
    if q is not None:
        p_row = _sublane_pick(ppt, ch_f)
        li_row = _sublane_pick(glt, ch_i)
        logw = (p_col - p_row + li_row) + mask_ref[d]
        log_inter = p_col + m_old
        m_q = jnp.maximum(log_inter, jnp.max(logw, axis=1, keepdims=True))
        sqk = lax.dot_general(q, k, (((1,), (1,)), ((), ())), preferred_element_type=F32) * jnp.exp(logw - m_q)
        w_inter = jnp.exp(log_inter - m_q)
        num = _mm(sqk.astype(BF16), v) + w_inter * _mm(q, ct.astype(BF16))
        den = (jnp.sum(sqk, axis=1, keepdims=True)
               + w_inter * jnp.sum(q.astype(F32) * n_row, axis=1, keepdims=True))
        h = num / jnp.maximum(jnp.abs(den), jnp.exp(-m_q))
    log_w_end = btot - p_col + li_col
    m_new = jnp.maximum(btot + m_old, jnp.max(log_w_end, axis=0, keepdims=True))
    w_end = jnp.exp(log_w_end - m_new)
    decay = jnp.exp(btot + m_old - m_new)
    kw = k.astype(F32) * w_end
    ct_ref[d] = decay * ct + lax.dot_general(kw.astype(BF16), v, (((0,), (0,)), ((), ())),
                                             preferred_element_type=F32)
    n_ref[d] = decay * n_row + jnp.sum(kw, axis=0, keepdims=True)
    m_ref[d] = jnp.broadcast_to(m_new, (SUBLANES, LANES))
    return h


def _mlstm_kernel(q_ref, k_ref, v_ref, o_ref, kc_ref, vc_ref,
                  gl_ref, pp_ref, glt_ref, ppt_ref, tot_ref, glc_ref, ppc_ref, totc_ref,
                  cwq_ref, cwk_ref, nw_ref, mask_ref, y_ref,
                  xq_s, xk_s, xkc_s, qs_s, ks_s, kcs_s, hf_s, hb_s, ct_s, n_s, m_s,
                  *, L, H, S, Lc, dk, dv):
    hh = pl.program_id(1)
    nc, ncc = S // L, Lc // L
    zpad = jnp.zeros((CONV_PAD, dk), F32)

    def conv_silu(x_s, cw_ref, n_rows, out_s, scale):
        for c in range(n_rows // L):
            acc = jnp.zeros((L, dk), F32)
            for j in range(CONV_W):
                acc = acc + cw_ref[j:j + 1, :] * x_s[pl.ds(CONV_PAD + c * L + j - CONV_W // 2, L), :]
            out_s[c] = (_silu(acc) * scale).astype(BF16)

    def stage(x_s, src, n_rows):
        x_s[0:CONV_PAD, :] = zpad
        x_s[CONV_PAD + n_rows:2 * CONV_PAD + n_rows, :] = zpad
        x_s[CONV_PAD:CONV_PAD + n_rows, :] = src.astype(F32)

    stage(xq_s, q_ref[0], S)
    stage(xk_s, k_ref[0], S)
    stage(xkc_s, kc_ref[0], Lc)
    conv_silu(xq_s, cwq_ref, S, qs_s, float(dk) ** -0.5)
    conv_silu(xk_s, cwk_ref, S, ks_s, 1.0)
    conv_silu(xkc_s, cwk_ref, Lc, kcs_s, 1.0)

    ct_s[...] = jnp.zeros(ct_s.shape, F32)
    n_s[...] = jnp.zeros(n_s.shape, F32)
    m_s[...] = jnp.zeros(m_s.shape, F32)

    def chans(d):
        return d * 2 * H + hh, d * 2 * H + H + hh

    for d in (0, 1):
        ch_i, ch_f = chans(d)
        for c in (range(ncc) if d == 0 else range(ncc - 1, -1, -1)):
            _mlstm_chunk(d, ch_i, ch_f, None, kcs_s[c], vc_ref[0, c * L:(c + 1) * L, :],
                         glc_ref[0, c], ppc_ref[0, c], None, None, totc_ref[0, c], ct_s, n_s, m_s)

    def body(i, carry):
        for d in (0, 1):
            ch_i, ch_f = chans(d)
            c = i if d == 0 else nc - 1 - i
            r0 = pl.multiple_of(c * L, L)
            h = _mlstm_chunk(d, ch_i, ch_f, qs_s[c], ks_s[c], v_ref[0, pl.ds(r0, L), :],
                             gl_ref[0, c], pp_ref[0, c], glt_ref[0, c], ppt_ref[0, c], tot_ref[0, c],
                             ct_s, n_s, m_s, mask_ref)
            if d == 0:
                hf_s[c] = h
            else:
                hb_s[c] = h
        return carry

    lax.fori_loop(0, nc, body, 0)

    for c in range(nc):
        hs = hf_s[c] + hb_s[c]
        hn = _rms(hs, nw_ref[...])
        y_ref[0, c * L:(c + 1) * L, :] = (hn * jax.nn.sigmoid(o_ref[0, c * L:(c + 1) * L, :].astype(F32))).astype(BF16)


def _mlstm(ml, mlc, gp, gpc, conv_w, norm_w, L):
    b, s, _ = ml.shape
    lc = mlc.shape[1]
    H = ML_HEADS
    dv = norm_w.shape[1] // H
    dk = dv // 2
    nc, ncc = s // L, lc // L
    gl, pp, glt, ppt, tot = gp
    glc, ppc, _, _, totc = gpc
    kern = functools.partial(_mlstm_kernel, L=L, H=H, S=s, Lc=lc, dk=dk, dv=dv)
    tri = jnp.arange(L)[None, :] <= jnp.arange(L)[:, None]
    masks = jnp.where(jnp.stack([tri, tri.T]), 0.0, NEG_BIG).astype(F32)

    def colspec(rows, w, off):
        return pl.BlockSpec((1, rows, w), lambda bi, h: (bi, 0, off + h))

    def gspec(n, r, c):
        return pl.BlockSpec((1, n, r, c), lambda bi, h: (bi, 0, 0, 0))

    return pl.pallas_call(
        kern,
        grid=(b, H),
        in_specs=[colspec(s, dk, 0), colspec(s, dk, H), colspec(s, dv, H), colspec(s, dv, 2 * H),
                  colspec(lc, dk, H), colspec(lc, dv, H),
                  gspec(nc, L, LANES), gspec(nc, L, LANES), gspec(nc, LANES, L), gspec(nc, LANES, L),
                  gspec(nc, SUBLANES, LANES),
                  gspec(ncc, L, LANES), gspec(ncc, L, LANES), gspec(ncc, SUBLANES, LANES),
                  pl.BlockSpec((CONV_W, dk), lambda bi, h: (0, h)),
                  pl.BlockSpec((CONV_W, dk), lambda bi, h: (0, H + h)),
                  pl.BlockSpec((1, dv), lambda bi, h: (0, h)),
                  pl.BlockSpec((2, L, L), lambda bi, h: (0, 0, 0))],
        out_specs=pl.BlockSpec((1, s, dv), lambda bi, h: (bi, 0, h)),
        out_shape=jax.ShapeDtypeStruct((b, s, H * dv), BF16),
        scratch_shapes=[pltpu.VMEM((s + 2 * CONV_PAD, dk), F32), pltpu.VMEM((s + 2 * CONV_PAD, dk), F32),
                        pltpu.VMEM((lc + 2 * CONV_PAD, dk), F32),
                        pltpu.VMEM((nc, L, dk), BF16), pltpu.VMEM((nc, L, dk), BF16), pltpu.VMEM((ncc, L, dk), BF16),
                        pltpu.VMEM((nc, L, dv), F32), pltpu.VMEM((nc, L, dv), F32),
                        pltpu.VMEM((2, dk, dv), F32), pltpu.VMEM((2, 1, dk), F32),
                        pltpu.VMEM((2, SUBLANES, LANES), F32)],
        compiler_params=_cparams(("arbitrary", "arbitrary")),
        name="mlstm",
    )(ml, ml, ml, ml, mlc, mlc, gl, pp, glt, ppt, tot, glc, ppc, totc, conv_w, conv_w, norm_w, masks)


def _attn_kernel(sink_ref, q_ref, kp_ref, kc_ref, kn_ref, vp_ref, vc_ref, vn_ref, kx_ref, vx_ref, bias_ref,
                 o_ref, *, bq, dh, G):
    q = q_ref[0]
    for g in range(AT_KV_HEADS):
        sl = slice(g * dh, (g + 1) * dh)
        qg = jnp.concatenate([q[:, (g * G + j) * dh:(g * G + j + 1) * dh] for j in range(G)], axis=0)
        kcat = jnp.concatenate([kp_ref[0][:, sl], kc_ref[0][:, sl], kn_ref[0][:, sl], kx_ref[0][:, sl]], axis=0)
        vcat = jnp.concatenate([vp_ref[0][:, sl], vc_ref[0][:, sl], vn_ref[0][:, sl], vx_ref[0][:, sl]], axis=0)
        s = lax.dot_general(qg, kcat, (((1,), (1,)), ((), ())), preferred_element_type=F32) + bias_ref[0]
        ri = lax.broadcasted_iota(I32, (G * bq, 1), 0)
        sk = jnp.zeros((G * bq, 1), F32)
        for j in range(G):
            sk = jnp.where((ri >= j * bq) & (ri < (j + 1) * bq), sink_ref[g * G + j], sk)
        m = jnp.maximum(jnp.max(s, axis=1, keepdims=True), sk)
        p = jnp.exp(s - m)
        den = jnp.sum(p, axis=1, keepdims=True) + jnp.exp(sk - m)
        o = _mm(p.astype(BF16), vcat) / den
        for j in range(G):
            o_ref[0, :, (g * G + j) * dh:(g * G + j + 1) * dh] = o[j * bq:(j + 1) * bq, :].astype(BF16)


def _attn_bias(bq, lc, G):
    qoff = (jnp.arange(G * bq) % bq)[:, None]
    ci = jnp.arange(3 * bq + lc)[None, :]
    prev = (ci < bq) & (qoff <= ci)
    cur = (ci >= bq) & (ci < 2 * bq)
    nxt = (ci >= 2 * bq) & (ci < 3 * bq) & (ci - 2 * bq <= qoff)
    ctx = ci >= 3 * bq
    inner = prev | cur | nxt | ctx
    first = cur | nxt | ctx
    last = prev | cur | ctx
    if WINDOW != bq:
        raise NotImplementedError("window must equal the query block")
    return jnp.where(jnp.stack([inner, first, last]), 0.0, NEG_BIG).astype(F32)


def _attn(aq, ak, av, akc, avc, sink):
    b, s, hd = aq.shape
    dh = hd // AT_HEADS
    G = AT_HEADS // AT_KV_HEADS
    bq = BLOCK_Q
    nb = s // bq
    lc = akc.shape[1]
    kvw = AT_KV_HEADS * dh
    bias = _attn_bias(bq, lc, G)
    if nb < 2:
        raise NotImplementedError("needs at least two query blocks")
    kern = functools.partial(_attn_kernel, bq=bq, dh=dh, G=G)
    prev = pl.BlockSpec((1, bq, kvw), lambda bi, n: (bi, jnp.maximum(n - 1, 0), 0))
    cur = pl.BlockSpec((1, bq, kvw), lambda bi, n: (bi, n, 0))
    nxt = pl.BlockSpec((1, bq, kvw), lambda bi, n: (bi, jnp.minimum(n + 1, nb - 1), 0))
    cx = pl.BlockSpec((1, lc, kvw), lambda bi, n: (bi, 0, 0))
    return pl.pallas_call(
        kern,
        grid=(b, nb),
        in_specs=[pl.BlockSpec(memory_space=pltpu.SMEM),
                  pl.BlockSpec((1, bq, hd), lambda bi, n: (bi, n, 0)),
                  prev, cur, nxt, prev, cur, nxt, cx, cx,
                  pl.BlockSpec((1, G * bq, 3 * bq + lc),
                               lambda bi, n: (jnp.where(n == 0, 1, jnp.where(n == nb - 1, 2, 0)), 0, 0))],
        out_specs=pl.BlockSpec((1, bq, hd), lambda bi, n: (bi, n, 0)),
        out_shape=jax.ShapeDtypeStruct((b, s, hd), BF16),
        compiler_params=_cparams(("arbitrary", "arbitrary")),
        name="attn",
    )(sink, aq, ak, ak, ak, av, av, av, akc, avc, bias)


def _outproj_kernel(ym_ref, ya_ref, x_ref, g1_ref, sh2_ref, sc2_ref, nw1_ref, nw2_ref, wo_ref, wrh_ref, wrl_ref,
                    x1_ref, h2_ref, lg_ref, y_s, *, dm, cw):
    i = pl.program_id(0)
    d = y_s.shape[2]

    @pl.when(i == 0)
    def _():
        y_s[...] = jnp.zeros(y_s.shape, F32)

    cur = i & 1
    x1 = x_ref[0] + g1_ref[0] * _rms(y_s[1 - cur], nw1_ref[...])
    x1_ref[0] = x1
    h2 = _rms(x1, nw2_ref[...]) * (1.0 + sc2_ref[0]) + sh2_ref[0]
    hi = h2.astype(BF16)
    _store_row_tiled(h2_ref.at[0], _pack_halves(h2))
    lo = (h2 - hi.astype(F32)).astype(BF16)
    lg_ref[0] = _mm(hi, wrh_ref[...]) + (_mm(hi, wrl_ref[...]) + _mm(lo, wrh_ref[...]))
    for c in range(0, d, cw):
        y_s[cur, :, c:c + cw] = _mm(ym_ref[0], wo_ref[0:dm, c:c + cw]) + _mm(ya_ref[0], wo_ref[dm:, c:c + cw])


def _outproj(y_ml, y_at, x, g1, sh2, sc2, nw1, nw2, wo, wr, tm):
    b, s, d = x.shape
    dm = y_ml.shape[2]
    e = wr.shape[1]
    wr_hi = wr.astype(BF16)
    wr_lo = (wr - wr_hi.astype(F32)).astype(BF16)
    kern = functools.partial(_outproj_kernel, dm=dm, cw=min(d, 512))
    nt = s // tm
    n = b * nt

    def ahead(i):
        a = jnp.minimum(i, n - 1)
        return a // nt, a % nt

    def behind(i):
        a = jnp.maximum(i - 1, 0)
        return a // nt, a % nt

    def tok(w, which):
        return pl.BlockSpec((1, tm, w), lambda i: (*which(i), 0))

    vec = pl.BlockSpec((1, 1, d), lambda i: (behind(i)[0], 0, 0))
    nspec = pl.BlockSpec((1, d), lambda i: (0, 0))
    const = lambda r, c: pl.BlockSpec((r, c), lambda i: (0, 0), pipeline_mode=pl.Buffered(1))
    rc = d // 2 // LANES
    return pl.pallas_call(
        kern,
        grid=(n + 1,),
        in_specs=[tok(dm, ahead), tok(d - dm, ahead), tok(d, behind),
                  vec, vec, vec, nspec, nspec, const(d, d), const(d, e), const(d, e)],
        out_specs=[tok(d, behind), pl.BlockSpec((1, tm * rc, LANES), lambda i: (*behind(i), 0)), tok(e, behind)],
        out_shape=[jax.ShapeDtypeStruct((b, s, d), F32), jax.ShapeDtypeStruct((b, s * rc, LANES), U32),
                   jax.ShapeDtypeStruct((b, s, e), F32)],
        scratch_shapes=[pltpu.VMEM((2, tm, d), F32)],
        compiler_params=_cparams(("arbitrary",)),
        name="outproj",
    )(y_ml, y_at, x, g1, sh2, sc2, nw1, nw2, wo, wr_hi, wr_lo)


def _route_kernel(lg_ref, br_ref, eidx_ref, ew_ref, rank_ref, cnt_ref, carry_s, *, E, tr):
    i = pl.program_id(0)

    @pl.when(i == 0)
    def _():
        carry_s[...] = jnp.zeros(carry_s.shape, F32)

    scores = jax.nn.sigmoid(lg_ref[...].T)
    biased = scores + br_ref[:, 0:1]
    row = lax.broadcasted_iota(I32, (E, tr), 0).astype(F32)
    gs = E // N_GROUPS
    ninf = -jnp.inf
    grp = []
    for g in range(N_GROUPS):
        xg = biased[g * gs:(g + 1) * gs]
        rg = (lax.broadcasted_iota(I32, (gs, tr), 0) + g * gs).astype(F32)
        m1 = jnp.max(xg, axis=0, keepdims=True)
        i1 = jnp.min(jnp.where(xg == m1, rg, float(E)), axis=0, keepdims=True)
        m2 = jnp.max(jnp.where(rg == i1, ninf, xg), axis=0, keepdims=True)
        grp.append(m1 + m2)
    gsc = jnp.concatenate(grp, axis=0)
    gi = lax.broadcasted_iota(I32, (N_GROUPS, tr), 0)
    beaten = jnp.zeros((N_GROUPS, tr), I32)
    for g2 in range(N_GROUPS):
        sg = gsc[g2:g2 + 1]
        beaten = beaten + jnp.where((sg > gsc) | ((sg == gsc) & (gi > g2)), 1, 0)
    keep_g = jnp.where(beaten < TOPK_GROUPS, 1.0, 0.0)
    keep = jnp.concatenate([jnp.broadcast_to(keep_g[g:g + 1], (gs, tr)) for g in range(N_GROUPS)], axis=0)
    masked = jnp.where(keep > 0.5, biased, ninf)
    idxs, ws, sels = [], [], []
    for _ in range(TOP_K):
        m = jnp.max(masked, axis=0, keepdims=True)
        ik = jnp.min(jnp.where(masked == m, row, float(E)), axis=0, keepdims=True)
        sel = row == ik
        ws.append(jnp.sum(jnp.where(sel, scores, 0.0), axis=0, keepdims=True))
        idxs.append(ik)
        sels.append(sel)
        masked = jnp.where(sel, ninf, masked)
    w = jnp.concatenate(ws, axis=0)
    ew_ref[...] = w / jnp.sum(w, axis=0, keepdims=True) * ROUTE_SCALE
    eidx_ref[...] = jnp.concatenate(idxs, axis=0).astype(I32)

    assign = jnp.zeros((E, tr), F32)
    for sel in sels:
        assign = assign + jnp.where(sel, 1.0, 0.0)
    upper = jnp.where(lax.broadcasted_iota(I32, (tr, tr), 0) < lax.broadcasted_iota(I32, (tr, tr), 1), 1.0, 0.0)
    base = _mm(assign.astype(BF16), upper.astype(BF16)) + carry_s[:, 0:1]
    ranks = [jnp.sum(jnp.where(sel, base, 0.0), axis=0, keepdims=True) for sel in sels]
    rank_ref[...] = jnp.concatenate(ranks, axis=0).astype(I32)
    carry_s[...] = carry_s[...] + jnp.sum(assign, axis=1, keepdims=True)
    cnt_ref[...] = carry_s[...]


def _route(logits, b_router, tr):
    t, e = logits.shape
    kern = functools.partial(_route_kernel, E=e, tr=tr)
    kt = pl.BlockSpec((TOP_K, tr), lambda i: (0, i))
    return pl.pallas_call(
        kern,
        grid=(t // tr,),
        in_specs=[pl.BlockSpec((tr, e), lambda i: (i, 0)), pl.BlockSpec((e, LANES), lambda i: (0, 0))],
        out_specs=[kt, kt, kt, pl.BlockSpec((e, LANES), lambda i: (0, 0))],
        out_shape=[jax.ShapeDtypeStruct((TOP_K, t), I32), jax.ShapeDtypeStruct((TOP_K, t), F32),
                   jax.ShapeDtypeStruct((TOP_K, t), I32), jax.ShapeDtypeStruct((e, LANES), F32)],
        scratch_shapes=[pltpu.VMEM((e, LANES), F32)],
        compiler_params=_cparams(("arbitrary",)),
        name="route",
    )(logits, b_router)


def _positions_kernel(eidx_ref, rank_ref, ps_ref, pos_ref, *, E, tr):
    row = lax.broadcasted_iota(I32, (E, tr), 0)
    start = ps_ref[:, 0:1]
    rows = [jnp.sum(jnp.where(row == eidx_ref[k:k + 1, :], start, 0.0), axis=0, keepdims=True)
            for k in range(TOP_K)]
    pos_ref[...] = jnp.concatenate(rows, axis=0).astype(I32) + rank_ref[...]


def _positions(eidx, rank, pad_start, tr):
    t = eidx.shape[1]
    e = pad_start.shape[0]
    kern = functools.partial(_positions_kernel, E=e, tr=tr)
    kt = pl.BlockSpec((TOP_K, tr), lambda i: (0, i))
    return pl.pallas_call(
        kern,
        grid=(t // tr,),
        in_specs=[kt, kt, pl.BlockSpec((e, LANES), lambda i: (0, 0))],
        out_specs=kt,
        out_shape=jax.ShapeDtypeStruct((TOP_K, t), I32),
        compiler_params=_cparams(("arbitrary",)),
        name="positions",
    )(eidx, rank, jnp.broadcast_to(pad_start.astype(F32)[:, None], (e, LANES)))


def _dispatch_kernel(zstart_ref, zcount_ref, pos_hbm, h2_ref, xs_hbm, zero_s, pos_s, sem_p, sem_z, sem_r,
                     *, E, tt):
    i = pl.program_id(0)

    def zero_copy(e, j):
        return pltpu.make_async_copy(zero_s, xs_hbm.at[pl.ds(zstart_ref[e] + j * ZERO_ROWS, ZERO_ROWS)], sem_z)

    @pl.when(i == 0)
    def _():
        zero_s[...] = jnp.zeros(zero_s.shape, U32)

        def each(fn):
            def per_expert(e, c):
                def per_piece(j, c2):
                    fn(zero_copy(e, j))
                    return c2
                return lax.fori_loop(0, zcount_ref[e], per_piece, c)
            lax.fori_loop(0, E, per_expert, 0)

        each(lambda cp: cp.start())
        each(lambda cp: cp.wait())

    pos_copy = pltpu.make_async_copy(pos_hbm.at[i], pos_s, sem_p)
    pos_copy.start()
    pos_copy.wait()

    def row_copy(t, k):
        return pltpu.make_async_copy(h2_ref.at[t], xs_hbm.at[pos_s[k, t]], sem_r)

    def start_rows(t, c):
        for k in range(TOP_K):
            row_copy(t, k).start(priority=k % 2)
        return c

    def wait_rows(t, c):
        for k in range(TOP_K):
            row_copy(t, k).wait()
        return c

    lax.fori_loop(0, tt, start_rows, 0)
    lax.fori_loop(0, tt, wait_rows, 0)


def _dispatch(h2, pos_tiles, zstart, zcount, n_rows, tt):
    t, c, _ = h2.shape
    e = zstart.shape[0]
    kern = functools.partial(_dispatch_kernel, E=e, tt=tt)
    return pl.pallas_call(
        kern,
        grid_spec=pltpu.PrefetchScalarGridSpec(
            num_scalar_prefetch=2,
            grid=(t // tt,),
            in_specs=[pl.BlockSpec(memory_space=pl.ANY),
                      pl.BlockSpec((tt, c, LANES), lambda i, zs, zv: (i, 0, 0))],
            out_specs=pl.BlockSpec(memory_space=pl.ANY),
            scratch_shapes=[pltpu.VMEM((ZERO_ROWS, c, LANES), U32), pltpu.SMEM((TOP_K, tt), I32),
                            pltpu.SemaphoreType.DMA, pltpu.SemaphoreType.DMA, pltpu.SemaphoreType.DMA]),
        out_shape=jax.ShapeDtypeStruct((n_rows, c, LANES), U32),
        compiler_params=_cparams(("arbitrary",)),
        name="dispatch",
    )(zstart, zcount, pos_tiles, h2)


def _experts_kernel(be_ref, nu_ref, nxt_ref, ord_ref, half_ref, xs_ref, wg_hbm, wu_hbm, wd_hbm, ys_ref,
                    sg_s, su_s, sd_s, sem, *, E):
    b = pl.program_id(0)
    n_used = nu_ref[0]

    def copies(e, slot):
        out = []
        for m, (w_hbm, st) in enumerate(((wg_hbm, sg_s), (wu_hbm, su_s), (wd_hbm, sd_s))):
            rows = w_hbm.shape[1] // 2
            for p in range(2):
                out.append(pltpu.make_async_copy(w_hbm.at[e, pl.ds(p * rows, rows)],
                                                 st.at[slot, pl.ds(p * rows, rows)], sem.at[slot, 2 * m + p]))
        return out

    def start_all(cps):
        for n, cp in enumerate(cps):
            cp.start(priority=n % 2)

    @pl.when(b < n_used)
    def _():
        e = be_ref[b]
        prev = be_ref[jnp.maximum(b - 1, 0)]
        slot = ord_ref[e] & 1

        @pl.when(b == 0)
        def _():
            start_all(copies(e, slot))

        @pl.when((b == 0) | (e != prev))
        def _():
            for cp in copies(e, slot):
                cp.wait()
            ne = nxt_ref[e]

            @pl.when(ne < E)
            def _():
                start_all(copies(ne, 1 - slot))

        def block(rows):
            span = pl.ds(0, rows * (xs_ref.shape[0] // EXPERT_BLOCK))
            parts = [_unpack_halves(ch) for ch in _load_row_tiled(xs_ref.at[span], rows)]
            x_lo = jnp.concatenate([p[0] for p in parts], axis=1).astype(BF16)
            x_hi = jnp.concatenate([p[1] for p in parts], axis=1).astype(BF16)
            half = x_lo.shape[1]
            gate = _mm(x_lo, sg_s[slot, 0:half, :]) + _mm(x_hi, sg_s[slot, half:, :])
            up = _mm(x_lo, su_s[slot, 0:half, :]) + _mm(x_hi, su_s[slot, half:, :])
            a = _silu(gate) * up
            _store_row_tiled(ys_ref.at[span], _pack_halves(_mm(a.astype(BF16), sd_s[slot])))

        @pl.when(half_ref[b] == 0)
        def _():
            block(EXPERT_BLOCK)

        @pl.when(half_ref[b] != 0)
        def _():
            block(EXPERT_BLOCK // 2)


def _experts(xs, block_e, n_used, next_used, used_ord, half_blk, wg, wu, wd):
    e, d, de = wg.shape
    c = d // 2 // LANES
    nblk = xs.shape[0] // (EXPERT_BLOCK * c)
    kern = functools.partial(_experts_kernel, E=e)

    def live(b, be, nu, nx, od, hf):
        return (jnp.minimum(b, jnp.maximum(nu[0] - 1, 0)), 0)

    hbm = pl.BlockSpec(memory_space=pl.ANY)
    return pl.pallas_call(
        kern,
        grid_spec=pltpu.PrefetchScalarGridSpec(
            num_scalar_prefetch=5,
            grid=(nblk,),
            in_specs=[pl.BlockSpec((EXPERT_BLOCK * c, LANES), live), hbm, hbm, hbm],
            out_specs=pl.BlockSpec((EXPERT_BLOCK * c, LANES), live),
            scratch_shapes=[pltpu.VMEM((2, d, de), F32), pltpu.VMEM((2, d, de), F32), pltpu.VMEM((2, de, d), F32),
                            pltpu.SemaphoreType.DMA((2, 6))]),
        out_shape=jax.ShapeDtypeStruct(xs.shape, U32),
        compiler_params=_cparams(("arbitrary",)),
        name="experts",
    )(block_e, n_used, next_used, used_ord, half_blk, xs, wg, wu, wd)


def _combine_kernel(pos_hbm, ys_hbm, ew_ref, x1_ref, h2_ref, g2_ref, nw3_ref, wsg_ref, wsu_ref, wsd_ref,
                    out_ref, buf_s, pos_s, sem_p, sem_r, *, tc, nt):
    tile = pl.program_id(0) * nt + pl.program_id(1)
    pos_copy = pltpu.make_async_copy(pos_hbm.at[tile], pos_s, sem_p)
    pos_copy.start()
    pos_copy.wait()

    def row_copy(t, k):
        c = ys_hbm.shape[1]
        return pltpu.make_async_copy(ys_hbm.at[pos_s[k, t]], buf_s.at[k, pl.ds(pl.multiple_of(t * c, c), c)], sem_r)

    def start_rows(t, c):
        for k in range(TOP_K):
            row_copy(t, k).start(priority=k % 2)
        return c

    def wait_rows(t, c):
        for k in range(TOP_K):
            row_copy(t, k).wait()
        return c

    lax.fori_loop(0, tc, start_rows, 0)
    parts = [_unpack_halves(ch) for ch in _load_row_tiled(h2_ref.at[0], tc)]
    h_lo = jnp.concatenate([p[0] for p in parts], axis=1).astype(BF16)
    h_hi = jnp.concatenate([p[1] for p in parts], axis=1).astype(BF16)
    half = h_lo.shape[1]
    nch = half // LANES
    gate = _mm(h_lo, wsg_ref[0:half, :]) + _mm(h_hi, wsg_ref[half:, :])
    up = _mm(h_lo, wsu_ref[0:half, :]) + _mm(h_hi, wsu_ref[half:, :])
    shared = _mm((_silu(gate) * up).astype(BF16), wsd_ref[...])
    lax.fori_loop(0, tc, wait_rows, 0)
    ew = ew_ref[0]
    acc = [shared[:, j * LANES:(j + 1) * LANES] for j in range(2 * nch)]
    for k in range(TOP_K):
        wk = ew[:, k:k + 1]
        for j, ch in enumerate(_load_row_tiled(buf_s.at[k], tc)):
            y_lo, y_hi = _unpack_halves(ch)
            acc[j] = acc[j] + y_lo * wk
            acc[nch + j] = acc[nch + j] + y_hi * wk
    out_ref[0] = x1_ref[0] + g2_ref[0] * _rms(jnp.concatenate(acc, axis=1), nw3_ref[...])


def _combine(pos_tiles, ys, ew, x1, h2, g2, nw3, wsg, wsu, wsd, tc):
    b, s, d = x1.shape
    ds_ = wsg.shape[1]
    nt = s // tc
    c = ys.shape[1]
    kern = functools.partial(_combine_kernel, tc=tc, nt=nt)
    tok = lambda w: pl.BlockSpec((1, tc, w), lambda bi, i: (bi, i, 0))
    return pl.pallas_call(
        kern,
        grid=(b, nt),
        in_specs=[pl.BlockSpec(memory_space=pl.ANY), pl.BlockSpec(memory_space=pl.ANY),
                  tok(TOP_K), tok(d), pl.BlockSpec((1, tc * c, LANES), lambda bi, i: (bi, i, 0)),
                  pl.BlockSpec((1, 1, d), lambda bi, i: (bi, 0, 0)),
                  pl.BlockSpec((1, d), lambda bi, i: (0, 0)),
                  pl.BlockSpec((d, ds_), lambda bi, i: (0, 0)),
                  pl.BlockSpec((d, ds_), lambda bi, i: (0, 0)),
                  pl.BlockSpec((ds_, d), lambda bi, i: (0, 0))],
        out_specs=tok(d),
        out_shape=jax.ShapeDtypeStruct((b, s, d), F32),
        scratch_shapes=[pltpu.VMEM((TOP_K, tc * c, LANES), U32), pltpu.SMEM((TOP_K, tc), I32),
                        pltpu.SemaphoreType.DMA, pltpu.SemaphoreType.DMA],
        compiler_params=_cparams(("arbitrary", "arbitrary")),
        name="combine",
    )(pos_tiles, ys, ew, x1, h2, g2, nw3, wsg, wsu, wsd)


def _tile(n, pref):
    t = min(n, pref)
    if n % t:
        raise NotImplementedError(f"size {n} is not a multiple of tile {t}")
    return t


def kernel(x, c, ctx, c_ctx, w_ada, b_ada, norms, w_in, ml_conv, ml_gate_b, ml_norm_w, attn_sink, w_out,
           w_router, b_router, w_exp_gate, w_exp_up, w_exp_down, w_sh_gate, w_sh_up, w_sh_down):
    if w_ada.shape[0] != 1:
        raise NotImplementedError("single-layer configuration only")
    b, s, d = x.shape
    lc = ctx.shape[1]
    t = b * s
    H = ML_HEADS
    dv = d // 2 // H
    dk = dv // 2
    dh = d // 2 // AT_HEADS
    qk_w, v_w = 2 * H * dk, H * dv
    ng = 4 * H
    aq_w, akv_w = AT_HEADS * dh, AT_KV_HEADS * dh
    e = w_router.shape[-1]
    nw = norms[0]

    rows = -(-(b + 1) // SUBLANES) * SUBLANES
    cond = jnp.zeros((rows, d), F32).at[:b].set(c).at[b].set(c_ctx)
    mods = _ada(cond, w_ada[0], b_ada[0][None, :])
    sh1, sc1, g1, sh2, sc2, g2 = [m[:b, None, :] for m in jnp.split(mods, 6, axis=-1)]
    csh1, csc1 = [jnp.broadcast_to(m[b:b + 1, None, :], (b, 1, d)) for m in jnp.split(mods, 6, axis=-1)[:2]]

    w0 = w_in[0]
    o_ml, o_g, o_q = 0, qk_w + 2 * v_w, qk_w + 2 * v_w + ng
    ml_w = qk_w + 2 * v_w
    wp = jnp.concatenate([w0[:, o_ml:o_g], w0[:, o_q:], w0[:, o_g:o_q], jnp.zeros((d, LANES - ng), F32)],
                         axis=1).astype(BF16)
    layout = (("ml", 0, ml_w), ("aq", ml_w, aq_w), ("ak", ml_w + aq_w, akv_w),
              ("av", ml_w + aq_w + akv_w, akv_w), ("g", ml_w + aq_w + 2 * akv_w, LANES))
    tabs = _rope_tables(s, dh)
    lat = _inproj(x, sh1, sc1, nw[0:1], wp, tabs, layout=layout, dh=dh, rope=True, tm=_tile(s, 512))
    ml, aq, ak, av, gates = (lat[n] for n in ("ml", "aq", "ak", "av", "g"))
    tm_c = _tile(lc, 512)
    cx = _inproj(ctx, csh1, csc1, nw[0:1], wp, tuple(tb[:tm_c] for tb in tabs), layout=layout, dh=dh,
                 rope=False, tm=tm_c)
    mlc, akc, avc, gates_c = (cx[n] for n in ("ml", "ak", "av", "g"))

    L = _tile(lc, 256)
    if s % L:
        raise NotImplementedError("sequence must be a multiple of the mLSTM chunk")
    gate_b = jnp.zeros((1, LANES), F32).at[0, :ng].set(ml_gate_b[0])
    gp = _gate_prep(gates, gate_b, L)
    gpc = _gate_prep(gates_c, gate_b, L)
    y_ml = _mlstm(ml, mlc, gp, gpc, ml_conv[0], ml_norm_w[0][None, :], L)

    y_at = _attn(aq, ak, av, akc, avc, attn_sink[0])

    x1, h2, logits = _outproj(y_ml, y_at, x, g1, sh2, sc2, nw[1:2], nw[2:3], w_out[0].astype(BF16),
                              w_router[0], _tile(s, 512))

    eidx, ew, rank, cnt = _route(logits.reshape(t, e), jnp.broadcast_to(b_router[0][:, None], (e, LANES)),
                                 _tile(t, 512))
    counts = cnt[:, 0].astype(I32)
    padded = (counts + EXPERT_BLOCK - 1) // EXPERT_BLOCK * EXPERT_BLOCK
    pad_end = jnp.cumsum(padded)
    pad_start = pad_end - padded
    n_blocks = -(-(t * TOP_K + e * (EXPERT_BLOCK - 1)) // EXPERT_BLOCK)
    n_rows = n_blocks * EXPERT_BLOCK
    blk_row = jnp.arange(n_blocks, dtype=I32) * EXPERT_BLOCK
    block_e = jnp.minimum(jnp.sum((pad_end[None, :] <= blk_row[:, None]).astype(I32), axis=1), e - 1)
    n_used = (pad_end[-1] // EXPERT_BLOCK).astype(I32)[None]
    used = counts > 0
    eid = jnp.arange(e, dtype=I32)
    next_used = jnp.concatenate([lax.cummin(jnp.where(used, eid, e), reverse=True)[1:], jnp.full((1,), e, I32)])
    used_ord = jnp.cumsum(used.astype(I32)) - 1
    pos = _positions(eidx, rank, pad_start, _tile(t, 512))

    tt = _tile(t, 512)
    pos_d = pos.reshape(TOP_K, t // tt, tt).transpose(1, 0, 2)
    rc = d // 2 // LANES
    zcount = (padded - counts + ZERO_ROWS - 1) // ZERO_ROWS
    xs = _dispatch(h2.reshape(t, rc, LANES), pos_d, (pad_end - zcount * ZERO_ROWS).astype(I32), zcount.astype(I32),
                   n_rows, tt)
    blk_valid = counts[block_e] - (blk_row - pad_start[block_e])
    half_blk = (blk_valid <= EXPERT_BLOCK // 2).astype(I32)
    ys = _experts(xs.reshape(n_rows * rc, LANES), block_e, n_used, next_used, used_ord, half_blk,
                  w_exp_gate[0], w_exp_up[0], w_exp_down[0])
    tc = _tile(s, 128)
    pos_c = pos.reshape(TOP_K, t // tc, tc).transpose(1, 0, 2)
    ew_t = ew.T.reshape(b, s, TOP_K)
    return _combine(pos_c, ys.reshape(n_rows, rc, LANES), ew_t, x1, h2, g2, nw[3:4],
                    w_sh_gate[0].astype(BF16), w_sh_up[0].astype(BF16), w_sh_down[0].astype(BF16), tc)
```

```python
import functools

import jax
import jax.numpy as jnp
from jax import lax
from jax.experimental import pallas as pl
from jax.experimental.pallas import tpu as pltpu

F32 = jnp.float32
BF16 = jnp.bfloat16
I32 = jnp.int32
U32 = jnp.uint32

LANES = 128
SUBLANES = 8
VMEM_LIMIT_BYTES = 56 * 1024 * 1024

NORM_EPS = 1e-6
ML_HEADS = 4
CONV_W = 5
AT_HEADS = 8
AT_KV_HEADS = 2
GRID_W = 64
WINDOW = 128
BLOCK_Q = 128
ROPE_THETA = 10000.0
N_GROUPS = 8
TOPK_GROUPS = 4
TOP_K = 8
ROUTE_SCALE = 2.5
EXPERT_BLOCK = 256
ZERO_ROWS = 32
WEIGHT_SLOTS = 3
NEG_BIG = -1e30


def _cparams(sem):
    return pltpu.CompilerParams(dimension_semantics=sem, vmem_limit_bytes=VMEM_LIMIT_BYTES)


def _rms(xf, w):
    return xf * lax.rsqrt(jnp.mean(xf * xf, axis=-1, keepdims=True) + NORM_EPS) * w


def _silu(x):
    return x * jax.nn.sigmoid(x)


def _mm(a, b):
    return jnp.dot(a, b, preferred_element_type=F32)


def _pack_halves(x):
    n = x.shape[1] // 2
    u = lax.bitcast_convert_type(x.astype(BF16).astype(F32), U32)
    return (u[:, :n] >> 16) | (u[:, n:] & jnp.uint32(0xFFFF0000))


def _unpack_halves(u):
    return (lax.bitcast_convert_type(u << 16, F32),
            lax.bitcast_convert_type(u & jnp.uint32(0xFFFF0000), F32))


def _store_row_tiled(ref2d, packed):
    r, n = packed.shape
    c = n // LANES
    for j in range(c):
        ref2d[pl.ds(j, r, stride=c), :] = packed[:, j * LANES:(j + 1) * LANES]


def _load_row_tiled(ref2d, r):
    c = ref2d.shape[0] // r
    return [ref2d[pl.ds(j, r, stride=c), :] for j in range(c)]


def _ada_kernel(c_ref, w_ref, b_ref, o_ref):
    s = _silu(c_ref[...]).astype(BF16)
    o_ref[...] = _mm(s, w_ref[...].astype(BF16)) + b_ref[...]


def _ada(cond, w, b):
    r, d = cond.shape
    n = w.shape[1]
    tn = min(n, 1024)
    return pl.pallas_call(
        _ada_kernel,
        grid=(n // tn,),
        in_specs=[pl.BlockSpec((r, d), lambda j: (0, 0)),
                  pl.BlockSpec((d, tn), lambda j: (0, j)),
                  pl.BlockSpec((1, tn), lambda j: (0, j))],
        out_specs=pl.BlockSpec((r, tn), lambda j: (0, j)),
        out_shape=jax.ShapeDtypeStruct((r, n), F32),
        compiler_params=_cparams(("arbitrary",)),
        name="ada",
    )(cond, w, b)


def _inproj_kernel(x_ref, sh_ref, sc_ref, nw_ref, w_ref, rc_ref, ra_ref, rb_ref, *refs,
                   layout, dh, rope, qscale, cw):
    hb_ref = refs[-1]
    h = _rms(x_ref[0], nw_ref[...]) * (1.0 + sc_ref[0]) + sh_ref[0]
    hb_ref[...] = h.astype(BF16)
    for (name, col0, width), o_ref in zip(layout, refs):
        for c in range(0, width, cw):
            step = min(cw, width - c)
            acc = _mm(hb_ref[...], w_ref[:, col0 + c:col0 + c + step])
            if rope and name in ("aq", "ak"):
                for hh in range(step // dh):
                    a = acc[:, hh * dh:(hh + 1) * dh]
                    r = (a * rc_ref[...] + pltpu.roll(a, dh - dh // 4, 1) * ra_ref[...]
                         + pltpu.roll(a, dh // 4, 1) * rb_ref[...])
                    if name == "aq":
                        r = r * qscale
                    o_ref[0, :, c + hh * dh:c + (hh + 1) * dh] = r.astype(o_ref.dtype)
            else:
                if name == "aq":
                    acc = acc * qscale
                o_ref[0, :, c:c + step] = acc.astype(o_ref.dtype)


def _inproj(x, sh, sc, nw, wp, tabs, *, layout, dh, rope, tm):
    b, s, d = x.shape
    npad = wp.shape[1]
    rc, ra, rb = tabs
    kern = functools.partial(_inproj_kernel, layout=layout, dh=dh, rope=rope, qscale=float(dh) ** -0.5, cw=512)
    vec = pl.BlockSpec((1, 1, d), lambda bi, i: (bi, 0, 0))
    tab = pl.BlockSpec((tm, dh), lambda bi, i: (i, 0))
    out_dt = dict(ml=BF16, aq=BF16, ak=BF16, av=BF16, g=F32)
    names = [n for n, _, _ in layout]
    widths = {n: w for n, _, w in layout}
    outs = pl.pallas_call(
        kern,
        grid=(b, s // tm),
        in_specs=[pl.BlockSpec((1, tm, d), lambda bi, i: (bi, i, 0)), vec, vec,
                  pl.BlockSpec((1, d), lambda bi, i: (0, 0)),
                  pl.BlockSpec((d, npad), lambda bi, i: (0, 0), pipeline_mode=pl.Buffered(1)),
                  tab, tab, tab],
        out_specs=[pl.BlockSpec((1, tm, widths[n]), lambda bi, i: (bi, i, 0)) for n in names],
        out_shape=[jax.ShapeDtypeStruct((b, s, widths[n]), out_dt[n]) for n in names],
        scratch_shapes=[pltpu.VMEM((tm, d), BF16)],
        compiler_params=_cparams(("arbitrary", "arbitrary")),
        name="inproj_rope" if rope else "inproj_ctx",
    )(x, sh, sc, nw, wp, rc, ra, rb)
    return dict(zip(names, outs))


def _rope_tables(s, dh):
    rows = s // GRID_W
    row = jnp.repeat(jnp.arange(rows), GRID_W)
    col = jnp.tile(jnp.arange(GRID_W), rows)
    nf = dh // 4
    freqs = ROPE_THETA ** (-jnp.arange(nf, dtype=F32) / nf)
    pos = jnp.stack([row, col], axis=-1).astype(F32)
    ang = pos[:, :, None] * freqs
    cos, sin = jnp.cos(ang), jnp.sin(ang)
    z = jnp.zeros_like(sin[:, 0])
    rc = jnp.concatenate([cos[:, 0], cos[:, 0], cos[:, 1], cos[:, 1]], axis=-1)
    ra = jnp.concatenate([-sin[:, 0], z, -sin[:, 1], z], axis=-1)
    rb = jnp.concatenate([z, sin[:, 0], z, sin[:, 1]], axis=-1)
    return rc, ra, rb


def _gate_prep_kernel(g_ref, b_ref, gl_ref, pp_ref, glt_ref, ppt_ref, tot_ref, *, L, H, nc):
    for c in range(nc):
        z = g_ref[0, c * L:(c + 1) * L, :] + b_ref[...]
        lane = lax.broadcasted_iota(I32, z.shape, 1)
        row = lax.broadcasted_iota(I32, z.shape, 0)
        is_f = ((lane // H) % 2 == 1) & (lane < 4 * H)
        log_sig = jnp.minimum(z, 0.0) - jnp.log1p(jnp.exp(-jnp.abs(z)))
        gl = jnp.where(lane < 4 * H, jnp.where(is_f, log_sig, z), 0.0)
        cs = gl
        k = 1
        while k < L:
            cs = cs + jnp.where(row >= k, pltpu.roll(cs, k, 0), 0.0)
            k *= 2
        tot = cs[L - 1:L, :]
        suf = tot - cs + gl
        pp = jnp.where(lane >= 2 * H, suf, cs)
        gl_ref[0, c] = gl
        pp_ref[0, c] = pp
        glt_ref[0, c] = gl.T
        ppt_ref[0, c] = pp.T
        tot_ref[0, c] = jnp.broadcast_to(tot, (SUBLANES, LANES))


def _gate_prep(g, gate_b, L):
    b, s, _ = g.shape
    nc = s // L
    kern = functools.partial(_gate_prep_kernel, L=L, H=ML_HEADS, nc=nc)
    col = pl.BlockSpec((1, nc, L, LANES), lambda bi: (bi, 0, 0, 0))
    rowb = pl.BlockSpec((1, nc, LANES, L), lambda bi: (bi, 0, 0, 0))
    return pl.pallas_call(
        kern,
        grid=(b,),
        in_specs=[pl.BlockSpec((1, s, LANES), lambda bi: (bi, 0, 0)),
                  pl.BlockSpec((1, LANES), lambda bi: (0, 0))],
        out_specs=[col, col, rowb, rowb, pl.BlockSpec((1, nc, SUBLANES, LANES), lambda bi: (bi, 0, 0, 0))],
        out_shape=[jax.ShapeDtypeStruct((b, nc, L, LANES), F32), jax.ShapeDtypeStruct((b, nc, L, LANES), F32),
                   jax.ShapeDtypeStruct((b, nc, LANES, L), F32), jax.ShapeDtypeStruct((b, nc, LANES, L), F32),
                   jax.ShapeDtypeStruct((b, nc, SUBLANES, LANES), F32)],
        compiler_params=_cparams(("arbitrary",)),
        name="gate_prep",
    )(g, gate_b)


CONV_PAD = 8


def _lane_pick(tile, ch):
    lane = lax.broadcasted_iota(I32, tile.shape, 1)
    return jnp.sum(jnp.where(lane == ch, tile, 0.0), axis=1, keepdims=True)


def _sublane_pick(tile, ch):
    sub = lax.broadcasted_iota(I32, tile.shape, 0)
    return jnp.sum(jnp.where(sub == ch, tile, 0.0), axis=0, keepdims=True)


def _mlstm_chunk(d, ch_i, ch_f, q, k, v, gl, pp, glt, ppt, tot8, ct_ref, n_ref, m_ref, mask_ref=None):
    L = k.shape[0]
    p_col = _lane_pick(pp, ch_f)
    li_col = _lane_pick(gl, ch_i)
    btot = _lane_pick(tot8[0:1], ch_f)
    m_old = m_ref[d][0:1, 0:1]
    ct = ct_ref[d]
    n_row = n_ref[d]
    h = None
    if q is not None:
        p_row = _sublane_pick(ppt, ch_f)
        li_row = _sublane_pick(glt, ch_i)
        logw = (p_col - p_row + li_row) + mask_ref[d]
        log_inter = p_col + m_old
        m_q = jnp.maximum(log_inter, jnp.max(logw, axis=1, keepdims=True))
        sqk = lax.dot_general(q, k, (((1,), (1,)), ((), ())), preferred_element_type=F32) * jnp.exp(logw - m_q)
        w_inter = jnp.exp(log_inter - m_q)
        num = _mm(sqk.astype(BF16), v) + w_inter * _mm(q, ct.astype(BF16))
        den = (jnp.sum(sqk, axis=1, keepdims=True)
               + w_inter * jnp.sum(q.astype(F32) * n_row, axis=1, keepdims=True))
        h = num / jnp.maximum(jnp.abs(den), jnp.exp(-m_q))
    log_w_end = btot - p_col + li_col
    m_new = jnp.maximum(btot + m_old, jnp.max(log_w_end, axis=0, keepdims=True))
    w_end = jnp.exp(log_w_end - m_new)
    decay = jnp.exp(btot + m_old - m_new)
    kw = k.astype(F32) * w_end
    ct_ref[d] = decay * ct + lax.dot_general(kw.astype(BF16), v, (((0,), (0,)), ((), ())),
                                             preferred_element_type=F32)
    n_ref[d] = decay * n_row + jnp.sum(kw, axis=0, keepdims=True)
    m_ref[d] = jnp.broadcast_to(m_new, (SUBLANES, LANES))
    return h


def _mlstm_kernel(q_ref, k_ref, v_ref, o_ref, kc_ref, vc_ref,
                  gl_ref, pp_ref, glt_ref, ppt_ref, tot_ref, glc_ref, ppc_ref, totc_ref,
                  cwq_ref, cwk_ref, nw_ref, mask_ref, y_ref,
                  xq_s, xk_s, xkc_s, qs_s, ks_s, kcs_s, hf_s, hb_s, ct_s, n_s, m_s,
                  *, L, H, S, Lc, dk, dv):
    hh = pl.program_id(1)
    nc, ncc = S // L, Lc // L
    zpad = jnp.zeros((CONV_PAD, dk), F32)

    def conv_silu(x_s, cw_ref, n_rows, out_s, scale):
        for c in range(n_rows // L):
            acc = jnp.zeros((L, dk), F32)
            for j in range(CONV_W):
                acc = acc + cw_ref[j:j + 1, :] * x_s[pl.ds(CONV_PAD + c * L + j - CONV_W // 2, L), :]
            out_s[c] = (_silu(acc) * scale).astype(BF16)

    def stage(x_s, src, n_rows):
        x_s[0:CONV_PAD, :] = zpad
        x_s[CONV_PAD + n_rows:2 * CONV_PAD + n_rows, :] = zpad
        x_s[CONV_PAD:CONV_PAD + n_rows, :] = src.astype(F32)

    stage(xq_s, q_ref[0], S)
    stage(xk_s, k_ref[0], S)
    stage(xkc_s, kc_ref[0], Lc)
    conv_silu(xq_s, cwq_ref, S, qs_s, float(dk) ** -0.5)
    conv_silu(xk_s, cwk_ref, S, ks_s, 1.0)
    conv_silu(xkc_s, cwk_ref, Lc, kcs_s, 1.0)

    ct_s[...] = jnp.zeros(ct_s.shape, F32)
    n_s[...] = jnp.zeros(n_s.shape, F32)
    m_s[...] = jnp.zeros(m_s.shape, F32)

    def chans(d):
        return d * 2 * H + hh, d * 2 * H + H + hh

    for d in (0, 1):
        ch_i, ch_f = chans(d)
        for c in (range(ncc) if d == 0 else range(ncc - 1, -1, -1)):
            _mlstm_chunk(d, ch_i, ch_f, None, kcs_s[c], vc_ref[0, c * L:(c + 1) * L, :],
                         glc_ref[0, c], ppc_ref[0, c], None, None, totc_ref[0, c], ct_s, n_s, m_s)

    def body(i, carry):
        for d in (0, 1):
            ch_i, ch_f = chans(d)
            c = i if d == 0 else nc - 1 - i
            r0 = pl.multiple_of(c * L, L)
            h = _mlstm_chunk(d, ch_i, ch_f, qs_s[c], ks_s[c], v_ref[0, pl.ds(r0, L), :],
                             gl_ref[0, c], pp_ref[0, c], glt_ref[0, c], ppt_ref[0, c], tot_ref[0, c],
                             ct_s, n_s, m_s, mask_ref)
            if d == 0:
                hf_s[c] = h
            else:
                hb_s[c] = h
        return carry

    lax.fori_loop(0, nc, body, 0)

    for c in range(nc):
        hs = hf_s[c] + hb_s[c]
        hn = _rms(hs, nw_ref[...])
        y_ref[0, c * L:(c + 1) * L, :] = (hn * jax.nn.sigmoid(o_ref[0, c * L:(c + 1) * L, :].astype(F32))).astype(BF16)


def _mlstm(ml, mlc, gp, gpc, conv_w, norm_w, L):
    b, s, _ = ml.shape
    lc = mlc.shape[1]
    H = ML_HEADS
    dv = norm_w.shape[1] // H
    dk = dv // 2
    nc, ncc = s // L, lc // L
    gl, pp, glt, ppt, tot = gp
    glc, ppc, _, _, totc = gpc
    kern = functools.partial(_mlstm_kernel, L=L, H=H, S=s, Lc=lc, dk=dk, dv=dv)
    tri = jnp.arange(L)[None, :] <= jnp.arange(L)[:, None]
    masks = jnp.where(jnp.stack([tri, tri.T]), 0.0, NEG_BIG).astype(F32)

    def colspec(rows, w, off):
        return pl.BlockSpec((1, rows, w), lambda bi, h: (bi, 0, off + h))

    def gspec(n, r, c):
        return pl.BlockSpec((1, n, r, c), lambda bi, h: (bi, 0, 0, 0))

    return pl.pallas_call(
        kern,
        grid=(b, H),
        in_specs=[colspec(s, dk, 0), colspec(s, dk, H), colspec(s, dv, H), colspec(s, dv, 2 * H),
                  colspec(lc, dk, H), colspec(lc, dv, H),
                  gspec(nc, L, LANES), gspec(nc, L, LANES), gspec(nc, LANES, L), gspec(nc, LANES, L),
                  gspec(nc, SUBLANES, LANES),
                  gspec(ncc, L, LANES), gspec(ncc, L, LANES), gspec(ncc, SUBLANES, LANES),
                  pl.BlockSpec((CONV_W, dk), lambda bi, h: (0, h)),
                  pl.BlockSpec((CONV_W, dk), lambda bi, h: (0, H + h)),
                  pl.BlockSpec((1, dv), lambda bi, h: (0, h)),
                  pl.BlockSpec((2, L, L), lambda bi, h: (0, 0, 0))],
        out_specs=pl.BlockSpec((1, s, dv), lambda bi, h: (bi, 0, h)),
        out_shape=jax.ShapeDtypeStruct((b, s, H * dv), BF16),
        scratch_shapes=[pltpu.VMEM((s + 2 * CONV_PAD, dk), F32), pltpu.VMEM((s + 2 * CONV_PAD, dk), F32),
                        pltpu.VMEM((lc + 2 * CONV_PAD, dk), F32),
                        pltpu.VMEM((nc, L, dk), BF16), pltpu.VMEM((nc, L, dk), BF16), pltpu.VMEM((ncc, L, dk), BF16),
                        pltpu.VMEM((nc, L, dv), F32), pltpu.VMEM((nc, L, dv), F32),
                        pltpu.VMEM((2, dk, dv), F32), pltpu.VMEM((2, 1, dk), F32),
                        pltpu.VMEM((2, SUBLANES, LANES), F32)],
        compiler_params=_cparams(("arbitrary", "arbitrary")),
        name="mlstm",
    )(ml, ml, ml, ml, mlc, mlc, gl, pp, glt, ppt, tot, glc, ppc, totc, conv_w, conv_w, norm_w, masks)


def _attn_kernel(sink_ref, q_ref, kp_ref, kc_ref, kn_ref, vp_ref, vc_ref, vn_ref, kx_ref, vx_ref, bias_ref,
                 o_ref, *, bq, dh, G):
    q = q_ref[0]
    for g in range(AT_KV_HEADS):
        sl = slice(g * dh, (g + 1) * dh)
        qg = jnp.concatenate([q[:, (g * G + j) * dh:(g * G + j + 1) * dh] for j in range(G)], axis=0)
        kcat = jnp.concatenate([kp_ref[0][:, sl], kc_ref[0][:, sl], kn_ref[0][:, sl], kx_ref[0][:, sl]], axis=0)
        vcat = jnp.concatenate([vp_ref[0][:, sl], vc_ref[0][:, sl], vn_ref[0][:, sl], vx_ref[0][:, sl]], axis=0)
        s = lax.dot_general(qg, kcat, (((1,), (1,)), ((), ())), preferred_element_type=F32) + bias_ref[0]
        ri = lax.broadcasted_iota(I32, (G * bq, 1), 0)
        sk = jnp.zeros((G * bq, 1), F32)
        for j in range(G):
            sk = jnp.where((ri >= j * bq) & (ri < (j + 1) * bq), sink_ref[g * G + j], sk)
        m = jnp.maximum(jnp.max(s, axis=1, keepdims=True), sk)
        p = jnp.exp(s - m)
        den = jnp.sum(p, axis=1, keepdims=True) + jnp.exp(sk - m)
        o = _mm(p.astype(BF16), vcat) / den
        for j in range(G):
            o_ref[0, :, (g * G + j) * dh:(g * G + j + 1) * dh] = o[j * bq:(j + 1) * bq, :].astype(BF16)


def _attn_bias(bq, lc, G):
    qoff = (jnp.arange(G * bq) % bq)[:, None]
    ci = jnp.arange(3 * bq + lc)[None, :]
    prev = (ci < bq) & (qoff <= ci)
    cur = (ci >= bq) & (ci < 2 * bq)
    nxt = (ci >= 2 * bq) & (ci < 3 * bq) & (ci - 2 * bq <= qoff)
    ctx = ci >= 3 * bq
    inner = prev | cur | nxt | ctx
    first = cur | nxt | ctx
    last = prev | cur | ctx
    if WINDOW != bq:
        raise NotImplementedError("window must equal the query block")
    return jnp.where(jnp.stack([inner, first, last]), 0.0, NEG_BIG).astype(F32)


def _attn(aq, ak, av, akc, avc, sink):
    b, s, hd = aq.shape
    dh = hd // AT_HEADS
    G = AT_HEADS // AT_KV_HEADS
    bq = BLOCK_Q
    nb = s // bq
    lc = akc.shape[1]
    kvw = AT_KV_HEADS * dh
    bias = _attn_bias(bq, lc, G)
    if nb < 2:
        raise NotImplementedError("needs at least two query blocks")
    kern = functools.partial(_attn_kernel, bq=bq, dh=dh, G=G)
    prev = pl.BlockSpec((1, bq, kvw), lambda bi, n: (bi, jnp.maximum(n - 1, 0), 0))
    cur = pl.BlockSpec((1, bq, kvw), lambda bi, n: (bi, n, 0))
    nxt = pl.BlockSpec((1, bq, kvw), lambda bi, n: (bi, jnp.minimum(n + 1, nb - 1), 0))
    cx = pl.BlockSpec((1, lc, kvw), lambda bi, n: (bi, 0, 0))
    return pl.pallas_call(
        kern,
        grid=(b, nb),
        in_specs=[pl.BlockSpec(memory_space=pltpu.SMEM),
                  pl.BlockSpec((1, bq, hd), lambda bi, n: (bi, n, 0)),
                  prev, cur, nxt, prev, cur, nxt, cx, cx,
                  pl.BlockSpec((1, G * bq, 3 * bq + lc),
                               lambda bi, n: (jnp.where(n == 0, 1, jnp.where(n == nb - 1, 2, 0)), 0, 0))],
        out_specs=pl.BlockSpec((1, bq, hd), lambda bi, n: (bi, n, 0)),
        out_shape=jax.ShapeDtypeStruct((b, s, hd), BF16),
        compiler_params=_cparams(("arbitrary", "arbitrary")),
        name="attn",
    )(sink, aq, ak, ak, ak, av, av, av, akc, avc, bias)


def _outproj_kernel(ym_ref, ya_ref, x_ref, g1_ref, sh2_ref, sc2_ref, nw1_ref, nw2_ref, wo_ref, wrh_ref, wrl_ref,
                    x1_ref, h2_ref, lg_ref, y_s, *, dm, cw):
    i = pl.program_id(0)
    d = y_s.shape[2]

    @pl.when(i == 0)
    def _():
        y_s[...] = jnp.zeros(y_s.shape, F32)

    cur = i & 1
    x1 = x_ref[0] + g1_ref[0] * _rms(y_s[1 - cur], nw1_ref[...])
    x1_ref[0] = x1
    h2 = _rms(x1, nw2_ref[...]) * (1.0 + sc2_ref[0]) + sh2_ref[0]
    hi = h2.astype(BF16)
    _store_row_tiled(h2_ref.at[0], _pack_halves(h2))
    lo = (h2 - hi.astype(F32)).astype(BF16)
    lg_ref[0] = _mm(hi, wrh_ref[...]) + (_mm(hi, wrl_ref[...]) + _mm(lo, wrh_ref[...]))
    for c in range(0, d, cw):
        y_s[cur, :, c:c + cw] = _mm(ym_ref[0], wo_ref[0:dm, c:c + cw]) + _mm(ya_ref[0], wo_ref[dm:, c:c + cw])


def _outproj(y_ml, y_at, x, g1, sh2, sc2, nw1, nw2, wo, wr, tm):
    b, s, d = x.shape
    dm = y_ml.shape[2]
    e = wr.shape[1]
    wr_hi = wr.astype(BF16)
    wr_lo = (wr - wr_hi.astype(F32)).astype(BF16)
    kern = functools.partial(_outproj_kernel, dm=dm, cw=min(d, 512))
    nt = s // tm
    n = b * nt

    def ahead(i):
        a = jnp.minimum(i, n - 1)
        return a // nt, a % nt

    def behind(i):
        a = jnp.maximum(i - 1, 0)
        return a // nt, a % nt

    def tok(w, which):
        return pl.BlockSpec((1, tm, w), lambda i: (*which(i), 0))

    vec = pl.BlockSpec((1, 1, d), lambda i: (behind(i)[0], 0, 0))
    nspec = pl.BlockSpec((1, d), lambda i: (0, 0))
    const = lambda r, c: pl.BlockSpec((r, c), lambda i: (0, 0), pipeline_mode=pl.Buffered(1))
    rc = d // 2 // LANES
    return pl.pallas_call(
        kern,
        grid=(n + 1,),
        in_specs=[tok(dm, ahead), tok(d - dm, ahead), tok(d, behind),
                  vec, vec, vec, nspec, nspec, const(d, d), const(d, e), const(d, e)],
        out_specs=[tok(d, behind), pl.BlockSpec((1, tm * rc, LANES), lambda i: (*behind(i), 0)), tok(e, behind)],
        out_shape=[jax.ShapeDtypeStruct((b, s, d), F32), jax.ShapeDtypeStruct((b, s * rc, LANES), U32),
                   jax.ShapeDtypeStruct((b, s, e), F32)],
        scratch_shapes=[pltpu.VMEM((2, tm, d), F32)],
        compiler_params=_cparams(("arbitrary",)),
        name="outproj",
    )(y_ml, y_at, x, g1, sh2, sc2, nw1, nw2, wo, wr_hi, wr_lo)


def _route_kernel(lg_ref, br_ref, eidx_ref, ew_ref, rank_ref, cnt_ref, carry_s, *, E, tr):
    i = pl.program_id(0)

    @pl.when(i == 0)
    def _():
        carry_s[...] = jnp.zeros(carry_s.shape, F32)

    scores = jax.nn.sigmoid(lg_ref[...].T)
    biased = scores + br_ref[:, 0:1]
    row = lax.broadcasted_iota(I32, (E, tr), 0).astype(F32)
    gs = E // N_GROUPS
    ninf = -jnp.inf
    grp = []
    for g in range(N_GROUPS):
        xg = biased[g * gs:(g + 1) * gs]
        rg = (lax.broadcasted_iota(I32, (gs, tr), 0) + g * gs).astype(F32)
        m1 = jnp.max(xg, axis=0, keepdims=True)
        i1 = jnp.min(jnp.where(xg == m1, rg, float(E)), axis=0, keepdims=True)
        m2 = jnp.max(jnp.where(rg == i1, ninf, xg), axis=0, keepdims=True)
        grp.append(m1 + m2)
    gsc = jnp.concatenate(grp, axis=0)
    gi = lax.broadcasted_iota(I32, (N_GROUPS, tr), 0)
    beaten = jnp.zeros((N_GROUPS, tr), I32)
    for g2 in range(N_GROUPS):
        sg = gsc[g2:g2 + 1]
        beaten = beaten + jnp.where((sg > gsc) | ((sg == gsc) & (gi > g2)), 1, 0)
    keep_g = jnp.where(beaten < TOPK_GROUPS, 1.0, 0.0)
    keep = jnp.concatenate([jnp.broadcast_to(keep_g[g:g + 1], (gs, tr)) for g in range(N_GROUPS)], axis=0)
    masked = jnp.where(keep > 0.5, biased, ninf)
    idxs, ws, sels = [], [], []
    for _ in range(TOP_K):
        m = jnp.max(masked, axis=0, keepdims=True)
        ik = jnp.min(jnp.where(masked == m, row, float(E)), axis=0, keepdims=True)
        sel = row == ik
        ws.append(jnp.sum(jnp.where(sel, scores, 0.0), axis=0, keepdims=True))
        idxs.append(ik)
        sels.append(sel)
        masked = jnp.where(sel, ninf, masked)
    w = jnp.concatenate(ws, axis=0)
    ew_ref[...] = w / jnp.sum(w, axis=0, keepdims=True) * ROUTE_SCALE
    eidx_ref[...] = jnp.concatenate(idxs, axis=0).astype(I32)

    assign = jnp.zeros((E, tr), F32)
    for sel in sels:
        assign = assign + jnp.where(sel, 1.0, 0.0)
    upper = jnp.where(lax.broadcasted_iota(I32, (tr, tr), 0) < lax.broadcasted_iota(I32, (tr, tr), 1), 1.0, 0.0)
    base = _mm(assign.astype(BF16), upper.astype(BF16)) + carry_s[:, 0:1]
    ranks = [jnp.sum(jnp.where(sel, base, 0.0), axis=0, keepdims=True) for sel in sels]
    rank_ref[...] = jnp.concatenate(ranks, axis=0).astype(I32)
    carry_s[...] = carry_s[...] + jnp.sum(assign, axis=1, keepdims=True)
    cnt_ref[...] = carry_s[...]


def _route(logits, b_router, tr):
    t, e = logits.shape
    kern = functools.partial(_route_kernel, E=e, tr=tr)
    kt = pl.BlockSpec((TOP_K, tr), lambda i: (0, i))
    return pl.pallas_call(
        kern,
        grid=(t // tr,),
        in_specs=[pl.BlockSpec((tr, e), lambda i: (i, 0)), pl.BlockSpec((e, LANES), lambda i: (0, 0))],
        out_specs=[kt, kt, kt, pl.BlockSpec((e, LANES), lambda i: (0, 0))],
        out_shape=[jax.ShapeDtypeStruct((TOP_K, t), I32), jax.ShapeDtypeStruct((TOP_K, t), F32),
                   jax.ShapeDtypeStruct((TOP_K, t), I32), jax.ShapeDtypeStruct((e, LANES), F32)],
        scratch_shapes=[pltpu.VMEM((e, LANES), F32)],
        compiler_params=_cparams(("arbitrary",)),
        name="route",
    )(logits, b_router)


def _positions_kernel(eidx_ref, rank_ref, ps_ref, pos_ref, *, E, tr):
    row = lax.broadcasted_iota(I32, (E, tr), 0)
    start = ps_ref[:, 0:1]
    rows = [jnp.sum(jnp.where(row == eidx_ref[k:k + 1, :], start, 0.0), axis=0, keepdims=True)
            for k in range(TOP_K)]
    pos_ref[...] = jnp.concatenate(rows, axis=0).astype(I32) + rank_ref[...]


def _positions(eidx, rank, pad_start, tr):
    t = eidx.shape[1]
    e = pad_start.shape[0]
    kern = functools.partial(_positions_kernel, E=e, tr=tr)
    kt = pl.BlockSpec((TOP_K, tr), lambda i: (0, i))
    return pl.pallas_call(
        kern,
        grid=(t // tr,),
        in_specs=[kt, kt, pl.BlockSpec((e, LANES), lambda i: (0, 0))],
        out_specs=kt,
        out_shape=jax.ShapeDtypeStruct((TOP_K, t), I32),
        compiler_params=_cparams(("arbitrary",)),
        name="positions",
    )(eidx, rank, jnp.broadcast_to(pad_start.astype(F32)[:, None], (e, LANES)))


def _dispatch_kernel(zstart_ref, zcount_ref, pos_hbm, h2_ref, xs_hbm, zero_s, pos_s, sem_p, sem_z, sem_r,
                     *, E, tt):
    i = pl.program_id(0)

    def zero_copy(e, j):
        return pltpu.make_async_copy(zero_s, xs_hbm.at[pl.ds(zstart_ref[e] + j * ZERO_ROWS, ZERO_ROWS)], sem_z)

    @pl.when(i == 0)
    def _():
        zero_s[...] = jnp.zeros(zero_s.shape, U32)

        def each(fn):
            def per_expert(e, c):
                def per_piece(j, c2):
                    fn(zero_copy(e, j))
                    return c2
                return lax.fori_loop(0, zcount_ref[e], per_piece, c)
            lax.fori_loop(0, E, per_expert, 0)

        each(lambda cp: cp.start())
        each(lambda cp: cp.wait())

    pos_copy = pltpu.make_async_copy(pos_hbm.at[i], pos_s, sem_p)
    pos_copy.start()
    pos_copy.wait()

    def row_copy(t, k):
        return pltpu.make_async_copy(h2_ref.at[t], xs_hbm.at[pos_s[k, t]], sem_r)

    def start_rows(t, c):
        for k in range(TOP_K):
            row_copy(t, k).start(priority=k % 2)
        return c

    def wait_rows(t, c):
        for k in range(TOP_K):
            row_copy(t, k).wait()
        return c

    lax.fori_loop(0, tt, start_rows, 0)
    lax.fori_loop(0, tt, wait_rows, 0)


def _dispatch(h2, pos_tiles, zstart, zcount, n_rows, tt):
    t, c, _ = h2.shape
    e = zstart.shape[0]
    kern = functools.partial(_dispatch_kernel, E=e, tt=tt)
    return pl.pallas_call(
        kern,
        grid_spec=pltpu.PrefetchScalarGridSpec(
            num_scalar_prefetch=2,
            grid=(t // tt,),
            in_specs=[pl.BlockSpec(memory_space=pl.ANY),
                      pl.BlockSpec((tt, c, LANES), lambda i, zs, zv: (i, 0, 0))],
            out_specs=pl.BlockSpec(memory_space=pl.ANY),
            scratch_shapes=[pltpu.VMEM((ZERO_ROWS, c, LANES), U32), pltpu.SMEM((TOP_K, tt), I32),
                            pltpu.SemaphoreType.DMA, pltpu.SemaphoreType.DMA, pltpu.SemaphoreType.DMA]),
        out_shape=jax.ShapeDtypeStruct((n_rows, c, LANES), U32),
        compiler_params=_cparams(("arbitrary",)),
        name="dispatch",
    )(zstart, zcount, pos_tiles, h2)


def _experts_kernel(be_ref, nu_ref, nxt_ref, ord_ref, half_ref, xs_ref, wg_hbm, wu_hbm, wd_hbm, ys_ref,
                    sg_s, su_s, sd_s, sem, *, E):
    b = pl.program_id(0)
    n_used = nu_ref[0]

    def copies(e, slot):
        out = []
        for m, (w_hbm, st) in enumerate(((wg_hbm, sg_s), (wu_hbm, su_s), (wd_hbm, sd_s))):
            rows = w_hbm.shape[1] // 2
            for p in range(2):
                out.append(pltpu.make_async_copy(w_hbm.at[e, pl.ds(p * rows, rows)],
                                                 st.at[slot, pl.ds(p * rows, rows)], sem.at[slot, 2 * m + p]))
        return out

    def start_all(cps):
        for n, cp in enumerate(cps):
            cp.start(priority=n % 2)

    @pl.when(b < n_used)
    def _():
        e = be_ref[b]
        prev = be_ref[jnp.maximum(b - 1, 0)]
        slot = lax.rem(ord_ref[e], WEIGHT_SLOTS)
        ne = nxt_ref[e]
        n2 = nxt_ref[jnp.minimum(ne, E - 1)]

        @pl.when(b == 0)
        def _():
            start_all(copies(e, slot))

            @pl.when(ne < E)
            def _():
                start_all(copies(ne, lax.rem(slot + 1, WEIGHT_SLOTS)))

        @pl.when((b == 0) | (e != prev))
        def _():
            for cp in copies(e, slot):
                cp.wait()

            @pl.when((ne < E) & (n2 < E))
            def _():
                start_all(copies(n2, lax.rem(slot + 2, WEIGHT_SLOTS)))

        def block(rows):
            span = pl.ds(0, rows * (xs_ref.shape[0] // EXPERT_BLOCK))
            parts = [_unpack_halves(ch) for ch in _load_row_tiled(xs_ref.at[span], rows)]
            x_lo = jnp.concatenate([p[0] for p in parts], axis=1).astype(BF16)
            x_hi = jnp.concatenate([p[1] for p in parts], axis=1).astype(BF16)
            half = x_lo.shape[1]
            gate = _mm(x_lo, sg_s[slot, 0:half, :]) + _mm(x_hi, sg_s[slot, half:, :])
            up = _mm(x_lo, su_s[slot, 0:half, :]) + _mm(x_hi, su_s[slot, half:, :])
            a = _silu(gate) * up
            _store_row_tiled(ys_ref.at[span], _pack_halves(_mm(a.astype(BF16), sd_s[slot])))

        @pl.when(half_ref[b] == 0)
        def _():
            block(EXPERT_BLOCK)

        @pl.when(half_ref[b] != 0)
        def _():
            block(EXPERT_BLOCK // 2)


def _experts(xs, block_e, n_used, next_used, used_ord, half_blk, wg, wu, wd):
    e, d, de = wg.shape
    c = d // 2 // LANES
    nblk = xs.shape[0] // (EXPERT_BLOCK * c)
    kern = functools.partial(_experts_kernel, E=e)

    def live(b, be, nu, nx, od, hf):
        return (jnp.minimum(b, jnp.maximum(nu[0] - 1, 0)), 0)

    hbm = pl.BlockSpec(memory_space=pl.ANY)
    return pl.pallas_call(
        kern,
        grid_spec=pltpu.PrefetchScalarGridSpec(
            num_scalar_prefetch=5,
            grid=(nblk,),
            in_specs=[pl.BlockSpec((EXPERT_BLOCK * c, LANES), live), hbm, hbm, hbm],
            out_specs=pl.BlockSpec((EXPERT_BLOCK * c, LANES), live),
            scratch_shapes=[pltpu.VMEM((WEIGHT_SLOTS, d, de), F32), pltpu.VMEM((WEIGHT_SLOTS, d, de), F32),
                            pltpu.VMEM((WEIGHT_SLOTS, de, d), F32), pltpu.SemaphoreType.DMA((WEIGHT_SLOTS, 6))]),
        out_shape=jax.ShapeDtypeStruct(xs.shape, U32),
        compiler_params=_cparams(("arbitrary",)),
        name="experts",
    )(block_e, n_used, next_used, used_ord, half_blk, xs, wg, wu, wd)


def _combine_kernel(pos_hbm, ys_hbm, ew_ref, x1_ref, h2_ref, g2_ref, nw3_ref, wsg_ref, wsu_ref, wsd_ref,
                    out_ref, buf_s, pos_s, sem_p, sem_r, *, tc, nt):
    tile = pl.program_id(0) * nt + pl.program_id(1)
    pos_copy = pltpu.make_async_copy(pos_hbm.at[tile], pos_s, sem_p)
    pos_copy.start()
    pos_copy.wait()

    def row_copy(t, k):
        c = ys_hbm.shape[1]
        return pltpu.make_async_copy(ys_hbm.at[pos_s[k, t]], buf_s.at[k, pl.ds(pl.multiple_of(t * c, c), c)], sem_r)

    def start_rows(t, c):
        for k in range(TOP_K):
            row_copy(t, k).start(priority=k % 2)
        return c

    def wait_rows(t, c):
        for k in range(TOP_K):
            row_copy(t, k).wait()
        return c

    lax.fori_loop(0, tc, start_rows, 0)
    parts = [_unpack_halves(ch) for ch in _load_row_tiled(h2_ref.at[0], tc)]
    h_lo = jnp.concatenate([p[0] for p in parts], axis=1).astype(BF16)
    h_hi = jnp.concatenate([p[1] for p in parts], axis=1).astype(BF16)
    half = h_lo.shape[1]
    nch = half // LANES
    gate = _mm(h_lo, wsg_ref[0:half, :]) + _mm(h_hi, wsg_ref[half:, :])
    up = _mm(h_lo, wsu_ref[0:half, :]) + _mm(h_hi, wsu_ref[half:, :])
    shared = _mm((_silu(gate) * up).astype(BF16), wsd_ref[...])
    lax.fori_loop(0, tc, wait_rows, 0)
    ew = ew_ref[0]
    acc = [shared[:, j * LANES:(j + 1) * LANES] for j in range(2 * nch)]
    for k in range(TOP_K):
        wk = ew[:, k:k + 1]
        for j, ch in enumerate(_load_row_tiled(buf_s.at[k], tc)):
            y_lo, y_hi = _unpack_halves(ch)
            acc[j] = acc[j] + y_lo * wk
            acc[nch + j] = acc[nch + j] + y_hi * wk
    out_ref[0] = x1_ref[0] + g2_ref[0] * _rms(jnp.concatenate(acc, axis=1), nw3_ref[...])


def _combine(pos_tiles, ys, ew, x1, h2, g2, nw3, wsg, wsu, wsd, tc):
    b, s, d = x1.shape
    ds_ = wsg.shape[1]
    nt = s // tc
    c = ys.shape[1]
    kern = functools.partial(_combine_kernel, tc=tc, nt=nt)
    tok = lambda w: pl.BlockSpec((1, tc, w), lambda bi, i: (bi, i, 0))
    return pl.pallas_call(
        kern,
        grid=(b, nt),
        in_specs=[pl.BlockSpec(memory_space=pl.ANY), pl.BlockSpec(memory_space=pl.ANY),
                  tok(TOP_K), tok(d), pl.BlockSpec((1, tc * c, LANES), lambda bi, i: (bi, i, 0)),
                  pl.BlockSpec((1, 1, d), lambda bi, i: (bi, 0, 0)),
                  pl.BlockSpec((1, d), lambda bi, i: (0, 0)),
                  pl.BlockSpec((d, ds_), lambda bi, i: (0, 0)),
                  pl.BlockSpec((d, ds_), lambda bi, i: (0, 0)),
                  pl.BlockSpec((ds_, d), lambda bi, i: (0, 0))],
        out_specs=tok(d),
        out_shape=jax.ShapeDtypeStruct((b, s, d), F32),
        scratch_shapes=[pltpu.VMEM((TOP_K, tc * c, LANES), U32), pltpu.SMEM((TOP_K, tc), I32),
                        pltpu.SemaphoreType.DMA, pltpu.SemaphoreType.DMA],
        compiler_params=_cparams(("arbitrary", "arbitrary")),
        name="combine",
    )(pos_tiles, ys, ew, x1, h2, g2, nw3, wsg, wsu, wsd)


def _tile(n, pref):
    t = min(n, pref)
    if n % t:
        raise NotImplementedError(f"size {n} is not a multiple of tile {t}")
    return t


def kernel(x, c, ctx, c_ctx, w_ada, b_ada, norms, w_in, ml_conv, ml_gate_b, ml_norm_w, attn_sink, w_out,
           w_router, b_router, w_exp_gate, w_exp_up, w_exp_down, w_sh_gate, w_sh_up, w_sh_down):
    if w_ada.shape[0] != 1:
        raise NotImplementedError("single-layer configuration only")
    b, s, d = x.shape
    lc = ctx.shape[1]
    t = b * s
    H = ML_HEADS
    dv = d // 2 // H
    dk = dv // 2
    dh = d // 2 // AT_HEADS
    qk_w, v_w = 2 * H * dk, H * dv
    ng = 4 * H
    aq_w, akv_w = AT_HEADS * dh, AT_KV_HEADS * dh
    e = w_router.shape[-1]
    nw = norms[0]

    rows = -(-(b + 1) // SUBLANES) * SUBLANES
    cond = jnp.zeros((rows, d), F32).at[:b].set(c).at[b].set(c_ctx)
    mods = _ada(cond, w_ada[0], b_ada[0][None, :])
    sh1, sc1, g1, sh2, sc2, g2 = [m[:b, None, :] for m in jnp.split(mods, 6, axis=-1)]
    csh1, csc1 = [jnp.broadcast_to(m[b:b + 1, None, :], (b, 1, d)) for m in jnp.split(mods, 6, axis=-1)[:2]]

    w0 = w_in[0]
    o_ml, o_g, o_q = 0, qk_w + 2 * v_w, qk_w + 2 * v_w + ng
    ml_w = qk_w + 2 * v_w
    wp = jnp.concatenate([w0[:, o_ml:o_g], w0[:, o_q:], w0[:, o_g:o_q], jnp.zeros((d, LANES - ng), F32)],
                         axis=1).astype(BF16)
    layout = (("ml", 0, ml_w), ("aq", ml_w, aq_w), ("ak", ml_w + aq_w, akv_w),
              ("av", ml_w + aq_w + akv_w, akv_w), ("g", ml_w + aq_w + 2 * akv_w, LANES))
    tabs = _rope_tables(s, dh)
    lat = _inproj(x, sh1, sc1, nw[0:1], wp, tabs, layout=layout, dh=dh, rope=True, tm=_tile(s, 512))
    ml, aq, ak, av, gates = (lat[n] for n in ("ml", "aq", "ak", "av", "g"))
    tm_c = _tile(lc, 512)
    cx = _inproj(ctx, csh1, csc1, nw[0:1], wp, tuple(tb[:tm_c] for tb in tabs), layout=layout, dh=dh,
                 rope=False, tm=tm_c)
    mlc, akc, avc, gates_c = (cx[n] for n in ("ml", "ak", "av", "g"))

    L = _tile(lc, 256)
    if s % L:
        raise NotImplementedError("sequence must be a multiple of the mLSTM chunk")
    gate_b = jnp.zeros((1, LANES), F32).at[0, :ng].set(ml_gate_b[0])
    gp = _gate_prep(gates, gate_b, L)
    gpc = _gate_prep(gates_c, gate_b, L)
    y_ml = _mlstm(ml, mlc, gp, gpc, ml_conv[0], ml_norm_w[0][None, :], L)

    y_at = _attn(aq, ak, av, akc, avc, attn_sink[0])

    x1, h2, logits = _outproj(y_ml, y_at, x, g1, sh2, sc2, nw[1:2], nw[2:3], w_out[0].astype(BF16),
                              w_router[0], _tile(s, 512))

    eidx, ew, rank, cnt = _route(logits.reshape(t, e), jnp.broadcast_to(b_router[0][:, None], (e, LANES)),
                                 _tile(t, 512))
    counts = cnt[:, 0].astype(I32)
    padded = (counts + EXPERT_BLOCK - 1) // EXPERT_BLOCK * EXPERT_BLOCK
    pad_end = jnp.cumsum(padded)
    pad_start = pad_end - padded
    n_blocks = -(-(t * TOP_K + e * (EXPERT_BLOCK - 1)) // EXPERT_BLOCK)
    n_rows = n_blocks * EXPERT_BLOCK
    blk_row = jnp.arange(n_blocks, dtype=I32) * EXPERT_BLOCK
    block_e = jnp.minimum(jnp.sum((pad_end[None, :] <= blk_row[:, None]).astype(I32), axis=1), e - 1)
    n_used = (pad_end[-1] // EXPERT_BLOCK).astype(I32)[None]
    used = counts > 0
    eid = jnp.arange(e, dtype=I32)
    next_used = jnp.concatenate([lax.cummin(jnp.where(used, eid, e), reverse=True)[1:], jnp.full((1,), e, I32)])
    used_ord = jnp.cumsum(used.astype(I32)) - 1
    pos = _positions(eidx, rank, pad_start, _tile(t, 512))

    tt = _tile(t, 512)
    pos_d = pos.reshape(TOP_K, t // tt, tt).transpose(1, 0, 2)
    rc = d // 2 // LANES
    zcount = (padded - counts + ZERO_ROWS - 1) // ZERO_ROWS
    xs = _dispatch(h2.reshape(t, rc, LANES), pos_d, (pad_end - zcount * ZERO_ROWS).astype(I32), zcount.astype(I32),
                   n_rows, tt)
    blk_valid = counts[block_e] - (blk_row - pad_start[block_e])
    half_blk = (blk_valid <= EXPERT_BLOCK // 2).astype(I32)
    ys = _experts(xs.reshape(n_rows * rc, LANES), block_e, n_used, next_used, used_ord, half_blk,
                  w_exp_gate[0], w_exp_up[0], w_exp_down[0])
    tc = _tile(s, 128)
    pos_c = pos.reshape(TOP_K, t // tc, tc).transpose(1, 0, 2)
    ew_t = ew.T.reshape(b, s, TOP_K)
    return _combine(pos_c, ys.reshape(n_rows, rc, LANES), ew_t, x1, h2, g2, nw[3:4],
                    w_sh_gate[0].astype(BF16), w_sh_up[0].astype(BF16), w_sh_down[0].astype(BF16), tc)
```

```python
import functools

import jax
import jax.numpy as jnp
from jax import lax
from jax.experimental import pallas as pl
from jax.experimental.pallas import tpu as pltpu

F32 = jnp.float32
BF16 = jnp.bfloat16
I32 = jnp.int32
U32 = jnp.uint32

LANES = 128
SUBLANES = 8
VMEM_LIMIT_BYTES = 56 * 1024 * 1024

NORM_EPS = 1e-6
ML_HEADS = 4
CONV_W = 5
AT_HEADS = 8
AT_KV_HEADS = 2
GRID_W = 64
WINDOW = 128
BLOCK_Q = 128
ROPE_THETA = 10000.0
N_GROUPS = 8
TOPK_GROUPS = 4
TOP_K = 8
ROUTE_SCALE = 2.5
EXPERT_BLOCK = 256
ZERO_ROWS = 32
WEIGHT_SLOTS = 3
NEG_BIG = -1e30


def _cparams(sem):
    return pltpu.CompilerParams(dimension_semantics=sem, vmem_limit_bytes=VMEM_LIMIT_BYTES)


def _rms(xf, w):
    return xf * lax.rsqrt(jnp.mean(xf * xf, axis=-1, keepdims=True) + NORM_EPS) * w


def _silu(x):
    return x * jax.nn.sigmoid(x)


def _mm(a, b):
    return jnp.dot(a, b, preferred_element_type=F32)


def _pack_halves(x):
    n = x.shape[1] // 2
    u = lax.bitcast_convert_type(x.astype(BF16).astype(F32), U32)
    return (u[:, :n] >> 16) | (u[:, n:] & jnp.uint32(0xFFFF0000))


def _unpack_halves(u):
    return (lax.bitcast_convert_type(u << 16, F32),
            lax.bitcast_convert_type(u & jnp.uint32(0xFFFF0000), F32))


def _store_row_tiled(ref2d, packed):
    r, n = packed.shape
    c = n // LANES
    for j in range(c):
        ref2d[pl.ds(j, r, stride=c), :] = packed[:, j * LANES:(j + 1) * LANES]


def _load_row_tiled(ref2d, r):
    c = ref2d.shape[0] // r
    return [ref2d[pl.ds(j, r, stride=c), :] for j in range(c)]


def _ada_kernel(c_ref, w_ref, b_ref, o_ref):
    s = _silu(c_ref[...]).astype(BF16)
    o_ref[...] = _mm(s, w_ref[...].astype(BF16)) + b_ref[...]


def _ada(cond, w, b):
    r, d = cond.shape
    n = w.shape[1]
    tn = min(n, 1024)
    return pl.pallas_call(
        _ada_kernel,
        grid=(n // tn,),
        in_specs=[pl.BlockSpec((r, d), lambda j: (0, 0)),
                  pl.BlockSpec((d, tn), lambda j: (0, j)),
                  pl.BlockSpec((1, tn), lambda j: (0, j))],
        out_specs=pl.BlockSpec((r, tn), lambda j: (0, j)),
        out_shape=jax.ShapeDtypeStruct((r, n), F32),
        compiler_params=_cparams(("arbitrary",)),
        name="ada",
    )(cond, w, b)


def _inproj_kernel(x_ref, sh_ref, sc_ref, nw_ref, wm_ref, wt_ref, rc_ref, ra_ref, rb_ref, *refs,
                   layout, dh, rope, qscale, cw):
    hb_ref = refs[-1]
    h = _rms(x_ref[0], nw_ref[...]) * (1.0 + sc_ref[0]) + sh_ref[0]
    hb_ref[...] = h.astype(BF16)
    for (name, which, col0, width), o_ref in zip(layout, refs):
        w_ref = (wm_ref, wt_ref)[which]
        for c in range(0, width, cw):
            step = min(cw, width - c)
            acc = _mm(hb_ref[...], w_ref[:, col0 + c:col0 + c + step])
            if rope and name in ("aq", "ak"):
                for hh in range(step // dh):
                    a = acc[:, hh * dh:(hh + 1) * dh]
                    r = (a * rc_ref[...] + pltpu.roll(a, dh - dh // 4, 1) * ra_ref[...]
                         + pltpu.roll(a, dh // 4, 1) * rb_ref[...])
                    if name == "aq":
                        r = r * qscale
                    o_ref[0, :, c + hh * dh:c + (hh + 1) * dh] = r.astype(o_ref.dtype)
            else:
                if name == "aq":
                    acc = acc * qscale
                o_ref[0, :, c:c + step] = acc.astype(o_ref.dtype)


def _inproj(x, sh, sc, nw, w_all, w_tail, tabs, *, layout, ml_w, dh, rope, tm):
    b, s, d = x.shape
    rc, ra, rb = tabs
    kern = functools.partial(_inproj_kernel, layout=layout, dh=dh, rope=rope, qscale=float(dh) ** -0.5, cw=512)
    vec = pl.BlockSpec((1, 1, d), lambda bi, i: (bi, 0, 0))
    tab = pl.BlockSpec((tm, dh), lambda bi, i: (i, 0))
    out_dt = dict(ml=BF16, aq=BF16, ak=BF16, av=BF16, g=F32)
    names = [n for n, _, _, _ in layout]
    widths = {n: w for n, _, _, w in layout}
    outs = pl.pallas_call(
        kern,
        grid=(b, s // tm),
        in_specs=[pl.BlockSpec((1, tm, d), lambda bi, i: (bi, i, 0)), vec, vec,
                  pl.BlockSpec((1, d), lambda bi, i: (0, 0)),
                  pl.BlockSpec((d, ml_w), lambda bi, i: (0, 0), pipeline_mode=pl.Buffered(1)),
                  pl.BlockSpec(w_tail.shape, lambda bi, i: (0, 0), pipeline_mode=pl.Buffered(1)),
                  tab, tab, tab],
        out_specs=[pl.BlockSpec((1, tm, widths[n]), lambda bi, i: (bi, i, 0)) for n in names],
        out_shape=[jax.ShapeDtypeStruct((b, s, widths[n]), out_dt[n]) for n in names],
        scratch_shapes=[pltpu.VMEM((tm, d), BF16)],
        compiler_params=_cparams(("arbitrary", "arbitrary")),
        name="inproj_rope" if rope else "inproj_ctx",
    )(x, sh, sc, nw, w_all, w_tail, rc, ra, rb)
    return dict(zip(names, outs))


def _rope_tables(s, dh):
    rows = s // GRID_W
    row = jnp.repeat(jnp.arange(rows), GRID_W)
    col = jnp.tile(jnp.arange(GRID_W), rows)
    nf = dh // 4
    freqs = ROPE_THETA ** (-jnp.arange(nf, dtype=F32) / nf)
    pos = jnp.stack([row, col], axis=-1).astype(F32)
    ang = pos[:, :, None] * freqs
    cos, sin = jnp.cos(ang), jnp.sin(ang)
    z = jnp.zeros_like(sin[:, 0])
    rc = jnp.concatenate([cos[:, 0], cos[:, 0], cos[:, 1], cos[:, 1]], axis=-1)
    ra = jnp.concatenate([-sin[:, 0], z, -sin[:, 1], z], axis=-1)
    rb = jnp.concatenate([z, sin[:, 0], z, sin[:, 1]], axis=-1)
    return rc, ra, rb


def _gate_prep_kernel(g_ref, b_ref, gl_ref, pp_ref, glt_ref, ppt_ref, tot_ref, *, L, H, nc):
    for c in range(nc):
        z = g_ref[0, c * L:(c + 1) * L, :] + b_ref[...]
        lane = lax.broadcasted_iota(I32, z.shape, 1)
        row = lax.broadcasted_iota(I32, z.shape, 0)
        is_f = ((lane // H) % 2 == 1) & (lane < 4 * H)
        log_sig = jnp.minimum(z, 0.0) - jnp.log1p(jnp.exp(-jnp.abs(z)))
        gl = jnp.where(lane < 4 * H, jnp.where(is_f, log_sig, z), 0.0)
        cs = gl
        k = 1
        while k < L:
            cs = cs + jnp.where(row >= k, pltpu.roll(cs, k, 0), 0.0)
            k *= 2
        tot = cs[L - 1:L, :]
        suf = tot - cs + gl
        pp = jnp.where(lane >= 2 * H, suf, cs)
        gl_ref[0, c] = gl
        pp_ref[0, c] = pp
        glt_ref[0, c] = gl.T
        ppt_ref[0, c] = pp.T
        tot_ref[0, c] = jnp.broadcast_to(tot, (SUBLANES, LANES))


def _gate_prep(g, gate_b, L):
    b, s, _ = g.shape
    nc = s // L
    kern = functools.partial(_gate_prep_kernel, L=L, H=ML_HEADS, nc=nc)
    col = pl.BlockSpec((1, nc, L, LANES), lambda bi: (bi, 0, 0, 0))
    rowb = pl.BlockSpec((1, nc, LANES, L), lambda bi: (bi, 0, 0, 0))
    return pl.pallas_call(
        kern,
        grid=(b,),
        in_specs=[pl.BlockSpec((1, s, LANES), lambda bi: (bi, 0, 0)),
                  pl.BlockSpec((1, LANES), lambda bi: (0, 0))],
        out_specs=[col, col, rowb, rowb, pl.BlockSpec((1, nc, SUBLANES, LANES), lambda bi: (bi, 0, 0, 0))],
        out_shape=[jax.ShapeDtypeStruct((b, nc, L, LANES), F32), jax.ShapeDtypeStruct((b, nc, L, LANES), F32),
                   jax.ShapeDtypeStruct((b, nc, LANES, L), F32), jax.ShapeDtypeStruct((b, nc, LANES, L), F32),
                   jax.ShapeDtypeStruct((b, nc, SUBLANES, LANES), F32)],
        compiler_params=_cparams(("arbitrary",)),
        name="gate_prep",
    )(g, gate_b)


CONV_PAD = 8


def _lane_pick(tile, ch):
    lane = lax.broadcasted_iota(I32, tile.shape, 1)
    return jnp.sum(jnp.where(lane == ch, tile, 0.0), axis=1, keepdims=True)


def _sublane_pick(tile, ch):
    sub = lax.broadcasted_iota(I32, tile.shape, 0)
    return jnp.sum(jnp.where(sub == ch, tile, 0.0), axis=0, keepdims=True)


def _mlstm_chunk(d, ch_i, ch_f, q, k, v, gl, pp, glt, ppt, tot8, ct_ref, n_ref, m_ref, mask_ref=None):
    L = k.shape[0]
    p_col = _lane_pick(pp, ch_f)
    li_col = _lane_pick(gl, ch_i)
    btot = _lane_pick(tot8[0:1], ch_f)
    m_old = m_ref[d][0:1, 0:1]
    ct = ct_ref[d]
    n_row = n_ref[d]
    h = None
    if q is not None:
        p_row = _sublane_pick(ppt, ch_f)
        li_row = _sublane_pick(glt, ch_i)
        logw = (p_col - p_row + li_row) + mask_ref[d]
        log_inter = p_col + m_old
        m_q = jnp.maximum(log_inter, jnp.max(logw, axis=1, keepdims=True))
        sqk = lax.dot_general(q, k, (((1,), (1,)), ((), ())), preferred_element_type=F32) * jnp.exp(logw - m_q)
        w_inter = jnp.exp(log_inter - m_q)
        num = _mm(sqk.astype(BF16), v) + w_inter * _mm(q, ct.astype(BF16))
        den = (jnp.sum(sqk, axis=1, keepdims=True)
               + w_inter * jnp.sum(q.astype(F32) * n_row, axis=1, keepdims=True))
        h = num / jnp.maximum(jnp.abs(den), jnp.exp(-m_q))
    log_w_end = btot - p_col + li_col
    m_new = jnp.maximum(btot + m_old, jnp.max(log_w_end, axis=0, keepdims=True))
    w_end = jnp.exp(log_w_end - m_new)
    decay = jnp.exp(btot + m_old - m_new)
    kw = k.astype(F32) * w_end
    ct_ref[d] = decay * ct + lax.dot_general(kw.astype(BF16), v, (((0,), (0,)), ((), ())),
                                             preferred_element_type=F32)
    n_ref[d] = decay * n_row + jnp.sum(kw, axis=0, keepdims=True)
    m_ref[d] = jnp.broadcast_to(m_new, (SUBLANES, LANES))
    return h


def _mlstm_kernel(q_ref, k_ref, v_ref, o_ref, kc_ref, vc_ref,
                  gl_ref, pp_ref, glt_ref, ppt_ref, tot_ref, glc_ref, ppc_ref, totc_ref,
                  cwq_ref, cwk_ref, nw_ref, mask_ref, y_ref,
                  xq_s, xk_s, xkc_s, qs_s, ks_s, kcs_s, hf_s, hb_s, ct_s, n_s, m_s,
                  *, L, H, S, Lc, dk, dv):
    hh = pl.program_id(1)
    nc, ncc = S // L, Lc // L
    zpad = jnp.zeros((CONV_PAD, dk), F32)

    def conv_silu(x_s, cw_ref, n_rows, out_s, scale):
        for c in range(n_rows // L):
            acc = jnp.zeros((L, dk), F32)
            for j in range(CONV_W):
                acc = acc + cw_ref[j:j + 1, :] * x_s[pl.ds(CONV_PAD + c * L + j - CONV_W // 2, L), :]
            out_s[c] = (_silu(acc) * scale).astype(BF16)

    def stage(x_s, src, n_rows):
        x_s[0:CONV_PAD, :] = zpad
        x_s[CONV_PAD + n_rows:2 * CONV_PAD + n_rows, :] = zpad
        x_s[CONV_PAD:CONV_PAD + n_rows, :] = src.astype(F32)

    stage(xq_s, q_ref[0], S)
    stage(xk_s, k_ref[0], S)
    stage(xkc_s, kc_ref[0], Lc)
    conv_silu(xq_s, cwq_ref, S, qs_s, float(dk) ** -0.5)
    conv_silu(xk_s, cwk_ref, S, ks_s, 1.0)
    conv_silu(xkc_s, cwk_ref, Lc, kcs_s, 1.0)

    ct_s[...] = jnp.zeros(ct_s.shape, F32)
    n_s[...] = jnp.zeros(n_s.shape, F32)
    m_s[...] = jnp.zeros(m_s.shape, F32)

    def chans(d):
        return d * 2 * H + hh, d * 2 * H + H + hh

    for d in (0, 1):
        ch_i, ch_f = chans(d)
        for c in (range(ncc) if d == 0 else range(ncc - 1, -1, -1)):
            _mlstm_chunk(d, ch_i, ch_f, None, kcs_s[c], vc_ref[0, c * L:(c + 1) * L, :],
                         glc_ref[0, c], ppc_ref[0, c], None, None, totc_ref[0, c], ct_s, n_s, m_s)

    def body(i, carry):
        for d in (0, 1):
            ch_i, ch_f = chans(d)
            c = i if d == 0 else nc - 1 - i
            r0 = pl.multiple_of(c * L, L)
            h = _mlstm_chunk(d, ch_i, ch_f, qs_s[c], ks_s[c], v_ref[0, pl.ds(r0, L), :],
                             gl_ref[0, c], pp_ref[0, c], glt_ref[0, c], ppt_ref[0, c], tot_ref[0, c],
                             ct_s, n_s, m_s, mask_ref)
            if d == 0:
                hf_s[c] = h
            else:
                hb_s[c] = h
        return carry

    lax.fori_loop(0, nc, body, 0)

    for c in range(nc):
        hs = hf_s[c] + hb_s[c]
        hn = _rms(hs, nw_ref[...])
        y_ref[0, c * L:(c + 1) * L, :] = (hn * jax.nn.sigmoid(o_ref[0, c * L:(c + 1) * L, :].astype(F32))).astype(BF16)


def _mlstm(ml, mlc, gp, gpc, conv_w, norm_w, L):
    b, s, _ = ml.shape
    lc = mlc.shape[1]
    H = ML_HEADS
    dv = norm_w.shape[1] // H
    dk = dv // 2
    nc, ncc = s // L, lc // L
    gl, pp, glt, ppt, tot = gp
    glc, ppc, _, _, totc = gpc
    kern = functools.partial(_mlstm_kernel, L=L, H=H, S=s, Lc=lc, dk=dk, dv=dv)
    tri = jnp.arange(L)[None, :] <= jnp.arange(L)[:, None]
    masks = jnp.where(jnp.stack([tri, tri.T]), 0.0, NEG_BIG).astype(F32)

    def colspec(rows, w, off):
        return pl.BlockSpec((1, rows, w), lambda bi, h: (bi, 0, off + h))

    def gspec(n, r, c):
        return pl.BlockSpec((1, n, r, c), lambda bi, h: (bi, 0, 0, 0))

    return pl.pallas_call(
        kern,
        grid=(b, H),
        in_specs=[colspec(s, dk, 0), colspec(s, dk, H), colspec(s, dv, H), colspec(s, dv, 2 * H),
                  colspec(lc, dk, H), colspec(lc, dv, H),
                  gspec(nc, L, LANES), gspec(nc, L, LANES), gspec(nc, LANES, L), gspec(nc, LANES, L),
                  gspec(nc, SUBLANES, LANES),
                  gspec(ncc, L, LANES), gspec(ncc, L, LANES), gspec(ncc, SUBLANES, LANES),
                  pl.BlockSpec((CONV_W, dk), lambda bi, h: (0, h)),
                  pl.BlockSpec((CONV_W, dk), lambda bi, h: (0, H + h)),
                  pl.BlockSpec((1, dv), lambda bi, h: (0, h)),
                  pl.BlockSpec((2, L, L), lambda bi, h: (0, 0, 0))],
        out_specs=pl.BlockSpec((1, s, dv), lambda bi, h: (bi, 0, h)),
        out_shape=jax.ShapeDtypeStruct((b, s, H * dv), BF16),
        scratch_shapes=[pltpu.VMEM((s + 2 * CONV_PAD, dk), F32), pltpu.VMEM((s + 2 * CONV_PAD, dk), F32),
                        pltpu.VMEM((lc + 2 * CONV_PAD, dk), F32),
                        pltpu.VMEM((nc, L, dk), BF16), pltpu.VMEM((nc, L, dk), BF16), pltpu.VMEM((ncc, L, dk), BF16),
                        pltpu.VMEM((nc, L, dv), F32), pltpu.VMEM((nc, L, dv), F32),
                        pltpu.VMEM((2, dk, dv), F32), pltpu.VMEM((2, 1, dk), F32),
                        pltpu.VMEM((2, SUBLANES, LANES), F32)],
        compiler_params=_cparams(("arbitrary", "arbitrary")),
        name="mlstm",
    )(ml, ml, ml, ml, mlc, mlc, gl, pp, glt, ppt, tot, glc, ppc, totc, conv_w, conv_w, norm_w, masks)


def _attn_kernel(sink_ref, q_ref, kp_ref, kc_ref, kn_ref, vp_ref, vc_ref, vn_ref, kx_ref, vx_ref, bias_ref,
                 o_ref, *, bq, dh, G):
    q = q_ref[0]
    for g in range(AT_KV_HEADS):
        sl = slice(g * dh, (g + 1) * dh)
        qg = jnp.concatenate([q[:, (g * G + j) * dh:(g * G + j + 1) * dh] for j in range(G)], axis=0)
        kcat = jnp.concatenate([kp_ref[0][:, sl], kc_ref[0][:, sl], kn_ref[0][:, sl], kx_ref[0][:, sl]], axis=0)
        vcat = jnp.concatenate([vp_ref[0][:, sl], vc_ref[0][:, sl], vn_ref[0][:, sl], vx_ref[0][:, sl]], axis=0)
        s = lax.dot_general(qg, kcat, (((1,), (1,)), ((), ())), preferred_element_type=F32) + bias_ref[0]
        ri = lax.broadcasted_iota(I32, (G * bq, 1), 0)
        sk = jnp.zeros((G * bq, 1), F32)
        for j in range(G):
            sk = jnp.where((ri >= j * bq) & (ri < (j + 1) * bq), sink_ref[g * G + j], sk)
        m = jnp.maximum(jnp.max(s, axis=1, keepdims=True), sk)
        p = jnp.exp(s - m)
        den = jnp.sum(p, axis=1, keepdims=True) + jnp.exp(sk - m)
        o = _mm(p.astype(BF16), vcat) / den
        for j in range(G):
            o_ref[0, :, (g * G + j) * dh:(g * G + j + 1) * dh] = o[j * bq:(j + 1) * bq, :].astype(BF16)


def _attn_bias(bq, lc, G):
    qoff = (jnp.arange(G * bq) % bq)[:, None]
    ci = jnp.arange(3 * bq + lc)[None, :]
    prev = (ci < bq) & (qoff <= ci)
    cur = (ci >= bq) & (ci < 2 * bq)
    nxt = (ci >= 2 * bq) & (ci < 3 * bq) & (ci - 2 * bq <= qoff)
    ctx = ci >= 3 * bq
    inner = prev | cur | nxt | ctx
    first = cur | nxt | ctx
    last = prev | cur | ctx
    if WINDOW != bq:
        raise NotImplementedError("window must equal the query block")
    return jnp.where(jnp.stack([inner, first, last]), 0.0, NEG_BIG).astype(F32)


def _attn(aq, ak, av, akc, avc, sink):
    b, s, hd = aq.shape
    dh = hd // AT_HEADS
    G = AT_HEADS // AT_KV_HEADS
    bq = BLOCK_Q
    nb = s // bq
    lc = akc.shape[1]
    kvw = AT_KV_HEADS * dh
    bias = _attn_bias(bq, lc, G)
    if nb < 2:
        raise NotImplementedError("needs at least two query blocks")
    kern = functools.partial(_attn_kernel, bq=bq, dh=dh, G=G)
    prev = pl.BlockSpec((1, bq, kvw), lambda bi, n: (bi, jnp.maximum(n - 1, 0), 0))
    cur = pl.BlockSpec((1, bq, kvw), lambda bi, n: (bi, n, 0))
    nxt = pl.BlockSpec((1, bq, kvw), lambda bi, n: (bi, jnp.minimum(n + 1, nb - 1), 0))
    cx = pl.BlockSpec((1, lc, kvw), lambda bi, n: (bi, 0, 0))
    return pl.pallas_call(
        kern,
        grid=(b, nb),
        in_specs=[pl.BlockSpec(memory_space=pltpu.SMEM),
                  pl.BlockSpec((1, bq, hd), lambda bi, n: (bi, n, 0)),
                  prev, cur, nxt, prev, cur, nxt, cx, cx,
                  pl.BlockSpec((1, G * bq, 3 * bq + lc),
                               lambda bi, n: (jnp.where(n == 0, 1, jnp.where(n == nb - 1, 2, 0)), 0, 0))],
        out_specs=pl.BlockSpec((1, bq, hd), lambda bi, n: (bi, n, 0)),
        out_shape=jax.ShapeDtypeStruct((b, s, hd), BF16),
        compiler_params=_cparams(("arbitrary", "arbitrary")),
        name="attn",
    )(sink, aq, ak, ak, ak, av, av, av, akc, avc, bias)


def _outproj_kernel(ym_ref, ya_ref, x_ref, g1_ref, sh2_ref, sc2_ref, nw1_ref, nw2_ref, wo_ref, wrh_ref, wrl_ref,
                    x1_ref, h2_ref, lg_ref, y_s, *, dm, cw):
    i = pl.program_id(0)
    d = y_s.shape[2]

    @pl.when(i == 0)
    def _():
        y_s[...] = jnp.zeros(y_s.shape, F32)

    cur = i & 1
    x1 = x_ref[0] + g1_ref[0] * _rms(y_s[1 - cur], nw1_ref[...])
    x1_ref[0] = x1
    h2 = _rms(x1, nw2_ref[...]) * (1.0 + sc2_ref[0]) + sh2_ref[0]
    hi = h2.astype(BF16)
    _store_row_tiled(h2_ref.at[0], _pack_halves(h2))
    lo = (h2 - hi.astype(F32)).astype(BF16)
    lg_ref[0] = _mm(hi, wrh_ref[...]) + (_mm(hi, wrl_ref[...]) + _mm(lo, wrh_ref[...]))
    for c in range(0, d, cw):
        y_s[cur, :, c:c + cw] = _mm(ym_ref[0], wo_ref[0:dm, c:c + cw]) + _mm(ya_ref[0], wo_ref[dm:, c:c + cw])


def _outproj(y_ml, y_at, x, g1, sh2, sc2, nw1, nw2, wo, wr, tm):
    b, s, d = x.shape
    dm = y_ml.shape[2]
    e = wr.shape[1]
    wr_hi = wr.astype(BF16)
    wr_lo = (wr - wr_hi.astype(F32)).astype(BF16)
    kern = functools.partial(_outproj_kernel, dm=dm, cw=min(d, 512))
    nt = s // tm
    n = b * nt

    def ahead(i):
        a = jnp.minimum(i, n - 1)
        return a // nt, a % nt

    def behind(i):
        a = jnp.maximum(i - 1, 0)
        return a // nt, a % nt

    def tok(w, which):
        return pl.BlockSpec((1, tm, w), lambda i: (*which(i), 0))

    vec = pl.BlockSpec((1, 1, d), lambda i: (behind(i)[0], 0, 0))
    nspec = pl.BlockSpec((1, d), lambda i: (0, 0))
    const = lambda r, c: pl.BlockSpec((r, c), lambda i: (0, 0), pipeline_mode=pl.Buffered(1))
    rc = d // 2 // LANES
    return pl.pallas_call(
        kern,
        grid=(n + 1,),
        in_specs=[tok(dm, ahead), tok(d - dm, ahead), tok(d, behind),
                  vec, vec, vec, nspec, nspec, const(d, d), const(d, e), const(d, e)],
        out_specs=[tok(d, behind), pl.BlockSpec((1, tm * rc, LANES), lambda i: (*behind(i), 0)), tok(e, behind)],
        out_shape=[jax.ShapeDtypeStruct((b, s, d), F32), jax.ShapeDtypeStruct((b, s * rc, LANES), U32),
                   jax.ShapeDtypeStruct((b, s, e), F32)],
        scratch_shapes=[pltpu.VMEM((2, tm, d), F32)],
        compiler_params=_cparams(("arbitrary",)),
        name="outproj",
    )(y_ml, y_at, x, g1, sh2, sc2, nw1, nw2, wo, wr_hi, wr_lo)


def _route_kernel(lg_ref, br_ref, eidx_ref, ew_ref, rank_ref, cnt_ref, carry_s, *, E, tr):
    i = pl.program_id(0)

    @pl.when(i == 0)
    def _():
        carry_s[...] = jnp.zeros(carry_s.shape, F32)

    scores = jax.nn.sigmoid(lg_ref[...].T)
    biased = scores + br_ref[:, 0:1]
    row = lax.broadcasted_iota(I32, (E, tr), 0).astype(F32)
    gs = E // N_GROUPS
    ninf = -jnp.inf
    grp = []
    for g in range(N_GROUPS):
        xg = biased[g * gs:(g + 1) * gs]
        rg = (lax.broadcasted_iota(I32, (gs, tr), 0) + g * gs).astype(F32)
        m1 = jnp.max(xg, axis=0, keepdims=True)
        i1 = jnp.min(jnp.where(xg == m1, rg, float(E)), axis=0, keepdims=True)
        m2 = jnp.max(jnp.where(rg == i1, ninf, xg), axis=0, keepdims=True)
        grp.append(m1 + m2)
    gsc = jnp.concatenate(grp, axis=0)
    gi = lax.broadcasted_iota(I32, (N_GROUPS, tr), 0)
    beaten = jnp.zeros((N_GROUPS, tr), I32)
    for g2 in range(N_GROUPS):
        sg = gsc[g2:g2 + 1]
        beaten = beaten + jnp.where((sg > gsc) | ((sg == gsc) & (gi > g2)), 1, 0)
    keep_g = jnp.where(beaten < TOPK_GROUPS, 1.0, 0.0)
    keep = jnp.concatenate([jnp.broadcast_to(keep_g[g:g + 1], (gs, tr)) for g in range(N_GROUPS)], axis=0)
    masked = jnp.where(keep > 0.5, biased, ninf)
    idxs, ws, sels = [], [], []
    for _ in range(TOP_K):
        m = jnp.max(masked, axis=0, keepdims=True)
        ik = jnp.min(jnp.where(masked == m, row, float(E)), axis=0, keepdims=True)
        sel = row == ik
        ws.append(jnp.sum(jnp.where(sel, scores, 0.0), axis=0, keepdims=True))
        idxs.append(ik)
        sels.append(sel)
        masked = jnp.where(sel, ninf, masked)
    w = jnp.concatenate(ws, axis=0)
    ew_ref[...] = w / jnp.sum(w, axis=0, keepdims=True) * ROUTE_SCALE
    eidx_ref[...] = jnp.concatenate(idxs, axis=0).astype(I32)

    assign = jnp.zeros((E, tr), F32)
    for sel in sels:
        assign = assign + jnp.where(sel, 1.0, 0.0)
    upper = jnp.where(lax.broadcasted_iota(I32, (tr, tr), 0) < lax.broadcasted_iota(I32, (tr, tr), 1), 1.0, 0.0)
    base = _mm(assign.astype(BF16), upper.astype(BF16)) + carry_s[:, 0:1]
    ranks = [jnp.sum(jnp.where(sel, base, 0.0), axis=0, keepdims=True) for sel in sels]
    rank_ref[...] = jnp.concatenate(ranks, axis=0).astype(I32)
    carry_s[...] = carry_s[...] + jnp.sum(assign, axis=1, keepdims=True)
    cnt_ref[...] = carry_s[...]


def _route(logits, b_router, tr):
    t, e = logits.shape
    kern = functools.partial(_route_kernel, E=e, tr=tr)
    kt = pl.BlockSpec((TOP_K, tr), lambda i: (0, i))
    return pl.pallas_call(
        kern,
        grid=(t // tr,),
        in_specs=[pl.BlockSpec((tr, e), lambda i: (i, 0)), pl.BlockSpec((e, LANES), lambda i: (0, 0))],
        out_specs=[kt, kt, kt, pl.BlockSpec((e, LANES), lambda i: (0, 0))],
        out_shape=[jax.ShapeDtypeStruct((TOP_K, t), I32), jax.ShapeDtypeStruct((TOP_K, t), F32),
                   jax.ShapeDtypeStruct((TOP_K, t), I32), jax.ShapeDtypeStruct((e, LANES), F32)],
        scratch_shapes=[pltpu.VMEM((e, LANES), F32)],
        compiler_params=_cparams(("arbitrary",)),
        name="route",
    )(logits, b_router)


def _positions_kernel(eidx_ref, rank_ref, ps_ref, pos_ref, *, E, tr):
    row = lax.broadcasted_iota(I32, (E, tr), 0)
    start = ps_ref[:, 0:1]
    rows = [jnp.sum(jnp.where(row == eidx_ref[k:k + 1, :], start, 0.0), axis=0, keepdims=True)
            for k in range(TOP_K)]
    pos_ref[...] = jnp.concatenate(rows, axis=0).astype(I32) + rank_ref[...]


def _positions(eidx, rank, pad_start, tr):
    t = eidx.shape[1]
    e = pad_start.shape[0]
    kern = functools.partial(_positions_kernel, E=e, tr=tr)
    kt = pl.BlockSpec((TOP_K, tr), lambda i: (0, i))
    return pl.pallas_call(
        kern,
        grid=(t // tr,),
        in_specs=[kt, kt, pl.BlockSpec((e, LANES), lambda i: (0, 0))],
        out_specs=kt,
        out_shape=jax.ShapeDtypeStruct((TOP_K, t), I32),
        compiler_params=_cparams(("arbitrary",)),
        name="positions",
    )(eidx, rank, jnp.broadcast_to(pad_start.astype(F32)[:, None], (e, LANES)))


def _dispatch_kernel(zstart_ref, zcount_ref, pos_hbm, h2_ref, xs_hbm, zero_s, pos_s, sem_p, sem_z, sem_r,
                     *, E, tt):
    i = pl.program_id(0)

    def zero_copy(e, j):
        return pltpu.make_async_copy(zero_s, xs_hbm.at[pl.ds(zstart_ref[e] + j * ZERO_ROWS, ZERO_ROWS)], sem_z)

    @pl.when(i == 0)
    def _():
        zero_s[...] = jnp.zeros(zero_s.shape, U32)

        def each(fn):
            def per_expert(e, c):
                def per_piece(j, c2):
                    fn(zero_copy(e, j))
                    return c2
                return lax.fori_loop(0, zcount_ref[e], per_piece, c)
            lax.fori_loop(0, E, per_expert, 0)

        each(lambda cp: cp.start())
        each(lambda cp: cp.wait())

    pos_copy = pltpu.make_async_copy(pos_hbm.at[i], pos_s, sem_p)
    pos_copy.start()
    pos_copy.wait()

    def row_copy(t, k):
        return pltpu.make_async_copy(h2_ref.at[t], xs_hbm.at[pos_s[k, t]], sem_r)

    def start_rows(t, c):
        for k in range(TOP_K):
            row_copy(t, k).start(priority=k % 2)
        return c

    def wait_rows(t, c):
        for k in range(TOP_K):
            row_copy(t, k).wait()
        return c

    lax.fori_loop(0, tt, start_rows, 0)
    lax.fori_loop(0, tt, wait_rows, 0)


def _dispatch(h2, pos_tiles, zstart, zcount, n_rows, tt):
    t, c, _ = h2.shape
    e = zstart.shape[0]
    kern = functools.partial(_dispatch_kernel, E=e, tt=tt)
    return pl.pallas_call(
        kern,
        grid_spec=pltpu.PrefetchScalarGridSpec(
            num_scalar_prefetch=2,
            grid=(t // tt,),
            in_specs=[pl.BlockSpec(memory_space=pl.ANY),
                      pl.BlockSpec((tt, c, LANES), lambda i, zs, zv: (i, 0, 0))],
            out_specs=pl.BlockSpec(memory_space=pl.ANY),
            scratch_shapes=[pltpu.VMEM((ZERO_ROWS, c, LANES), U32), pltpu.SMEM((TOP_K, tt), I32),
                            pltpu.SemaphoreType.DMA, pltpu.SemaphoreType.DMA, pltpu.SemaphoreType.DMA]),
        out_shape=jax.ShapeDtypeStruct((n_rows, c, LANES), U32),
        compiler_params=_cparams(("arbitrary",)),
        name="dispatch",
    )(zstart, zcount, pos_tiles, h2)


def _experts_kernel(be_ref, nu_ref, nxt_ref, ord_ref, half_ref, xs_ref, wg_hbm, wu_hbm, wd_hbm, ys_ref,
                    sg_s, su_s, sd_s, sem, *, E):
    b = pl.program_id(0)
    n_used = nu_ref[0]

    def copies(e, slot):
        out = []
        for m, (w_hbm, st) in enumerate(((wg_hbm, sg_s), (wu_hbm, su_s), (wd_hbm, sd_s))):
            rows = w_hbm.shape[1] // 2
            for p in range(2):
                out.append(pltpu.make_async_copy(w_hbm.at[e, pl.ds(p * rows, rows)],
                                                 st.at[slot, pl.ds(p * rows, rows)], sem.at[slot, 2 * m + p]))
        return out

    def start_all(cps):
        for n, cp in enumerate(cps):
            cp.start(priority=n % 2)

    @pl.when(b < n_used)
    def _():
        e = be_ref[b]
        prev = be_ref[jnp.maximum(b - 1, 0)]
        slot = lax.rem(ord_ref[e], WEIGHT_SLOTS)
        ne = nxt_ref[e]
        n2 = nxt_ref[jnp.minimum(ne, E - 1)]

        @pl.when(b == 0)
        def _():
            start_all(copies(e, slot))

            @pl.when(ne < E)
            def _():
                start_all(copies(ne, lax.rem(slot + 1, WEIGHT_SLOTS)))

        @pl.when((b == 0) | (e != prev))
        def _():
            for cp in copies(e, slot):
                cp.wait()

            @pl.when((ne < E) & (n2 < E))
            def _():
                start_all(copies(n2, lax.rem(slot + 2, WEIGHT_SLOTS)))

        def block(rows):
            span = pl.ds(0, rows * (xs_ref.shape[0] // EXPERT_BLOCK))
            parts = [_unpack_halves(ch) for ch in _load_row_tiled(xs_ref.at[span], rows)]
            x_lo = jnp.concatenate([p[0] for p in parts], axis=1).astype(BF16)
            x_hi = jnp.concatenate([p[1] for p in parts], axis=1).astype(BF16)
            half = x_lo.shape[1]
            gate = _mm(x_lo, sg_s[slot, 0:half, :]) + _mm(x_hi, sg_s[slot, half:, :])
            up = _mm(x_lo, su_s[slot, 0:half, :]) + _mm(x_hi, su_s[slot, half:, :])
            a = _silu(gate) * up
            _store_row_tiled(ys_ref.at[span], _pack_halves(_mm(a.astype(BF16), sd_s[slot])))

        @pl.when(half_ref[b] == 0)
        def _():
            block(EXPERT_BLOCK)

        @pl.when(half_ref[b] != 0)
        def _():
            block(EXPERT_BLOCK // 2)


def _experts(xs, block_e, n_used, next_used, used_ord, half_blk, wg, wu, wd):
    e, d, de = wg.shape
    c = d // 2 // LANES
    nblk = xs.shape[0] // (EXPERT_BLOCK * c)
    kern = functools.partial(_experts_kernel, E=e)

    def live(b, be, nu, nx, od, hf):
        return (jnp.minimum(b, jnp.maximum(nu[0] - 1, 0)), 0)

    hbm = pl.BlockSpec(memory_space=pl.ANY)
    return pl.pallas_call(
        kern,
        grid_spec=pltpu.PrefetchScalarGridSpec(
            num_scalar_prefetch=5,
            grid=(nblk,),
            in_specs=[pl.BlockSpec((EXPERT_BLOCK * c, LANES), live), hbm, hbm, hbm],
            out_specs=pl.BlockSpec((EXPERT_BLOCK * c, LANES), live),
            scratch_shapes=[pltpu.VMEM((WEIGHT_SLOTS, d, de), F32), pltpu.VMEM((WEIGHT_SLOTS, d, de), F32),
                            pltpu.VMEM((WEIGHT_SLOTS, de, d), F32), pltpu.SemaphoreType.DMA((WEIGHT_SLOTS, 6))]),
        out_shape=jax.ShapeDtypeStruct(xs.shape, U32),
        compiler_params=_cparams(("arbitrary",)),
        name="experts",
    )(block_e, n_used, next_used, used_ord, half_blk, xs, wg, wu, wd)


def _combine_kernel(pos_hbm, ys_hbm, ew_ref, x1_ref, h2_ref, g2_ref, nw3_ref, wsg_ref, wsu_ref, wsd_ref,
                    out_ref, buf_s, pos_s, sem_p, sem_r, *, tc, nt):
    tile = pl.program_id(0) * nt + pl.program_id(1)
    pos_copy = pltpu.make_async_copy(pos_hbm.at[tile], pos_s, sem_p)
    pos_copy.start()
    pos_copy.wait()

    def row_copy(t, k):
        c = ys_hbm.shape[1]
        return pltpu.make_async_copy(ys_hbm.at[pos_s[k, t]], buf_s.at[k, pl.ds(pl.multiple_of(t * c, c), c)], sem_r)

    def start_rows(t, c):
        for k in range(TOP_K):
            row_copy(t, k).start(priority=k % 2)
        return c

    def wait_rows(t, c):
        for k in range(TOP_K):
            row_copy(t, k).wait()
        return c

    lax.fori_loop(0, tc, start_rows, 0)
    parts = [_unpack_halves(ch) for ch in _load_row_tiled(h2_ref.at[0], tc)]
    h_lo = jnp.concatenate([p[0] for p in parts], axis=1).astype(BF16)
    h_hi = jnp.concatenate([p[1] for p in parts], axis=1).astype(BF16)
    half = h_lo.shape[1]
    nch = half // LANES
    gate = _mm(h_lo, wsg_ref[0:half, :]) + _mm(h_hi, wsg_ref[half:, :])
    up = _mm(h_lo, wsu_ref[0:half, :]) + _mm(h_hi, wsu_ref[half:, :])
    shared = _mm((_silu(gate) * up).astype(BF16), wsd_ref[...])
    lax.fori_loop(0, tc, wait_rows, 0)
    ew = ew_ref[0]
    acc = [shared[:, j * LANES:(j + 1) * LANES] for j in range(2 * nch)]
    for k in range(TOP_K):
        wk = ew[:, k:k + 1]
        for j, ch in enumerate(_load_row_tiled(buf_s.at[k], tc)):
            y_lo, y_hi = _unpack_halves(ch)
            acc[j] = acc[j] + y_lo * wk
            acc[nch + j] = acc[nch + j] + y_hi * wk
    out_ref[0] = x1_ref[0] + g2_ref[0] * _rms(jnp.concatenate(acc, axis=1), nw3_ref[...])


def _combine(pos_tiles, ys, ew, x1, h2, g2, nw3, wsg, wsu, wsd, tc):
    b, s, d = x1.shape
    ds_ = wsg.shape[1]
    nt = s // tc
    c = ys.shape[1]
    kern = functools.partial(_combine_kernel, tc=tc, nt=nt)
    tok = lambda w: pl.BlockSpec((1, tc, w), lambda bi, i: (bi, i, 0))
    return pl.pallas_call(
        kern,
        grid=(b, nt),
        in_specs=[pl.BlockSpec(memory_space=pl.ANY), pl.BlockSpec(memory_space=pl.ANY),
                  tok(TOP_K), tok(d), pl.BlockSpec((1, tc * c, LANES), lambda bi, i: (bi, i, 0)),
                  pl.BlockSpec((1, 1, d), lambda bi, i: (bi, 0, 0)),
                  pl.BlockSpec((1, d), lambda bi, i: (0, 0)),
                  pl.BlockSpec((d, ds_), lambda bi, i: (0, 0)),
                  pl.BlockSpec((d, ds_), lambda bi, i: (0, 0)),
                  pl.BlockSpec((ds_, d), lambda bi, i: (0, 0))],
        out_specs=tok(d),
        out_shape=jax.ShapeDtypeStruct((b, s, d), F32),
        scratch_shapes=[pltpu.VMEM((TOP_K, tc * c, LANES), U32), pltpu.SMEM((TOP_K, tc), I32),
                        pltpu.SemaphoreType.DMA, pltpu.SemaphoreType.DMA],
        compiler_params=_cparams(("arbitrary", "arbitrary")),
        name="combine",
    )(pos_tiles, ys, ew, x1, h2, g2, nw3, wsg, wsu, wsd)


def _tile(n, pref):
    t = min(n, pref)
    if n % t:
        raise NotImplementedError(f"size {n} is not a multiple of tile {t}")
    return t


def kernel(x, c, ctx, c_ctx, w_ada, b_ada, norms, w_in, ml_conv, ml_gate_b, ml_norm_w, attn_sink, w_out,
           w_router, b_router, w_exp_gate, w_exp_up, w_exp_down, w_sh_gate, w_sh_up, w_sh_down):
    if w_ada.shape[0] != 1:
        raise NotImplementedError("single-layer configuration only")
    b, s, d = x.shape
    lc = ctx.shape[1]
    t = b * s
    H = ML_HEADS
    dv = d // 2 // H
    dk = dv // 2
    dh = d // 2 // AT_HEADS
    qk_w, v_w = 2 * H * dk, H * dv
    ng = 4 * H
    aq_w, akv_w = AT_HEADS * dh, AT_KV_HEADS * dh
    e = w_router.shape[-1]
    nw = norms[0]

    rows = -(-(b + 1) // SUBLANES) * SUBLANES
    cond = jnp.zeros((rows, d), F32).at[:b].set(c).at[b].set(c_ctx)
    mods = _ada(cond, w_ada[0], b_ada[0][None, :])
    sh1, sc1, g1, sh2, sc2, g2 = [m[:b, None, :] for m in jnp.split(mods, 6, axis=-1)]
    csh1, csc1 = [jnp.broadcast_to(m[b:b + 1, None, :], (b, 1, d)) for m in jnp.split(mods, 6, axis=-1)[:2]]

    w0 = w_in[0].astype(BF16)
    ml_w = qk_w + 2 * v_w
    o_g, o_q = ml_w, ml_w + ng
    if ml_w % LANES:
        raise NotImplementedError("mLSTM projection width must be lane aligned")
    w_tail = jnp.concatenate([w0[:, o_q:], w0[:, o_g:o_q], jnp.zeros((d, LANES - ng), BF16)], axis=1)
    layout = (("ml", 0, 0, ml_w), ("aq", 1, 0, aq_w), ("ak", 1, aq_w, akv_w),
              ("av", 1, aq_w + akv_w, akv_w), ("g", 1, aq_w + 2 * akv_w, LANES))
    tabs = _rope_tables(s, dh)
    proj = functools.partial(_inproj, layout=layout, ml_w=ml_w, dh=dh)
    lat = proj(x, sh1, sc1, nw[0:1], w0, w_tail, tabs, rope=True, tm=_tile(s, 512))
    ml, aq, ak, av, gates = (lat[n] for n in ("ml", "aq", "ak", "av", "g"))
    tm_c = _tile(lc, 512)
    cx = proj(ctx, csh1, csc1, nw[0:1], w0, w_tail, tuple(tb[:tm_c] for tb in tabs), rope=False, tm=tm_c)
    mlc, akc, avc, gates_c = (cx[n] for n in ("ml", "ak", "av", "g"))

    L = _tile(lc, 256)
    if s % L:
        raise NotImplementedError("sequence must be a multiple of the mLSTM chunk")
    gate_b = jnp.zeros((1, LANES), F32).at[0, :ng].set(ml_gate_b[0])
    gp = _gate_prep(gates, gate_b, L)
    gpc = _gate_prep(gates_c, gate_b, L)
    y_ml = _mlstm(ml, mlc, gp, gpc, ml_conv[0], ml_norm_w[0][None, :], L)

    y_at = _attn(aq, ak, av, akc, avc, attn_sink[0])

    x1, h2, logits = _outproj(y_ml, y_at, x, g1, sh2, sc2, nw[1:2], nw[2:3], w_out[0].astype(BF16),
                              w_router[0], _tile(s, 512))

    eidx, ew, rank, cnt = _route(logits.reshape(t, e), jnp.broadcast_to(b_router[0][:, None], (e, LANES)),
                                 _tile(t, 512))
    counts = cnt[:, 0].astype(I32)
    padded = (counts + EXPERT_BLOCK - 1) // EXPERT_BLOCK * EXPERT_BLOCK
    pad_end = jnp.cumsum(padded)
    pad_start = pad_end - padded
    n_blocks = -(-(t * TOP_K + e * (EXPERT_BLOCK - 1)) // EXPERT_BLOCK)
    n_rows = n_blocks * EXPERT_BLOCK
    blk_row = jnp.arange(n_blocks, dtype=I32) * EXPERT_BLOCK
    block_e = jnp.minimum(jnp.sum((pad_end[None, :] <= blk_row[:, None]).astype(I32), axis=1), e - 1)
    n_used = (pad_end[-1] // EXPERT_BLOCK).astype(I32)[None]
    used = counts > 0
    eid = jnp.arange(e, dtype=I32)
    next_used = jnp.concatenate([lax.cummin(jnp.where(used, eid, e), reverse=True)[1:], jnp.full((1,), e, I32)])
    used_ord = jnp.cumsum(used.astype(I32)) - 1
    pos = _positions(eidx, rank, pad_start, _tile(t, 512))

    tt = _tile(t, 512)
    pos_d = pos.reshape(TOP_K, t // tt, tt).transpose(1, 0, 2)
    rc = d // 2 // LANES
    zcount = (padded - counts + ZERO_ROWS - 1) // ZERO_ROWS
    xs = _dispatch(h2.reshape(t, rc, LANES), pos_d, (pad_end - zcount * ZERO_ROWS).astype(I32), zcount.astype(I32),
                   n_rows, tt)
    blk_valid = counts[block_e] - (blk_row - pad_start[block_e])
    half_blk = (blk_valid <= EXPERT_BLOCK // 2).astype(I32)
    ys = _experts(xs.reshape(n_rows * rc, LANES), block_e, n_used, next_used, used_ord, half_blk,
                  w_exp_gate[0], w_exp_up[0], w_exp_down[0])
    tc = _tile(s, 128)
    pos_c = pos.reshape(TOP_K, t // tc, tc).transpose(1, 0, 2)
    ew_t = ew.T.reshape(b, s, TOP_K)
    return _combine(pos_c, ys.reshape(n_rows, rc, LANES), ew_t, x1, h2, g2, nw[3:4],
                    w_sh_gate[0].astype(BF16), w_sh_up[0].astype(BF16), w_sh_down[0].astype(BF16), tc)
```

```python
import functools

import jax
import jax.numpy as jnp
from jax import lax
from jax.experimental import pallas as pl
from jax.experimental.pallas import tpu as pltpu

F32 = jnp.float32
BF16 = jnp.bfloat16
I32 = jnp.int32
U32 = jnp.uint32

LANES = 128
SUBLANES = 8
VMEM_LIMIT_BYTES = 56 * 1024 * 1024

NORM_EPS = 1e-6
ML_HEADS = 4
CONV_W = 5
AT_HEADS = 8
AT_KV_HEADS = 2
GRID_W = 64
WINDOW = 128
BLOCK_Q = 128
ROPE_THETA = 10000.0
N_GROUPS = 8
TOPK_GROUPS = 4
TOP_K = 8
ROUTE_SCALE = 2.5
EXPERT_BLOCK = 256
ZERO_ROWS = 32
WEIGHT_SLOTS = 3
NEG_BIG = -1e30


def _cparams(sem):
    return pltpu.CompilerParams(dimension_semantics=sem, vmem_limit_bytes=VMEM_LIMIT_BYTES)


def _rms(xf, w):
    return xf * lax.rsqrt(jnp.mean(xf * xf, axis=-1, keepdims=True) + NORM_EPS) * w


def _silu(x):
    return x * jax.nn.sigmoid(x)


def _mm(a, b):
    return jnp.dot(a, b, preferred_element_type=F32)


def _pack_halves(x):
    n = x.shape[1] // 2
    u = lax.bitcast_convert_type(x.astype(BF16).astype(F32), U32)
    return (u[:, :n] >> 16) | (u[:, n:] & jnp.uint32(0xFFFF0000))


def _unpack_halves(u):
    return (lax.bitcast_convert_type(u << 16, F32),
            lax.bitcast_convert_type(u & jnp.uint32(0xFFFF0000), F32))


def _store_row_tiled(ref2d, packed):
    r, n = packed.shape
    c = n // LANES
    for j in range(c):
        ref2d[pl.ds(j, r, stride=c), :] = packed[:, j * LANES:(j + 1) * LANES]


def _load_row_tiled(ref2d, r):
    c = ref2d.shape[0] // r
    return [ref2d[pl.ds(j, r, stride=c), :] for j in range(c)]


def _ada_kernel(c_ref, w_ref, b_ref, o_ref):
    s = _silu(c_ref[...]).astype(BF16)
    o_ref[...] = _mm(s, w_ref[...].astype(BF16)) + b_ref[...]


def _ada(cond, w, b):
    r, d = cond.shape
    n = w.shape[1]
    tn = min(n, 1024)
    return pl.pallas_call(
        _ada_kernel,
        grid=(n // tn,),
        in_specs=[pl.BlockSpec((r, d), lambda j: (0, 0)),
                  pl.BlockSpec((d, tn), lambda j: (0, j)),
                  pl.BlockSpec((1, tn), lambda j: (0, j))],
        out_specs=pl.BlockSpec((r, tn), lambda j: (0, j)),
        out_shape=jax.ShapeDtypeStruct((r, n), F32),
        compiler_params=_cparams(("arbitrary",)),
        name="ada",
    )(cond, w, b)


def _inproj_kernel(x_ref, sh_ref, sc_ref, nw_ref, wm_ref, wt_ref, rc_ref, ra_ref, rb_ref, *refs,
                   layout, dh, rope, qscale, cw):
    hb_ref = refs[-1]
    h = _rms(x_ref[0], nw_ref[...]) * (1.0 + sc_ref[0]) + sh_ref[0]
    hb_ref[...] = h.astype(BF16)
    for (name, which, col0, width), o_ref in zip(layout, refs):
        w_ref = (wm_ref, wt_ref)[which]
        for c in range(0, width, cw):
            step = min(cw, width - c)
            acc = _mm(hb_ref[...], w_ref[:, col0 + c:col0 + c + step])
            if rope and name in ("aq", "ak"):
                for hh in range(step // dh):
                    a = acc[:, hh * dh:(hh + 1) * dh]
                    r = (a * rc_ref[...] + pltpu.roll(a, dh - dh // 4, 1) * ra_ref[...]
                         + pltpu.roll(a, dh // 4, 1) * rb_ref[...])
                    if name == "aq":
                        r = r * qscale
                    o_ref[0, :, c + hh * dh:c + (hh + 1) * dh] = r.astype(o_ref.dtype)
            else:
                if name == "aq":
                    acc = acc * qscale
                o_ref[0, :, c:c + step] = acc.astype(o_ref.dtype)


def _inproj(x, sh, sc, nw, w_all, w_tail, tabs, *, layout, ml_w, dh, rope, tm):
    b, s, d = x.shape
    rc, ra, rb = tabs
    kern = functools.partial(_inproj_kernel, layout=layout, dh=dh, rope=rope, qscale=float(dh) ** -0.5, cw=512)
    vec = pl.BlockSpec((1, 1, d), lambda bi, i: (bi, 0, 0))
    tab = pl.BlockSpec((tm, dh), lambda bi, i: (i, 0))
    out_dt = dict(ml=BF16, aq=BF16, ak=BF16, av=BF16, g=F32)
    names = [n for n, _, _, _ in layout]
    widths = {n: w for n, _, _, w in layout}
    outs = pl.pallas_call(
        kern,
        grid=(b, s // tm),
        in_specs=[pl.BlockSpec((1, tm, d), lambda bi, i: (bi, i, 0)), vec, vec,
                  pl.BlockSpec((1, d), lambda bi, i: (0, 0)),
                  pl.BlockSpec((d, ml_w), lambda bi, i: (0, 0), pipeline_mode=pl.Buffered(1)),
                  pl.BlockSpec(w_tail.shape, lambda bi, i: (0, 0), pipeline_mode=pl.Buffered(1)),
                  tab, tab, tab],
        out_specs=[pl.BlockSpec((1, tm, widths[n]), lambda bi, i: (bi, i, 0)) for n in names],
        out_shape=[jax.ShapeDtypeStruct((b, s, widths[n]), out_dt[n]) for n in names],
        scratch_shapes=[pltpu.VMEM((tm, d), BF16)],
        compiler_params=_cparams(("arbitrary", "arbitrary")),
        name="inproj_rope" if rope else "inproj_ctx",
    )(x, sh, sc, nw, w_all, w_tail, rc, ra, rb)
    return dict(zip(names, outs))


def _rope_tables(s, dh):
    rows = s // GRID_W
    row = jnp.repeat(jnp.arange(rows), GRID_W)
    col = jnp.tile(jnp.arange(GRID_W), rows)
    nf = dh // 4
    freqs = ROPE_THETA ** (-jnp.arange(nf, dtype=F32) / nf)
    pos = jnp.stack([row, col], axis=-1).astype(F32)
    ang = pos[:, :, None] * freqs
    cos, sin = jnp.cos(ang), jnp.sin(ang)
    z = jnp.zeros_like(sin[:, 0])
    rc = jnp.concatenate([cos[:, 0], cos[:, 0], cos[:, 1], cos[:, 1]], axis=-1)
    ra = jnp.concatenate([-sin[:, 0], z, -sin[:, 1], z], axis=-1)
    rb = jnp.concatenate([z, sin[:, 0], z, sin[:, 1]], axis=-1)
    return rc, ra, rb


def _gate_prep_kernel(g_ref, b_ref, gl_ref, pp_ref, glt_ref, ppt_ref, tot_ref, *, L, H, nc):
    for c in range(nc):
        z = g_ref[0, c * L:(c + 1) * L, :] + b_ref[...]
        lane = lax.broadcasted_iota(I32, z.shape, 1)
        row = lax.broadcasted_iota(I32, z.shape, 0)
        is_f = ((lane // H) % 2 == 1) & (lane < 4 * H)
        log_sig = jnp.minimum(z, 0.0) - jnp.log1p(jnp.exp(-jnp.abs(z)))
        gl = jnp.where(lane < 4 * H, jnp.where(is_f, log_sig, z), 0.0)
        cs = gl
        k = 1
        while k < L:
            cs = cs + jnp.where(row >= k, pltpu.roll(cs, k, 0), 0.0)
            k *= 2
        tot = cs[L - 1:L, :]
        suf = tot - cs + gl
        pp = jnp.where(lane >= 2 * H, suf, cs)
        gl_ref[0, c] = gl
        pp_ref[0, c] = pp
        glt_ref[0, c] = gl.T
        ppt_ref[0, c] = pp.T
        tot_ref[0, c] = jnp.broadcast_to(tot, (SUBLANES, LANES))


def _gate_prep(g, gate_b, L):
    b, s, _ = g.shape
    nc = s // L
    kern = functools.partial(_gate_prep_kernel, L=L, H=ML_HEADS, nc=nc)
    col = pl.BlockSpec((1, nc, L, LANES), lambda bi: (bi, 0, 0, 0))
    rowb = pl.BlockSpec((1, nc, LANES, L), lambda bi: (bi, 0, 0, 0))
    return pl.pallas_call(
        kern,
        grid=(b,),
        in_specs=[pl.BlockSpec((1, s, LANES), lambda bi: (bi, 0, 0)),
                  pl.BlockSpec((1, LANES), lambda bi: (0, 0))],
        out_specs=[col, col, rowb, rowb, pl.BlockSpec((1, nc, SUBLANES, LANES), lambda bi: (bi, 0, 0, 0))],
        out_shape=[jax.ShapeDtypeStruct((b, nc, L, LANES), F32), jax.ShapeDtypeStruct((b, nc, L, LANES), F32),
                   jax.ShapeDtypeStruct((b, nc, LANES, L), F32), jax.ShapeDtypeStruct((b, nc, LANES, L), F32),
                   jax.ShapeDtypeStruct((b, nc, SUBLANES, LANES), F32)],
        compiler_params=_cparams(("arbitrary",)),
        name="gate_prep",
    )(g, gate_b)


CONV_PAD = 8


def _lane_pick(tile, ch):
    lane = lax.broadcasted_iota(I32, tile.shape, 1)
    return jnp.sum(jnp.where(lane == ch, tile, 0.0), axis=1, keepdims=True)


def _sublane_pick(tile, ch):
    sub = lax.broadcasted_iota(I32, tile.shape, 0)
    return jnp.sum(jnp.where(sub == ch, tile, 0.0), axis=0, keepdims=True)


def _mlstm_chunk(d, ch_i, ch_f, q, k, v, gl, pp, glt, ppt, tot8, ct_ref, n_ref, m_ref, mask_ref=None):
    L = k.shape[0]
    p_col = _lane_pick(pp, ch_f)
    li_col = _lane_pick(gl, ch_i)
    btot = _lane_pick(tot8[0:1], ch_f)
    m_old = m_ref[d][0:1, 0:1]
    ct = ct_ref[d]
    n_row = n_ref[d]
    h = None
    if q is not None:
        p_row = _sublane_pick(ppt, ch_f)
        li_row = _sublane_pick(glt, ch_i)
        logw = (p_col - p_row + li_row) + mask_ref[d]
        log_inter = p_col + m_old
        m_q = jnp.maximum(log_inter, jnp.max(logw, axis=1, keepdims=True))
        sqk = lax.dot_general(q, k, (((1,), (1,)), ((), ())), preferred_element_type=F32) * jnp.exp(logw - m_q)
        w_inter = jnp.exp(log_inter - m_q)
        num = _mm(sqk.astype(BF16), v) + w_inter * _mm(q, ct.astype(BF16))
        den = (jnp.sum(sqk, axis=1, keepdims=True)
               + w_inter * jnp.sum(q.astype(F32) * n_row, axis=1, keepdims=True))
        h = num / jnp.maximum(jnp.abs(den), jnp.exp(-m_q))
    log_w_end = btot - p_col + li_col
    m_new = jnp.maximum(btot + m_old, jnp.max(log_w_end, axis=0, keepdims=True))
    w_end = jnp.exp(log_w_end - m_new)
    decay = jnp.exp(btot + m_old - m_new)
    kw = k.astype(F32) * w_end
    ct_ref[d] = decay * ct + lax.dot_general(kw.astype(BF16), v, (((0,), (0,)), ((), ())),
                                             preferred_element_type=F32)
    n_ref[d] = decay * n_row + jnp.sum(kw, axis=0, keepdims=True)
    m_ref[d] = jnp.broadcast_to(m_new, (SUBLANES, LANES))
    return h


def _mlstm_kernel(q_ref, k_ref, v_ref, o_ref, kc_ref, vc_ref,
                  gl_ref, pp_ref, glt_ref, ppt_ref, tot_ref, glc_ref, ppc_ref, totc_ref,
                  cwq_ref, cwk_ref, nw_ref, mask_ref, y_ref,
                  xq_s, xk_s, xkc_s, qs_s, ks_s, kcs_s, hf_s, hb_s, ct_s, n_s, m_s,
                  *, L, H, S, Lc, dk, dv):
    hh = pl.program_id(1)
    nc, ncc = S // L, Lc // L
    zpad = jnp.zeros((CONV_PAD, dk), F32)

    def conv_silu(x_s, cw_ref, n_rows, out_s, scale):
        for c in range(n_rows // L):
            acc = jnp.zeros((L, dk), F32)
            for j in range(CONV_W):
                acc = acc + cw_ref[j:j + 1, :] * x_s[pl.ds(CONV_PAD + c * L + j - CONV_W // 2, L), :]
            out_s[c] = (_silu(acc) * scale).astype(BF16)

    def stage(x_s, src, n_rows):
        x_s[0:CONV_PAD, :] = zpad
        x_s[CONV_PAD + n_rows:2 * CONV_PAD + n_rows, :] = zpad
        x_s[CONV_PAD:CONV_PAD + n_rows, :] = src.astype(F32)

    stage(xq_s, q_ref[0], S)
    stage(xk_s, k_ref[0], S)
    stage(xkc_s, kc_ref[0], Lc)
    conv_silu(xq_s, cwq_ref, S, qs_s, float(dk) ** -0.5)
    conv_silu(xk_s, cwk_ref, S, ks_s, 1.0)
    conv_silu(xkc_s, cwk_ref, Lc, kcs_s, 1.0)

    ct_s[...] = jnp.zeros(ct_s.shape, F32)
    n_s[...] = jnp.zeros(n_s.shape, F32)
    m_s[...] = jnp.zeros(m_s.shape, F32)

    def chans(d):
        return d * 2 * H + hh, d * 2 * H + H + hh

    for d in (0, 1):
        ch_i, ch_f = chans(d)
        for c in (range(ncc) if d == 0 else range(ncc - 1, -1, -1)):
            _mlstm_chunk(d, ch_i, ch_f, None, kcs_s[c], vc_ref[0, c * L:(c + 1) * L, :],
                         glc_ref[0, c], ppc_ref[0, c], None, None, totc_ref[0, c], ct_s, n_s, m_s)

    def body(i, carry):
        for d in (0, 1):
            ch_i, ch_f = chans(d)
            c = i if d == 0 else nc - 1 - i
            r0 = pl.multiple_of(c * L, L)
            h = _mlstm_chunk(d, ch_i, ch_f, qs_s[c], ks_s[c], v_ref[0, pl.ds(r0, L), :],
                             gl_ref[0, c], pp_ref[0, c], glt_ref[0, c], ppt_ref[0, c], tot_ref[0, c],
                             ct_s, n_s, m_s, mask_ref)
            if d == 0:
                hf_s[c] = h
            else:
                hb_s[c] = h
        return carry

    lax.fori_loop(0, nc, body, 0)

    for c in range(nc):
        hs = hf_s[c] + hb_s[c]
        hn = _rms(hs, nw_ref[...])
        y_ref[0, c * L:(c + 1) * L, :] = (hn * jax.nn.sigmoid(o_ref[0, c * L:(c + 1) * L, :].astype(F32))).astype(BF16)


def _mlstm(ml, mlc, gp, gpc, conv_w, norm_w, L):
    b, s, _ = ml.shape
    lc = mlc.shape[1]
    H = ML_HEADS
    dv = norm_w.shape[1] // H
    dk = dv // 2
    nc, ncc = s // L, lc // L
    gl, pp, glt, ppt, tot = gp
    glc, ppc, _, _, totc = gpc
    kern = functools.partial(_mlstm_kernel, L=L, H=H, S=s, Lc=lc, dk=dk, dv=dv)
    tri = jnp.arange(L)[None, :] <= jnp.arange(L)[:, None]
    masks = jnp.where(jnp.stack([tri, tri.T]), 0.0, NEG_BIG).astype(F32)

    def colspec(rows, w, off):
        return pl.BlockSpec((1, rows, w), lambda bi, h: (bi, 0, off + h))

    def gspec(n, r, c):
        return pl.BlockSpec((1, n, r, c), lambda bi, h: (bi, 0, 0, 0))

    return pl.pallas_call(
        kern,
        grid=(b, H),
        in_specs=[colspec(s, dk, 0), colspec(s, dk, H), colspec(s, dv, H), colspec(s, dv, 2 * H),
                  colspec(lc, dk, H), colspec(lc, dv, H),
                  gspec(nc, L, LANES), gspec(nc, L, LANES), gspec(nc, LANES, L), gspec(nc, LANES, L),
                  gspec(nc, SUBLANES, LANES),
                  gspec(ncc, L, LANES), gspec(ncc, L, LANES), gspec(ncc, SUBLANES, LANES),
                  pl.BlockSpec((CONV_W, dk), lambda bi, h: (0, h)),
                  pl.BlockSpec((CONV_W, dk), lambda bi, h: (0, H + h)),
                  pl.BlockSpec((1, dv), lambda bi, h: (0, h)),
                  pl.BlockSpec((2, L, L), lambda bi, h: (0, 0, 0))],
        out_specs=pl.BlockSpec((1, s, dv), lambda bi, h: (bi, 0, h)),
        out_shape=jax.ShapeDtypeStruct((b, s, H * dv), BF16),
        scratch_shapes=[pltpu.VMEM((s + 2 * CONV_PAD, dk), F32), pltpu.VMEM((s + 2 * CONV_PAD, dk), F32),
                        pltpu.VMEM((lc + 2 * CONV_PAD, dk), F32),
                        pltpu.VMEM((nc, L, dk), BF16), pltpu.VMEM((nc, L, dk), BF16), pltpu.VMEM((ncc, L, dk), BF16),
                        pltpu.VMEM((nc, L, dv), F32), pltpu.VMEM((nc, L, dv), F32),
                        pltpu.VMEM((2, dk, dv), F32), pltpu.VMEM((2, 1, dk), F32),
                        pltpu.VMEM((2, SUBLANES, LANES), F32)],
        compiler_params=_cparams(("arbitrary", "arbitrary")),
        name="mlstm",
    )(ml, ml, ml, ml, mlc, mlc, gl, pp, glt, ppt, tot, glc, ppc, totc, conv_w, conv_w, norm_w, masks)


def _attn_kernel(sink_ref, q_ref, kp_ref, kc_ref, kn_ref, vp_ref, vc_ref, vn_ref, kx_ref, vx_ref, bias_ref,
                 o_ref, *, bq, dh, G):
    q = q_ref[0]
    for g in range(AT_KV_HEADS):
        sl = slice(g * dh, (g + 1) * dh)
        qg = jnp.concatenate([q[:, (g * G + j) * dh:(g * G + j + 1) * dh] for j in range(G)], axis=0)
        kcat = jnp.concatenate([kp_ref[0][:, sl], kc_ref[0][:, sl], kn_ref[0][:, sl], kx_ref[0][:, sl]], axis=0)
        vcat = jnp.concatenate([vp_ref[0][:, sl], vc_ref[0][:, sl], vn_ref[0][:, sl], vx_ref[0][:, sl]], axis=0)
        s = lax.dot_general(qg, kcat, (((1,), (1,)), ((), ())), preferred_element_type=F32) + bias_ref[0]
        ri = lax.broadcasted_iota(I32, (G * bq, 1), 0)
        sk = jnp.zeros((G * bq, 1), F32)
        for j in range(G):
            sk = jnp.where((ri >= j * bq) & (ri < (j + 1) * bq), sink_ref[g * G + j], sk)
        m = jnp.maximum(jnp.max(s, axis=1, keepdims=True), sk)
        p = jnp.exp(s - m)
        den = jnp.sum(p, axis=1, keepdims=True) + jnp.exp(sk - m)
        o = _mm(p.astype(BF16), vcat) / den
        for j in range(G):
            o_ref[0, :, (g * G + j) * dh:(g * G + j + 1) * dh] = o[j * bq:(j + 1) * bq, :].astype(BF16)


def _attn_bias(bq, lc, G):
    qoff = (jnp.arange(G * bq) % bq)[:, None]
    ci = jnp.arange(3 * bq + lc)[None, :]
    prev = (ci < bq) & (qoff <= ci)
    cur = (ci >= bq) & (ci < 2 * bq)
    nxt = (ci >= 2 * bq) & (ci < 3 * bq) & (ci - 2 * bq <= qoff)
    ctx = ci >= 3 * bq
    inner = prev | cur | nxt | ctx
    first = cur | nxt | ctx
    last = prev | cur | ctx
    if WINDOW != bq:
        raise NotImplementedError("window must equal the query block")
    return jnp.where(jnp.stack([inner, first, last]), 0.0, NEG_BIG).astype(F32)


def _attn(aq, ak, av, akc, avc, sink):
    b, s, hd = aq.shape
    dh = hd // AT_HEADS
    G = AT_HEADS // AT_KV_HEADS
    bq = BLOCK_Q
    nb = s // bq
    lc = akc.shape[1]
    kvw = AT_KV_HEADS * dh
    bias = _attn_bias(bq, lc, G)
    if nb < 2:
        raise NotImplementedError("needs at least two query blocks")
    kern = functools.partial(_attn_kernel, bq=bq, dh=dh, G=G)
    prev = pl.BlockSpec((1, bq, kvw), lambda bi, n: (bi, jnp.maximum(n - 1, 0), 0))
    cur = pl.BlockSpec((1, bq, kvw), lambda bi, n: (bi, n, 0))
    nxt = pl.BlockSpec((1, bq, kvw), lambda bi, n: (bi, jnp.minimum(n + 1, nb - 1), 0))
    cx = pl.BlockSpec((1, lc, kvw), lambda bi, n: (bi, 0, 0))
    return pl.pallas_call(
        kern,
        grid=(b, nb),
        in_specs=[pl.BlockSpec(memory_space=pltpu.SMEM),
                  pl.BlockSpec((1, bq, hd), lambda bi, n: (bi, n, 0)),
                  prev, cur, nxt, prev, cur, nxt, cx, cx,
                  pl.BlockSpec((1, G * bq, 3 * bq + lc),
                               lambda bi, n: (jnp.where(n == 0, 1, jnp.where(n == nb - 1, 2, 0)), 0, 0))],
        out_specs=pl.BlockSpec((1, bq, hd), lambda bi, n: (bi, n, 0)),
        out_shape=jax.ShapeDtypeStruct((b, s, hd), BF16),
        compiler_params=_cparams(("arbitrary", "arbitrary")),
        name="attn",
    )(sink, aq, ak, ak, ak, av, av, av, akc, avc, bias)


def _outproj_kernel(ym_ref, ya_ref, x_ref, g1_ref, sh2_ref, sc2_ref, nw1_ref, nw2_ref, wo_ref, wrh_ref, wrl_ref,
                    x1_ref, h2_ref, lg_ref, y_s, *, dm, cw):
    i = pl.program_id(0)
    d = y_s.shape[2]

    @pl.when(i == 0)
    def _():
        y_s[...] = jnp.zeros(y_s.shape, F32)

    cur = i & 1
    x1 = x_ref[0] + g1_ref[0] * _rms(y_s[1 - cur], nw1_ref[...])
    x1_ref[0] = x1
    h2 = _rms(x1, nw2_ref[...]) * (1.0 + sc2_ref[0]) + sh2_ref[0]
    hi = h2.astype(BF16)
    _store_row_tiled(h2_ref.at[0], _pack_halves(h2))
    lo = (h2 - hi.astype(F32)).astype(BF16)
    lg_ref[0] = _mm(hi, wrh_ref[...]) + (_mm(hi, wrl_ref[...]) + _mm(lo, wrh_ref[...]))
    for c in range(0, d, cw):
        y_s[cur, :, c:c + cw] = _mm(ym_ref[0], wo_ref[0:dm, c:c + cw]) + _mm(ya_ref[0], wo_ref[dm:, c:c + cw])


def _outproj(y_ml, y_at, x, g1, sh2, sc2, nw1, nw2, wo, wr, tm):
    b, s, d = x.shape
    dm = y_ml.shape[2]
    e = wr.shape[1]
    wr_hi = wr.astype(BF16)
    wr_lo = (wr - wr_hi.astype(F32)).astype(BF16)
    kern = functools.partial(_outproj_kernel, dm=dm, cw=min(d, 512))
    nt = s // tm
    n = b * nt

    def ahead(i):
        a = jnp.minimum(i, n - 1)
        return a // nt, a % nt

    def behind(i):
        a = jnp.maximum(i - 1, 0)
        return a // nt, a % nt

    def tok(w, which):
        return pl.BlockSpec((1, tm, w), lambda i: (*which(i), 0))

    vec = pl.BlockSpec((1, 1, d), lambda i: (behind(i)[0], 0, 0))
    nspec = pl.BlockSpec((1, d), lambda i: (0, 0))
    const = lambda r, c: pl.BlockSpec((r, c), lambda i: (0, 0), pipeline_mode=pl.Buffered(1))
    rc = d // 2 // LANES
    return pl.pallas_call(
        kern,
        grid=(n + 1,),
        in_specs=[tok(dm, ahead), tok(d - dm, ahead), tok(d, behind),
                  vec, vec, vec, nspec, nspec, const(d, d), const(d, e), const(d, e)],
        out_specs=[tok(d, behind), pl.BlockSpec((1, tm * rc, LANES), lambda i: (*behind(i), 0)), tok(e, behind)],
        out_shape=[jax.ShapeDtypeStruct((b, s, d), F32), jax.ShapeDtypeStruct((b, s * rc, LANES), U32),
                   jax.ShapeDtypeStruct((b, s, e), F32)],
        scratch_shapes=[pltpu.VMEM((2, tm, d), F32)],
        compiler_params=_cparams(("arbitrary",)),
        name="outproj",
    )(y_ml, y_at, x, g1, sh2, sc2, nw1, nw2, wo, wr_hi, wr_lo)


def _route_kernel(lg_ref, br_ref, eidx_ref, ew_ref, rank_ref, cnt_ref, carry_s, *, E, tr):
    i = pl.program_id(0)

    @pl.when(i == 0)
    def _():
        carry_s[...] = jnp.zeros(carry_s.shape, F32)

    scores = jax.nn.sigmoid(lg_ref[...].T)
    biased = scores + br_ref[:, 0:1]
    row = lax.broadcasted_iota(I32, (E, tr), 0).astype(F32)
    gs = E // N_GROUPS
    ninf = -jnp.inf
    grp = []
    for g in range(N_GROUPS):
        xg = biased[g * gs:(g + 1) * gs]
        rg = (lax.broadcasted_iota(I32, (gs, tr), 0) + g * gs).astype(F32)
        m1 = jnp.max(xg, axis=0, keepdims=True)
        i1 = jnp.min(jnp.where(xg == m1, rg, float(E)), axis=0, keepdims=True)
        m2 = jnp.max(jnp.where(rg == i1, ninf, xg), axis=0, keepdims=True)
        grp.append(m1 + m2)
    gsc = jnp.concatenate(grp, axis=0)
    gi = lax.broadcasted_iota(I32, (N_GROUPS, tr), 0)
    beaten = jnp.zeros((N_GROUPS, tr), I32)
    for g2 in range(N_GROUPS):
        sg = gsc[g2:g2 + 1]
        beaten = beaten + jnp.where((sg > gsc) | ((sg == gsc) & (gi > g2)), 1, 0)
    keep_g = jnp.where(beaten < TOPK_GROUPS, 1.0, 0.0)
    keep = jnp.concatenate([jnp.broadcast_to(keep_g[g:g + 1], (gs, tr)) for g in range(N_GROUPS)], axis=0)
    masked = jnp.where(keep > 0.5, biased, ninf)
    idxs, ws, sels = [], [], []
    for _ in range(TOP_K):
        m = jnp.max(masked, axis=0, keepdims=True)
        ik = jnp.min(jnp.where(masked == m, row, float(E)), axis=0, keepdims=True)
        sel = row == ik
        ws.append(jnp.sum(jnp.where(sel, scores, 0.0), axis=0, keepdims=True))
        idxs.append(ik)
        sels.append(sel)
        masked = jnp.where(sel, ninf, masked)
    w = jnp.concatenate(ws, axis=0)
    ew_ref[...] = w / jnp.sum(w, axis=0, keepdims=True) * ROUTE_SCALE
    eidx_ref[...] = jnp.concatenate(idxs, axis=0).astype(I32)

    assign = jnp.zeros((E, tr), F32)
    for sel in sels:
        assign = assign + jnp.where(sel, 1.0, 0.0)
    upper = jnp.where(lax.broadcasted_iota(I32, (tr, tr), 0) < lax.broadcasted_iota(I32, (tr, tr), 1), 1.0, 0.0)
    base = _mm(assign.astype(BF16), upper.astype(BF16)) + carry_s[:, 0:1]
    ranks = [jnp.sum(jnp.where(sel, base, 0.0), axis=0, keepdims=True) for sel in sels]
    rank_ref[...] = jnp.concatenate(ranks, axis=0).astype(I32)
    carry_s[...] = carry_s[...] + jnp.sum(assign, axis=1, keepdims=True)
    cnt_ref[...] = carry_s[...]


def _route(logits, b_router, tr):
    t, e = logits.shape
    kern = functools.partial(_route_kernel, E=e, tr=tr)
    kt = pl.BlockSpec((TOP_K, tr), lambda i: (0, i))
    return pl.pallas_call(
        kern,
        grid=(t // tr,),
        in_specs=[pl.BlockSpec((tr, e), lambda i: (i, 0)), pl.BlockSpec((e, LANES), lambda i: (0, 0))],
        out_specs=[kt, kt, kt, pl.BlockSpec((e, LANES), lambda i: (0, 0))],
        out_shape=[jax.ShapeDtypeStruct((TOP_K, t), I32), jax.ShapeDtypeStruct((TOP_K, t), F32),
                   jax.ShapeDtypeStruct((TOP_K, t), I32), jax.ShapeDtypeStruct((e, LANES), F32)],
        scratch_shapes=[pltpu.VMEM((e, LANES), F32)],
        compiler_params=_cparams(("arbitrary",)),
        name="route",
    )(logits, b_router)


def _positions_kernel(eidx_ref, rank_ref, ps_ref, pos_ref, *, E, tr):
    row = lax.broadcasted_iota(I32, (E, tr), 0)
    start = ps_ref[:, 0:1]
    rows = [jnp.sum(jnp.where(row == eidx_ref[k:k + 1, :], start, 0.0), axis=0, keepdims=True)
            for k in range(TOP_K)]
    pos_ref[...] = jnp.concatenate(rows, axis=0).astype(I32) + rank_ref[...]


def _positions(eidx, rank, pad_start, tr):
    t = eidx.shape[1]
    e = pad_start.shape[0]
    kern = functools.partial(_positions_kernel, E=e, tr=tr)
    kt = pl.BlockSpec((TOP_K, tr), lambda i: (0, i))
    return pl.pallas_call(
        kern,
        grid=(t // tr,),
        in_specs=[kt, kt, pl.BlockSpec((e, LANES), lambda i: (0, 0))],
        out_specs=kt,
        out_shape=jax.ShapeDtypeStruct((TOP_K, t), I32),
        compiler_params=_cparams(("arbitrary",)),
        name="positions",
    )(eidx, rank, jnp.broadcast_to(pad_start.astype(F32)[:, None], (e, LANES)))


def _dispatch_kernel(zstart_ref, zcount_ref, pos_hbm, h2_ref, xs_hbm, zero_s, pos_s, sem_p, sem_z, sem_r,
                     *, E, tt):
    i = pl.program_id(0)

    def zero_copy(e, j):
        return pltpu.make_async_copy(zero_s, xs_hbm.at[pl.ds(zstart_ref[e] + j * ZERO_ROWS, ZERO_ROWS)], sem_z)

    @pl.when(i == 0)
    def _():
        zero_s[...] = jnp.zeros(zero_s.shape, U32)

        def each(fn):
            def per_expert(e, c):
                def per_piece(j, c2):
                    fn(zero_copy(e, j))
                    return c2
                return lax.fori_loop(0, zcount_ref[e], per_piece, c)
            lax.fori_loop(0, E, per_expert, 0)

        each(lambda cp: cp.start())
        each(lambda cp: cp.wait())

    pos_copy = pltpu.make_async_copy(pos_hbm.at[i], pos_s, sem_p)
    pos_copy.start()
    pos_copy.wait()

    def row_copy(t, k):
        return pltpu.make_async_copy(h2_ref.at[t], xs_hbm.at[pos_s[k, t]], sem_r)

    def start_rows(t, c):
        for k in range(TOP_K):
            row_copy(t, k).start(priority=k % 2)
        return c

    def wait_rows(t, c):
        for k in range(TOP_K):
            row_copy(t, k).wait()
        return c

    lax.fori_loop(0, tt, start_rows, 0)
    lax.fori_loop(0, tt, wait_rows, 0)


def _dispatch(h2, pos_tiles, zstart, zcount, n_rows, tt):
    t, c, _ = h2.shape
    e = zstart.shape[0]
    kern = functools.partial(_dispatch_kernel, E=e, tt=tt)
    return pl.pallas_call(
        kern,
        grid_spec=pltpu.PrefetchScalarGridSpec(
            num_scalar_prefetch=2,
            grid=(t // tt,),
            in_specs=[pl.BlockSpec(memory_space=pl.ANY),
                      pl.BlockSpec((tt, c, LANES), lambda i, zs, zv: (i, 0, 0))],
            out_specs=pl.BlockSpec(memory_space=pl.ANY),
            scratch_shapes=[pltpu.VMEM((ZERO_ROWS, c, LANES), U32), pltpu.SMEM((TOP_K, tt), I32),
                            pltpu.SemaphoreType.DMA, pltpu.SemaphoreType.DMA, pltpu.SemaphoreType.DMA]),
        out_shape=jax.ShapeDtypeStruct((n_rows, c, LANES), U32),
        compiler_params=_cparams(("arbitrary",)),
        name="dispatch",
    )(zstart, zcount, pos_tiles, h2)


def _experts_kernel(be_ref, nu_ref, nxt_ref, ord_ref, half_ref, xs_ref, wg_hbm, wu_hbm, wd_hbm, ys_ref,
                    sg_s, su_s, sd_s, sem, *, E):
    b = pl.program_id(0)
    n_used = nu_ref[0]

    def copies(e, slot):
        out = []
        for m, (w_hbm, st) in enumerate(((wg_hbm, sg_s), (wu_hbm, su_s), (wd_hbm, sd_s))):
            rows = w_hbm.shape[1] // 2
            for p in range(2):
                out.append(pltpu.make_async_copy(w_hbm.at[e, pl.ds(p * rows, rows)],
                                                 st.at[slot, pl.ds(p * rows, rows)], sem.at[slot, 2 * m + p]))
        return out

    def start_all(cps):
        for n, cp in enumerate(cps):
            cp.start(priority=n % 2)

    @pl.when(b < n_used)
    def _():
        e = be_ref[b]
        prev = be_ref[jnp.maximum(b - 1, 0)]
        slot = lax.rem(ord_ref[e], WEIGHT_SLOTS)
        ne = nxt_ref[e]
        n2 = nxt_ref[jnp.minimum(ne, E - 1)]

        @pl.when(b == 0)
        def _():
            start_all(copies(e, slot))

            @pl.when(ne < E)
            def _():
                start_all(copies(ne, lax.rem(slot + 1, WEIGHT_SLOTS)))

        @pl.when((b == 0) | (e != prev))
        def _():
            for cp in copies(e, slot):
                cp.wait()

            @pl.when((ne < E) & (n2 < E))
            def _():
                start_all(copies(n2, lax.rem(slot + 2, WEIGHT_SLOTS)))

        def block(rows):
            span = pl.ds(0, rows * (xs_ref.shape[0] // EXPERT_BLOCK))
            parts = [_unpack_halves(ch) for ch in _load_row_tiled(xs_ref.at[span], rows)]
            x_lo = jnp.concatenate([p[0] for p in parts], axis=1).astype(BF16)
            x_hi = jnp.concatenate([p[1] for p in parts], axis=1).astype(BF16)
            half = x_lo.shape[1]
            gate = _mm(x_lo, sg_s[slot, 0:half, :]) + _mm(x_hi, sg_s[slot, half:, :])
            up = _mm(x_lo, su_s[slot, 0:half, :]) + _mm(x_hi, su_s[slot, half:, :])
            a = _silu(gate) * up
            _store_row_tiled(ys_ref.at[span], _pack_halves(_mm(a.astype(BF16), sd_s[slot])))

        @pl.when(half_ref[b] == 0)
        def _():
            block(EXPERT_BLOCK)

        @pl.when(half_ref[b] != 0)
        def _():
            block(EXPERT_BLOCK // 2)


def _experts(xs, block_e, n_used, next_used, used_ord, half_blk, wg, wu, wd):
    e, d, de = wg.shape
    c = d // 2 // LANES
    nblk = xs.shape[0] // (EXPERT_BLOCK * c)
    kern = functools.partial(_experts_kernel, E=e)

    def live(b, be, nu, nx, od, hf):
        return (jnp.minimum(b, jnp.maximum(nu[0] - 1, 0)), 0)

    hbm = pl.BlockSpec(memory_space=pl.ANY)
    return pl.pallas_call(
        kern,
        grid_spec=pltpu.PrefetchScalarGridSpec(
            num_scalar_prefetch=5,
            grid=(nblk,),
            in_specs=[pl.BlockSpec((EXPERT_BLOCK * c, LANES), live), hbm, hbm, hbm],
            out_specs=pl.BlockSpec((EXPERT_BLOCK * c, LANES), live),
            scratch_shapes=[pltpu.VMEM((WEIGHT_SLOTS, d, de), F32), pltpu.VMEM((WEIGHT_SLOTS, d, de), F32),
                            pltpu.VMEM((WEIGHT_SLOTS, de, d), F32), pltpu.SemaphoreType.DMA((WEIGHT_SLOTS, 6))]),
        out_shape=jax.ShapeDtypeStruct(xs.shape, U32),
        compiler_params=_cparams(("arbitrary",)),
        name="experts",
    )(block_e, n_used, next_used, used_ord, half_blk, xs, wg, wu, wd)


def _combine_kernel(pos_hbm, ys_hbm, ew_ref, x1_ref, h2_ref, g2_ref, nw3_ref, wsg_ref, wsu_ref, wsd_ref,
                    out_ref, buf_s, pos_s, sem_p, sem_r, *, tc, nt):
    tile = pl.program_id(0) * nt + pl.program_id(1)
    pos_copy = pltpu.make_async_copy(pos_hbm.at[tile], pos_s, sem_p)
    pos_copy.start()
    pos_copy.wait()

    def row_copy(t, k):
        c = ys_hbm.shape[1]
        return pltpu.make_async_copy(ys_hbm.at[pos_s[k, t]], buf_s.at[k, pl.ds(pl.multiple_of(t * c, c), c)], sem_r)

    def start_rows(t, c):
        for k in range(TOP_K):
            row_copy(t, k).start(priority=k % 2)
        return c

    def wait_rows(t, c):
        for k in range(TOP_K):
            row_copy(t, k).wait()
        return c

    lax.fori_loop(0, tc, start_rows, 0)
    parts = [_unpack_halves(ch) for ch in _load_row_tiled(h2_ref.at[0], tc)]
    h_lo = jnp.concatenate([p[0] for p in parts], axis=1).astype(BF16)
    h_hi = jnp.concatenate([p[1] for p in parts], axis=1).astype(BF16)
    half = h_lo.shape[1]
    nch = half // LANES
    gate = _mm(h_lo, wsg_ref[0:half, :]) + _mm(h_hi, wsg_ref[half:, :])
    up = _mm(h_lo, wsu_ref[0:half, :]) + _mm(h_hi, wsu_ref[half:, :])
    shared = _mm((_silu(gate) * up).astype(BF16), wsd_ref[...])
    lax.fori_loop(0, tc, wait_rows, 0)
    ew = ew_ref[0]
    acc = [shared[:, j * LANES:(j + 1) * LANES] for j in range(2 * nch)]
    for k in range(TOP_K):
        wk = ew[:, k:k + 1]
        for j, ch in enumerate(_load_row_tiled(buf_s.at[k], tc)):
            y_lo, y_hi = _unpack_halves(ch)
            acc[j] = acc[j] + y_lo * wk
            acc[nch + j] = acc[nch + j] + y_hi * wk
    out_ref[0] = x1_ref[0] + g2_ref[0] * _rms(jnp.concatenate(acc, axis=1), nw3_ref[...])


def _combine(pos_tiles, ys, ew, x1, h2, g2, nw3, wsg, wsu, wsd, tc):
    b, s, d = x1.shape
    ds_ = wsg.shape[1]
    nt = s // tc
    c = ys.shape[1]
    kern = functools.partial(_combine_kernel, tc=tc, nt=nt)
    tok = lambda w: pl.BlockSpec((1, tc, w), lambda bi, i: (bi, i, 0))
    return pl.pallas_call(
        kern,
        grid=(b, nt),
        in_specs=[pl.BlockSpec(memory_space=pl.ANY), pl.BlockSpec(memory_space=pl.ANY),
                  tok(TOP_K), tok(d), pl.BlockSpec((1, tc * c, LANES), lambda bi, i: (bi, i, 0)),
                  pl.BlockSpec((1, 1, d), lambda bi, i: (bi, 0, 0)),
                  pl.BlockSpec((1, d), lambda bi, i: (0, 0)),
                  pl.BlockSpec((d, ds_), lambda bi, i: (0, 0)),
                  pl.BlockSpec((d, ds_), lambda bi, i: (0, 0)),
                  pl.BlockSpec((ds_, d), lambda bi, i: (0, 0))],
        out_specs=tok(d),
        out_shape=jax.ShapeDtypeStruct((b, s, d), F32),
        scratch_shapes=[pltpu.VMEM((TOP_K, tc * c, LANES), U32), pltpu.SMEM((TOP_K, tc), I32),
                        pltpu.SemaphoreType.DMA, pltpu.SemaphoreType.DMA],
        compiler_params=_cparams(("arbitrary", "arbitrary")),
        name="combine",
    )(pos_tiles, ys, ew, x1, h2, g2, nw3, wsg, wsu, wsd)


def _tile(n, pref):
    t = min(n, pref)
    if n % t:
        raise NotImplementedError(f"size {n} is not a multiple of tile {t}")
    return t


def kernel(x, c, ctx, c_ctx, w_ada, b_ada, norms, w_in, ml_conv, ml_gate_b, ml_norm_w, attn_sink, w_out,
           w_router, b_router, w_exp_gate, w_exp_up, w_exp_down, w_sh_gate, w_sh_up, w_sh_down):
    if w_ada.shape[0] != 1:
        raise NotImplementedError("single-layer configuration only")
    b, s, d = x.shape
    lc = ctx.shape[1]
    t = b * s
    H = ML_HEADS
    dv = d // 2 // H
    dk = dv // 2
    dh = d // 2 // AT_HEADS
    qk_w, v_w = 2 * H * dk, H * dv
    ng = 4 * H
    aq_w, akv_w = AT_HEADS * dh, AT_KV_HEADS * dh
    e = w_router.shape[-1]
    nw = norms[0]

    rows = -(-(b + 1) // SUBLANES) * SUBLANES
    cond = jnp.zeros((rows, d), F32).at[:b].set(c).at[b].set(c_ctx)
    mods = _ada(cond, w_ada[0], b_ada[0][None, :])
    sh1, sc1, g1, sh2, sc2, g2 = [m[:b, None, :] for m in jnp.split(mods, 6, axis=-1)]
    csh1, csc1 = [jnp.broadcast_to(m[b:b + 1, None, :], (b, 1, d)) for m in jnp.split(mods, 6, axis=-1)[:2]]

    w0 = w_in[0].astype(BF16)
    ml_w = qk_w + 2 * v_w
    o_g, o_q = ml_w, ml_w + ng
    if ml_w % LANES:
        raise NotImplementedError("mLSTM projection width must be lane aligned")
    w_tail = jnp.concatenate([w0[:, o_q:], w0[:, o_g:o_q], jnp.zeros((d, LANES - ng), BF16)], axis=1)
    layout = (("ml", 0, 0, ml_w), ("aq", 1, 0, aq_w), ("ak", 1, aq_w, akv_w),
              ("av", 1, aq_w + akv_w, akv_w), ("g", 1, aq_w + 2 * akv_w, LANES))
    tabs = _rope_tables(s, dh)
    proj = functools.partial(_inproj, layout=layout, ml_w=ml_w, dh=dh)
    lat = proj(x, sh1, sc1, nw[0:1], w0, w_tail, tabs, rope=True, tm=_tile(s, 512))
    ml, aq, ak, av, gates = (lat[n] for n in ("ml", "aq", "ak", "av", "g"))
    tm_c = _tile(lc, 512)
    cx = proj(ctx, csh1, csc1, nw[0:1], w0, w_tail, tuple(tb[:tm_c] for tb in tabs), rope=False, tm=tm_c)
    mlc, akc, avc, gates_c = (cx[n] for n in ("ml", "ak", "av", "g"))

    L = _tile(lc, 256)
    if s % L:
        raise NotImplementedError("sequence must be a multiple of the mLSTM chunk")
    gate_b = jnp.zeros((1, LANES), F32).at[0, :ng].set(ml_gate_b[0])
    gp = _gate_prep(gates, gate_b, L)
    gpc = _gate_prep(gates_c, gate_b, L)
    y_ml = _mlstm(ml, mlc, gp, gpc, ml_conv[0], ml_norm_w[0][None, :], L)

    y_at = _attn(aq, ak, av, akc, avc, attn_sink[0])

    x1, h2, logits = _outproj(y_ml, y_at, x, g1, sh2, sc2, nw[1:2], nw[2:3], w_out[0].astype(BF16),
                              w_router[0], _tile(s, 512))

    eidx, ew, rank, cnt = _route(logits.reshape(t, e), jnp.broadcast_to(b_router[0][:, None], (e, LANES)),
                                 _tile(t, 512))
    counts = cnt[:, 0].astype(I32)
    padded = (counts + EXPERT_BLOCK - 1) // EXPERT_BLOCK * EXPERT_BLOCK
    pad_end = jnp.cumsum(padded)
    pad_start = pad_end - padded
    n_blocks = -(-(t * TOP_K + e * (EXPERT_BLOCK - 1)) // EXPERT_BLOCK)
    n_rows = n_blocks * EXPERT_BLOCK
    blk_row = jnp.arange(n_blocks, dtype=I32) * EXPERT_BLOCK
    block_e = jnp.minimum(jnp.sum((pad_end[None, :] <= blk_row[:, None]).astype(I32), axis=1), e - 1)
    n_used = (pad_end[-1] // EXPERT_BLOCK).astype(I32)[None]
    used = counts > 0
    eid = jnp.arange(e, dtype=I32)
    next_used = jnp.concatenate([lax.cummin(jnp.where(used, eid, e), reverse=True)[1:], jnp.full((1,), e, I32)])
    used_ord = jnp.cumsum(used.astype(I32)) - 1
    pos = _positions(eidx, rank, pad_start, _tile(t, 512))

    tt = _tile(t, 512)
    pos_d = pos.reshape(TOP_K, t // tt, tt).transpose(1, 0, 2)
    rc = d // 2 // LANES
    half_rows = EXPERT_BLOCK // 2
    read_end = pad_start + (counts + half_rows - 1) // half_rows * half_rows
    zcount = (read_end - pad_start - counts + ZERO_ROWS - 1) // ZERO_ROWS
    xs = _dispatch(h2.reshape(t, rc, LANES), pos_d, (read_end - zcount * ZERO_ROWS).astype(I32), zcount.astype(I32),
                   n_rows, tt)
    blk_valid = counts[block_e] - (blk_row - pad_start[block_e])
    half_blk = (blk_valid <= EXPERT_BLOCK // 2).astype(I32)
    ys = _experts(xs.reshape(n_rows * rc, LANES), block_e, n_used, next_used, used_ord, half_blk,
                  w_exp_gate[0], w_exp_up[0], w_exp_down[0])
    tc = _tile(s, 128)
    pos_c = pos.reshape(TOP_K, t // tc, tc).transpose(1, 0, 2)
    ew_t = ew.T.reshape(b, s, TOP_K)
    return _combine(pos_c, ys.reshape(n_rows, rc, LANES), ew_t, x1, h2, g2, nw[3:4],
                    w_sh_gate[0].astype(BF16), w_sh_up[0].astype(BF16), w_sh_down[0].astype(BF16), tc)
```
